```python
import jax, jax.numpy as jnp
from jax import lax
import numpy as np

D_MODEL = 1024
BATCH = 32
SEQ = 256
DEPTH = 2
DEC_BATCH = 2
DEC_SEQ = 1024
PAST_LEN = 256

GRID_W = 64
HEAD_DIM = 64
A_HEADS = 8
A_KV_HEADS = 2
A_GROUP = A_HEADS // A_KV_HEADS
B_HEADS = 8
C_WIDTH = 512
CONV_W = 3
NA_WIN_R = 8
NA_WIN_C = 16
Q_BLOCK = 128
N_EXPERTS = 16
EXPERT_FF = 1024
CAP_FACTOR = 2
ROPE_THETA = 10000.0
EPS = 1e-6
N_BRANCH = 3
N_MOD = 6

A_Q = A_HEADS * HEAD_DIM
A_KV = A_KV_HEADS * HEAD_DIM
B_W = B_HEADS * HEAD_DIM
IN_SIZES = (A_Q, A_KV, A_KV, B_W, B_W, B_W, C_WIDTH, C_WIDTH, C_WIDTH, N_BRANCH * D_MODEL)
IN_COLS = sum(IN_SIZES)

kernel_name = "hybrid_diffusion_gqa_natten_shortconv_ec_step"


def rmsnorm(x, g):
    x32 = x.astype(jnp.float32)
    y = x32 * lax.rsqrt(jnp.mean(x32 * x32, axis=-1, keepdims=True) + EPS)
    return (y * g).astype(x.dtype)


def adaln(cond, w_ada, b_ada):
    m = jax.nn.silu(cond) @ w_ada + b_ada
    m = m.reshape(cond.shape[0], N_MOD, 1, D_MODEL)
    return [m[:, i] for i in range(N_MOD)]


def split_in(h):
    return jnp.split(h, np.cumsum(IN_SIZES)[:-1], axis=-1)


def to_heads(x, n):
    b, s, _ = x.shape
    return x.reshape(b, s, n, HEAD_DIM).transpose(0, 2, 1, 3)


def from_heads(x):
    b, h, s, hd = x.shape
    return x.transpose(0, 2, 1, 3).reshape(b, s, h * hd)


def axial_rope(n_tok):
    t = jnp.arange(n_tok)
    row = (t // GRID_W).astype(jnp.float32)
    col = (t % GRID_W).astype(jnp.float32)
    n_freq = HEAD_DIM // 4
    inv = ROPE_THETA ** (-jnp.arange(n_freq, dtype=jnp.float32) / n_freq)
    ang = jnp.concatenate([row[:, None] * inv, col[:, None] * inv], axis=-1)
    return jnp.cos(ang), jnp.sin(ang)


def apply_rope(x, cos, sin):
    xp = x.astype(jnp.float32).reshape(x.shape[:-1] + (HEAD_DIM // 2, 2))
    x1, x2 = xp[..., 0], xp[..., 1]
    out = jnp.stack([x1 * cos - x2 * sin, x1 * sin + x2 * cos], axis=-1)
    return out.reshape(x.shape).astype(x.dtype)


def block_attention(q, k, v):
    b, hk, g, s, hd = q.shape
    nb = s // Q_BLOCK
    qb = jnp.moveaxis(q.reshape(b, hk, g, nb, Q_BLOCK, hd), 3, 0)

    def attend(qblk):
        sc = jnp.einsum('bkgqd,bkld->bkgql', qblk, k).astype(jnp.float32) * (HEAD_DIM ** -0.5)
        p = jax.nn.softmax(sc, axis=-1).astype(v.dtype)
        return jnp.einsum('bkgql,bkld->bkgqd', p, v)

    o = lax.map(attend, qb)
    return jnp.moveaxis(o, 0, 3).reshape(b, hk * g, s, hd)


def neighbourhood_attention(q, k, v, ck, cv, rpb):
    b, h, s, hd = q.shape
    rows = s // GRID_W
    wr = min(NA_WIN_R, rows)
    nk = wr * GRID_W
    r = jnp.arange(rows)
    r0 = jnp.clip(r - wr // 2, 0, rows - wr)
    krow = r0[:, None] + jnp.arange(wr)[None, :]
    kg = k.reshape(b, h, rows, GRID_W, hd)[:, :, krow].reshape(b, h, rows, nk, hd)
    vg = v.reshape(b, h, rows, GRID_W, hd)[:, :, krow].reshape(b, h, rows, nk, hd)
    qg = q.reshape(b, h, rows, GRID_W, hd)
    col = jnp.arange(GRID_W)
    c0 = jnp.clip(col - NA_WIN_C // 2, 0, GRID_W - NA_WIN_C)
    key_col = jnp.tile(col, wr)
    key_row = jnp.repeat(krow, GRID_W, axis=1)
    dr = key_row - r[:, None]
    dc = key_col[None, :] - col[:, None]
    in_win = (key_col[None, :] >= c0[:, None]) & (key_col[None, :] < c0[:, None] + NA_WIN_C)
    bias = rpb[:, (dr + NA_WIN_R - 1)[:, None, :],
               jnp.clip(dc + NA_WIN_C - 1, 0, 2 * NA_WIN_C - 2)[None, :, :]]
    scale = HEAD_DIM ** -0.5
    s_nb = jnp.einsum('bhrqd,bhrkd->bhrqk', qg, kg).astype(jnp.float32) * scale + bias.astype(jnp.float32)
    s_nb = jnp.where(in_win[None, None, None], s_nb, -jnp.inf)
    s_ctx = jnp.einsum('bhrqd,bhld->bhrql', qg, ck).astype(jnp.float32) * scale
    lc = ck.shape[2]
    p = jax.nn.softmax(jnp.concatenate([s_ctx, s_nb], axis=-1), axis=-1).astype(v.dtype)
    o = (jnp.einsum('bhrql,bhld->bhrqd', p[..., :lc], cv)
         + jnp.einsum('bhrqk,bhrkd->bhrqd', p[..., lc:], vg))
    return o.reshape(b, h, s, hd)


def short_conv(bg, cg, xin, conv_w):
    u = cg * xin
    up = jnp.pad(u, ((0, 0), (1, 1), (0, 0)))
    y = up[:, :-2] * conv_w[0] + up[:, 1:-1] * conv_w[1] + up[:, 2:] * conv_w[2]
    return bg * y


def expert_choice_ffn(x, w_router, w_gate_e, w_up_e, w_down_e):
    n = x.shape[0]
    cap = CAP_FACTOR * n // N_EXPERTS
    probs = jax.nn.softmax((x @ w_router).astype(jnp.float32), axis=-1)
    aff, idx = lax.top_k(probs.T, cap)
    xe = x[idx]
    hg = jnp.einsum('ecd,edf->ecf', xe, w_gate_e)
    hu = jnp.einsum('ecd,edf->ecf', xe, w_up_e)
    ye = jnp.einsum('ecf,efd->ecd', jax.nn.silu(hg) * hu, w_down_e) * aff[..., None].astype(x.dtype)
    return jnp.zeros_like(x).at[idx.reshape(-1)].add(ye.reshape(-1, D_MODEL))


def trunk_layer(x, mod, norm1_g, w_in, q_norm_g, k_norm_g, na_rpb, conv_w, w_proj_a, w_proj_b,
                w_proj_c, w_out, norm2_g, w_router, w_gate_e, w_up_e, w_down_e, ctx_kv=None):
    sh1, sc1, g1, sh2, sc2, g2 = mod
    b, s, _ = x.shape
    xn = rmsnorm(x, norm1_g) * (1 + sc1) + sh1
    qa, ka, va, qb, kb, vb, bc, cc, xc, gates = split_in(xn @ w_in)
    qa = rmsnorm(to_heads(qa, A_HEADS), q_norm_g)
    ka = rmsnorm(to_heads(ka, A_KV_HEADS), k_norm_g)
    va = to_heads(va, A_KV_HEADS)
    qb, kb, vb = to_heads(qb, B_HEADS), to_heads(kb, B_HEADS), to_heads(vb, B_HEADS)
    if ctx_kv is None:
        o_a = block_attention(qa.reshape(b, A_KV_HEADS, A_GROUP, s, HEAD_DIM), ka, va)
        o_b = block_attention(qb[:, :, None], kb, vb)
        new_kv = (ka, va, kb, vb)
    else:
        cka, cva, ckb, cvb = ctx_kv
        cos, sin = axial_rope(s)
        qa = apply_rope(qa, cos, sin)
        ka = apply_rope(ka, cos, sin)
        o_a = block_attention(qa.reshape(b, A_KV_HEADS, A_GROUP, s, HEAD_DIM),
                              jnp.concatenate([cka, ka], axis=2), jnp.concatenate([cva, va], axis=2))
        o_b = neighbourhood_attention(qb, kb, vb, ckb, cvb, na_rpb)
        new_kv = None
    o_c = short_conv(bc, cc, xc, conv_w)
    ga, gb, gc = jnp.split(jax.nn.sigmoid(gates.astype(jnp.float32)).astype(x.dtype), N_BRANCH, axis=-1)
    merged = (ga * (from_heads(o_a) @ w_proj_a) + gb * (from_heads(o_b) @ w_proj_b)
              + gc * (o_c @ w_proj_c))
    x = x + g1 * (merged @ w_out)
    xn2 = rmsnorm(x, norm2_g) * (1 + sc2) + sh2
    y = expert_choice_ffn(xn2.reshape(b * s, D_MODEL), w_router, w_gate_e, w_up_e, w_down_e)
    x = x + g2 * y.reshape(b, s, D_MODEL)
    return x, new_kv


def setup_inputs(seed: int = 0) -> dict:
    key = jax.random.key(seed)
    ks = jax.random.split(key, 26)

    def nrm(k, shape, scale):
        return jax.random.normal(k, shape, jnp.float32) * scale

    def gain(k, shape):
        return 1.0 + nrm(k, shape, 0.05)

    return {
        "x_prompt": nrm(ks[0], (BATCH, SEQ, D_MODEL), 1.0),
        "x_sample": nrm(ks[1], (DEC_BATCH, DEC_SEQ, D_MODEL), 1.0),
        "cache_attn_k": nrm(ks[2], (DEC_BATCH, DEPTH, A_KV_HEADS, PAST_LEN, HEAD_DIM), 1.0),
        "cache_attn_v": nrm(ks[3], (DEC_BATCH, DEPTH, A_KV_HEADS, PAST_LEN, HEAD_DIM), 1.0),
        "cache_na_k": nrm(ks[4], (DEC_BATCH, DEPTH, B_HEADS, PAST_LEN, HEAD_DIM), 1.0),
        "cache_na_v": nrm(ks[5], (DEC_BATCH, DEPTH, B_HEADS, PAST_LEN, HEAD_DIM), 1.0),
        "c": nrm(ks[6], (DEC_BATCH, D_MODEL), 1.0),
        "c_ctx": nrm(ks[7], (D_MODEL,), 1.0),
        "w_ada": nrm(ks[8], (DEPTH, D_MODEL, N_MOD * D_MODEL), 0.5 * D_MODEL ** -0.5),
        "b_ada": nrm(ks[9], (DEPTH, N_MOD * D_MODEL), 0.02),
        "norm1_g": gain(ks[10], (DEPTH, D_MODEL)),
        "w_in": nrm(ks[11], (DEPTH, D_MODEL, IN_COLS), D_MODEL ** -0.5),
        "q_norm_g": gain(ks[12], (DEPTH, HEAD_DIM)),
        "k_norm_g": gain(ks[13], (DEPTH, HEAD_DIM)),
        "na_rpb": nrm(ks[14], (DEPTH, B_HEADS, 2 * NA_WIN_R - 1, 2 * NA_WIN_C - 1), 0.5),
        "conv_w": nrm(ks[15], (DEPTH, CONV_W, C_WIDTH), CONV_W ** -0.5),
        "w_proj_a": nrm(ks[16], (DEPTH, A_Q, D_MODEL), A_Q ** -0.5),
        "w_proj_b": nrm(ks[17], (DEPTH, B_W, D_MODEL), B_W ** -0.5),
        "w_proj_c": nrm(ks[18], (DEPTH, C_WIDTH, D_MODEL), C_WIDTH ** -0.5),
        "w_out": nrm(ks[19], (DEPTH, D_MODEL, D_MODEL), D_MODEL ** -0.5),
        "norm2_g": gain(ks[20], (DEPTH, D_MODEL)),
        "w_router": nrm(ks[21], (DEPTH, D_MODEL, N_EXPERTS), D_MODEL ** -0.5),
        "w_gate_e": nrm(ks[22], (DEPTH, N_EXPERTS, D_MODEL, EXPERT_FF), D_MODEL ** -0.5),
        "w_up_e": nrm(ks[23], (DEPTH, N_EXPERTS, D_MODEL, EXPERT_FF), D_MODEL ** -0.5),
        "w_down_e": nrm(ks[24], (DEPTH, N_EXPERTS, EXPERT_FF, D_MODEL), EXPERT_FF ** -0.5),
        "final_g": gain(ks[25], (D_MODEL,)),
    }


def reference(x_prompt, x_sample, cache_attn_k, cache_attn_v, cache_na_k, cache_na_v, c, c_ctx,
              w_ada, b_ada, norm1_g, w_in, q_norm_g, k_norm_g, na_rpb, conv_w, w_proj_a, w_proj_b,
              w_proj_c, w_out, norm2_g, w_router, w_gate_e, w_up_e, w_down_e, final_g):
    hp, hs = x_prompt, x_sample
    new_ak, new_av, new_nk, new_nv = [], [], [], []
    for l in range(DEPTH):
        lw = (norm1_g[l], w_in[l], q_norm_g[l], k_norm_g[l], na_rpb[l], conv_w[l], w_proj_a[l],
              w_proj_b[l], w_proj_c[l], w_out[l], norm2_g[l], w_router[l], w_gate_e[l], w_up_e[l],
              w_down_e[l])
        mod_ctx = adaln(c_ctx[None], w_ada[l], b_ada[l])
        mod_lat = adaln(c, w_ada[l], b_ada[l])
        hp, (ka, va, kb, vb) = trunk_layer(hp, mod_ctx, *lw)
        new_ak.append(ka)
        new_av.append(va)
        new_nk.append(kb)
        new_nv.append(vb)
        ctx_kv = (cache_attn_k[:, l], cache_attn_v[:, l], cache_na_k[:, l], cache_na_v[:, l])
        hs, _ = trunk_layer(hs, mod_lat, *lw, ctx_kv=ctx_kv)
    y_prompt = rmsnorm(hp, final_g)
    y_sample = rmsnorm(hs, final_g)
    return (y_prompt, y_sample, jnp.stack(new_ak, axis=1), jnp.stack(new_av, axis=1),
            jnp.stack(new_nk, axis=1), jnp.stack(new_nv, axis=1))
```

```python
import functools

import jax
import jax.numpy as jnp
import numpy as np
from jax import lax
from jax.experimental import pallas as pl
from jax.experimental.pallas import tpu as pltpu

D_MODEL = 1024
BATCH = 32
SEQ = 256
DEPTH = 2
DEC_BATCH = 2
DEC_SEQ = 1024
PAST_LEN = 256
GRID_W = 64
HEAD_DIM = 64
A_HEADS = 8
A_KV_HEADS = 2
A_GROUP = A_HEADS // A_KV_HEADS
B_HEADS = 8
C_WIDTH = 512
NA_WIN_R = 8
NA_WIN_C = 16
N_EXPERTS = 16
EXPERT_FF = 1024
CAP_FACTOR = 2
ROPE_THETA = 10000.0
EPS = 1e-6
N_MOD = 6

A_Q = A_HEADS * HEAD_DIM
A_KV = A_KV_HEADS * HEAD_DIM
B_W = B_HEADS * HEAD_DIM
GATE_W = 3 * D_MODEL
COL_A = 0
W_A = A_Q + 2 * A_KV
COL_B = COL_A + W_A
W_B = 3 * B_W
COL_C = COL_B + W_B
W_C = 3 * C_WIDTH
COL_G = COL_C + W_C

N_CTX = BATCH * SEQ
N_LAT = DEC_BATCH * DEC_SEQ
CAP_CTX = CAP_FACTOR * N_CTX // N_EXPERTS
CAP_LAT = CAP_FACTOR * N_LAT // N_EXPERTS
CAP_ALL = CAP_CTX + CAP_LAT
LAT_ROWS = DEC_SEQ // GRID_W
NA_NK = NA_WIN_R * GRID_W
MASK_NEG = -1e30
ATT_SCALE = HEAD_DIM ** -0.5

F32 = jnp.float32
BF16 = jnp.bfloat16
_NT = (((1,), (1,)), ((), ()))

_MIB = 1024 * 1024


def _params(sem, vmem_mib):
    return pltpu.CompilerParams(dimension_semantics=sem, vmem_limit_bytes=vmem_mib * _MIB)


def _adaln_kernel(cond_ref, w_ref, b_ref, o_ref):
    c = cond_ref[...]
    s = c * jax.nn.sigmoid(c)
    o_ref[...] = jnp.dot(s.astype(BF16), w_ref[...].astype(BF16), preferred_element_type=F32) + b_ref[...]


def _adaln(cond8, w_ada, b_ada):
    tn = 1536
    ncol = N_MOD * D_MODEL
    return pl.pallas_call(
        _adaln_kernel,
        grid=(DEPTH, ncol // tn),
        in_specs=[
            pl.BlockSpec((8, D_MODEL), lambda l, j: (0, 0)),
            pl.BlockSpec((None, D_MODEL, tn), lambda l, j: (l, 0, j)),
            pl.BlockSpec((None, 1, tn), lambda l, j: (l, 0, j)),
        ],
        out_specs=pl.BlockSpec((None, 8, tn), lambda l, j: (l, 0, j)),
        out_shape=jax.ShapeDtypeStruct((DEPTH, 8, ncol), F32),
        compiler_params=_params(("arbitrary", "arbitrary"), 40),
        name="adaln",
    )(cond8, w_ada, b_ada.reshape(DEPTH, 1, ncol))


def _norm_mod_kernel(x_ref, g_ref, sc_ref, sh_ref, o_ref):
    x = x_ref[...]
    y = x * lax.rsqrt(jnp.mean(x * x, axis=-1, keepdims=True) + EPS) * g_ref[...]
    o_ref[...] = (y * (1.0 + sc_ref[...]) + sh_ref[...]).astype(o_ref.dtype)


def _norm_mod(x, g, sc, sh, seq, out_dtype):
    n = x.shape[0]
    tm = 512
    per_batch = sc.shape[0] > 1
    mod_map = (lambda i: ((i * tm) // seq, 0, 0)) if per_batch else (lambda i: (0, 0, 0))
    return pl.pallas_call(
        _norm_mod_kernel,
        grid=(n // tm,),
        in_specs=[
            pl.BlockSpec((tm, D_MODEL), lambda i: (i, 0)),
            pl.BlockSpec((1, D_MODEL), lambda i: (0, 0)),
            pl.BlockSpec((None, 1, D_MODEL), mod_map),
            pl.BlockSpec((None, 1, D_MODEL), mod_map),
        ],
        out_specs=pl.BlockSpec((tm, D_MODEL), lambda i: (i, 0)),
        out_shape=jax.ShapeDtypeStruct((n, D_MODEL), out_dtype),
        compiler_params=_params(("arbitrary",), 32),
        name="norm_mod",
    )(x, g, sc, sh)


def _norm_kernel(x_ref, g_ref, o_ref):
    x = x_ref[...]
    o_ref[...] = x * lax.rsqrt(jnp.mean(x * x, axis=-1, keepdims=True) + EPS) * g_ref[...]


def _final_norm(x, g):
    n = x.shape[0]
    tm = 512
    return pl.pallas_call(
        _norm_kernel,
        grid=(n // tm,),
        in_specs=[pl.BlockSpec((tm, D_MODEL), lambda i: (i, 0)), pl.BlockSpec((1, D_MODEL), lambda i: (0, 0))],
        out_specs=pl.BlockSpec((tm, D_MODEL), lambda i: (i, 0)),
        out_shape=jax.ShapeDtypeStruct((n, D_MODEL), F32),
        compiler_params=_params(("arbitrary",), 32),
        name="final_norm",
    )(x, g)


def _inproj_kernel(a_ref, w_ref, o_ref, wbf_ref):
    @pl.when(pl.program_id(1) == 0)
    def _():
        wbf_ref[...] = w_ref[...].astype(BF16)

    o_ref[...] = jnp.dot(a_ref[...], wbf_ref[...], preferred_element_type=F32).astype(o_ref.dtype)


def _inproj(xn, w_in, layer, col0, width, tn, out_dtype):
    n = xn.shape[0]
    tm = 1024
    joff = col0 // tn
    assert col0 % tn == 0 and width % tn == 0
    return pl.pallas_call(
        _inproj_kernel,
        grid=(width // tn, n // tm),
        in_specs=[
            pl.BlockSpec((tm, D_MODEL), lambda j, i: (i, 0)),
            pl.BlockSpec((None, D_MODEL, tn), lambda j, i: (layer, 0, j + joff)),
        ],
        out_specs=pl.BlockSpec((tm, tn), lambda j, i: (i, j)),
        out_shape=jax.ShapeDtypeStruct((n, width), out_dtype),
        scratch_shapes=[pltpu.VMEM((D_MODEL, tn), BF16)],
        compiler_params=_params(("arbitrary", "arbitrary"), 48),
        name="inproj",
    )(xn, w_in)


def _rms_head(x, g):
    return x * lax.rsqrt(jnp.mean(x * x, axis=-1, keepdims=True) + EPS) * g


def _softmax_pv(scores, values):
    m = scores[0].max(axis=-1, keepdims=True)
    for s in scores[1:]:
        m = jnp.maximum(m, s.max(axis=-1, keepdims=True))
    acc = None
    den = None
    for s, v in zip(scores, values):
        p = jnp.exp(s - m)
        d = p.sum(axis=-1, keepdims=True)
        o = jnp.dot(p.astype(BF16), v, preferred_element_type=F32)
        acc = o if acc is None else acc + o
        den = d if den is None else den + d
    return acc / den


def _qk(q, k):
    return lax.dot_general(q, k, _NT, preferred_element_type=F32)


def _attn_ctx_kernel(ha_ref, hb_ref, qg_ref, kg_ref, oab_ref, ak_ref, av_ref, nk_ref, nv_ref):
    qg = qg_ref[...]
    kg = kg_ref[...]
    for kv in range(A_KV_HEADS):
        k = _rms_head(ha_ref[:, A_Q + kv * HEAD_DIM:A_Q + (kv + 1) * HEAD_DIM], kg)
        v = ha_ref[:, A_Q + A_KV + kv * HEAD_DIM:A_Q + A_KV + (kv + 1) * HEAD_DIM]
        ak_ref[kv] = k
        av_ref[kv] = v
        kb = k.astype(BF16)
        vb = v.astype(BF16)
        for g in range(A_GROUP):
            hq = kv * A_GROUP + g
            q = _rms_head(ha_ref[:, hq * HEAD_DIM:(hq + 1) * HEAD_DIM], qg) * ATT_SCALE
            o = _softmax_pv([_qk(q.astype(BF16), kb)], [vb])
            oab_ref[:, hq * HEAD_DIM:(hq + 1) * HEAD_DIM] = o.astype(BF16)
    for h in range(B_HEADS):
        q = hb_ref[:, h * HEAD_DIM:(h + 1) * HEAD_DIM] * ATT_SCALE
        k = hb_ref[:, B_W + h * HEAD_DIM:B_W + (h + 1) * HEAD_DIM]
        v = hb_ref[:, 2 * B_W + h * HEAD_DIM:2 * B_W + (h + 1) * HEAD_DIM]
        nk_ref[h] = k
        nv_ref[h] = v
        o = _softmax_pv([_qk(q.astype(BF16), k.astype(BF16))], [v.astype(BF16)])
        oab_ref[:, A_Q + h * HEAD_DIM:A_Q + (h + 1) * HEAD_DIM] = o.astype(BF16)


def _attn_ctx(h_a, h_b, qg, kg):
    kv_spec = lambda nh: pl.BlockSpec((None, nh, SEQ, HEAD_DIM), lambda b: (b, 0, 0, 0))
    kv_shape = lambda nh: jax.ShapeDtypeStruct((BATCH, nh, SEQ, HEAD_DIM), F32)
    return pl.pallas_call(
        _attn_ctx_kernel,
        grid=(BATCH,),
        in_specs=[
            pl.BlockSpec((SEQ, W_A), lambda b: (b, 0)),
            pl.BlockSpec((SEQ, W_B), lambda b: (b, 0)),
            pl.BlockSpec((1, HEAD_DIM), lambda b: (0, 0)),
            pl.BlockSpec((1, HEAD_DIM), lambda b: (0, 0)),
        ],
        out_specs=[
            pl.BlockSpec((SEQ, A_Q + B_W), lambda b: (b, 0)),
            kv_spec(A_KV_HEADS), kv_spec(A_KV_HEADS), kv_spec(B_HEADS), kv_spec(B_HEADS),
        ],
        out_shape=[
            jax.ShapeDtypeStruct((N_CTX, A_Q + B_W), BF16),
            kv_shape(A_KV_HEADS), kv_shape(A_KV_HEADS), kv_shape(B_HEADS), kv_shape(B_HEADS),
        ],
        compiler_params=_params(("arbitrary",), 32),
        name="attn_ctx",
    )(h_a, h_b, qg, kg)


LAT_TQ = 256
LAT_QROWS = LAT_TQ // GRID_W


def _rope(x, cos, sin):
    lane = lax.broadcasted_iota(jnp.int32, x.shape, 1)
    nxt = pltpu.roll(x, x.shape[1] - 1, axis=1)
    prv = pltpu.roll(x, 1, axis=1)
    partner = jnp.where((lane & 1) == 0, nxt, prv)
    return x * cos + partner * sin


def _attn_lat_kernel(haq_ref, haf_ref, hbq_ref, hbf_ref, cak_ref, cav_ref, cnk_ref, cnv_ref,
                     qg_ref, kg_ref, cos_ref, sin_ref, bias_ref, oab_ref, qn_ref, kn_ref):
    t = pl.program_id(1)
    row0 = pl.multiple_of(t * LAT_TQ, LAT_TQ)
    qg = qg_ref[...]
    kg = kg_ref[...]

    for hq in range(A_HEADS):
        qn_ref[:, hq * HEAD_DIM:(hq + 1) * HEAD_DIM] = _rms_head(
            haq_ref[:, hq * HEAD_DIM:(hq + 1) * HEAD_DIM], qg) * ATT_SCALE
    for kv in range(A_KV_HEADS):
        kn_ref[:, kv * HEAD_DIM:(kv + 1) * HEAD_DIM] = _rms_head(
            haf_ref[:, A_Q + kv * HEAD_DIM:A_Q + (kv + 1) * HEAD_DIM], kg)
    cos_q = cos_ref[pl.ds(row0, LAT_TQ), :]
    sin_q = sin_ref[pl.ds(row0, LAT_TQ), :]
    for c in range(A_Q // 128):
        qn_ref[:, c * 128:(c + 1) * 128] = _rope(qn_ref[:, c * 128:(c + 1) * 128], cos_q, sin_q)
    kn_ref[...] = _rope(kn_ref[...], cos_ref[...], sin_ref[...])

    for kv in range(A_KV_HEADS):
        k_lat = kn_ref[:, kv * HEAD_DIM:(kv + 1) * HEAD_DIM].astype(BF16)
        v_lat = haf_ref[:, A_Q + A_KV + kv * HEAD_DIM:A_Q + A_KV + (kv + 1) * HEAD_DIM].astype(BF16)
        k_ctx = cak_ref[kv].astype(BF16)
        v_ctx = cav_ref[kv].astype(BF16)
        for g in range(A_GROUP):
            hq = kv * A_GROUP + g
            q = qn_ref[:, hq * HEAD_DIM:(hq + 1) * HEAD_DIM].astype(BF16)
            o = _softmax_pv([_qk(q, k_ctx), _qk(q, k_lat)], [v_ctx, v_lat])
            oab_ref[:, hq * HEAD_DIM:(hq + 1) * HEAD_DIM] = o.astype(BF16)

    for h in range(B_HEADS):
        k_ctx = cnk_ref[h].astype(BF16)
        v_ctx = cnv_ref[h].astype(BF16)
        for i in range(LAT_QROWS):
            r = t * LAT_QROWS + i
            r0 = jnp.clip(r - NA_WIN_R // 2, 0, LAT_ROWS - NA_WIN_R)
            k0 = pl.multiple_of(r0 * GRID_W, GRID_W)
            q = (hbq_ref[i * GRID_W:(i + 1) * GRID_W, h * HEAD_DIM:(h + 1) * HEAD_DIM] * ATT_SCALE).astype(BF16)
            k_win = hbf_ref[pl.ds(k0, NA_NK), B_W + h * HEAD_DIM:B_W + (h + 1) * HEAD_DIM].astype(BF16)
            v_win = hbf_ref[pl.ds(k0, NA_NK), 2 * B_W + h * HEAD_DIM:2 * B_W + (h + 1) * HEAD_DIM].astype(BF16)
            s_nb = _qk(q, k_win) + bias_ref[h, i]
            o = _softmax_pv([_qk(q, k_ctx), s_nb], [v_ctx, v_win])
            oab_ref[i * GRID_W:(i + 1) * GRID_W, A_Q + h * HEAD_DIM:A_Q + (h + 1) * HEAD_DIM] = o.astype(BF16)


def _attn_lat(h_a, h_b, cak, cav, cnk, cnv, layer, qg, kg, cos128, sin128, bias):
    nt = DEC_SEQ // LAT_TQ
    cache_spec = lambda nh: pl.BlockSpec((None, None, nh, PAST_LEN, HEAD_DIM), lambda b, t: (b, layer, 0, 0, 0))
    return pl.pallas_call(
        _attn_lat_kernel,
        grid=(DEC_BATCH, nt),
        in_specs=[
            pl.BlockSpec((LAT_TQ, W_A), lambda b, t: (b * nt + t, 0)),
            pl.BlockSpec((DEC_SEQ, W_A), lambda b, t: (b, 0)),
            pl.BlockSpec((LAT_TQ, W_B), lambda b, t: (b * nt + t, 0)),
            pl.BlockSpec((DEC_SEQ, W_B), lambda b, t: (b, 0)),
            cache_spec(A_KV_HEADS), cache_spec(A_KV_HEADS), cache_spec(B_HEADS), cache_spec(B_HEADS),
            pl.BlockSpec((1, HEAD_DIM), lambda b, t: (0, 0)),
            pl.BlockSpec((1, HEAD_DIM), lambda b, t: (0, 0)),
            pl.BlockSpec((DEC_SEQ, 128), lambda b, t: (0, 0)),
            pl.BlockSpec((DEC_SEQ, 128), lambda b, t: (0, 0)),
            pl.BlockSpec((B_HEADS, LAT_QROWS, GRID_W, NA_NK), lambda b, t: (0, t, 0, 0)),
        ],
        out_specs=pl.BlockSpec((LAT_TQ, A_Q + B_W), lambda b, t: (b * nt + t, 0)),
        out_shape=jax.ShapeDtypeStruct((N_LAT, A_Q + B_W), BF16),
        scratch_shapes=[pltpu.VMEM((LAT_TQ, A_Q), F32), pltpu.VMEM((DEC_SEQ, A_KV), F32)],
        compiler_params=_params(("arbitrary", "arbitrary"), 60),
        name="attn_lat",
    )(h_a, h_a, h_b, h_b, cak, cav, cnk, cnv, qg, kg, cos128, sin128, bias)


def _rope_tables():
    t = jnp.arange(DEC_SEQ)
    row = (t // GRID_W).astype(F32)
    col = (t % GRID_W).astype(F32)
    n_freq = HEAD_DIM // 4
    inv = ROPE_THETA ** (-jnp.arange(n_freq, dtype=F32) / n_freq)
    ang = jnp.concatenate([row[:, None] * inv, col[:, None] * inv], axis=-1)
    cos = jnp.repeat(jnp.cos(ang), 2, axis=-1)
    sign = jnp.tile(jnp.array([-1.0, 1.0], F32), HEAD_DIM // 2)
    sin = jnp.repeat(jnp.sin(ang), 2, axis=-1) * sign
    return jnp.tile(cos, (1, 2)), jnp.tile(sin, (1, 2))


def _na_bias(rpb):
    rows = LAT_ROWS
    wr = NA_WIN_R
    r = np.arange(rows)
    r0 = np.clip(r - wr // 2, 0, rows - wr)
    krow = r0[:, None] + np.arange(wr)[None, :]
    col = np.arange(GRID_W)
    c0 = np.clip(col - NA_WIN_C // 2, 0, GRID_W - NA_WIN_C)
    key_col = np.tile(col, wr)
    key_row = np.repeat(krow, GRID_W, axis=1)
    dr = key_row - r[:, None]
    dc = key_col[None, :] - col[:, None]
    in_win = (key_col[None, :] >= c0[:, None]) & (key_col[None, :] < c0[:, None] + NA_WIN_C)
    bias = rpb[:, (dr + NA_WIN_R - 1)[:, None, :], np.clip(dc + NA_WIN_C - 1, 0, 2 * NA_WIN_C - 2)[None, :, :]]
    return jnp.where(in_win[None, None], bias, MASK_NEG)


def _conv_kernel(hc_ref, w_ref, o_ref, *, seq):
    bg = hc_ref[:, 0:C_WIDTH]
    u = hc_ref[:, C_WIDTH:2 * C_WIDTH] * hc_ref[:, 2 * C_WIDTH:3 * C_WIDTH]
    rows = u.shape[0]
    assert seq & (seq - 1) == 0
    pos = lax.broadcasted_iota(jnp.int32, u.shape, 0) & (seq - 1)
    u_prev = jnp.where(pos == 0, 0.0, pltpu.roll(u, 1, axis=0))
    u_next = jnp.where(pos == seq - 1, 0.0, pltpu.roll(u, rows - 1, axis=0))
    y = u_prev * w_ref[0:1, :] + u * w_ref[1:2, :] + u_next * w_ref[2:3, :]
    o_ref[...] = (bg * y).astype(o_ref.dtype)


def _conv(h_c, conv_w, seq):
    n = h_c.shape[0]
    tm = 1024
    return pl.pallas_call(
        functools.partial(_conv_kernel, seq=seq),
        grid=(n // tm,),
        in_specs=[pl.BlockSpec((tm, W_C), lambda i: (i, 0)), pl.BlockSpec((3, C_WIDTH), lambda i: (0, 0))],
        out_specs=pl.BlockSpec((tm, C_WIDTH), lambda i: (i, 0)),
        out_shape=jax.ShapeDtypeStruct((n, C_WIDTH), BF16),
        compiler_params=_params(("arbitrary",), 40),
        name="short_conv",
    )(h_c, conv_w)


def _mix_kernel(x_ref, oab_ref, oc_ref, hg_ref, wa_ref, wb_ref, wc_ref, wo_ref, g1_ref, n2g_ref, sc2_ref, sh2_ref,
                wrt_ref, xo_ref, xn2_ref, pt_ref):
    ga = jax.nn.sigmoid(hg_ref[:, 0:D_MODEL])
    gb = jax.nn.sigmoid(hg_ref[:, D_MODEL:2 * D_MODEL])
    gc = jax.nn.sigmoid(hg_ref[:, 2 * D_MODEL:3 * D_MODEL])
    pa = jnp.dot(oab_ref[:, 0:A_Q], wa_ref[...], preferred_element_type=F32)
    pb = jnp.dot(oab_ref[:, A_Q:A_Q + B_W], wb_ref[...], preferred_element_type=F32)
    pc = jnp.dot(oc_ref[...], wc_ref[...], preferred_element_type=F32)
    merged = ga * pa + gb * pb + gc * pc
    xo = x_ref[...] + g1_ref[...] * jnp.dot(merged.astype(BF16), wo_ref[...], preferred_element_type=F32)
    xo_ref[...] = xo
    y = xo * lax.rsqrt(jnp.mean(xo * xo, axis=-1, keepdims=True) + EPS) * n2g_ref[...]
    xn2 = y * (1.0 + sc2_ref[...]) + sh2_ref[...]
    xn2_ref[...] = xn2
    logits_t = lax.dot_general(wrt_ref[...], xn2.astype(BF16), _NT, preferred_element_type=F32)
    z = jnp.exp(logits_t - logits_t.max(axis=0, keepdims=True))
    pt_ref[...] = z / z.sum(axis=0, keepdims=True)


def _mix(x, oab, oc, h_g, wa, wb, wc, wo, g1, n2g, sc2, sh2, wrt, seq):
    n = x.shape[0]
    tm = 512
    per_batch = g1.shape[0] > 1
    mod_map = (lambda i: ((i * tm) // seq, 0, 0)) if per_batch else (lambda i: (0, 0, 0))
    mod_spec = pl.BlockSpec((None, 1, D_MODEL), mod_map)
    full = lambda a: pl.BlockSpec(a.shape, lambda i: (0,) * a.ndim)
    row = lambda w: pl.BlockSpec((tm, w), lambda i: (i, 0))
    return pl.pallas_call(
        _mix_kernel,
        grid=(n // tm,),
        in_specs=[row(D_MODEL), row(A_Q + B_W), row(C_WIDTH), row(GATE_W), full(wa), full(wb), full(wc), full(wo),
                  mod_spec, full(n2g), mod_spec, mod_spec, full(wrt)],
        out_specs=[row(D_MODEL), row(D_MODEL), pl.BlockSpec((N_EXPERTS, tm), lambda i: (0, i))],
        out_shape=[jax.ShapeDtypeStruct((n, D_MODEL), F32), jax.ShapeDtypeStruct((n, D_MODEL), F32),
                   jax.ShapeDtypeStruct((N_EXPERTS, n), F32)],
        compiler_params=_params(("arbitrary",), 56),
        name="mix",
    )(x, oab, oc, h_g, wa, wb, wc, wo, g1, n2g, sc2, sh2, wrt)


MOE_TF = 512
MOE_ROW_CHUNKS = ((0, 512), (512, 512), (1024, 256))


def _moe_kernel(idx_ref, xc_hbm, xl_hbm, aff_ref, wg_ref, wu_ref, wd_ref, ye_ref, xe_ref, sem):
    e = pl.program_id(0)
    f = pl.program_id(1)

    @pl.when(f == 0)
    def _():
        base = e * CAP_ALL

        def issue_ctx(r, carry):
            tok = idx_ref[base + r]
            pltpu.make_async_copy(xc_hbm.at[pl.ds(tok, 1)], xe_ref.at[pl.ds(r, 1)], sem).start()
            return carry

        def issue_lat(r, carry):
            tok = idx_ref[base + CAP_CTX + r]
            pltpu.make_async_copy(xl_hbm.at[pl.ds(tok, 1)], xe_ref.at[pl.ds(CAP_CTX + r, 1)], sem).start()
            return carry

        lax.fori_loop(0, CAP_CTX, issue_ctx, 0)
        lax.fori_loop(0, CAP_LAT, issue_lat, 0)
        pltpu.make_async_copy(xc_hbm.at[pl.ds(0, CAP_ALL)], xe_ref, sem).wait()

    wg = wg_ref[...].astype(BF16)
    wu = wu_ref[...].astype(BF16)
    wd = wd_ref[...].astype(BF16)
    for r0, rn in MOE_ROW_CHUNKS:
        xe = xe_ref[r0:r0 + rn, :].astype(BF16)
        hg = jnp.dot(xe, wg, preferred_element_type=F32)
        hu = jnp.dot(xe, wu, preferred_element_type=F32)
        act = (hg * jax.nn.sigmoid(hg) * hu).astype(BF16)
        part = jnp.dot(act, wd, preferred_element_type=F32) * aff_ref[r0:r0 + rn, :]

        @pl.when(f == 0)
        def _():
            ye_ref[r0:r0 + rn, :] = part

        @pl.when(f != 0)
        def _():
            ye_ref[r0:r0 + rn, :] += part


def _moe(idx_all, xn2_ctx, xn2_lat, aff_all, w_gate_e, w_up_e, w_down_e, layer):
    nf = EXPERT_FF // MOE_TF
    grid_spec = pltpu.PrefetchScalarGridSpec(
        num_scalar_prefetch=1,
        grid=(N_EXPERTS, nf),
        in_specs=[
            pl.BlockSpec(memory_space=pl.ANY),
            pl.BlockSpec(memory_space=pl.ANY),
            pl.BlockSpec((None, CAP_ALL, 1), lambda e, f, idx: (e, 0, 0)),
            pl.BlockSpec((None, None, D_MODEL, MOE_TF), lambda e, f, idx: (layer, e, 0, f)),
            pl.BlockSpec((None, None, D_MODEL, MOE_TF), lambda e, f, idx: (layer, e, 0, f)),
            pl.BlockSpec((None, None, MOE_TF, D_MODEL), lambda e, f, idx: (layer, e, f, 0)),
        ],
        out_specs=pl.BlockSpec((None, CAP_ALL, D_MODEL), lambda e, f, idx: (e, 0, 0)),
        scratch_shapes=[pltpu.VMEM((CAP_ALL, D_MODEL), F32), pltpu.SemaphoreType.DMA],
    )
    return pl.pallas_call(
        _moe_kernel,
        grid_spec=grid_spec,
        out_shape=jax.ShapeDtypeStruct((N_EXPERTS, CAP_ALL, D_MODEL), F32),
        compiler_params=_params(("arbitrary", "arbitrary"), 56),
        name="moe_experts",
    )(idx_all, xn2_ctx, xn2_lat, aff_all, w_gate_e, w_up_e, w_down_e)


def _route(probs_t, cap):
    aff, idx = lax.top_k(probs_t, cap)
    order = jnp.argsort(idx, axis=-1)
    return jnp.take_along_axis(idx, order, axis=-1), jnp.take_along_axis(aff, order, axis=-1)


def kernel(x_prompt, x_sample, cache_attn_k, cache_attn_v, cache_na_k, cache_na_v, c, c_ctx, w_ada, b_ada, norm1_g,
           w_in, q_norm_g, k_norm_g, na_rpb, conv_w, w_proj_a, w_proj_b, w_proj_c, w_out, norm2_g, w_router,
           w_gate_e, w_up_e, w_down_e, final_g):
    cond8 = jnp.zeros((8, D_MODEL), F32).at[0].set(c_ctx).at[1:1 + DEC_BATCH].set(c)
    mods = _adaln(cond8, w_ada, b_ada).reshape(DEPTH, 8, N_MOD, D_MODEL)
    cos128, sin128 = _rope_tables()

    hp = x_prompt.reshape(N_CTX, D_MODEL)
    hs = x_sample.reshape(N_LAT, D_MODEL)
    new_ak, new_av, new_nk, new_nv = [], [], [], []
    for l in range(DEPTH):
        mod_ctx = [mods[l, 0:1, i][:, None, :] for i in range(N_MOD)]
        mod_lat = [mods[l, 1:1 + DEC_BATCH, i][:, None, :] for i in range(N_MOD)]
        n1g = norm1_g[l][None]
        n2g = norm2_g[l][None]
        qg = q_norm_g[l][None]
        kg = k_norm_g[l][None]
        wa, wb, wc, wo = (w_proj_a[l].astype(BF16), w_proj_b[l].astype(BF16), w_proj_c[l].astype(BF16),
                          w_out[l].astype(BF16))
        wrt = w_router[l].T.astype(BF16)
        bias = _na_bias(na_rpb[l])

        def sublayer1(x, mod, seq, is_ctx):
            sh1, sc1, g1, sh2, sc2, _ = mod
            xn = _norm_mod(x, n1g, sc1, sh1, seq, BF16)
            h_a = _inproj(xn, w_in, l, COL_A, W_A, 768, F32)
            h_b = _inproj(xn, w_in, l, COL_B, W_B, 768, F32)
            h_c = _inproj(xn, w_in, l, COL_C, W_C, 768, F32)
            h_g = _inproj(xn, w_in, l, COL_G, GATE_W, 768, F32)
            kv = None
            if is_ctx:
                oab, ka, va, kb, vb = _attn_ctx(h_a, h_b, qg, kg)
                kv = (ka, va, kb, vb)
            else:
                oab = _attn_lat(h_a, h_b, cache_attn_k, cache_attn_v, cache_na_k, cache_na_v, l, qg, kg,
                                cos128, sin128, bias)
            oc = _conv(h_c, conv_w[l], seq)
            xo, xn2, probs_t = _mix(x, oab, oc, h_g, wa, wb, wc, wo, g1, n2g, sc2, sh2, wrt, seq)
            return xo, xn2, probs_t, kv

        hp, xn2_c, pt_c, (ka, va, kb, vb) = sublayer1(hp, mod_ctx, SEQ, True)
        hs, xn2_l, pt_l, _ = sublayer1(hs, mod_lat, DEC_SEQ, False)
        new_ak.append(ka)
        new_av.append(va)
        new_nk.append(kb)
        new_nv.append(vb)

        idx_c, aff_c = _route(pt_c, CAP_CTX)
        idx_l, aff_l = _route(pt_l, CAP_LAT)
        idx_all = jnp.concatenate([idx_c, idx_l], axis=1).reshape(-1).astype(jnp.int32)
        aff_all = jnp.concatenate([aff_c, aff_l], axis=1)[..., None]
        ye = _moe(idx_all, xn2_c, xn2_l, aff_all, w_gate_e, w_up_e, w_down_e, l)
        y_c = jnp.zeros((N_CTX, D_MODEL), F32).at[idx_c.reshape(-1)].add(ye[:, :CAP_CTX].reshape(-1, D_MODEL))
        y_l = jnp.zeros((N_LAT, D_MODEL), F32).at[idx_l.reshape(-1)].add(ye[:, CAP_CTX:].reshape(-1, D_MODEL))
        hp = hp + (mod_ctx[5] * y_c.reshape(1, N_CTX, D_MODEL)).reshape(N_CTX, D_MODEL)
        hs = hs + (mod_lat[5] * y_l.reshape(DEC_BATCH, DEC_SEQ, D_MODEL)).reshape(N_LAT, D_MODEL)

    fg = final_g[None]
    y_prompt = _final_norm(hp, fg).reshape(BATCH, SEQ, D_MODEL)
    y_sample = _final_norm(hs, fg).reshape(DEC_BATCH, DEC_SEQ, D_MODEL)
    return (y_prompt, y_sample, jnp.stack(new_ak, axis=1), jnp.stack(new_av, axis=1),
            jnp.stack(new_nk, axis=1), jnp.stack(new_nv, axis=1))
```

```python
import functools

import jax
import jax.numpy as jnp
import numpy as np
from jax import lax
from jax.experimental import pallas as pl
from jax.experimental.pallas import tpu as pltpu

D_MODEL = 1024
BATCH = 32
SEQ = 256
DEPTH = 2
DEC_BATCH = 2
DEC_SEQ = 1024
PAST_LEN = 256
GRID_W = 64
HEAD_DIM = 64
A_HEADS = 8
A_KV_HEADS = 2
A_GROUP = A_HEADS // A_KV_HEADS
B_HEADS = 8
C_WIDTH = 512
NA_WIN_R = 8
NA_WIN_C = 16
N_EXPERTS = 16
EXPERT_FF = 1024
CAP_FACTOR = 2
ROPE_THETA = 10000.0
EPS = 1e-6
N_MOD = 6

A_Q = A_HEADS * HEAD_DIM
A_KV = A_KV_HEADS * HEAD_DIM
B_W = B_HEADS * HEAD_DIM
GATE_W = 3 * D_MODEL
COL_A = 0
W_A = A_Q + 2 * A_KV
COL_B = COL_A + W_A
W_B = 3 * B_W
COL_C = COL_B + W_B
W_C = 3 * C_WIDTH
COL_G = COL_C + W_C

N_CTX = BATCH * SEQ
N_LAT = DEC_BATCH * DEC_SEQ
CAP_CTX = CAP_FACTOR * N_CTX // N_EXPERTS
CAP_LAT = CAP_FACTOR * N_LAT // N_EXPERTS
CAP_ALL = CAP_CTX + CAP_LAT
LAT_ROWS = DEC_SEQ // GRID_W
NA_NK = NA_WIN_R * GRID_W
MASK_NEG = -1e30
ATT_SCALE = HEAD_DIM ** -0.5

F32 = jnp.float32
BF16 = jnp.bfloat16
_NT = (((1,), (1,)), ((), ()))

_MIB = 1024 * 1024


def _params(sem, vmem_mib):
    return pltpu.CompilerParams(dimension_semantics=sem, vmem_limit_bytes=vmem_mib * _MIB)


def _adaln_kernel(cond_ref, w_ref, b_ref, o_ref):
    c = cond_ref[...]
    s = c * jax.nn.sigmoid(c)
    o_ref[...] = jnp.dot(s.astype(BF16), w_ref[...].astype(BF16), preferred_element_type=F32) + b_ref[...]


def _adaln(cond8, w_ada, b_ada):
    tn = 1536
    ncol = N_MOD * D_MODEL
    return pl.pallas_call(
        _adaln_kernel,
        grid=(DEPTH, ncol // tn),
        in_specs=[
            pl.BlockSpec((8, D_MODEL), lambda l, j: (0, 0)),
            pl.BlockSpec((None, D_MODEL, tn), lambda l, j: (l, 0, j)),
            pl.BlockSpec((None, 1, tn), lambda l, j: (l, 0, j)),
        ],
        out_specs=pl.BlockSpec((None, 8, tn), lambda l, j: (l, 0, j)),
        out_shape=jax.ShapeDtypeStruct((DEPTH, 8, ncol), F32),
        compiler_params=_params(("arbitrary", "arbitrary"), 40),
        name="adaln",
    )(cond8, w_ada, b_ada.reshape(DEPTH, 1, ncol))


def _norm_mod_kernel(x_ref, g_ref, sc_ref, sh_ref, o_ref):
    x = x_ref[...]
    y = x * lax.rsqrt(jnp.mean(x * x, axis=-1, keepdims=True) + EPS) * g_ref[...]
    o_ref[...] = (y * (1.0 + sc_ref[...]) + sh_ref[...]).astype(o_ref.dtype)


def _norm_mod(x, g, sc, sh, seq, out_dtype):
    n = x.shape[0]
    tm = 512
    per_batch = sc.shape[0] > 1
    mod_map = (lambda i: ((i * tm) // seq, 0, 0)) if per_batch else (lambda i: (0, 0, 0))
    return pl.pallas_call(
        _norm_mod_kernel,
        grid=(n // tm,),
        in_specs=[
            pl.BlockSpec((tm, D_MODEL), lambda i: (i, 0)),
            pl.BlockSpec((1, D_MODEL), lambda i: (0, 0)),
            pl.BlockSpec((None, 1, D_MODEL), mod_map),
            pl.BlockSpec((None, 1, D_MODEL), mod_map),
        ],
        out_specs=pl.BlockSpec((tm, D_MODEL), lambda i: (i, 0)),
        out_shape=jax.ShapeDtypeStruct((n, D_MODEL), out_dtype),
        compiler_params=_params(("arbitrary",), 32),
        name="norm_mod",
    )(x, g, sc, sh)


def _norm_kernel(x_ref, g_ref, o_ref):
    x = x_ref[...]
    o_ref[...] = x * lax.rsqrt(jnp.mean(x * x, axis=-1, keepdims=True) + EPS) * g_ref[...]


def _final_norm(x, g):
    n = x.shape[0]
    tm = 512
    return pl.pallas_call(
        _norm_kernel,
        grid=(n // tm,),
        in_specs=[pl.BlockSpec((tm, D_MODEL), lambda i: (i, 0)), pl.BlockSpec((1, D_MODEL), lambda i: (0, 0))],
        out_specs=pl.BlockSpec((tm, D_MODEL), lambda i: (i, 0)),
        out_shape=jax.ShapeDtypeStruct((n, D_MODEL), F32),
        compiler_params=_params(("arbitrary",), 32),
        name="final_norm",
    )(x, g)


def _inproj_kernel(a_ref, w_ref, o_ref, wbf_ref):
    @pl.when(pl.program_id(1) == 0)
    def _():
        wbf_ref[...] = w_ref[...].astype(BF16)

    o_ref[...] = jnp.dot(a_ref[...], wbf_ref[...], preferred_element_type=F32).astype(o_ref.dtype)


def _inproj(xn, w_in, layer, col0, width, tn, out_dtype):
    n = xn.shape[0]
    tm = 1024
    joff = col0 // tn
    assert col0 % tn == 0 and width % tn == 0
    return pl.pallas_call(
        _inproj_kernel,
        grid=(width // tn, n // tm),
        in_specs=[
            pl.BlockSpec((tm, D_MODEL), lambda j, i: (i, 0)),
            pl.BlockSpec((None, D_MODEL, tn), lambda j, i: (layer, 0, j + joff)),
        ],
        out_specs=pl.BlockSpec((tm, tn), lambda j, i: (i, j)),
        out_shape=jax.ShapeDtypeStruct((n, width), out_dtype),
        scratch_shapes=[pltpu.VMEM((D_MODEL, tn), BF16)],
        compiler_params=_params(("arbitrary", "arbitrary"), 48),
        name="inproj",
    )(xn, w_in)


def _rms_head(x, g):
    return x * lax.rsqrt(jnp.mean(x * x, axis=-1, keepdims=True) + EPS) * g


def _softmax_pv(scores, values):
    m = scores[0].max(axis=-1, keepdims=True)
    for s in scores[1:]:
        m = jnp.maximum(m, s.max(axis=-1, keepdims=True))
    acc = None
    den = None
    for s, v in zip(scores, values):
        p = jnp.exp(s - m)
        d = p.sum(axis=-1, keepdims=True)
        o = jnp.dot(p.astype(BF16), v, preferred_element_type=F32)
        acc = o if acc is None else acc + o
        den = d if den is None else den + d
    return acc / den


def _qk(q, k):
    return lax.dot_general(q, k, _NT, preferred_element_type=F32)


def _attn_ctx_kernel(ha_ref, hb_ref, qg_ref, kg_ref, oab_ref, ak_ref, av_ref, nk_ref, nv_ref):
    qg = qg_ref[...]
    kg = kg_ref[...]
    for kv in range(A_KV_HEADS):
        k = _rms_head(ha_ref[:, A_Q + kv * HEAD_DIM:A_Q + (kv + 1) * HEAD_DIM], kg)
        v = ha_ref[:, A_Q + A_KV + kv * HEAD_DIM:A_Q + A_KV + (kv + 1) * HEAD_DIM]
        ak_ref[kv] = k
        av_ref[kv] = v
        kb = k.astype(BF16)
        vb = v.astype(BF16)
        for g in range(A_GROUP):
            hq = kv * A_GROUP + g
            q = _rms_head(ha_ref[:, hq * HEAD_DIM:(hq + 1) * HEAD_DIM], qg) * ATT_SCALE
            o = _softmax_pv([_qk(q.astype(BF16), kb)], [vb])
            oab_ref[:, hq * HEAD_DIM:(hq + 1) * HEAD_DIM] = o.astype(BF16)
    for h in range(B_HEADS):
        q = hb_ref[:, h * HEAD_DIM:(h + 1) * HEAD_DIM] * ATT_SCALE
        k = hb_ref[:, B_W + h * HEAD_DIM:B_W + (h + 1) * HEAD_DIM]
        v = hb_ref[:, 2 * B_W + h * HEAD_DIM:2 * B_W + (h + 1) * HEAD_DIM]
        nk_ref[h] = k
        nv_ref[h] = v
        o = _softmax_pv([_qk(q.astype(BF16), k.astype(BF16))], [v.astype(BF16)])
        oab_ref[:, A_Q + h * HEAD_DIM:A_Q + (h + 1) * HEAD_DIM] = o.astype(BF16)


def _attn_ctx(h_a, h_b, qg, kg):
    kv_spec = lambda nh: pl.BlockSpec((None, nh, SEQ, HEAD_DIM), lambda b: (b, 0, 0, 0))
    kv_shape = lambda nh: jax.ShapeDtypeStruct((BATCH, nh, SEQ, HEAD_DIM), F32)
    return pl.pallas_call(
        _attn_ctx_kernel,
        grid=(BATCH,),
        in_specs=[
            pl.BlockSpec((SEQ, W_A), lambda b: (b, 0)),
            pl.BlockSpec((SEQ, W_B), lambda b: (b, 0)),
            pl.BlockSpec((1, HEAD_DIM), lambda b: (0, 0)),
            pl.BlockSpec((1, HEAD_DIM), lambda b: (0, 0)),
        ],
        out_specs=[
            pl.BlockSpec((SEQ, A_Q + B_W), lambda b: (b, 0)),
            kv_spec(A_KV_HEADS), kv_spec(A_KV_HEADS), kv_spec(B_HEADS), kv_spec(B_HEADS),
        ],
        out_shape=[
            jax.ShapeDtypeStruct((N_CTX, A_Q + B_W), BF16),
            kv_shape(A_KV_HEADS), kv_shape(A_KV_HEADS), kv_shape(B_HEADS), kv_shape(B_HEADS),
        ],
        compiler_params=_params(("arbitrary",), 32),
        name="attn_ctx",
    )(h_a, h_b, qg, kg)


LAT_TQ = 256
LAT_QROWS = LAT_TQ // GRID_W


def _rope(x, cos, sin):
    lane = lax.broadcasted_iota(jnp.int32, x.shape, 1)
    nxt = pltpu.roll(x, x.shape[1] - 1, axis=1)
    prv = pltpu.roll(x, 1, axis=1)
    partner = jnp.where((lane & 1) == 0, nxt, prv)
    return x * cos + partner * sin


def _attn_lat_kernel(haq_ref, haf_ref, hbq_ref, hbf_ref, cak_ref, cav_ref, cnk_ref, cnv_ref,
                     qg_ref, kg_ref, cos_ref, sin_ref, bias_ref, oab_ref, qn_ref, kn_ref):
    t = pl.program_id(1)
    row0 = pl.multiple_of(t * LAT_TQ, LAT_TQ)
    qg = qg_ref[...]
    kg = kg_ref[...]

    for hq in range(A_HEADS):
        qn_ref[:, hq * HEAD_DIM:(hq + 1) * HEAD_DIM] = _rms_head(
            haq_ref[:, hq * HEAD_DIM:(hq + 1) * HEAD_DIM], qg) * ATT_SCALE
    for kv in range(A_KV_HEADS):
        kn_ref[:, kv * HEAD_DIM:(kv + 1) * HEAD_DIM] = _rms_head(
            haf_ref[:, A_Q + kv * HEAD_DIM:A_Q + (kv + 1) * HEAD_DIM], kg)
    cos_q = cos_ref[pl.ds(row0, LAT_TQ), :]
    sin_q = sin_ref[pl.ds(row0, LAT_TQ), :]
    for c in range(A_Q // 128):
        qn_ref[:, c * 128:(c + 1) * 128] = _rope(qn_ref[:, c * 128:(c + 1) * 128], cos_q, sin_q)
    kn_ref[...] = _rope(kn_ref[...], cos_ref[...], sin_ref[...])

    for kv in range(A_KV_HEADS):
        k_lat = kn_ref[:, kv * HEAD_DIM:(kv + 1) * HEAD_DIM].astype(BF16)
        v_lat = haf_ref[:, A_Q + A_KV + kv * HEAD_DIM:A_Q + A_KV + (kv + 1) * HEAD_DIM].astype(BF16)
        k_ctx = cak_ref[kv].astype(BF16)
        v_ctx = cav_ref[kv].astype(BF16)
        for g in range(A_GROUP):
            hq = kv * A_GROUP + g
            q = qn_ref[:, hq * HEAD_DIM:(hq + 1) * HEAD_DIM].astype(BF16)
            o = _softmax_pv([_qk(q, k_ctx), _qk(q, k_lat)], [v_ctx, v_lat])
            oab_ref[:, hq * HEAD_DIM:(hq + 1) * HEAD_DIM] = o.astype(BF16)

    for h in range(B_HEADS):
        k_ctx = cnk_ref[h].astype(BF16)
        v_ctx = cnv_ref[h].astype(BF16)
        for i in range(LAT_QROWS):
            r = t * LAT_QROWS + i
            r0 = jnp.clip(r - NA_WIN_R // 2, 0, LAT_ROWS - NA_WIN_R)
            k0 = pl.multiple_of(r0 * GRID_W, GRID_W)
            q = (hbq_ref[i * GRID_W:(i + 1) * GRID_W, h * HEAD_DIM:(h + 1) * HEAD_DIM] * ATT_SCALE).astype(BF16)
            k_win = hbf_ref[pl.ds(k0, NA_NK), B_W + h * HEAD_DIM:B_W + (h + 1) * HEAD_DIM].astype(BF16)
            v_win = hbf_ref[pl.ds(k0, NA_NK), 2 * B_W + h * HEAD_DIM:2 * B_W + (h + 1) * HEAD_DIM].astype(BF16)
            s_nb = _qk(q, k_win) + bias_ref[h, i]
            o = _softmax_pv([_qk(q, k_ctx), s_nb], [v_ctx, v_win])
            oab_ref[i * GRID_W:(i + 1) * GRID_W, A_Q + h * HEAD_DIM:A_Q + (h + 1) * HEAD_DIM] = o.astype(BF16)


def _attn_lat(h_a, h_b, cak, cav, cnk, cnv, layer, qg, kg, cos128, sin128, bias):
    nt = DEC_SEQ // LAT_TQ
    cache_spec = lambda nh: pl.BlockSpec((None, None, nh, PAST_LEN, HEAD_DIM), lambda b, t: (b, layer, 0, 0, 0))
    return pl.pallas_call(
        _attn_lat_kernel,
        grid=(DEC_BATCH, nt),
        in_specs=[
            pl.BlockSpec((LAT_TQ, W_A), lambda b, t: (b * nt + t, 0)),
            pl.BlockSpec((DEC_SEQ, W_A), lambda b, t: (b, 0)),
            pl.BlockSpec((LAT_TQ, W_B), lambda b, t: (b * nt + t, 0)),
            pl.BlockSpec((DEC_SEQ, W_B), lambda b, t: (b, 0)),
            cache_spec(A_KV_HEADS), cache_spec(A_KV_HEADS), cache_spec(B_HEADS), cache_spec(B_HEADS),
            pl.BlockSpec((1, HEAD_DIM), lambda b, t: (0, 0)),
            pl.BlockSpec((1, HEAD_DIM), lambda b, t: (0, 0)),
            pl.BlockSpec((DEC_SEQ, 128), lambda b, t: (0, 0)),
            pl.BlockSpec((DEC_SEQ, 128), lambda b, t: (0, 0)),
            pl.BlockSpec((B_HEADS, LAT_QROWS, GRID_W, NA_NK), lambda b, t: (0, t, 0, 0)),
        ],
        out_specs=pl.BlockSpec((LAT_TQ, A_Q + B_W), lambda b, t: (b * nt + t, 0)),
        out_shape=jax.ShapeDtypeStruct((N_LAT, A_Q + B_W), BF16),
        scratch_shapes=[pltpu.VMEM((LAT_TQ, A_Q), F32), pltpu.VMEM((DEC_SEQ, A_KV), F32)],
        compiler_params=_params(("arbitrary", "arbitrary"), 60),
        name="attn_lat",
    )(h_a, h_a, h_b, h_b, cak, cav, cnk, cnv, qg, kg, cos128, sin128, bias)


def _rope_tables():
    t = jnp.arange(DEC_SEQ)
    row = (t // GRID_W).astype(F32)
    col = (t % GRID_W).astype(F32)
    n_freq = HEAD_DIM // 4
    inv = ROPE_THETA ** (-jnp.arange(n_freq, dtype=F32) / n_freq)
    ang = jnp.concatenate([row[:, None] * inv, col[:, None] * inv], axis=-1)
    cos = jnp.repeat(jnp.cos(ang), 2, axis=-1)
    sign = jnp.tile(jnp.array([-1.0, 1.0], F32), HEAD_DIM // 2)
    sin = jnp.repeat(jnp.sin(ang), 2, axis=-1) * sign
    return jnp.tile(cos, (1, 2)), jnp.tile(sin, (1, 2))


def _na_bias(rpb):
    col = np.arange(GRID_W)
    dc = np.clip(col[None, :] - col[:, None] + NA_WIN_C - 1, 0, 2 * NA_WIN_C - 2)
    onehot = (dc[..., None] == np.arange(2 * NA_WIN_C - 1)).astype(np.float32)
    c0 = np.clip(col - NA_WIN_C // 2, 0, GRID_W - NA_WIN_C)
    in_win = (col[None, :] >= c0[:, None]) & (col[None, :] < c0[:, None] + NA_WIN_C)
    toep = jnp.einsum('hdm,ckm->hdck', rpb, onehot, precision=lax.Precision.HIGHEST)
    toep = jnp.where(in_win[None, None], toep, MASK_NEG)
    r0 = np.clip(np.arange(LAT_ROWS) - NA_WIN_R // 2, 0, LAT_ROWS - NA_WIN_R)
    rows = [jnp.concatenate([toep[:, r0[r] + j - r + NA_WIN_R - 1] for j in range(NA_WIN_R)], axis=-1)
            for r in range(LAT_ROWS)]
    return jnp.stack(rows, axis=1)


def _conv_kernel(hc_ref, w_ref, o_ref, *, seq):
    bg = hc_ref[:, 0:C_WIDTH]
    u = hc_ref[:, C_WIDTH:2 * C_WIDTH] * hc_ref[:, 2 * C_WIDTH:3 * C_WIDTH]
    rows = u.shape[0]
    assert seq & (seq - 1) == 0
    pos = lax.broadcasted_iota(jnp.int32, u.shape, 0) & (seq - 1)
    u_prev = jnp.where(pos == 0, 0.0, pltpu.roll(u, 1, axis=0))
    u_next = jnp.where(pos == seq - 1, 0.0, pltpu.roll(u, rows - 1, axis=0))
    y = u_prev * w_ref[0:1, :] + u * w_ref[1:2, :] + u_next * w_ref[2:3, :]
    o_ref[...] = (bg * y).astype(o_ref.dtype)


def _conv(h_c, conv_w, seq):
    n = h_c.shape[0]
    tm = 1024
    return pl.pallas_call(
        functools.partial(_conv_kernel, seq=seq),
        grid=(n // tm,),
        in_specs=[pl.BlockSpec((tm, W_C), lambda i: (i, 0)), pl.BlockSpec((3, C_WIDTH), lambda i: (0, 0))],
        out_specs=pl.BlockSpec((tm, C_WIDTH), lambda i: (i, 0)),
        out_shape=jax.ShapeDtypeStruct((n, C_WIDTH), BF16),
        compiler_params=_params(("arbitrary",), 40),
        name="short_conv",
    )(h_c, conv_w)


def _mix_kernel(x_ref, oab_ref, oc_ref, hg_ref, wa_ref, wb_ref, wc_ref, wo_ref, g1_ref, n2g_ref, sc2_ref, sh2_ref,
                wrt_ref, xo_ref, xn2_ref, pt_ref):
    ga = jax.nn.sigmoid(hg_ref[:, 0:D_MODEL])
    gb = jax.nn.sigmoid(hg_ref[:, D_MODEL:2 * D_MODEL])
    gc = jax.nn.sigmoid(hg_ref[:, 2 * D_MODEL:3 * D_MODEL])
    pa = jnp.dot(oab_ref[:, 0:A_Q], wa_ref[...], preferred_element_type=F32)
    pb = jnp.dot(oab_ref[:, A_Q:A_Q + B_W], wb_ref[...], preferred_element_type=F32)
    pc = jnp.dot(oc_ref[...], wc_ref[...], preferred_element_type=F32)
    merged = ga * pa + gb * pb + gc * pc
    xo = x_ref[...] + g1_ref[...] * jnp.dot(merged.astype(BF16), wo_ref[...], preferred_element_type=F32)
    xo_ref[...] = xo
    y = xo * lax.rsqrt(jnp.mean(xo * xo, axis=-1, keepdims=True) + EPS) * n2g_ref[...]
    xn2 = y * (1.0 + sc2_ref[...]) + sh2_ref[...]
    xn2_ref[...] = xn2
    logits_t = lax.dot_general(wrt_ref[...], xn2.astype(BF16), _NT, preferred_element_type=F32)
    z = jnp.exp(logits_t - logits_t.max(axis=0, keepdims=True))
    pt_ref[...] = z / z.sum(axis=0, keepdims=True)


def _mix(x, oab, oc, h_g, wa, wb, wc, wo, g1, n2g, sc2, sh2, wrt, seq):
    n = x.shape[0]
    tm = 512
    per_batch = g1.shape[0] > 1
    mod_map = (lambda i: ((i * tm) // seq, 0, 0)) if per_batch else (lambda i: (0, 0, 0))
    mod_spec = pl.BlockSpec((None, 1, D_MODEL), mod_map)
    full = lambda a: pl.BlockSpec(a.shape, lambda i: (0,) * a.ndim)
    row = lambda w: pl.BlockSpec((tm, w), lambda i: (i, 0))
    return pl.pallas_call(
        _mix_kernel,
        grid=(n // tm,),
        in_specs=[row(D_MODEL), row(A_Q + B_W), row(C_WIDTH), row(GATE_W), full(wa), full(wb), full(wc), full(wo),
                  mod_spec, full(n2g), mod_spec, mod_spec, full(wrt)],
        out_specs=[row(D_MODEL), row(D_MODEL), pl.BlockSpec((N_EXPERTS, tm), lambda i: (0, i))],
        out_shape=[jax.ShapeDtypeStruct((n, D_MODEL), F32), jax.ShapeDtypeStruct((n, D_MODEL), F32),
                   jax.ShapeDtypeStruct((N_EXPERTS, n), F32)],
        compiler_params=_params(("arbitrary",), 56),
        name="mix",
    )(x, oab, oc, h_g, wa, wb, wc, wo, g1, n2g, sc2, sh2, wrt)


MOE_TF = 512
MOE_ROW_CHUNKS = ((0, 512), (512, 512), (1024, 256))


def _moe_kernel(idx_ref, xc_hbm, xl_hbm, aff_ref, wg_ref, wu_ref, wd_ref, ye_ref, xe_ref, sem):
    e = pl.program_id(0)
    f = pl.program_id(1)

    @pl.when(f == 0)
    def _():
        base = e * CAP_ALL

        def issue_ctx(r, carry):
            tok = idx_ref[base + r]
            pltpu.make_async_copy(xc_hbm.at[pl.ds(tok, 1)], xe_ref.at[pl.ds(r, 1)], sem).start()
            return carry

        def issue_lat(r, carry):
            tok = idx_ref[base + CAP_CTX + r]
            pltpu.make_async_copy(xl_hbm.at[pl.ds(tok, 1)], xe_ref.at[pl.ds(CAP_CTX + r, 1)], sem).start()
            return carry

        lax.fori_loop(0, CAP_CTX, issue_ctx, 0)
        lax.fori_loop(0, CAP_LAT, issue_lat, 0)
        pltpu.make_async_copy(xc_hbm.at[pl.ds(0, CAP_ALL)], xe_ref, sem).wait()

    wg = wg_ref[...].astype(BF16)
    wu = wu_ref[...].astype(BF16)
    wd = wd_ref[...].astype(BF16)
    for r0, rn in MOE_ROW_CHUNKS:
        xe = xe_ref[r0:r0 + rn, :].astype(BF16)
        hg = jnp.dot(xe, wg, preferred_element_type=F32)
        hu = jnp.dot(xe, wu, preferred_element_type=F32)
        act = (hg * jax.nn.sigmoid(hg) * hu).astype(BF16)
        part = jnp.dot(act, wd, preferred_element_type=F32) * aff_ref[r0:r0 + rn, :]

        @pl.when(f == 0)
        def _():
            ye_ref[r0:r0 + rn, :] = part

        @pl.when(f != 0)
        def _():
            ye_ref[r0:r0 + rn, :] += part


def _moe(idx_all, xn2_ctx, xn2_lat, aff_all, w_gate_e, w_up_e, w_down_e, layer):
    nf = EXPERT_FF // MOE_TF
    grid_spec = pltpu.PrefetchScalarGridSpec(
        num_scalar_prefetch=1,
        grid=(N_EXPERTS, nf),
        in_specs=[
            pl.BlockSpec(memory_space=pl.ANY),
            pl.BlockSpec(memory_space=pl.ANY),
            pl.BlockSpec((None, CAP_ALL, 1), lambda e, f, idx: (e, 0, 0)),
            pl.BlockSpec((None, None, D_MODEL, MOE_TF), lambda e, f, idx: (layer, e, 0, f)),
            pl.BlockSpec((None, None, D_MODEL, MOE_TF), lambda e, f, idx: (layer, e, 0, f)),
            pl.BlockSpec((None, None, MOE_TF, D_MODEL), lambda e, f, idx: (layer, e, f, 0)),
        ],
        out_specs=pl.BlockSpec((None, CAP_ALL, D_MODEL), lambda e, f, idx: (e, 0, 0)),
        scratch_shapes=[pltpu.VMEM((CAP_ALL, D_MODEL), F32), pltpu.SemaphoreType.DMA],
    )
    return pl.pallas_call(
        _moe_kernel,
        grid_spec=grid_spec,
        out_shape=jax.ShapeDtypeStruct((N_EXPERTS, CAP_ALL, D_MODEL), F32),
        compiler_params=_params(("arbitrary", "arbitrary"), 56),
        name="moe_experts",
    )(idx_all, xn2_ctx, xn2_lat, aff_all, w_gate_e, w_up_e, w_down_e)


def _route(probs_t, cap):
    aff, idx = lax.top_k(probs_t, cap)
    order = jnp.argsort(idx, axis=-1)
    return jnp.take_along_axis(idx, order, axis=-1), jnp.take_along_axis(aff, order, axis=-1)


def kernel(x_prompt, x_sample, cache_attn_k, cache_attn_v, cache_na_k, cache_na_v, c, c_ctx, w_ada, b_ada, norm1_g,
           w_in, q_norm_g, k_norm_g, na_rpb, conv_w, w_proj_a, w_proj_b, w_proj_c, w_out, norm2_g, w_router,
           w_gate_e, w_up_e, w_down_e, final_g):
    cond8 = jnp.zeros((8, D_MODEL), F32).at[0].set(c_ctx).at[1:1 + DEC_BATCH].set(c)
    mods = _adaln(cond8, w_ada, b_ada).reshape(DEPTH, 8, N_MOD, D_MODEL)
    cos128, sin128 = _rope_tables()

    hp = x_prompt.reshape(N_CTX, D_MODEL)
    hs = x_sample.reshape(N_LAT, D_MODEL)
    new_ak, new_av, new_nk, new_nv = [], [], [], []
    for l in range(DEPTH):
        mod_ctx = [mods[l, 0:1, i][:, None, :] for i in range(N_MOD)]
        mod_lat = [mods[l, 1:1 + DEC_BATCH, i][:, None, :] for i in range(N_MOD)]
        n1g = norm1_g[l][None]
        n2g = norm2_g[l][None]
        qg = q_norm_g[l][None]
        kg = k_norm_g[l][None]
        wa, wb, wc, wo = (w_proj_a[l].astype(BF16), w_proj_b[l].astype(BF16), w_proj_c[l].astype(BF16),
                          w_out[l].astype(BF16))
        wrt = w_router[l].T.astype(BF16)
        bias = _na_bias(na_rpb[l])

        def sublayer1(x, mod, seq, is_ctx):
            sh1, sc1, g1, sh2, sc2, _ = mod
            xn = _norm_mod(x, n1g, sc1, sh1, seq, BF16)
            h_a = _inproj(xn, w_in, l, COL_A, W_A, 768, F32)
            h_b = _inproj(xn, w_in, l, COL_B, W_B, 768, F32)
            h_c = _inproj(xn, w_in, l, COL_C, W_C, 768, F32)
            h_g = _inproj(xn, w_in, l, COL_G, GATE_W, 768, F32)
            kv = None
            if is_ctx:
                oab, ka, va, kb, vb = _attn_ctx(h_a, h_b, qg, kg)
                kv = (ka, va, kb, vb)
            else:
                oab = _attn_lat(h_a, h_b, cache_attn_k, cache_attn_v, cache_na_k, cache_na_v, l, qg, kg,
                                cos128, sin128, bias)
            oc = _conv(h_c, conv_w[l], seq)
            xo, xn2, probs_t = _mix(x, oab, oc, h_g, wa, wb, wc, wo, g1, n2g, sc2, sh2, wrt, seq)
            return xo, xn2, probs_t, kv

        hp, xn2_c, pt_c, (ka, va, kb, vb) = sublayer1(hp, mod_ctx, SEQ, True)
        hs, xn2_l, pt_l, _ = sublayer1(hs, mod_lat, DEC_SEQ, False)
        new_ak.append(ka)
        new_av.append(va)
        new_nk.append(kb)
        new_nv.append(vb)

        idx_c, aff_c = _route(pt_c, CAP_CTX)
        idx_l, aff_l = _route(pt_l, CAP_LAT)
        idx_all = jnp.concatenate([idx_c, idx_l], axis=1).reshape(-1).astype(jnp.int32)
        aff_all = jnp.concatenate([aff_c, aff_l], axis=1)[..., None]
        ye = _moe(idx_all, xn2_c, xn2_l, aff_all, w_gate_e, w_up_e, w_down_e, l)
        y_c = jnp.zeros((N_CTX, D_MODEL), F32).at[idx_c.reshape(-1)].add(ye[:, :CAP_CTX].reshape(-1, D_MODEL))
        y_l = jnp.zeros((N_LAT, D_MODEL), F32).at[idx_l.reshape(-1)].add(ye[:, CAP_CTX:].reshape(-1, D_MODEL))
        hp = hp + (mod_ctx[5] * y_c.reshape(1, N_CTX, D_MODEL)).reshape(N_CTX, D_MODEL)
        hs = hs + (mod_lat[5] * y_l.reshape(DEC_BATCH, DEC_SEQ, D_MODEL)).reshape(N_LAT, D_MODEL)

    fg = final_g[None]
    y_prompt = _final_norm(hp, fg).reshape(BATCH, SEQ, D_MODEL)
    y_sample = _final_norm(hs, fg).reshape(DEC_BATCH, DEC_SEQ, D_MODEL)
    return (y_prompt, y_sample, jnp.stack(new_ak, axis=1), jnp.stack(new_av, axis=1),
            jnp.stack(new_nk, axis=1), jnp.stack(new_nv, axis=1))
```

```python
import functools

import jax
import jax.numpy as jnp
import numpy as np
from jax import lax
from jax.experimental import pallas as pl
from jax.experimental.pallas import tpu as pltpu

D_MODEL = 1024
BATCH = 32
SEQ = 256
DEPTH = 2
DEC_BATCH = 2
DEC_SEQ = 1024
PAST_LEN = 256
GRID_W = 64
HEAD_DIM = 64
A_HEADS = 8
A_KV_HEADS = 2
A_GROUP = A_HEADS // A_KV_HEADS
B_HEADS = 8
C_WIDTH = 512
NA_WIN_R = 8
NA_WIN_C = 16
N_EXPERTS = 16
EXPERT_FF = 1024
CAP_FACTOR = 2
ROPE_THETA = 10000.0
EPS = 1e-6
N_MOD = 6

A_Q = A_HEADS * HEAD_DIM
A_KV = A_KV_HEADS * HEAD_DIM
B_W = B_HEADS * HEAD_DIM
GATE_W = 3 * D_MODEL
COL_A = 0
W_A = A_Q + 2 * A_KV
COL_B = COL_A + W_A
W_B = 3 * B_W
COL_C = COL_B + W_B
W_C = 3 * C_WIDTH
COL_G = COL_C + W_C

N_CTX = BATCH * SEQ
N_LAT = DEC_BATCH * DEC_SEQ
CAP_CTX = CAP_FACTOR * N_CTX // N_EXPERTS
CAP_LAT = CAP_FACTOR * N_LAT // N_EXPERTS
CAP_ALL = CAP_CTX + CAP_LAT
LAT_ROWS = DEC_SEQ // GRID_W
NA_NK = NA_WIN_R * GRID_W
MASK_NEG = -1e30
ATT_SCALE = HEAD_DIM ** -0.5

F32 = jnp.float32
BF16 = jnp.bfloat16
_NT = (((1,), (1,)), ((), ()))

_MIB = 1024 * 1024


def _params(sem, vmem_mib):
    return pltpu.CompilerParams(dimension_semantics=sem, vmem_limit_bytes=vmem_mib * _MIB)


def _adaln_kernel(cond_ref, w_ref, b_ref, o_ref):
    c = cond_ref[...]
    s = c * jax.nn.sigmoid(c)
    o_ref[...] = jnp.dot(s.astype(BF16), w_ref[...].astype(BF16), preferred_element_type=F32) + b_ref[...]


def _adaln(cond8, w_ada, b_ada):
    tn = 1536
    ncol = N_MOD * D_MODEL
    return pl.pallas_call(
        _adaln_kernel,
        grid=(DEPTH, ncol // tn),
        in_specs=[
            pl.BlockSpec((8, D_MODEL), lambda l, j: (0, 0)),
            pl.BlockSpec((None, D_MODEL, tn), lambda l, j: (l, 0, j)),
            pl.BlockSpec((None, 1, tn), lambda l, j: (l, 0, j)),
        ],
        out_specs=pl.BlockSpec((None, 8, tn), lambda l, j: (l, 0, j)),
        out_shape=jax.ShapeDtypeStruct((DEPTH, 8, ncol), F32),
        compiler_params=_params(("arbitrary", "arbitrary"), 40),
        name="adaln",
    )(cond8, w_ada, b_ada.reshape(DEPTH, 1, ncol))


def _norm_mod_kernel(x_ref, g_ref, sc_ref, sh_ref, o_ref):
    x = x_ref[...]
    y = x * lax.rsqrt(jnp.mean(x * x, axis=-1, keepdims=True) + EPS) * g_ref[...]
    o_ref[...] = (y * (1.0 + sc_ref[...]) + sh_ref[...]).astype(o_ref.dtype)


def _norm_mod(x, g, sc, sh, seq, out_dtype):
    n = x.shape[0]
    tm = 512
    per_batch = sc.shape[0] > 1
    mod_map = (lambda i: ((i * tm) // seq, 0, 0)) if per_batch else (lambda i: (0, 0, 0))
    return pl.pallas_call(
        _norm_mod_kernel,
        grid=(n // tm,),
        in_specs=[
            pl.BlockSpec((tm, D_MODEL), lambda i: (i, 0)),
            pl.BlockSpec((1, D_MODEL), lambda i: (0, 0)),
            pl.BlockSpec((None, 1, D_MODEL), mod_map),
            pl.BlockSpec((None, 1, D_MODEL), mod_map),
        ],
        out_specs=pl.BlockSpec((tm, D_MODEL), lambda i: (i, 0)),
        out_shape=jax.ShapeDtypeStruct((n, D_MODEL), out_dtype),
        compiler_params=_params(("arbitrary",), 32),
        name="norm_mod",
    )(x, g, sc, sh)


def _norm_kernel(x_ref, g_ref, o_ref):
    x = x_ref[...]
    o_ref[...] = x * lax.rsqrt(jnp.mean(x * x, axis=-1, keepdims=True) + EPS) * g_ref[...]


def _final_norm(x, g):
    n = x.shape[0]
    tm = 512
    return pl.pallas_call(
        _norm_kernel,
        grid=(n // tm,),
        in_specs=[pl.BlockSpec((tm, D_MODEL), lambda i: (i, 0)), pl.BlockSpec((1, D_MODEL), lambda i: (0, 0))],
        out_specs=pl.BlockSpec((tm, D_MODEL), lambda i: (i, 0)),
        out_shape=jax.ShapeDtypeStruct((n, D_MODEL), F32),
        compiler_params=_params(("arbitrary",), 32),
        name="final_norm",
    )(x, g)


def _inproj_kernel(a_ref, w_ref, o_ref, wbf_ref):
    @pl.when(pl.program_id(1) == 0)
    def _():
        wbf_ref[...] = w_ref[...].astype(BF16)

    o_ref[...] = jnp.dot(a_ref[...], wbf_ref[...], preferred_element_type=F32).astype(o_ref.dtype)


def _inproj(xn, w_in, layer, col0, width, tn, out_dtype):
    n = xn.shape[0]
    tm = 1024
    joff = col0 // tn
    assert col0 % tn == 0 and width % tn == 0
    return pl.pallas_call(
        _inproj_kernel,
        grid=(width // tn, n // tm),
        in_specs=[
            pl.BlockSpec((tm, D_MODEL), lambda j, i: (i, 0)),
            pl.BlockSpec((None, D_MODEL, tn), lambda j, i: (layer, 0, j + joff)),
        ],
        out_specs=pl.BlockSpec((tm, tn), lambda j, i: (i, j)),
        out_shape=jax.ShapeDtypeStruct((n, width), out_dtype),
        scratch_shapes=[pltpu.VMEM((D_MODEL, tn), BF16)],
        compiler_params=_params(("arbitrary", "arbitrary"), 48),
        name="inproj",
    )(xn, w_in)


def _rms_head(x, g):
    return x * lax.rsqrt(jnp.mean(x * x, axis=-1, keepdims=True) + EPS) * g


def _softmax_pv(scores, values):
    m = scores[0].max(axis=-1, keepdims=True)
    for s in scores[1:]:
        m = jnp.maximum(m, s.max(axis=-1, keepdims=True))
    acc = None
    den = None
    for s, v in zip(scores, values):
        p = jnp.exp(s - m)
        d = p.sum(axis=-1, keepdims=True)
        o = jnp.dot(p.astype(BF16), v, preferred_element_type=F32)
        acc = o if acc is None else acc + o
        den = d if den is None else den + d
    return acc / den


def _qk(q, k):
    return lax.dot_general(q, k, _NT, preferred_element_type=F32)


CTX_NB = 2
PAIR = 2 * HEAD_DIM


def _pair_consts():
    lane = lax.broadcasted_iota(jnp.int32, (1, PAIR), 1)
    is_lo = lane < HEAD_DIM
    r = lax.broadcasted_iota(jnp.int32, (PAIR, PAIR), 0)
    c = lax.broadcasted_iota(jnp.int32, (PAIR, PAIR), 1)
    half_ones = jnp.where((r < HEAD_DIM) == (c < HEAD_DIM), 1.0, 0.0).astype(BF16)
    return is_lo, half_ones


def _pair_mean_sq(x, half_ones):
    x2 = x * x
    hi = x2.astype(BF16)
    lo = (x2 - hi.astype(F32)).astype(BF16)
    tot = (jnp.dot(hi, half_ones, preferred_element_type=F32) + jnp.dot(lo, half_ones, preferred_element_type=F32))
    return tot * (1.0 / HEAD_DIM)


def _pair_rms(x, g2, half_ones):
    return x * lax.rsqrt(_pair_mean_sq(x, half_ones) + EPS) * g2


def _softmax_pv_pair(scores, values, ones_v):
    m = scores[0].max(axis=-1, keepdims=True)
    for s in scores[1:]:
        m = jnp.maximum(m, s.max(axis=-1, keepdims=True))
    acc = None
    den = None
    for s, v, ones in zip(scores, values, ones_v):
        p = jnp.exp(s - m).astype(BF16)
        o = jnp.dot(p, v, preferred_element_type=F32)
        d = jnp.dot(p, ones, preferred_element_type=F32)
        acc = o if acc is None else acc + o
        den = d if den is None else den + d
    return acc / den


def _attn_ctx_kernel(ha_ref, hb_ref, qg_ref, kg_ref, oab_ref, ak_ref, av_ref, nk_ref, nv_ref):
    is_lo, half_ones = _pair_consts()
    ones_v = jnp.ones((SEQ, PAIR), BF16)
    qg2 = qg_ref[...] * ATT_SCALE
    kg2 = kg_ref[...]
    for b in range(CTX_NB):
        rows = slice(b * SEQ, (b + 1) * SEQ)
        kp = _pair_rms(ha_ref[rows, A_Q:A_Q + PAIR], kg2, half_ones)
        vp = ha_ref[rows, A_Q + A_KV:A_Q + A_KV + PAIR]
        for kv in range(A_KV_HEADS):
            ak_ref[b, kv] = kp[:, kv * HEAD_DIM:(kv + 1) * HEAD_DIM]
            av_ref[b, kv] = vp[:, kv * HEAD_DIM:(kv + 1) * HEAD_DIM]
        kp_sw = pltpu.roll(kp, HEAD_DIM, axis=1)
        vp_sw = pltpu.roll(vp, HEAD_DIM, axis=1)
        qs = [_pair_rms(ha_ref[rows, c * PAIR:(c + 1) * PAIR], qg2, half_ones).astype(BF16)
              for c in range(A_HEADS // 2)]
        for kv in range(A_KV_HEADS):
            k_src_lo, k_src_hi = (kp, kp_sw) if kv == 0 else (kp_sw, kp)
            v_src_lo, v_src_hi = (vp, vp_sw) if kv == 0 else (vp_sw, vp)
            k_lo = jnp.where(is_lo, k_src_lo, 0.0).astype(BF16)
            k_hi = jnp.where(is_lo, 0.0, k_src_hi).astype(BF16)
            v_lo = jnp.where(is_lo, v_src_lo, 0.0).astype(BF16)
            v_hi = jnp.where(is_lo, 0.0, v_src_hi).astype(BF16)
            q_st = jnp.concatenate([qs[2 * kv], qs[2 * kv + 1]], axis=0)
            o = (_softmax_pv_pair([_qk(q_st, k_lo)], [v_lo], [ones_v])
                 + _softmax_pv_pair([_qk(q_st, k_hi)], [v_hi], [ones_v]))
            oab_ref[rows, (2 * kv) * PAIR:(2 * kv + 1) * PAIR] = o[0:SEQ].astype(BF16)
            oab_ref[rows, (2 * kv + 1) * PAIR:(2 * kv + 2) * PAIR] = o[SEQ:2 * SEQ].astype(BF16)
        for j in range(B_HEADS // 2):
            qp = (hb_ref[rows, j * PAIR:(j + 1) * PAIR] * ATT_SCALE).astype(BF16)
            kp_b = hb_ref[rows, B_W + j * PAIR:B_W + (j + 1) * PAIR]
            vp_b = hb_ref[rows, 2 * B_W + j * PAIR:2 * B_W + (j + 1) * PAIR]
            for u in range(2):
                nk_ref[b, 2 * j + u] = kp_b[:, u * HEAD_DIM:(u + 1) * HEAD_DIM]
                nv_ref[b, 2 * j + u] = vp_b[:, u * HEAD_DIM:(u + 1) * HEAD_DIM]
            k_lo = jnp.where(is_lo, kp_b, 0.0).astype(BF16)
            k_hi = jnp.where(is_lo, 0.0, kp_b).astype(BF16)
            v_lo = jnp.where(is_lo, vp_b, 0.0).astype(BF16)
            v_hi = jnp.where(is_lo, 0.0, vp_b).astype(BF16)
            o = (_softmax_pv_pair([_qk(qp, k_lo)], [v_lo], [ones_v])
                 + _softmax_pv_pair([_qk(qp, k_hi)], [v_hi], [ones_v]))
            oab_ref[rows, A_Q + j * PAIR:A_Q + (j + 1) * PAIR] = o.astype(BF16)


def _attn_ctx(h_a, h_b, qg2, kg2):
    rows = CTX_NB * SEQ
    kv_spec = lambda nh: pl.BlockSpec((CTX_NB, nh, SEQ, HEAD_DIM), lambda b: (b, 0, 0, 0))
    kv_shape = lambda nh: jax.ShapeDtypeStruct((BATCH, nh, SEQ, HEAD_DIM), F32)
    return pl.pallas_call(
        _attn_ctx_kernel,
        grid=(BATCH // CTX_NB,),
        in_specs=[
            pl.BlockSpec((rows, W_A), lambda b: (b, 0)),
            pl.BlockSpec((rows, W_B), lambda b: (b, 0)),
            pl.BlockSpec((1, PAIR), lambda b: (0, 0)),
            pl.BlockSpec((1, PAIR), lambda b: (0, 0)),
        ],
        out_specs=[
            pl.BlockSpec((rows, A_Q + B_W), lambda b: (b, 0)),
            kv_spec(A_KV_HEADS), kv_spec(A_KV_HEADS), kv_spec(B_HEADS), kv_spec(B_HEADS),
        ],
        out_shape=[
            jax.ShapeDtypeStruct((N_CTX, A_Q + B_W), BF16),
            kv_shape(A_KV_HEADS), kv_shape(A_KV_HEADS), kv_shape(B_HEADS), kv_shape(B_HEADS),
        ],
        compiler_params=_params(("arbitrary",), 40),
        name="attn_ctx",
    )(h_a, h_b, qg2, kg2)


LAT_TQ = 256
LAT_QROWS = LAT_TQ // GRID_W


def _rope(x, cos, sin):
    lane = lax.broadcasted_iota(jnp.int32, x.shape, 1)
    nxt = pltpu.roll(x, x.shape[1] - 1, axis=1)
    prv = pltpu.roll(x, 1, axis=1)
    partner = jnp.where((lane & 1) == 0, nxt, prv)
    return x * cos + partner * sin


def _attn_lat_kernel(haq_ref, haf_ref, hbq_ref, hbf_ref, cak_ref, cav_ref, cnk_ref, cnv_ref,
                     qg_ref, kg_ref, cos_ref, sin_ref, bias_ref, oab_ref, qn_ref, kn_ref):
    t = pl.program_id(1)
    row0 = pl.multiple_of(t * LAT_TQ, LAT_TQ)
    qg = qg_ref[...]
    kg = kg_ref[...]

    for hq in range(A_HEADS):
        qn_ref[:, hq * HEAD_DIM:(hq + 1) * HEAD_DIM] = _rms_head(
            haq_ref[:, hq * HEAD_DIM:(hq + 1) * HEAD_DIM], qg) * ATT_SCALE
    for kv in range(A_KV_HEADS):
        kn_ref[:, kv * HEAD_DIM:(kv + 1) * HEAD_DIM] = _rms_head(
            haf_ref[:, A_Q + kv * HEAD_DIM:A_Q + (kv + 1) * HEAD_DIM], kg)
    cos_q = cos_ref[pl.ds(row0, LAT_TQ), :]
    sin_q = sin_ref[pl.ds(row0, LAT_TQ), :]
    for c in range(A_Q // 128):
        qn_ref[:, c * 128:(c + 1) * 128] = _rope(qn_ref[:, c * 128:(c + 1) * 128], cos_q, sin_q)
    kn_ref[...] = _rope(kn_ref[...], cos_ref[...], sin_ref[...])

    for kv in range(A_KV_HEADS):
        k_lat = kn_ref[:, kv * HEAD_DIM:(kv + 1) * HEAD_DIM].astype(BF16)
        v_lat = haf_ref[:, A_Q + A_KV + kv * HEAD_DIM:A_Q + A_KV + (kv + 1) * HEAD_DIM].astype(BF16)
        k_ctx = cak_ref[kv].astype(BF16)
        v_ctx = cav_ref[kv].astype(BF16)
        for g in range(A_GROUP):
            hq = kv * A_GROUP + g
            q = qn_ref[:, hq * HEAD_DIM:(hq + 1) * HEAD_DIM].astype(BF16)
            o = _softmax_pv([_qk(q, k_ctx), _qk(q, k_lat)], [v_ctx, v_lat])
            oab_ref[:, hq * HEAD_DIM:(hq + 1) * HEAD_DIM] = o.astype(BF16)

    for h in range(B_HEADS):
        k_ctx = cnk_ref[h].astype(BF16)
        v_ctx = cnv_ref[h].astype(BF16)
        for i in range(LAT_QROWS):
            r = t * LAT_QROWS + i
            r0 = jnp.clip(r - NA_WIN_R // 2, 0, LAT_ROWS - NA_WIN_R)
            k0 = pl.multiple_of(r0 * GRID_W, GRID_W)
            q = (hbq_ref[i * GRID_W:(i + 1) * GRID_W, h * HEAD_DIM:(h + 1) * HEAD_DIM] * ATT_SCALE).astype(BF16)
            k_win = hbf_ref[pl.ds(k0, NA_NK), B_W + h * HEAD_DIM:B_W + (h + 1) * HEAD_DIM].astype(BF16)
            v_win = hbf_ref[pl.ds(k0, NA_NK), 2 * B_W + h * HEAD_DIM:2 * B_W + (h + 1) * HEAD_DIM].astype(BF16)
            s_nb = _qk(q, k_win) + bias_ref[h, i]
            o = _softmax_pv([_qk(q, k_ctx), s_nb], [v_ctx, v_win])
            oab_ref[i * GRID_W:(i + 1) * GRID_W, A_Q + h * HEAD_DIM:A_Q + (h + 1) * HEAD_DIM] = o.astype(BF16)


def _attn_lat(h_a, h_b, cak, cav, cnk, cnv, layer, qg, kg, cos128, sin128, bias):
    nt = DEC_SEQ // LAT_TQ
    cache_spec = lambda nh: pl.BlockSpec((None, None, nh, PAST_LEN, HEAD_DIM), lambda b, t: (b, layer, 0, 0, 0))
    return pl.pallas_call(
        _attn_lat_kernel,
        grid=(DEC_BATCH, nt),
        in_specs=[
            pl.BlockSpec((LAT_TQ, W_A), lambda b, t: (b * nt + t, 0)),
            pl.BlockSpec((DEC_SEQ, W_A), lambda b, t: (b, 0)),
            pl.BlockSpec((LAT_TQ, W_B), lambda b, t: (b * nt + t, 0)),
            pl.BlockSpec((DEC_SEQ, W_B), lambda b, t: (b, 0)),
            cache_spec(A_KV_HEADS), cache_spec(A_KV_HEADS), cache_spec(B_HEADS), cache_spec(B_HEADS),
            pl.BlockSpec((1, HEAD_DIM), lambda b, t: (0, 0)),
            pl.BlockSpec((1, HEAD_DIM), lambda b, t: (0, 0)),
            pl.BlockSpec((DEC_SEQ, 128), lambda b, t: (0, 0)),
            pl.BlockSpec((DEC_SEQ, 128), lambda b, t: (0, 0)),
            pl.BlockSpec((B_HEADS, LAT_QROWS, GRID_W, NA_NK), lambda b, t: (0, t, 0, 0)),
        ],
        out_specs=pl.BlockSpec((LAT_TQ, A_Q + B_W), lambda b, t: (b * nt + t, 0)),
        out_shape=jax.ShapeDtypeStruct((N_LAT, A_Q + B_W), BF16),
        scratch_shapes=[pltpu.VMEM((LAT_TQ, A_Q), F32), pltpu.VMEM((DEC_SEQ, A_KV), F32)],
        compiler_params=_params(("arbitrary", "arbitrary"), 60),
        name="attn_lat",
    )(h_a, h_a, h_b, h_b, cak, cav, cnk, cnv, qg, kg, cos128, sin128, bias)


def _rope_tables():
    t = jnp.arange(DEC_SEQ)
    row = (t // GRID_W).astype(F32)
    col = (t % GRID_W).astype(F32)
    n_freq = HEAD_DIM // 4
    inv = ROPE_THETA ** (-jnp.arange(n_freq, dtype=F32) / n_freq)
    ang = jnp.concatenate([row[:, None] * inv, col[:, None] * inv], axis=-1)
    cos = jnp.repeat(jnp.cos(ang), 2, axis=-1)
    sign = jnp.tile(jnp.array([-1.0, 1.0], F32), HEAD_DIM // 2)
    sin = jnp.repeat(jnp.sin(ang), 2, axis=-1) * sign
    return jnp.tile(cos, (1, 2)), jnp.tile(sin, (1, 2))


def _na_bias(rpb):
    col = np.arange(GRID_W)
    dc = np.clip(col[None, :] - col[:, None] + NA_WIN_C - 1, 0, 2 * NA_WIN_C - 2)
    onehot = (dc[..., None] == np.arange(2 * NA_WIN_C - 1)).astype(np.float32)
    c0 = np.clip(col - NA_WIN_C // 2, 0, GRID_W - NA_WIN_C)
    in_win = (col[None, :] >= c0[:, None]) & (col[None, :] < c0[:, None] + NA_WIN_C)
    toep = jnp.einsum('hdm,ckm->hdck', rpb, onehot, precision=lax.Precision.HIGHEST)
    toep = jnp.where(in_win[None, None], toep, MASK_NEG)
    r0 = np.clip(np.arange(LAT_ROWS) - NA_WIN_R // 2, 0, LAT_ROWS - NA_WIN_R)
    rows = [jnp.concatenate([toep[:, r0[r] + j - r + NA_WIN_R - 1] for j in range(NA_WIN_R)], axis=-1)
            for r in range(LAT_ROWS)]
    return jnp.stack(rows, axis=1)


def _conv_kernel(hc_ref, w_ref, o_ref, *, seq):
    bg = hc_ref[:, 0:C_WIDTH]
    u = hc_ref[:, C_WIDTH:2 * C_WIDTH] * hc_ref[:, 2 * C_WIDTH:3 * C_WIDTH]
    rows = u.shape[0]
    assert seq & (seq - 1) == 0
    pos = lax.broadcasted_iota(jnp.int32, u.shape, 0) & (seq - 1)
    u_prev = jnp.where(pos == 0, 0.0, pltpu.roll(u, 1, axis=0))
    u_next = jnp.where(pos == seq - 1, 0.0, pltpu.roll(u, rows - 1, axis=0))
    y = u_prev * w_ref[0:1, :] + u * w_ref[1:2, :] + u_next * w_ref[2:3, :]
    o_ref[...] = (bg * y).astype(o_ref.dtype)


def _conv(h_c, conv_w, seq):
    n = h_c.shape[0]
    tm = 1024
    return pl.pallas_call(
        functools.partial(_conv_kernel, seq=seq),
        grid=(n // tm,),
        in_specs=[pl.BlockSpec((tm, W_C), lambda i: (i, 0)), pl.BlockSpec((3, C_WIDTH), lambda i: (0, 0))],
        out_specs=pl.BlockSpec((tm, C_WIDTH), lambda i: (i, 0)),
        out_shape=jax.ShapeDtypeStruct((n, C_WIDTH), BF16),
        compiler_params=_params(("arbitrary",), 40),
        name="short_conv",
    )(h_c, conv_w)


def _mix_kernel(x_ref, oab_ref, oc_ref, hg_ref, wa_ref, wb_ref, wc_ref, wo_ref, g1_ref, n2g_ref, sc2_ref, sh2_ref,
                wrt_ref, xo_ref, xn2_ref, pt_ref):
    ga = jax.nn.sigmoid(hg_ref[:, 0:D_MODEL])
    gb = jax.nn.sigmoid(hg_ref[:, D_MODEL:2 * D_MODEL])
    gc = jax.nn.sigmoid(hg_ref[:, 2 * D_MODEL:3 * D_MODEL])
    pa = jnp.dot(oab_ref[:, 0:A_Q], wa_ref[...], preferred_element_type=F32)
    pb = jnp.dot(oab_ref[:, A_Q:A_Q + B_W], wb_ref[...], preferred_element_type=F32)
    pc = jnp.dot(oc_ref[...], wc_ref[...], preferred_element_type=F32)
    merged = ga * pa + gb * pb + gc * pc
    xo = x_ref[...] + g1_ref[...] * jnp.dot(merged.astype(BF16), wo_ref[...], preferred_element_type=F32)
    xo_ref[...] = xo
    y = xo * lax.rsqrt(jnp.mean(xo * xo, axis=-1, keepdims=True) + EPS) * n2g_ref[...]
    xn2 = y * (1.0 + sc2_ref[...]) + sh2_ref[...]
    xn2_ref[...] = xn2
    logits_t = lax.dot_general(wrt_ref[...], xn2.astype(BF16), _NT, preferred_element_type=F32)
    z = jnp.exp(logits_t - logits_t.max(axis=0, keepdims=True))
    pt_ref[...] = z / z.sum(axis=0, keepdims=True)


def _mix(x, oab, oc, h_g, wa, wb, wc, wo, g1, n2g, sc2, sh2, wrt, seq):
    n = x.shape[0]
    tm = 512
    per_batch = g1.shape[0] > 1
    mod_map = (lambda i: ((i * tm) // seq, 0, 0)) if per_batch else (lambda i: (0, 0, 0))
    mod_spec = pl.BlockSpec((None, 1, D_MODEL), mod_map)
    full = lambda a: pl.BlockSpec(a.shape, lambda i: (0,) * a.ndim)
    row = lambda w: pl.BlockSpec((tm, w), lambda i: (i, 0))
    return pl.pallas_call(
        _mix_kernel,
        grid=(n // tm,),
        in_specs=[row(D_MODEL), row(A_Q + B_W), row(C_WIDTH), row(GATE_W), full(wa), full(wb), full(wc), full(wo),
                  mod_spec, full(n2g), mod_spec, mod_spec, full(wrt)],
        out_specs=[row(D_MODEL), row(D_MODEL), pl.BlockSpec((N_EXPERTS, tm), lambda i: (0, i))],
        out_shape=[jax.ShapeDtypeStruct((n, D_MODEL), F32), jax.ShapeDtypeStruct((n, D_MODEL), F32),
                   jax.ShapeDtypeStruct((N_EXPERTS, n), F32)],
        compiler_params=_params(("arbitrary",), 56),
        name="mix",
    )(x, oab, oc, h_g, wa, wb, wc, wo, g1, n2g, sc2, sh2, wrt)


MOE_TF = 512
MOE_ROW_CHUNKS = ((0, 512), (512, 512), (1024, 256))


MOE_ISSUE_UNROLL = 8


def _moe_issue_rows(idx_ref, idx0, src_hbm, dst_ref, dst_row0, count, sem):
    def body(i, carry):
        for u in range(MOE_ISSUE_UNROLL):
            r = i * MOE_ISSUE_UNROLL + u
            tok = idx_ref[idx0 + r]
            pltpu.make_async_copy(src_hbm.at[pl.ds(tok, 1)], dst_ref.at[pl.ds(dst_row0 + r, 1)], sem).start()
        return carry

    lax.fori_loop(0, count // MOE_ISSUE_UNROLL, body, 0)


def _moe_kernel(idx_ref, xc_hbm, xl_hbm, aff_ref, wg_ref, wu_ref, wd_ref, ye_ref, xe_ref, sem):
    e = pl.program_id(0)
    f = pl.program_id(1)
    slot = e % 2

    def gather(expert, s):
        base = expert * CAP_ALL
        _moe_issue_rows(idx_ref, base, xc_hbm, xe_ref.at[s], 0, CAP_CTX, sem.at[s])
        _moe_issue_rows(idx_ref, base + CAP_CTX, xl_hbm, xe_ref.at[s], CAP_CTX, CAP_LAT, sem.at[s])

    @pl.when(f == 0)
    def _():
        @pl.when(e == 0)
        def _():
            gather(0, 0)

        pltpu.make_async_copy(xc_hbm.at[pl.ds(0, CAP_ALL)], xe_ref.at[slot], sem.at[slot]).wait()

        @pl.when(e + 1 < N_EXPERTS)
        def _():
            gather(e + 1, 1 - slot)

    wg = wg_ref[...].astype(BF16)
    wu = wu_ref[...].astype(BF16)
    wd = wd_ref[...].astype(BF16)
    for r0, rn in MOE_ROW_CHUNKS:
        xe = xe_ref[slot, r0:r0 + rn, :].astype(BF16)
        hg = jnp.dot(xe, wg, preferred_element_type=F32)
        hu = jnp.dot(xe, wu, preferred_element_type=F32)
        act = (hg * jax.nn.sigmoid(hg) * hu).astype(BF16)
        part = jnp.dot(act, wd, preferred_element_type=F32) * aff_ref[r0:r0 + rn, :]

        @pl.when(f == 0)
        def _():
            ye_ref[r0:r0 + rn, :] = part

        @pl.when(f != 0)
        def _():
            ye_ref[r0:r0 + rn, :] += part


def _moe(idx_all, xn2_ctx, xn2_lat, aff_all, w_gate_e, w_up_e, w_down_e, layer):
    nf = EXPERT_FF // MOE_TF
    grid_spec = pltpu.PrefetchScalarGridSpec(
        num_scalar_prefetch=1,
        grid=(N_EXPERTS, nf),
        in_specs=[
            pl.BlockSpec(memory_space=pl.ANY),
            pl.BlockSpec(memory_space=pl.ANY),
            pl.BlockSpec((None, CAP_ALL, 1), lambda e, f, idx: (e, 0, 0)),
            pl.BlockSpec((None, None, D_MODEL, MOE_TF), lambda e, f, idx: (layer, e, 0, f)),
            pl.BlockSpec((None, None, D_MODEL, MOE_TF), lambda e, f, idx: (layer, e, 0, f)),
            pl.BlockSpec((None, None, MOE_TF, D_MODEL), lambda e, f, idx: (layer, e, f, 0)),
        ],
        out_specs=pl.BlockSpec((None, CAP_ALL, D_MODEL), lambda e, f, idx: (e, 0, 0)),
        scratch_shapes=[pltpu.VMEM((2, CAP_ALL, D_MODEL), F32), pltpu.SemaphoreType.DMA((2,))],
    )
    return pl.pallas_call(
        _moe_kernel,
        grid_spec=grid_spec,
        out_shape=jax.ShapeDtypeStruct((N_EXPERTS, CAP_ALL, D_MODEL), F32),
        compiler_params=_params(("arbitrary", "arbitrary"), 56),
        name="moe_experts",
    )(idx_all, xn2_ctx, xn2_lat, aff_all, w_gate_e, w_up_e, w_down_e)


def _route(probs_t, cap):
    aff, idx = lax.top_k(probs_t, cap)
    order = jnp.argsort(idx, axis=-1)
    return jnp.take_along_axis(idx, order, axis=-1), jnp.take_along_axis(aff, order, axis=-1)


def kernel(x_prompt, x_sample, cache_attn_k, cache_attn_v, cache_na_k, cache_na_v, c, c_ctx, w_ada, b_ada, norm1_g,
           w_in, q_norm_g, k_norm_g, na_rpb, conv_w, w_proj_a, w_proj_b, w_proj_c, w_out, norm2_g, w_router,
           w_gate_e, w_up_e, w_down_e, final_g):
    cond8 = jnp.zeros((8, D_MODEL), F32).at[0].set(c_ctx).at[1:1 + DEC_BATCH].set(c)
    mods = _adaln(cond8, w_ada, b_ada).reshape(DEPTH, 8, N_MOD, D_MODEL)
    cos128, sin128 = _rope_tables()

    hp = x_prompt.reshape(N_CTX, D_MODEL)
    hs = x_sample.reshape(N_LAT, D_MODEL)
    new_ak, new_av, new_nk, new_nv = [], [], [], []
    for l in range(DEPTH):
        mod_ctx = [mods[l, 0:1, i][:, None, :] for i in range(N_MOD)]
        mod_lat = [mods[l, 1:1 + DEC_BATCH, i][:, None, :] for i in range(N_MOD)]
        n1g = norm1_g[l][None]
        n2g = norm2_g[l][None]
        qg = q_norm_g[l][None]
        kg = k_norm_g[l][None]
        wa, wb, wc, wo = (w_proj_a[l].astype(BF16), w_proj_b[l].astype(BF16), w_proj_c[l].astype(BF16),
                          w_out[l].astype(BF16))
        wrt = w_router[l].T.astype(BF16)
        bias = _na_bias(na_rpb[l])

        def sublayer1(x, mod, seq, is_ctx):
            sh1, sc1, g1, sh2, sc2, _ = mod
            xn = _norm_mod(x, n1g, sc1, sh1, seq, BF16)
            h_a = _inproj(xn, w_in, l, COL_A, W_A, 768, F32)
            h_b = _inproj(xn, w_in, l, COL_B, W_B, 768, F32)
            h_c = _inproj(xn, w_in, l, COL_C, W_C, 768, F32)
            h_g = _inproj(xn, w_in, l, COL_G, GATE_W, 768, F32)
            kv = None
            if is_ctx:
                oab, ka, va, kb, vb = _attn_ctx(h_a, h_b, jnp.tile(qg, (1, 2)), jnp.tile(kg, (1, 2)))
                kv = (ka, va, kb, vb)
            else:
                oab = _attn_lat(h_a, h_b, cache_attn_k, cache_attn_v, cache_na_k, cache_na_v, l, qg, kg,
                                cos128, sin128, bias)
            oc = _conv(h_c, conv_w[l], seq)
            xo, xn2, probs_t = _mix(x, oab, oc, h_g, wa, wb, wc, wo, g1, n2g, sc2, sh2, wrt, seq)
            return xo, xn2, probs_t, kv

        hp, xn2_c, pt_c, (ka, va, kb, vb) = sublayer1(hp, mod_ctx, SEQ, True)
        hs, xn2_l, pt_l, _ = sublayer1(hs, mod_lat, DEC_SEQ, False)
        new_ak.append(ka)
        new_av.append(va)
        new_nk.append(kb)
        new_nv.append(vb)

        idx_c, aff_c = _route(pt_c, CAP_CTX)
        idx_l, aff_l = _route(pt_l, CAP_LAT)
        idx_all = jnp.concatenate([idx_c, idx_l], axis=1).reshape(-1).astype(jnp.int32)
        aff_all = jnp.concatenate([aff_c, aff_l], axis=1)[..., None]
        ye = _moe(idx_all, xn2_c, xn2_l, aff_all, w_gate_e, w_up_e, w_down_e, l)
        y_c = jnp.zeros((N_CTX, D_MODEL), F32).at[idx_c.reshape(-1)].add(ye[:, :CAP_CTX].reshape(-1, D_MODEL))
        y_l = jnp.zeros((N_LAT, D_MODEL), F32).at[idx_l.reshape(-1)].add(ye[:, CAP_CTX:].reshape(-1, D_MODEL))
        hp = hp + (mod_ctx[5] * y_c.reshape(1, N_CTX, D_MODEL)).reshape(N_CTX, D_MODEL)
        hs = hs + (mod_lat[5] * y_l.reshape(DEC_BATCH, DEC_SEQ, D_MODEL)).reshape(N_LAT, D_MODEL)

    fg = final_g[None]
    y_prompt = _final_norm(hp, fg).reshape(BATCH, SEQ, D_MODEL)
    y_sample = _final_norm(hs, fg).reshape(DEC_BATCH, DEC_SEQ, D_MODEL)
    return (y_prompt, y_sample, jnp.stack(new_ak, axis=1), jnp.stack(new_av, axis=1),
            jnp.stack(new_nk, axis=1), jnp.stack(new_nv, axis=1))
```

```python
import functools

import jax
import jax.numpy as jnp
import numpy as np
from jax import lax
from jax.experimental import pallas as pl
from jax.experimental.pallas import tpu as pltpu

D_MODEL = 1024
BATCH = 32
SEQ = 256
DEPTH = 2
DEC_BATCH = 2
DEC_SEQ = 1024
PAST_LEN = 256
GRID_W = 64
HEAD_DIM = 64
A_HEADS = 8
A_KV_HEADS = 2
A_GROUP = A_HEADS // A_KV_HEADS
B_HEADS = 8
C_WIDTH = 512
NA_WIN_R = 8
NA_WIN_C = 16
N_EXPERTS = 16
EXPERT_FF = 1024
CAP_FACTOR = 2
ROPE_THETA = 10000.0
EPS = 1e-6
N_MOD = 6

A_Q = A_HEADS * HEAD_DIM
A_KV = A_KV_HEADS * HEAD_DIM
B_W = B_HEADS * HEAD_DIM
GATE_W = 3 * D_MODEL
COL_A = 0
W_A = A_Q + 2 * A_KV
COL_B = COL_A + W_A
W_B = 3 * B_W
COL_C = COL_B + W_B
W_C = 3 * C_WIDTH
COL_G = COL_C + W_C

N_CTX = BATCH * SEQ
N_LAT = DEC_BATCH * DEC_SEQ
CAP_CTX = CAP_FACTOR * N_CTX // N_EXPERTS
CAP_LAT = CAP_FACTOR * N_LAT // N_EXPERTS
CAP_ALL = CAP_CTX + CAP_LAT
LAT_ROWS = DEC_SEQ // GRID_W
NA_NK = NA_WIN_R * GRID_W
MASK_NEG = -1e30
ATT_SCALE = HEAD_DIM ** -0.5

F32 = jnp.float32
BF16 = jnp.bfloat16
_NT = (((1,), (1,)), ((), ()))

_MIB = 1024 * 1024


def _params(sem, vmem_mib):
    return pltpu.CompilerParams(dimension_semantics=sem, vmem_limit_bytes=vmem_mib * _MIB)


def _adaln_kernel(cond_ref, w_ref, b_ref, o_ref):
    c = cond_ref[...]
    s = c * jax.nn.sigmoid(c)
    o_ref[...] = jnp.dot(s.astype(BF16), w_ref[...].astype(BF16), preferred_element_type=F32) + b_ref[...]


def _adaln(cond8, w_ada, b_ada):
    tn = 1536
    ncol = N_MOD * D_MODEL
    return pl.pallas_call(
        _adaln_kernel,
        grid=(DEPTH, ncol // tn),
        in_specs=[
            pl.BlockSpec((8, D_MODEL), lambda l, j: (0, 0)),
            pl.BlockSpec((None, D_MODEL, tn), lambda l, j: (l, 0, j)),
            pl.BlockSpec((None, 1, tn), lambda l, j: (l, 0, j)),
        ],
        out_specs=pl.BlockSpec((None, 8, tn), lambda l, j: (l, 0, j)),
        out_shape=jax.ShapeDtypeStruct((DEPTH, 8, ncol), F32),
        compiler_params=_params(("arbitrary", "arbitrary"), 40),
        name="adaln",
    )(cond8, w_ada, b_ada.reshape(DEPTH, 1, ncol))


def _norm_mod_kernel(x_ref, g_ref, sc_ref, sh_ref, o_ref):
    x = x_ref[...]
    y = x * lax.rsqrt(jnp.mean(x * x, axis=-1, keepdims=True) + EPS) * g_ref[...]
    o_ref[...] = (y * (1.0 + sc_ref[...]) + sh_ref[...]).astype(o_ref.dtype)


def _norm_mod(x, g, sc, sh, seq, out_dtype):
    n = x.shape[0]
    tm = 512
    per_batch = sc.shape[0] > 1
    mod_map = (lambda i: ((i * tm) // seq, 0, 0)) if per_batch else (lambda i: (0, 0, 0))
    return pl.pallas_call(
        _norm_mod_kernel,
        grid=(n // tm,),
        in_specs=[
            pl.BlockSpec((tm, D_MODEL), lambda i: (i, 0)),
            pl.BlockSpec((1, D_MODEL), lambda i: (0, 0)),
            pl.BlockSpec((None, 1, D_MODEL), mod_map),
            pl.BlockSpec((None, 1, D_MODEL), mod_map),
        ],
        out_specs=pl.BlockSpec((tm, D_MODEL), lambda i: (i, 0)),
        out_shape=jax.ShapeDtypeStruct((n, D_MODEL), out_dtype),
        compiler_params=_params(("arbitrary",), 32),
        name="norm_mod",
    )(x, g, sc, sh)


def _norm_kernel(x_ref, g_ref, o_ref):
    x = x_ref[...]
    o_ref[...] = x * lax.rsqrt(jnp.mean(x * x, axis=-1, keepdims=True) + EPS) * g_ref[...]


def _final_norm(x, g):
    n = x.shape[0]
    tm = 512
    return pl.pallas_call(
        _norm_kernel,
        grid=(n // tm,),
        in_specs=[pl.BlockSpec((tm, D_MODEL), lambda i: (i, 0)), pl.BlockSpec((1, D_MODEL), lambda i: (0, 0))],
        out_specs=pl.BlockSpec((tm, D_MODEL), lambda i: (i, 0)),
        out_shape=jax.ShapeDtypeStruct((n, D_MODEL), F32),
        compiler_params=_params(("arbitrary",), 32),
        name="final_norm",
    )(x, g)


def _inproj_kernel(a_ref, w_ref, o_ref, wbf_ref):
    @pl.when(pl.program_id(1) == 0)
    def _():
        wbf_ref[...] = w_ref[...].astype(BF16)

    o_ref[...] = jnp.dot(a_ref[...], wbf_ref[...], preferred_element_type=F32).astype(o_ref.dtype)


def _inproj(xn, w_in, layer, col0, width, tn, out_dtype):
    n = xn.shape[0]
    tm = 2048
    joff = col0 // tn
    assert col0 % tn == 0 and width % tn == 0
    return pl.pallas_call(
        _inproj_kernel,
        grid=(width // tn, n // tm),
        in_specs=[
            pl.BlockSpec((tm, D_MODEL), lambda j, i: (i, 0)),
            pl.BlockSpec((None, D_MODEL, tn), lambda j, i: (layer, 0, j + joff)),
        ],
        out_specs=pl.BlockSpec((tm, tn), lambda j, i: (i, j)),
        out_shape=jax.ShapeDtypeStruct((n, width), out_dtype),
        scratch_shapes=[pltpu.VMEM((D_MODEL, tn), BF16)],
        compiler_params=_params(("arbitrary", "arbitrary"), 48),
        name="inproj",
    )(xn, w_in)


def _rms_head(x, g):
    return x * lax.rsqrt(jnp.mean(x * x, axis=-1, keepdims=True) + EPS) * g


def _softmax_pv(scores, values):
    m = scores[0].max(axis=-1, keepdims=True)
    for s in scores[1:]:
        m = jnp.maximum(m, s.max(axis=-1, keepdims=True))
    acc = None
    den = None
    for s, v in zip(scores, values):
        p = jnp.exp(s - m)
        d = p.sum(axis=-1, keepdims=True)
        o = jnp.dot(p.astype(BF16), v, preferred_element_type=F32)
        acc = o if acc is None else acc + o
        den = d if den is None else den + d
    return acc / den


def _qk(q, k):
    return lax.dot_general(q, k, _NT, preferred_element_type=F32)


CTX_NB = 2
PAIR = 2 * HEAD_DIM


def _pair_consts():
    lane = lax.broadcasted_iota(jnp.int32, (1, PAIR), 1)
    is_lo = lane < HEAD_DIM
    r = lax.broadcasted_iota(jnp.int32, (PAIR, PAIR), 0)
    c = lax.broadcasted_iota(jnp.int32, (PAIR, PAIR), 1)
    half_ones = jnp.where((r < HEAD_DIM) == (c < HEAD_DIM), 1.0, 0.0).astype(BF16)
    return is_lo, half_ones


def _pair_mean_sq(x, half_ones):
    x2 = x * x
    hi = x2.astype(BF16)
    lo = (x2 - hi.astype(F32)).astype(BF16)
    tot = (jnp.dot(hi, half_ones, preferred_element_type=F32) + jnp.dot(lo, half_ones, preferred_element_type=F32))
    return tot * (1.0 / HEAD_DIM)


def _pair_rms(x, g2, half_ones):
    return x * lax.rsqrt(_pair_mean_sq(x, half_ones) + EPS) * g2


def _softmax_pv_pair(scores, values, ones_v):
    m = scores[0].max(axis=-1, keepdims=True)
    for s in scores[1:]:
        m = jnp.maximum(m, s.max(axis=-1, keepdims=True))
    acc = None
    den = None
    for s, v, ones in zip(scores, values, ones_v):
        p = jnp.exp(s - m).astype(BF16)
        o = jnp.dot(p, v, preferred_element_type=F32)
        d = jnp.dot(p, ones, preferred_element_type=F32)
        acc = o if acc is None else acc + o
        den = d if den is None else den + d
    return acc / den


ATT_RB = 128


def _attend_pair(q, keys, values, ones_v):
    outs = []
    for r0 in range(0, q.shape[0], ATT_RB):
        qb = q[r0:r0 + ATT_RB]
        outs.append(_softmax_pv_pair([_qk(qb, k) for k in keys], values, ones_v))
    return outs[0] if len(outs) == 1 else jnp.concatenate(outs, axis=0)


def _attn_ctx_kernel(ha_ref, hb_ref, qg_ref, kg_ref, oab_ref, ak_ref, av_ref, nk_ref, nv_ref):
    is_lo, half_ones = _pair_consts()
    ones_v = jnp.ones((SEQ, PAIR), BF16)
    qg2 = qg_ref[...] * ATT_SCALE
    kg2 = kg_ref[...]
    for b in range(CTX_NB):
        rows = slice(b * SEQ, (b + 1) * SEQ)
        kp = _pair_rms(ha_ref[rows, A_Q:A_Q + PAIR], kg2, half_ones)
        vp = ha_ref[rows, A_Q + A_KV:A_Q + A_KV + PAIR]
        for kv in range(A_KV_HEADS):
            ak_ref[b, kv] = kp[:, kv * HEAD_DIM:(kv + 1) * HEAD_DIM]
            av_ref[b, kv] = vp[:, kv * HEAD_DIM:(kv + 1) * HEAD_DIM]
        kp_sw = pltpu.roll(kp, HEAD_DIM, axis=1)
        vp_sw = pltpu.roll(vp, HEAD_DIM, axis=1)
        qs = [_pair_rms(ha_ref[rows, c * PAIR:(c + 1) * PAIR], qg2, half_ones).astype(BF16)
              for c in range(A_HEADS // 2)]
        for kv in range(A_KV_HEADS):
            k_src_lo, k_src_hi = (kp, kp_sw) if kv == 0 else (kp_sw, kp)
            v_src_lo, v_src_hi = (vp, vp_sw) if kv == 0 else (vp_sw, vp)
            k_lo = jnp.where(is_lo, k_src_lo, 0.0).astype(BF16)
            k_hi = jnp.where(is_lo, 0.0, k_src_hi).astype(BF16)
            v_lo = jnp.where(is_lo, v_src_lo, 0.0).astype(BF16)
            v_hi = jnp.where(is_lo, 0.0, v_src_hi).astype(BF16)
            q_st = jnp.concatenate([qs[2 * kv], qs[2 * kv + 1]], axis=0)
            o = _attend_pair(q_st, [k_lo], [v_lo], [ones_v]) + _attend_pair(q_st, [k_hi], [v_hi], [ones_v])
            oab_ref[rows, (2 * kv) * PAIR:(2 * kv + 1) * PAIR] = o[0:SEQ].astype(BF16)
            oab_ref[rows, (2 * kv + 1) * PAIR:(2 * kv + 2) * PAIR] = o[SEQ:2 * SEQ].astype(BF16)
        for j in range(B_HEADS // 2):
            qp = (hb_ref[rows, j * PAIR:(j + 1) * PAIR] * ATT_SCALE).astype(BF16)
            kp_b = hb_ref[rows, B_W + j * PAIR:B_W + (j + 1) * PAIR]
            vp_b = hb_ref[rows, 2 * B_W + j * PAIR:2 * B_W + (j + 1) * PAIR]
            for u in range(2):
                nk_ref[b, 2 * j + u] = kp_b[:, u * HEAD_DIM:(u + 1) * HEAD_DIM]
                nv_ref[b, 2 * j + u] = vp_b[:, u * HEAD_DIM:(u + 1) * HEAD_DIM]
            k_lo = jnp.where(is_lo, kp_b, 0.0).astype(BF16)
            k_hi = jnp.where(is_lo, 0.0, kp_b).astype(BF16)
            v_lo = jnp.where(is_lo, vp_b, 0.0).astype(BF16)
            v_hi = jnp.where(is_lo, 0.0, vp_b).astype(BF16)
            o = _attend_pair(qp, [k_lo], [v_lo], [ones_v]) + _attend_pair(qp, [k_hi], [v_hi], [ones_v])
            oab_ref[rows, A_Q + j * PAIR:A_Q + (j + 1) * PAIR] = o.astype(BF16)


def _attn_ctx(h_a, h_b, qg2, kg2):
    rows = CTX_NB * SEQ
    kv_spec = lambda nh: pl.BlockSpec((CTX_NB, nh, SEQ, HEAD_DIM), lambda b: (b, 0, 0, 0))
    kv_shape = lambda nh: jax.ShapeDtypeStruct((BATCH, nh, SEQ, HEAD_DIM), F32)
    return pl.pallas_call(
        _attn_ctx_kernel,
        grid=(BATCH // CTX_NB,),
        in_specs=[
            pl.BlockSpec((rows, W_A), lambda b: (b, 0)),
            pl.BlockSpec((rows, W_B), lambda b: (b, 0)),
            pl.BlockSpec((1, PAIR), lambda b: (0, 0)),
            pl.BlockSpec((1, PAIR), lambda b: (0, 0)),
        ],
        out_specs=[
            pl.BlockSpec((rows, A_Q + B_W), lambda b: (b, 0)),
            kv_spec(A_KV_HEADS), kv_spec(A_KV_HEADS), kv_spec(B_HEADS), kv_spec(B_HEADS),
        ],
        out_shape=[
            jax.ShapeDtypeStruct((N_CTX, A_Q + B_W), BF16),
            kv_shape(A_KV_HEADS), kv_shape(A_KV_HEADS), kv_shape(B_HEADS), kv_shape(B_HEADS),
        ],
        compiler_params=_params(("arbitrary",), 40),
        name="attn_ctx",
    )(h_a, h_b, qg2, kg2)


LAT_TQ = 256
LAT_QROWS = LAT_TQ // GRID_W


def _rope(x, cos, sin):
    lane = lax.broadcasted_iota(jnp.int32, x.shape, 1)
    nxt = pltpu.roll(x, x.shape[1] - 1, axis=1)
    prv = pltpu.roll(x, 1, axis=1)
    partner = jnp.where((lane & 1) == 0, nxt, prv)
    return x * cos + partner * sin


def _attn_lat_kernel(haq_ref, haf_ref, hbq_ref, hbf_ref, cak_ref, cav_ref, cnk_ref, cnv_ref,
                     qg_ref, kg_ref, cos_ref, sin_ref, bias_ref, oab_ref, qn_ref, kn_ref):
    t = pl.program_id(1)
    row0 = pl.multiple_of(t * LAT_TQ, LAT_TQ)
    qg = qg_ref[...]
    kg = kg_ref[...]

    for hq in range(A_HEADS):
        qn_ref[:, hq * HEAD_DIM:(hq + 1) * HEAD_DIM] = _rms_head(
            haq_ref[:, hq * HEAD_DIM:(hq + 1) * HEAD_DIM], qg) * ATT_SCALE
    for kv in range(A_KV_HEADS):
        kn_ref[:, kv * HEAD_DIM:(kv + 1) * HEAD_DIM] = _rms_head(
            haf_ref[:, A_Q + kv * HEAD_DIM:A_Q + (kv + 1) * HEAD_DIM], kg)
    cos_q = cos_ref[pl.ds(row0, LAT_TQ), :]
    sin_q = sin_ref[pl.ds(row0, LAT_TQ), :]
    for c in range(A_Q // 128):
        qn_ref[:, c * 128:(c + 1) * 128] = _rope(qn_ref[:, c * 128:(c + 1) * 128], cos_q, sin_q)
    kn_ref[...] = _rope(kn_ref[...], cos_ref[...], sin_ref[...])

    for kv in range(A_KV_HEADS):
        k_lat = kn_ref[:, kv * HEAD_DIM:(kv + 1) * HEAD_DIM].astype(BF16)
        v_lat = haf_ref[:, A_Q + A_KV + kv * HEAD_DIM:A_Q + A_KV + (kv + 1) * HEAD_DIM].astype(BF16)
        k_ctx = cak_ref[kv].astype(BF16)
        v_ctx = cav_ref[kv].astype(BF16)
        for g in range(A_GROUP):
            hq = kv * A_GROUP + g
            q = qn_ref[:, hq * HEAD_DIM:(hq + 1) * HEAD_DIM].astype(BF16)
            o = _softmax_pv([_qk(q, k_ctx), _qk(q, k_lat)], [v_ctx, v_lat])
            oab_ref[:, hq * HEAD_DIM:(hq + 1) * HEAD_DIM] = o.astype(BF16)

    for h in range(B_HEADS):
        k_ctx = cnk_ref[h].astype(BF16)
        v_ctx = cnv_ref[h].astype(BF16)
        for i in range(LAT_QROWS):
            r = t * LAT_QROWS + i
            r0 = jnp.clip(r - NA_WIN_R // 2, 0, LAT_ROWS - NA_WIN_R)
            k0 = pl.multiple_of(r0 * GRID_W, GRID_W)
            q = (hbq_ref[i * GRID_W:(i + 1) * GRID_W, h * HEAD_DIM:(h + 1) * HEAD_DIM] * ATT_SCALE).astype(BF16)
            k_win = hbf_ref[pl.ds(k0, NA_NK), B_W + h * HEAD_DIM:B_W + (h + 1) * HEAD_DIM].astype(BF16)
            v_win = hbf_ref[pl.ds(k0, NA_NK), 2 * B_W + h * HEAD_DIM:2 * B_W + (h + 1) * HEAD_DIM].astype(BF16)
            s_nb = _qk(q, k_win) + bias_ref[h, i]
            o = _softmax_pv([_qk(q, k_ctx), s_nb], [v_ctx, v_win])
            oab_ref[i * GRID_W:(i + 1) * GRID_W, A_Q + h * HEAD_DIM:A_Q + (h + 1) * HEAD_DIM] = o.astype(BF16)


def _attn_lat(h_a, h_b, cak, cav, cnk, cnv, layer, qg, kg, cos128, sin128, bias):
    nt = DEC_SEQ // LAT_TQ
    cache_spec = lambda nh: pl.BlockSpec((None, None, nh, PAST_LEN, HEAD_DIM), lambda b, t: (b, layer, 0, 0, 0))
    return pl.pallas_call(
        _attn_lat_kernel,
        grid=(DEC_BATCH, nt),
        in_specs=[
            pl.BlockSpec((LAT_TQ, W_A), lambda b, t: (b * nt + t, 0)),
            pl.BlockSpec((DEC_SEQ, W_A), lambda b, t: (b, 0)),
            pl.BlockSpec((LAT_TQ, W_B), lambda b, t: (b * nt + t, 0)),
            pl.BlockSpec((DEC_SEQ, W_B), lambda b, t: (b, 0)),
            cache_spec(A_KV_HEADS), cache_spec(A_KV_HEADS), cache_spec(B_HEADS), cache_spec(B_HEADS),
            pl.BlockSpec((1, HEAD_DIM), lambda b, t: (0, 0)),
            pl.BlockSpec((1, HEAD_DIM), lambda b, t: (0, 0)),
            pl.BlockSpec((DEC_SEQ, 128), lambda b, t: (0, 0)),
            pl.BlockSpec((DEC_SEQ, 128), lambda b, t: (0, 0)),
            pl.BlockSpec((B_HEADS, LAT_QROWS, GRID_W, NA_NK), lambda b, t: (0, t, 0, 0)),
        ],
        out_specs=pl.BlockSpec((LAT_TQ, A_Q + B_W), lambda b, t: (b * nt + t, 0)),
        out_shape=jax.ShapeDtypeStruct((N_LAT, A_Q + B_W), BF16),
        scratch_shapes=[pltpu.VMEM((LAT_TQ, A_Q), F32), pltpu.VMEM((DEC_SEQ, A_KV), F32)],
        compiler_params=_params(("arbitrary", "arbitrary"), 60),
        name="attn_lat",
    )(h_a, h_a, h_b, h_b, cak, cav, cnk, cnv, qg, kg, cos128, sin128, bias)


def _rope_tables():
    t = jnp.arange(DEC_SEQ)
    row = (t // GRID_W).astype(F32)
    col = (t % GRID_W).astype(F32)
    n_freq = HEAD_DIM // 4
    inv = ROPE_THETA ** (-jnp.arange(n_freq, dtype=F32) / n_freq)
    ang = jnp.concatenate([row[:, None] * inv, col[:, None] * inv], axis=-1)
    cos = jnp.repeat(jnp.cos(ang), 2, axis=-1)
    sign = jnp.tile(jnp.array([-1.0, 1.0], F32), HEAD_DIM // 2)
    sin = jnp.repeat(jnp.sin(ang), 2, axis=-1) * sign
    return jnp.tile(cos, (1, 2)), jnp.tile(sin, (1, 2))


def _na_bias(rpb):
    col = np.arange(GRID_W)
    dc = np.clip(col[None, :] - col[:, None] + NA_WIN_C - 1, 0, 2 * NA_WIN_C - 2)
    onehot = (dc[..., None] == np.arange(2 * NA_WIN_C - 1)).astype(np.float32)
    c0 = np.clip(col - NA_WIN_C // 2, 0, GRID_W - NA_WIN_C)
    in_win = (col[None, :] >= c0[:, None]) & (col[None, :] < c0[:, None] + NA_WIN_C)
    toep = jnp.einsum('hdm,ckm->hdck', rpb, onehot, precision=lax.Precision.HIGHEST)
    toep = jnp.where(in_win[None, None], toep, MASK_NEG)
    r0 = np.clip(np.arange(LAT_ROWS) - NA_WIN_R // 2, 0, LAT_ROWS - NA_WIN_R)
    rows = [jnp.concatenate([toep[:, r0[r] + j - r + NA_WIN_R - 1] for j in range(NA_WIN_R)], axis=-1)
            for r in range(LAT_ROWS)]
    return jnp.stack(rows, axis=1)


def _conv_kernel(hc_ref, w_ref, o_ref, *, seq):
    bg = hc_ref[:, 0:C_WIDTH].astype(F32)
    u = hc_ref[:, C_WIDTH:2 * C_WIDTH].astype(F32) * hc_ref[:, 2 * C_WIDTH:3 * C_WIDTH].astype(F32)
    rows = u.shape[0]
    assert seq & (seq - 1) == 0
    pos = lax.broadcasted_iota(jnp.int32, u.shape, 0) & (seq - 1)
    u_prev = jnp.where(pos == 0, 0.0, pltpu.roll(u, 1, axis=0))
    u_next = jnp.where(pos == seq - 1, 0.0, pltpu.roll(u, rows - 1, axis=0))
    y = u_prev * w_ref[0:1, :] + u * w_ref[1:2, :] + u_next * w_ref[2:3, :]
    o_ref[...] = (bg * y).astype(o_ref.dtype)


def _conv(h_c, conv_w, seq):
    n = h_c.shape[0]
    tm = 1024
    return pl.pallas_call(
        functools.partial(_conv_kernel, seq=seq),
        grid=(n // tm,),
        in_specs=[pl.BlockSpec((tm, W_C), lambda i: (i, 0)), pl.BlockSpec((3, C_WIDTH), lambda i: (0, 0))],
        out_specs=pl.BlockSpec((tm, C_WIDTH), lambda i: (i, 0)),
        out_shape=jax.ShapeDtypeStruct((n, C_WIDTH), BF16),
        compiler_params=_params(("arbitrary",), 40),
        name="short_conv",
    )(h_c, conv_w)


def _mix_kernel(x_ref, oab_ref, oc_ref, hg_ref, wa_ref, wb_ref, wc_ref, wo_ref, g1_ref, n2g_ref, sc2_ref, sh2_ref,
                wrt_ref, xo_ref, xn2_ref, pt_ref):
    ga = jax.nn.sigmoid(hg_ref[:, 0:D_MODEL].astype(F32))
    gb = jax.nn.sigmoid(hg_ref[:, D_MODEL:2 * D_MODEL].astype(F32))
    gc = jax.nn.sigmoid(hg_ref[:, 2 * D_MODEL:3 * D_MODEL].astype(F32))
    pa = jnp.dot(oab_ref[:, 0:A_Q], wa_ref[...], preferred_element_type=F32)
    pb = jnp.dot(oab_ref[:, A_Q:A_Q + B_W], wb_ref[...], preferred_element_type=F32)
    pc = jnp.dot(oc_ref[...], wc_ref[...], preferred_element_type=F32)
    merged = ga * pa + gb * pb + gc * pc
    xo = x_ref[...] + g1_ref[...] * jnp.dot(merged.astype(BF16), wo_ref[...], preferred_element_type=F32)
    xo_ref[...] = xo
    y = xo * lax.rsqrt(jnp.mean(xo * xo, axis=-1, keepdims=True) + EPS) * n2g_ref[...]
    xn2 = y * (1.0 + sc2_ref[...]) + sh2_ref[...]
    xn2_ref[...] = xn2
    logits_t = lax.dot_general(wrt_ref[...], xn2.astype(BF16), _NT, preferred_element_type=F32)
    z = jnp.exp(logits_t - logits_t.max(axis=0, keepdims=True))
    pt_ref[...] = z / z.sum(axis=0, keepdims=True)


def _mix(x, oab, oc, h_g, wa, wb, wc, wo, g1, n2g, sc2, sh2, wrt, seq):
    n = x.shape[0]
    tm = 512
    per_batch = g1.shape[0] > 1
    mod_map = (lambda i: ((i * tm) // seq, 0, 0)) if per_batch else (lambda i: (0, 0, 0))
    mod_spec = pl.BlockSpec((None, 1, D_MODEL), mod_map)
    full = lambda a: pl.BlockSpec(a.shape, lambda i: (0,) * a.ndim)
    row = lambda w: pl.BlockSpec((tm, w), lambda i: (i, 0))
    return pl.pallas_call(
        _mix_kernel,
        grid=(n // tm,),
        in_specs=[row(D_MODEL), row(A_Q + B_W), row(C_WIDTH), row(GATE_W), full(wa), full(wb), full(wc), full(wo),
                  mod_spec, full(n2g), mod_spec, mod_spec, full(wrt)],
        out_specs=[row(D_MODEL), row(D_MODEL), pl.BlockSpec((N_EXPERTS, tm), lambda i: (0, i))],
        out_shape=[jax.ShapeDtypeStruct((n, D_MODEL), F32), jax.ShapeDtypeStruct((n, D_MODEL), F32),
                   jax.ShapeDtypeStruct((N_EXPERTS, n), F32)],
        compiler_params=_params(("arbitrary",), 56),
        name="mix",
    )(x, oab, oc, h_g, wa, wb, wc, wo, g1, n2g, sc2, sh2, wrt)


MOE_TF = 512
MOE_ROW_CHUNKS = ((0, 512), (512, 512), (1024, 256))


MOE_ISSUE_UNROLL = 8


def _moe_issue_rows(idx_ref, idx0, src_hbm, dst_ref, dst_row0, count, sem):
    def body(i, carry):
        for u in range(MOE_ISSUE_UNROLL):
            r = i * MOE_ISSUE_UNROLL + u
            tok = idx_ref[idx0 + r]
            pltpu.make_async_copy(src_hbm.at[pl.ds(tok, 1)], dst_ref.at[pl.ds(dst_row0 + r, 1)], sem).start()
        return carry

    lax.fori_loop(0, count // MOE_ISSUE_UNROLL, body, 0)


def _moe_kernel(idx_ref, xc_hbm, xl_hbm, aff_ref, wg_ref, wu_ref, wd_ref, ye_ref, xe_ref, sem):
    e = pl.program_id(0)
    f = pl.program_id(1)
    slot = e % 2

    def gather(expert, s):
        base = expert * CAP_ALL
        _moe_issue_rows(idx_ref, base, xc_hbm, xe_ref.at[s], 0, CAP_CTX, sem.at[s])
        _moe_issue_rows(idx_ref, base + CAP_CTX, xl_hbm, xe_ref.at[s], CAP_CTX, CAP_LAT, sem.at[s])

    @pl.when(f == 0)
    def _():
        @pl.when(e == 0)
        def _():
            gather(0, 0)

        pltpu.make_async_copy(xc_hbm.at[pl.ds(0, CAP_ALL)], xe_ref.at[slot], sem.at[slot]).wait()

        @pl.when(e + 1 < N_EXPERTS)
        def _():
            gather(e + 1, 1 - slot)

    wg = wg_ref[...].astype(BF16)
    wu = wu_ref[...].astype(BF16)
    wd = wd_ref[...].astype(BF16)
    for r0, rn in MOE_ROW_CHUNKS:
        xe = xe_ref[slot, r0:r0 + rn, :].astype(BF16)
        hg = jnp.dot(xe, wg, preferred_element_type=F32)
        hu = jnp.dot(xe, wu, preferred_element_type=F32)
        act = (hg * jax.nn.sigmoid(hg) * hu).astype(BF16)
        part = jnp.dot(act, wd, preferred_element_type=F32) * aff_ref[r0:r0 + rn, :]

        @pl.when(f == 0)
        def _():
            ye_ref[r0:r0 + rn, :] = part

        @pl.when(f != 0)
        def _():
            ye_ref[r0:r0 + rn, :] += part


def _moe(idx_all, xn2_ctx, xn2_lat, aff_all, w_gate_e, w_up_e, w_down_e, layer):
    nf = EXPERT_FF // MOE_TF
    grid_spec = pltpu.PrefetchScalarGridSpec(
        num_scalar_prefetch=1,
        grid=(N_EXPERTS, nf),
        in_specs=[
            pl.BlockSpec(memory_space=pl.ANY),
            pl.BlockSpec(memory_space=pl.ANY),
            pl.BlockSpec((None, CAP_ALL, 1), lambda e, f, idx: (e, 0, 0)),
            pl.BlockSpec((None, None, D_MODEL, MOE_TF), lambda e, f, idx: (layer, e, 0, f)),
            pl.BlockSpec((None, None, D_MODEL, MOE_TF), lambda e, f, idx: (layer, e, 0, f)),
            pl.BlockSpec((None, None, MOE_TF, D_MODEL), lambda e, f, idx: (layer, e, f, 0)),
        ],
        out_specs=pl.BlockSpec((None, CAP_ALL, D_MODEL), lambda e, f, idx: (e, 0, 0)),
        scratch_shapes=[pltpu.VMEM((2, CAP_ALL, D_MODEL), F32), pltpu.SemaphoreType.DMA((2,))],
    )
    return pl.pallas_call(
        _moe_kernel,
        grid_spec=grid_spec,
        out_shape=jax.ShapeDtypeStruct((N_EXPERTS, CAP_ALL, D_MODEL), F32),
        compiler_params=_params(("arbitrary", "arbitrary"), 56),
        name="moe_experts",
    )(idx_all, xn2_ctx, xn2_lat, aff_all, w_gate_e, w_up_e, w_down_e)


def _route(probs_t, cap):
    aff, idx = lax.top_k(probs_t, cap)
    order = jnp.argsort(idx, axis=-1)
    return jnp.take_along_axis(idx, order, axis=-1), jnp.take_along_axis(aff, order, axis=-1)


def kernel(x_prompt, x_sample, cache_attn_k, cache_attn_v, cache_na_k, cache_na_v, c, c_ctx, w_ada, b_ada, norm1_g,
           w_in, q_norm_g, k_norm_g, na_rpb, conv_w, w_proj_a, w_proj_b, w_proj_c, w_out, norm2_g, w_router,
           w_gate_e, w_up_e, w_down_e, final_g):
    cond8 = jnp.zeros((8, D_MODEL), F32).at[0].set(c_ctx).at[1:1 + DEC_BATCH].set(c)
    mods = _adaln(cond8, w_ada, b_ada).reshape(DEPTH, 8, N_MOD, D_MODEL)
    cos128, sin128 = _rope_tables()

    hp = x_prompt.reshape(N_CTX, D_MODEL)
    hs = x_sample.reshape(N_LAT, D_MODEL)
    new_ak, new_av, new_nk, new_nv = [], [], [], []
    for l in range(DEPTH):
        mod_ctx = [mods[l, 0:1, i][:, None, :] for i in range(N_MOD)]
        mod_lat = [mods[l, 1:1 + DEC_BATCH, i][:, None, :] for i in range(N_MOD)]
        n1g = norm1_g[l][None]
        n2g = norm2_g[l][None]
        qg = q_norm_g[l][None]
        kg = k_norm_g[l][None]
        wa, wb, wc, wo = (w_proj_a[l].astype(BF16), w_proj_b[l].astype(BF16), w_proj_c[l].astype(BF16),
                          w_out[l].astype(BF16))
        wrt = w_router[l].T.astype(BF16)
        bias = _na_bias(na_rpb[l])

        def sublayer1(x, mod, seq, is_ctx):
            sh1, sc1, g1, sh2, sc2, _ = mod
            xn = _norm_mod(x, n1g, sc1, sh1, seq, BF16)
            h_a = _inproj(xn, w_in, l, COL_A, W_A, 768, F32)
            h_b = _inproj(xn, w_in, l, COL_B, W_B, 768, F32)
            h_c = _inproj(xn, w_in, l, COL_C, W_C, 768, BF16)
            h_g = _inproj(xn, w_in, l, COL_G, GATE_W, 768, BF16)
            kv = None
            if is_ctx:
                oab, ka, va, kb, vb = _attn_ctx(h_a, h_b, jnp.tile(qg, (1, 2)), jnp.tile(kg, (1, 2)))
                kv = (ka, va, kb, vb)
            else:
                oab = _attn_lat(h_a, h_b, cache_attn_k, cache_attn_v, cache_na_k, cache_na_v, l, qg, kg,
                                cos128, sin128, bias)
            oc = _conv(h_c, conv_w[l], seq)
            xo, xn2, probs_t = _mix(x, oab, oc, h_g, wa, wb, wc, wo, g1, n2g, sc2, sh2, wrt, seq)
            return xo, xn2, probs_t, kv

        hp, xn2_c, pt_c, (ka, va, kb, vb) = sublayer1(hp, mod_ctx, SEQ, True)
        hs, xn2_l, pt_l, _ = sublayer1(hs, mod_lat, DEC_SEQ, False)
        new_ak.append(ka)
        new_av.append(va)
        new_nk.append(kb)
        new_nv.append(vb)

        idx_c, aff_c = _route(pt_c, CAP_CTX)
        idx_l, aff_l = _route(pt_l, CAP_LAT)
        idx_all = jnp.concatenate([idx_c, idx_l], axis=1).reshape(-1).astype(jnp.int32)
        aff_all = jnp.concatenate([aff_c, aff_l], axis=1)[..., None]
        ye = _moe(idx_all, xn2_c, xn2_l, aff_all, w_gate_e, w_up_e, w_down_e, l)
        y_c = jnp.zeros((N_CTX, D_MODEL), F32).at[idx_c.reshape(-1)].add(ye[:, :CAP_CTX].reshape(-1, D_MODEL))
        y_l = jnp.zeros((N_LAT, D_MODEL), F32).at[idx_l.reshape(-1)].add(ye[:, CAP_CTX:].reshape(-1, D_MODEL))
        hp = hp + (mod_ctx[5] * y_c.reshape(1, N_CTX, D_MODEL)).reshape(N_CTX, D_MODEL)
        hs = hs + (mod_lat[5] * y_l.reshape(DEC_BATCH, DEC_SEQ, D_MODEL)).reshape(N_LAT, D_MODEL)

    fg = final_g[None]
    y_prompt = _final_norm(hp, fg).reshape(BATCH, SEQ, D_MODEL)
    y_sample = _final_norm(hs, fg).reshape(DEC_BATCH, DEC_SEQ, D_MODEL)
    return (y_prompt, y_sample, jnp.stack(new_ak, axis=1), jnp.stack(new_av, axis=1),
            jnp.stack(new_nk, axis=1), jnp.stack(new_nv, axis=1))
```

```python
import functools

import jax
import jax.numpy as jnp
import numpy as np
from jax import lax
from jax.experimental import pallas as pl
from jax.experimental.pallas import tpu as pltpu

D_MODEL = 1024
BATCH = 32
SEQ = 256
DEPTH = 2
DEC_BATCH = 2
DEC_SEQ = 1024
PAST_LEN = 256
GRID_W = 64
HEAD_DIM = 64
A_HEADS = 8
A_KV_HEADS = 2
A_GROUP = A_HEADS // A_KV_HEADS
B_HEADS = 8
C_WIDTH = 512
NA_WIN_R = 8
NA_WIN_C = 16
N_EXPERTS = 16
EXPERT_FF = 1024
CAP_FACTOR = 2
ROPE_THETA = 10000.0
EPS = 1e-6
N_MOD = 6

A_Q = A_HEADS * HEAD_DIM
A_KV = A_KV_HEADS * HEAD_DIM
B_W = B_HEADS * HEAD_DIM
GATE_W = 3 * D_MODEL
COL_A = 0
W_A = A_Q + 2 * A_KV
COL_B = COL_A + W_A
W_B = 3 * B_W
COL_C = COL_B + W_B
W_C = 3 * C_WIDTH
COL_G = COL_C + W_C

N_CTX = BATCH * SEQ
N_LAT = DEC_BATCH * DEC_SEQ
CAP_CTX = CAP_FACTOR * N_CTX // N_EXPERTS
CAP_LAT = CAP_FACTOR * N_LAT // N_EXPERTS
CAP_ALL = CAP_CTX + CAP_LAT
LAT_ROWS = DEC_SEQ // GRID_W
MASK_NEG = -1e30
ATT_SCALE = HEAD_DIM ** -0.5

F32 = jnp.float32
BF16 = jnp.bfloat16
_NT = (((1,), (1,)), ((), ()))

_MIB = 1024 * 1024


def _params(sem, vmem_mib):
    return pltpu.CompilerParams(dimension_semantics=sem, vmem_limit_bytes=vmem_mib * _MIB)


def _adaln_kernel(cond_ref, w_ref, b_ref, o_ref):
    c = cond_ref[...]
    s = c * jax.nn.sigmoid(c)
    o_ref[...] = jnp.dot(s.astype(BF16), w_ref[...].astype(BF16), preferred_element_type=F32) + b_ref[...]


def _adaln(cond8, w_ada, b_ada):
    tn = 1536
    ncol = N_MOD * D_MODEL
    return pl.pallas_call(
        _adaln_kernel,
        grid=(DEPTH, ncol // tn),
        in_specs=[
            pl.BlockSpec((8, D_MODEL), lambda l, j: (0, 0)),
            pl.BlockSpec((None, D_MODEL, tn), lambda l, j: (l, 0, j)),
            pl.BlockSpec((None, 1, tn), lambda l, j: (l, 0, j)),
        ],
        out_specs=pl.BlockSpec((None, 8, tn), lambda l, j: (l, 0, j)),
        out_shape=jax.ShapeDtypeStruct((DEPTH, 8, ncol), F32),
        compiler_params=_params(("arbitrary", "arbitrary"), 40),
        name="adaln",
    )(cond8, w_ada, b_ada.reshape(DEPTH, 1, ncol))


def _norm_mod_kernel(x_ref, g_ref, sc_ref, sh_ref, o_ref):
    x = x_ref[...]
    y = x * lax.rsqrt(jnp.mean(x * x, axis=-1, keepdims=True) + EPS) * g_ref[...]
    o_ref[...] = (y * (1.0 + sc_ref[...]) + sh_ref[...]).astype(o_ref.dtype)


def _norm_mod(x, g, sc, sh, seq, out_dtype):
    n = x.shape[0]
    tm = 512
    per_batch = sc.shape[0] > 1
    mod_map = (lambda i: ((i * tm) // seq, 0, 0)) if per_batch else (lambda i: (0, 0, 0))
    return pl.pallas_call(
        _norm_mod_kernel,
        grid=(n // tm,),
        in_specs=[
            pl.BlockSpec((tm, D_MODEL), lambda i: (i, 0)),
            pl.BlockSpec((1, D_MODEL), lambda i: (0, 0)),
            pl.BlockSpec((None, 1, D_MODEL), mod_map),
            pl.BlockSpec((None, 1, D_MODEL), mod_map),
        ],
        out_specs=pl.BlockSpec((tm, D_MODEL), lambda i: (i, 0)),
        out_shape=jax.ShapeDtypeStruct((n, D_MODEL), out_dtype),
        compiler_params=_params(("arbitrary",), 32),
        name="norm_mod",
    )(x, g, sc, sh)


def _norm_kernel(x_ref, g_ref, o_ref):
    x = x_ref[...]
    o_ref[...] = x * lax.rsqrt(jnp.mean(x * x, axis=-1, keepdims=True) + EPS) * g_ref[...]


def _final_norm(x, g):
    n = x.shape[0]
    tm = 512
    return pl.pallas_call(
        _norm_kernel,
        grid=(n // tm,),
        in_specs=[pl.BlockSpec((tm, D_MODEL), lambda i: (i, 0)), pl.BlockSpec((1, D_MODEL), lambda i: (0, 0))],
        out_specs=pl.BlockSpec((tm, D_MODEL), lambda i: (i, 0)),
        out_shape=jax.ShapeDtypeStruct((n, D_MODEL), F32),
        compiler_params=_params(("arbitrary",), 32),
        name="final_norm",
    )(x, g)


def _inproj_kernel(a_ref, w_ref, o_ref, wbf_ref):
    @pl.when(pl.program_id(1) == 0)
    def _():
        wbf_ref[...] = w_ref[...].astype(BF16)

    o_ref[...] = jnp.dot(a_ref[...], wbf_ref[...], preferred_element_type=F32).astype(o_ref.dtype)


def _inproj(xn, w_in, layer, col0, width, tn, out_dtype):
    n = xn.shape[0]
    tm = 2048
    joff = col0 // tn
    assert col0 % tn == 0 and width % tn == 0
    return pl.pallas_call(
        _inproj_kernel,
        grid=(width // tn, n // tm),
        in_specs=[
            pl.BlockSpec((tm, D_MODEL), lambda j, i: (i, 0)),
            pl.BlockSpec((None, D_MODEL, tn), lambda j, i: (layer, 0, j + joff)),
        ],
        out_specs=pl.BlockSpec((tm, tn), lambda j, i: (i, j)),
        out_shape=jax.ShapeDtypeStruct((n, width), out_dtype),
        scratch_shapes=[pltpu.VMEM((D_MODEL, tn), BF16)],
        compiler_params=_params(("arbitrary", "arbitrary"), 48),
        name="inproj",
    )(xn, w_in)


PAIR = 2 * HEAD_DIM


def _is_lo_half():
    return lax.broadcasted_iota(jnp.int32, (1, PAIR), 1) < HEAD_DIM


def _pair_rms(x, g2, is_lo):
    x2 = x * x
    s_lo = jnp.sum(jnp.where(is_lo, x2, 0.0), axis=-1, keepdims=True)
    s_hi = jnp.sum(jnp.where(is_lo, 0.0, x2), axis=-1, keepdims=True)
    ms = jnp.where(is_lo, s_lo, s_hi) * (1.0 / HEAD_DIM)
    return x * lax.rsqrt(ms + EPS) * g2


def _split_halves(x, is_lo):
    return jnp.where(is_lo, x, 0.0).astype(BF16), jnp.where(is_lo, 0.0, x).astype(BF16)


def _pad_lo(x):
    return jnp.concatenate([x, jnp.zeros_like(x)], axis=1)


def _pad_hi(x):
    return jnp.concatenate([jnp.zeros_like(x), x], axis=1)


def _qk(q, k):
    return lax.dot_general(q, k, _NT, preferred_element_type=F32)


def _softmax_pv(scores, values):
    m = scores[0].max(axis=-1, keepdims=True)
    for s in scores[1:]:
        m = jnp.maximum(m, s.max(axis=-1, keepdims=True))
    acc = None
    den = None
    for s, v in zip(scores, values):
        p = jnp.exp(s - m)
        d = p.sum(axis=-1, keepdims=True)
        o = jnp.dot(p.astype(BF16), v, preferred_element_type=F32)
        acc = o if acc is None else acc + o
        den = d if den is None else den + d
    return acc * (1.0 / den)


CTX_NB = 2


def _attn_ctx_kernel(ha_ref, hb_ref, qg_ref, kg_ref, oab_ref, ak_ref, av_ref, nk_ref, nv_ref):
    is_lo = _is_lo_half()
    qg2 = qg_ref[...] * ATT_SCALE
    kg2 = kg_ref[...]
    for b in range(CTX_NB):
        rows = slice(b * SEQ, (b + 1) * SEQ)
        kp = _pair_rms(ha_ref[rows, A_Q:A_Q + PAIR], kg2, is_lo)
        vp = ha_ref[rows, A_Q + A_KV:A_Q + A_KV + PAIR]
        for kv in range(A_KV_HEADS):
            ak_ref[b, kv] = kp[:, kv * HEAD_DIM:(kv + 1) * HEAD_DIM]
            av_ref[b, kv] = vp[:, kv * HEAD_DIM:(kv + 1) * HEAD_DIM]
        kp_sw = pltpu.roll(kp, HEAD_DIM, axis=1)
        vp_sw = pltpu.roll(vp, HEAD_DIM, axis=1)
        qs = [_pair_rms(ha_ref[rows, c * PAIR:(c + 1) * PAIR], qg2, is_lo).astype(BF16)
              for c in range(A_HEADS // 2)]
        for kv in range(A_KV_HEADS):
            k_lo = jnp.where(is_lo, kp if kv == 0 else kp_sw, 0.0).astype(BF16)
            k_hi = jnp.where(is_lo, 0.0, kp_sw if kv == 0 else kp).astype(BF16)
            v_lo = jnp.where(is_lo, vp if kv == 0 else vp_sw, 0.0).astype(BF16)
            v_hi = jnp.where(is_lo, 0.0, vp_sw if kv == 0 else vp).astype(BF16)
            q_st = jnp.concatenate([qs[2 * kv], qs[2 * kv + 1]], axis=0)
            o = _softmax_pv([_qk(q_st, k_lo)], [v_lo]) + _softmax_pv([_qk(q_st, k_hi)], [v_hi])
            oab_ref[rows, (2 * kv) * PAIR:(2 * kv + 1) * PAIR] = o[0:SEQ].astype(BF16)
            oab_ref[rows, (2 * kv + 1) * PAIR:(2 * kv + 2) * PAIR] = o[SEQ:2 * SEQ].astype(BF16)
        for j in range(B_HEADS // 2):
            qp = (hb_ref[rows, j * PAIR:(j + 1) * PAIR] * ATT_SCALE).astype(BF16)
            kp_b = hb_ref[rows, B_W + j * PAIR:B_W + (j + 1) * PAIR]
            vp_b = hb_ref[rows, 2 * B_W + j * PAIR:2 * B_W + (j + 1) * PAIR]
            for u in range(2):
                nk_ref[b, 2 * j + u] = kp_b[:, u * HEAD_DIM:(u + 1) * HEAD_DIM]
                nv_ref[b, 2 * j + u] = vp_b[:, u * HEAD_DIM:(u + 1) * HEAD_DIM]
            k_lo, k_hi = _split_halves(kp_b, is_lo)
            v_lo, v_hi = _split_halves(vp_b, is_lo)
            o = _softmax_pv([_qk(qp, k_lo)], [v_lo]) + _softmax_pv([_qk(qp, k_hi)], [v_hi])
            oab_ref[rows, A_Q + j * PAIR:A_Q + (j + 1) * PAIR] = o.astype(BF16)


def _attn_ctx(h_a, h_b, qg2, kg2):
    rows = CTX_NB * SEQ
    kv_spec = lambda nh: pl.BlockSpec((CTX_NB, nh, SEQ, HEAD_DIM), lambda b: (b, 0, 0, 0))
    kv_shape = lambda nh: jax.ShapeDtypeStruct((BATCH, nh, SEQ, HEAD_DIM), F32)
    return pl.pallas_call(
        _attn_ctx_kernel,
        grid=(BATCH // CTX_NB,),
        in_specs=[
            pl.BlockSpec((rows, W_A), lambda b: (b, 0)),
            pl.BlockSpec((rows, W_B), lambda b: (b, 0)),
            pl.BlockSpec((1, PAIR), lambda b: (0, 0)),
            pl.BlockSpec((1, PAIR), lambda b: (0, 0)),
        ],
        out_specs=[
            pl.BlockSpec((rows, A_Q + B_W), lambda b: (b, 0)),
            kv_spec(A_KV_HEADS), kv_spec(A_KV_HEADS), kv_spec(B_HEADS), kv_spec(B_HEADS),
        ],
        out_shape=[
            jax.ShapeDtypeStruct((N_CTX, A_Q + B_W), BF16),
            kv_shape(A_KV_HEADS), kv_shape(A_KV_HEADS), kv_shape(B_HEADS), kv_shape(B_HEADS),
        ],
        compiler_params=_params(("arbitrary",), 40),
        name="attn_ctx",
    )(h_a, h_b, qg2, kg2)


LAT_TQ = 256
LAT_QROWS = LAT_TQ // GRID_W
LAT_NT = DEC_SEQ // LAT_TQ
NA_UNION_ROWS = 12
NA_UNION = NA_UNION_ROWS * GRID_W


def _na_union_row0(t):
    lo = min(max(r - NA_WIN_R // 2, 0) for r in range(t * LAT_QROWS, (t + 1) * LAT_QROWS))
    return min(min(lo, LAT_ROWS - NA_WIN_R), LAT_ROWS - NA_UNION_ROWS)


def _rope(x, cos, sin):
    lane = lax.broadcasted_iota(jnp.int32, x.shape, 1)
    nxt = pltpu.roll(x, x.shape[1] - 1, axis=1)
    prv = pltpu.roll(x, 1, axis=1)
    partner = jnp.where((lane & 1) == 0, nxt, prv)
    return x * cos + partner * sin


def _attn_lat_kernel(haq_ref, haf_ref, hbq_ref, hbf_ref, cak_ref, cav_ref, cnk_ref, cnv_ref,
                     qg_ref, kg_ref, cos_ref, sin_ref, bias_ref, oab_ref, kva_ref):
    t = pl.program_id(1)
    is_lo = _is_lo_half()
    row0 = pl.multiple_of(t * LAT_TQ, LAT_TQ)
    qg2 = qg_ref[...] * ATT_SCALE
    kg2 = kg_ref[...]

    @pl.when(t == 0)
    def _():
        kp = _rope(_pair_rms(haf_ref[:, A_Q:A_Q + PAIR].astype(F32), kg2, is_lo), cos_ref[...], sin_ref[...])
        vp = haf_ref[:, A_Q + A_KV:A_Q + A_KV + PAIR].astype(F32)
        kp_sw = pltpu.roll(kp, HEAD_DIM, axis=1)
        vp_sw = pltpu.roll(vp, HEAD_DIM, axis=1)
        for kv in range(A_KV_HEADS):
            kva_ref[4 * kv + 0] = jnp.where(is_lo, kp if kv == 0 else kp_sw, 0.0).astype(BF16)
            kva_ref[4 * kv + 1] = jnp.where(is_lo, 0.0, kp_sw if kv == 0 else kp).astype(BF16)
            kva_ref[4 * kv + 2] = jnp.where(is_lo, vp if kv == 0 else vp_sw, 0.0).astype(BF16)
            kva_ref[4 * kv + 3] = jnp.where(is_lo, 0.0, vp_sw if kv == 0 else vp).astype(BF16)

    cos_q = cos_ref[pl.ds(row0, LAT_TQ), :]
    sin_q = sin_ref[pl.ds(row0, LAT_TQ), :]
    qs = [_rope(_pair_rms(haq_ref[:, c * PAIR:(c + 1) * PAIR].astype(F32), qg2, is_lo), cos_q, sin_q).astype(BF16)
          for c in range(A_HEADS // 2)]
    for kv in range(A_KV_HEADS):
        kc = cak_ref[kv].astype(BF16)
        vc = cav_ref[kv].astype(BF16)
        q_st = jnp.concatenate([qs[2 * kv], qs[2 * kv + 1]], axis=0)
        o = (_softmax_pv([_qk(q_st, _pad_lo(kc)), _qk(q_st, kva_ref[4 * kv + 0])], [_pad_lo(vc), kva_ref[4 * kv + 2]])
             + _softmax_pv([_qk(q_st, _pad_hi(kc)), _qk(q_st, kva_ref[4 * kv + 1])], [_pad_hi(vc), kva_ref[4 * kv + 3]]))
        oab_ref[:, (2 * kv) * PAIR:(2 * kv + 1) * PAIR] = o[0:LAT_TQ].astype(BF16)
        oab_ref[:, (2 * kv + 1) * PAIR:(2 * kv + 2) * PAIR] = o[LAT_TQ:2 * LAT_TQ].astype(BF16)

    k0 = _na_union_row0(0) * GRID_W
    for tt in range(1, LAT_NT):
        k0 = jnp.where(t >= tt, _na_union_row0(tt) * GRID_W, k0)
    k0 = pl.multiple_of(k0, GRID_W)
    for j in range(B_HEADS // 2):
        qp = (hbq_ref[:, j * PAIR:(j + 1) * PAIR] * ATT_SCALE).astype(BF16)
        k_lo, k_hi = _split_halves(hbf_ref[pl.ds(k0, NA_UNION), B_W + j * PAIR:B_W + (j + 1) * PAIR], is_lo)
        v_lo, v_hi = _split_halves(hbf_ref[pl.ds(k0, NA_UNION), 2 * B_W + j * PAIR:2 * B_W + (j + 1) * PAIR], is_lo)
        kc_lo = _pad_lo(cnk_ref[2 * j].astype(BF16))
        kc_hi = _pad_hi(cnk_ref[2 * j + 1].astype(BF16))
        vc_lo = _pad_lo(cnv_ref[2 * j].astype(BF16))
        vc_hi = _pad_hi(cnv_ref[2 * j + 1].astype(BF16))
        o = (_softmax_pv([_qk(qp, kc_lo), _qk(qp, k_lo) + bias_ref[2 * j]], [vc_lo, v_lo])
             + _softmax_pv([_qk(qp, kc_hi), _qk(qp, k_hi) + bias_ref[2 * j + 1]], [vc_hi, v_hi]))
        oab_ref[:, A_Q + j * PAIR:A_Q + (j + 1) * PAIR] = o.astype(BF16)


def _attn_lat(h_a, h_b, cak, cav, cnk, cnv, layer, qg2, kg2, cos128, sin128, bias):
    nt = LAT_NT
    cache_spec = lambda nh: pl.BlockSpec((None, None, nh, PAST_LEN, HEAD_DIM), lambda b, t: (b, layer, 0, 0, 0))
    return pl.pallas_call(
        _attn_lat_kernel,
        grid=(DEC_BATCH, nt),
        in_specs=[
            pl.BlockSpec((LAT_TQ, W_A), lambda b, t: (b * nt + t, 0)),
            pl.BlockSpec((DEC_SEQ, W_A), lambda b, t: (b, 0)),
            pl.BlockSpec((LAT_TQ, W_B), lambda b, t: (b * nt + t, 0)),
            pl.BlockSpec((DEC_SEQ, W_B), lambda b, t: (b, 0)),
            cache_spec(A_KV_HEADS), cache_spec(A_KV_HEADS), cache_spec(B_HEADS), cache_spec(B_HEADS),
            pl.BlockSpec((1, PAIR), lambda b, t: (0, 0)),
            pl.BlockSpec((1, PAIR), lambda b, t: (0, 0)),
            pl.BlockSpec((DEC_SEQ, PAIR), lambda b, t: (0, 0)),
            pl.BlockSpec((DEC_SEQ, PAIR), lambda b, t: (0, 0)),
            pl.BlockSpec((B_HEADS, None, LAT_TQ, NA_UNION), lambda b, t: (0, t, 0, 0)),
        ],
        out_specs=pl.BlockSpec((LAT_TQ, A_Q + B_W), lambda b, t: (b * nt + t, 0)),
        out_shape=jax.ShapeDtypeStruct((N_LAT, A_Q + B_W), BF16),
        scratch_shapes=[pltpu.VMEM((4 * A_KV_HEADS, DEC_SEQ, PAIR), BF16)],
        compiler_params=_params(("arbitrary", "arbitrary"), 60),
        name="attn_lat",
    )(h_a, h_a, h_b, h_b, cak, cav, cnk, cnv, qg2, kg2, cos128, sin128, bias)


def _rope_tables():
    t = jnp.arange(DEC_SEQ)
    row = (t // GRID_W).astype(F32)
    col = (t % GRID_W).astype(F32)
    n_freq = HEAD_DIM // 4
    inv = ROPE_THETA ** (-jnp.arange(n_freq, dtype=F32) / n_freq)
    ang = jnp.concatenate([row[:, None] * inv, col[:, None] * inv], axis=-1)
    cos = jnp.repeat(jnp.cos(ang), 2, axis=-1)
    sign = jnp.tile(jnp.array([-1.0, 1.0], F32), HEAD_DIM // 2)
    sin = jnp.repeat(jnp.sin(ang), 2, axis=-1) * sign
    return jnp.tile(cos, (1, 2)), jnp.tile(sin, (1, 2))


def _na_bias(rpb):
    col = np.arange(GRID_W)
    dc = np.clip(col[None, :] - col[:, None] + NA_WIN_C - 1, 0, 2 * NA_WIN_C - 2)
    onehot = (dc[..., None] == np.arange(2 * NA_WIN_C - 1)).astype(np.float32)
    c0 = np.clip(col - NA_WIN_C // 2, 0, GRID_W - NA_WIN_C)
    in_win = (col[None, :] >= c0[:, None]) & (col[None, :] < c0[:, None] + NA_WIN_C)
    toep = jnp.einsum('hdm,ckm->hdck', rpb, onehot, precision=lax.Precision.HIGHEST)
    toep = jnp.where(in_win[None, None], toep, MASK_NEG)
    masked = jnp.full((B_HEADS, GRID_W, GRID_W), MASK_NEG, F32)
    tiles = []
    for t in range(LAT_NT):
        u0 = _na_union_row0(t)
        q_rows = []
        for r in range(t * LAT_QROWS, (t + 1) * LAT_QROWS):
            r0 = min(max(r - NA_WIN_R // 2, 0), LAT_ROWS - NA_WIN_R)
            q_rows.append(jnp.concatenate(
                [toep[:, kr - r + NA_WIN_R - 1] if r0 <= kr < r0 + NA_WIN_R else masked
                 for kr in range(u0, u0 + NA_UNION_ROWS)], axis=-1))
        tiles.append(jnp.concatenate(q_rows, axis=1))
    return jnp.stack(tiles, axis=1)


def _conv_kernel(hc_ref, w_ref, o_ref, *, seq):
    bg = hc_ref[:, 0:C_WIDTH].astype(F32)
    u = hc_ref[:, C_WIDTH:2 * C_WIDTH].astype(F32) * hc_ref[:, 2 * C_WIDTH:3 * C_WIDTH].astype(F32)
    rows = u.shape[0]
    assert seq & (seq - 1) == 0
    pos = lax.broadcasted_iota(jnp.int32, u.shape, 0) & (seq - 1)
    u_prev = jnp.where(pos == 0, 0.0, pltpu.roll(u, 1, axis=0))
    u_next = jnp.where(pos == seq - 1, 0.0, pltpu.roll(u, rows - 1, axis=0))
    y = u_prev * w_ref[0:1, :] + u * w_ref[1:2, :] + u_next * w_ref[2:3, :]
    o_ref[...] = (bg * y).astype(o_ref.dtype)


def _conv(h_c, conv_w, seq):
    n = h_c.shape[0]
    tm = 1024
    return pl.pallas_call(
        functools.partial(_conv_kernel, seq=seq),
        grid=(n // tm,),
        in_specs=[pl.BlockSpec((tm, W_C), lambda i: (i, 0)), pl.BlockSpec((3, C_WIDTH), lambda i: (0, 0))],
        out_specs=pl.BlockSpec((tm, C_WIDTH), lambda i: (i, 0)),
        out_shape=jax.ShapeDtypeStruct((n, C_WIDTH), BF16),
        compiler_params=_params(("arbitrary",), 40),
        name="short_conv",
    )(h_c, conv_w)


def _mix_kernel(x_ref, oab_ref, oc_ref, hg_ref, wa_ref, wb_ref, wc_ref, wo_ref, g1_ref, n2g_ref, sc2_ref, sh2_ref,
                wrt_ref, xo_ref, xn2_ref, pt_ref):
    ga = jax.nn.sigmoid(hg_ref[:, 0:D_MODEL].astype(F32))
    gb = jax.nn.sigmoid(hg_ref[:, D_MODEL:2 * D_MODEL].astype(F32))
    gc = jax.nn.sigmoid(hg_ref[:, 2 * D_MODEL:3 * D_MODEL].astype(F32))
    pa = jnp.dot(oab_ref[:, 0:A_Q], wa_ref[...], preferred_element_type=F32)
    pb = jnp.dot(oab_ref[:, A_Q:A_Q + B_W], wb_ref[...], preferred_element_type=F32)
    pc = jnp.dot(oc_ref[...], wc_ref[...], preferred_element_type=F32)
    merged = ga * pa + gb * pb + gc * pc
    xo = x_ref[...] + g1_ref[...] * jnp.dot(merged.astype(BF16), wo_ref[...], preferred_element_type=F32)
    xo_ref[...] = xo
    y = xo * lax.rsqrt(jnp.mean(xo * xo, axis=-1, keepdims=True) + EPS) * n2g_ref[...]
    xn2 = y * (1.0 + sc2_ref[...]) + sh2_ref[...]
    xn2_ref[...] = xn2
    logits_t = lax.dot_general(wrt_ref[...], xn2.astype(BF16), _NT, preferred_element_type=F32)
    z = jnp.exp(logits_t - logits_t.max(axis=0, keepdims=True))
    pt_ref[...] = z / z.sum(axis=0, keepdims=True)


def _mix(x, oab, oc, h_g, wa, wb, wc, wo, g1, n2g, sc2, sh2, wrt, seq):
    n = x.shape[0]
    tm = 512
    per_batch = g1.shape[0] > 1
    mod_map = (lambda i: ((i * tm) // seq, 0, 0)) if per_batch else (lambda i: (0, 0, 0))
    mod_spec = pl.BlockSpec((None, 1, D_MODEL), mod_map)
    full = lambda a: pl.BlockSpec(a.shape, lambda i: (0,) * a.ndim)
    row = lambda w: pl.BlockSpec((tm, w), lambda i: (i, 0))
    return pl.pallas_call(
        _mix_kernel,
        grid=(n // tm,),
        in_specs=[row(D_MODEL), row(A_Q + B_W), row(C_WIDTH), row(GATE_W), full(wa), full(wb), full(wc), full(wo),
                  mod_spec, full(n2g), mod_spec, mod_spec, full(wrt)],
        out_specs=[row(D_MODEL), row(D_MODEL), pl.BlockSpec((N_EXPERTS, tm), lambda i: (0, i))],
        out_shape=[jax.ShapeDtypeStruct((n, D_MODEL), F32), jax.ShapeDtypeStruct((n, D_MODEL), F32),
                   jax.ShapeDtypeStruct((N_EXPERTS, n), F32)],
        compiler_params=_params(("arbitrary",), 56),
        name="mix",
    )(x, oab, oc, h_g, wa, wb, wc, wo, g1, n2g, sc2, sh2, wrt)


MOE_TF = 512
MOE_ROW_CHUNKS = ((0, 512), (512, 512), (1024, 256))
MOE_ISSUE_UNROLL = 8


def _moe_issue_rows(idx_ref, idx0, src_hbm, dst_ref, dst_row0, count, sem):
    def body(i, carry):
        for u in range(MOE_ISSUE_UNROLL):
            r = i * MOE_ISSUE_UNROLL + u
            tok = idx_ref[idx0 + r]
            pltpu.make_async_copy(src_hbm.at[pl.ds(tok, 1)], dst_ref.at[pl.ds(dst_row0 + r, 1)], sem).start()
        return carry

    lax.fori_loop(0, count // MOE_ISSUE_UNROLL, body, 0)


def _moe_kernel(idx_ref, xc_hbm, xl_hbm, aff_ref, wg_ref, wu_ref, wd_ref, ye_ref, xe_ref, sem):
    e = pl.program_id(0)
    f = pl.program_id(1)
    slot = e % 2

    def gather(expert, s):
        base = expert * CAP_ALL
        _moe_issue_rows(idx_ref, base, xc_hbm, xe_ref.at[s], 0, CAP_CTX, sem.at[s])
        _moe_issue_rows(idx_ref, base + CAP_CTX, xl_hbm, xe_ref.at[s], CAP_CTX, CAP_LAT, sem.at[s])

    @pl.when(f == 0)
    def _():
        @pl.when(e == 0)
        def _():
            gather(0, 0)

        pltpu.make_async_copy(xc_hbm.at[pl.ds(0, CAP_ALL)], xe_ref.at[slot], sem.at[slot]).wait()

        @pl.when(e + 1 < N_EXPERTS)
        def _():
            gather(e + 1, 1 - slot)

    wg = wg_ref[...].astype(BF16)
    wu = wu_ref[...].astype(BF16)
    wd = wd_ref[...].astype(BF16)
    for r0, rn in MOE_ROW_CHUNKS:
        xe = xe_ref[slot, r0:r0 + rn, :].astype(BF16)
        hg = jnp.dot(xe, wg, preferred_element_type=F32)
        hu = jnp.dot(xe, wu, preferred_element_type=F32)
        act = (hg * jax.nn.sigmoid(hg) * hu).astype(BF16)
        part = jnp.dot(act, wd, preferred_element_type=F32) * aff_ref[r0:r0 + rn, :]

        @pl.when(f == 0)
        def _():
            ye_ref[r0:r0 + rn, :] = part

        @pl.when(f != 0)
        def _():
            ye_ref[r0:r0 + rn, :] += part


def _moe(idx_all, xn2_ctx, xn2_lat, aff_all, w_gate_e, w_up_e, w_down_e, layer):
    nf = EXPERT_FF // MOE_TF
    grid_spec = pltpu.PrefetchScalarGridSpec(
        num_scalar_prefetch=1,
        grid=(N_EXPERTS, nf),
        in_specs=[
            pl.BlockSpec(memory_space=pl.ANY),
            pl.BlockSpec(memory_space=pl.ANY),
            pl.BlockSpec((None, CAP_ALL, 1), lambda e, f, idx: (e, 0, 0)),
            pl.BlockSpec((None, None, D_MODEL, MOE_TF), lambda e, f, idx: (layer, e, 0, f)),
            pl.BlockSpec((None, None, D_MODEL, MOE_TF), lambda e, f, idx: (layer, e, 0, f)),
            pl.BlockSpec((None, None, MOE_TF, D_MODEL), lambda e, f, idx: (layer, e, f, 0)),
        ],
        out_specs=pl.BlockSpec((None, CAP_ALL, D_MODEL), lambda e, f, idx: (e, 0, 0)),
        scratch_shapes=[pltpu.VMEM((2, CAP_ALL, D_MODEL), F32), pltpu.SemaphoreType.DMA((2,))],
    )
    return pl.pallas_call(
        _moe_kernel,
        grid_spec=grid_spec,
        out_shape=jax.ShapeDtypeStruct((N_EXPERTS, CAP_ALL, D_MODEL), F32),
        compiler_params=_params(("arbitrary", "arbitrary"), 56),
        name="moe_experts",
    )(idx_all, xn2_ctx, xn2_lat, aff_all, w_gate_e, w_up_e, w_down_e)


def _route(probs_t, cap):
    aff, idx = lax.top_k(probs_t, cap)
    order = jnp.argsort(idx, axis=-1)
    return jnp.take_along_axis(idx, order, axis=-1), jnp.take_along_axis(aff, order, axis=-1)


def kernel(x_prompt, x_sample, cache_attn_k, cache_attn_v, cache_na_k, cache_na_v, c, c_ctx, w_ada, b_ada, norm1_g,
           w_in, q_norm_g, k_norm_g, na_rpb, conv_w, w_proj_a, w_proj_b, w_proj_c, w_out, norm2_g, w_router,
           w_gate_e, w_up_e, w_down_e, final_g):
    cond8 = jnp.zeros((8, D_MODEL), F32).at[0].set(c_ctx).at[1:1 + DEC_BATCH].set(c)
    mods = _adaln(cond8, w_ada, b_ada).reshape(DEPTH, 8, N_MOD, D_MODEL)
    cos128, sin128 = _rope_tables()

    hp = x_prompt.reshape(N_CTX, D_MODEL)
    hs = x_sample.reshape(N_LAT, D_MODEL)
    new_ak, new_av, new_nk, new_nv = [], [], [], []
    for l in range(DEPTH):
        mod_ctx = [mods[l, 0:1, i][:, None, :] for i in range(N_MOD)]
        mod_lat = [mods[l, 1:1 + DEC_BATCH, i][:, None, :] for i in range(N_MOD)]
        n1g = norm1_g[l][None]
        n2g = norm2_g[l][None]
        qg2 = jnp.tile(q_norm_g[l][None], (1, 2))
        kg2 = jnp.tile(k_norm_g[l][None], (1, 2))
        wa, wb, wc, wo = (w_proj_a[l].astype(BF16), w_proj_b[l].astype(BF16), w_proj_c[l].astype(BF16),
                          w_out[l].astype(BF16))
        wrt = w_router[l].T.astype(BF16)
        bias = _na_bias(na_rpb[l])

        def sublayer1(x, mod, seq, is_ctx):
            sh1, sc1, g1, sh2, sc2, _ = mod
            xn = _norm_mod(x, n1g, sc1, sh1, seq, BF16)
            kv_dtype = F32 if is_ctx else BF16
            h_a = _inproj(xn, w_in, l, COL_A, W_A, 768, kv_dtype)
            h_b = _inproj(xn, w_in, l, COL_B, W_B, 768, kv_dtype)
            h_c = _inproj(xn, w_in, l, COL_C, W_C, 768, BF16)
            h_g = _inproj(xn, w_in, l, COL_G, GATE_W, 768, BF16)
            kv = None
            if is_ctx:
                oab, ka, va, kb, vb = _attn_ctx(h_a, h_b, qg2, kg2)
                kv = (ka, va, kb, vb)
            else:
                oab = _attn_lat(h_a, h_b, cache_attn_k, cache_attn_v, cache_na_k, cache_na_v, l, qg2, kg2,
                                cos128, sin128, bias)
            oc = _conv(h_c, conv_w[l], seq)
            xo, xn2, probs_t = _mix(x, oab, oc, h_g, wa, wb, wc, wo, g1, n2g, sc2, sh2, wrt, seq)
            return xo, xn2, probs_t, kv

        hp, xn2_c, pt_c, (ka, va, kb, vb) = sublayer1(hp, mod_ctx, SEQ, True)
        hs, xn2_l, pt_l, _ = sublayer1(hs, mod_lat, DEC_SEQ, False)
        new_ak.append(ka)
        new_av.append(va)
        new_nk.append(kb)
        new_nv.append(vb)

        idx_c, aff_c = _route(pt_c, CAP_CTX)
        idx_l, aff_l = _route(pt_l, CAP_LAT)
        idx_all = jnp.concatenate([idx_c, idx_l], axis=1).reshape(-1).astype(jnp.int32)
        aff_all = jnp.concatenate([aff_c, aff_l], axis=1)[..., None]
        ye = _moe(idx_all, xn2_c, xn2_l, aff_all, w_gate_e, w_up_e, w_down_e, l)
        y_c = jnp.zeros((N_CTX, D_MODEL), F32).at[idx_c.reshape(-1)].add(ye[:, :CAP_CTX].reshape(-1, D_MODEL))
        y_l = jnp.zeros((N_LAT, D_MODEL), F32).at[idx_l.reshape(-1)].add(ye[:, CAP_CTX:].reshape(-1, D_MODEL))
        hp = hp + (mod_ctx[5] * y_c.reshape(1, N_CTX, D_MODEL)).reshape(N_CTX, D_MODEL)
        hs = hs + (mod_lat[5] * y_l.reshape(DEC_BATCH, DEC_SEQ, D_MODEL)).reshape(N_LAT, D_MODEL)

    fg = final_g[None]
    y_prompt = _final_norm(hp, fg).reshape(BATCH, SEQ, D_MODEL)
    y_sample = _final_norm(hs, fg).reshape(DEC_BATCH, DEC_SEQ, D_MODEL)
    return (y_prompt, y_sample, jnp.stack(new_ak, axis=1), jnp.stack(new_av, axis=1),
            jnp.stack(new_nk, axis=1), jnp.stack(new_nv, axis=1))
```

```python
import functools

import jax
import jax.numpy as jnp
import numpy as np
from jax import lax
from jax.experimental import pallas as pl
from jax.experimental.pallas import tpu as pltpu

D_MODEL = 1024
BATCH = 32
SEQ = 256
DEPTH = 2
DEC_BATCH = 2
DEC_SEQ = 1024
PAST_LEN = 256
GRID_W = 64
HEAD_DIM = 64
A_HEADS = 8
A_KV_HEADS = 2
A_GROUP = A_HEADS // A_KV_HEADS
B_HEADS = 8
C_WIDTH = 512
NA_WIN_R = 8
NA_WIN_C = 16
N_EXPERTS = 16
EXPERT_FF = 1024
CAP_FACTOR = 2
ROPE_THETA = 10000.0
EPS = 1e-6
N_MOD = 6

A_Q = A_HEADS * HEAD_DIM
A_KV = A_KV_HEADS * HEAD_DIM
B_W = B_HEADS * HEAD_DIM
GATE_W = 3 * D_MODEL
COL_A = 0
W_A = A_Q + 2 * A_KV
COL_B = COL_A + W_A
W_B = 3 * B_W
COL_C = COL_B + W_B
W_C = 3 * C_WIDTH
COL_G = COL_C + W_C

N_CTX = BATCH * SEQ
N_LAT = DEC_BATCH * DEC_SEQ
CAP_CTX = CAP_FACTOR * N_CTX // N_EXPERTS
CAP_LAT = CAP_FACTOR * N_LAT // N_EXPERTS
CAP_ALL = CAP_CTX + CAP_LAT
LAT_ROWS = DEC_SEQ // GRID_W
MASK_NEG = -1e30
ATT_SCALE = HEAD_DIM ** -0.5

F32 = jnp.float32
BF16 = jnp.bfloat16
_NT = (((1,), (1,)), ((), ()))

_MIB = 1024 * 1024


def _params(sem, vmem_mib):
    return pltpu.CompilerParams(dimension_semantics=sem, vmem_limit_bytes=vmem_mib * _MIB)


def _adaln_kernel(cond_ref, w_ref, b_ref, o_ref):
    c = cond_ref[...]
    s = c * jax.nn.sigmoid(c)
    o_ref[...] = jnp.dot(s.astype(BF16), w_ref[...].astype(BF16), preferred_element_type=F32) + b_ref[...]


def _adaln(cond8, w_ada, b_ada):
    tn = 1536
    ncol = N_MOD * D_MODEL
    return pl.pallas_call(
        _adaln_kernel,
        grid=(DEPTH, ncol // tn),
        in_specs=[
            pl.BlockSpec((8, D_MODEL), lambda l, j: (0, 0)),
            pl.BlockSpec((None, D_MODEL, tn), lambda l, j: (l, 0, j)),
            pl.BlockSpec((None, 1, tn), lambda l, j: (l, 0, j)),
        ],
        out_specs=pl.BlockSpec((None, 8, tn), lambda l, j: (l, 0, j)),
        out_shape=jax.ShapeDtypeStruct((DEPTH, 8, ncol), F32),
        compiler_params=_params(("arbitrary", "arbitrary"), 40),
        name="adaln",
    )(cond8, w_ada, b_ada.reshape(DEPTH, 1, ncol))


def _norm_mod_kernel(x_ref, g_ref, sc_ref, sh_ref, o_ref):
    x = x_ref[...]
    y = x * lax.rsqrt(jnp.mean(x * x, axis=-1, keepdims=True) + EPS) * g_ref[...]
    o_ref[...] = (y * (1.0 + sc_ref[...]) + sh_ref[...]).astype(o_ref.dtype)


def _norm_mod(x, g, sc, sh, seq, out_dtype):
    n = x.shape[0]
    tm = 512
    per_batch = sc.shape[0] > 1
    mod_map = (lambda i: ((i * tm) // seq, 0, 0)) if per_batch else (lambda i: (0, 0, 0))
    return pl.pallas_call(
        _norm_mod_kernel,
        grid=(n // tm,),
        in_specs=[
            pl.BlockSpec((tm, D_MODEL), lambda i: (i, 0)),
            pl.BlockSpec((1, D_MODEL), lambda i: (0, 0)),
            pl.BlockSpec((None, 1, D_MODEL), mod_map),
            pl.BlockSpec((None, 1, D_MODEL), mod_map),
        ],
        out_specs=pl.BlockSpec((tm, D_MODEL), lambda i: (i, 0)),
        out_shape=jax.ShapeDtypeStruct((n, D_MODEL), out_dtype),
        compiler_params=_params(("arbitrary",), 32),
        name="norm_mod",
    )(x, g, sc, sh)


def _inproj_kernel(a_ref, w_ref, o_ref, wbf_ref):
    @pl.when(pl.program_id(1) == 0)
    def _():
        wbf_ref[...] = w_ref[...].astype(BF16)

    o_ref[...] = jnp.dot(a_ref[...], wbf_ref[...], preferred_element_type=F32).astype(o_ref.dtype)


def _inproj(xn, w_in, layer, col0, width, tn, out_dtype):
    n = xn.shape[0]
    tm = 2048
    joff = col0 // tn
    assert col0 % tn == 0 and width % tn == 0
    return pl.pallas_call(
        _inproj_kernel,
        grid=(width // tn, n // tm),
        in_specs=[
            pl.BlockSpec((tm, D_MODEL), lambda j, i: (i, 0)),
            pl.BlockSpec((None, D_MODEL, tn), lambda j, i: (layer, 0, j + joff)),
        ],
        out_specs=pl.BlockSpec((tm, tn), lambda j, i: (i, j)),
        out_shape=jax.ShapeDtypeStruct((n, width), out_dtype),
        scratch_shapes=[pltpu.VMEM((D_MODEL, tn), BF16)],
        compiler_params=_params(("arbitrary", "arbitrary"), 48),
        name="inproj",
    )(xn, w_in)


PAIR = 2 * HEAD_DIM


def _is_lo_half():
    return lax.broadcasted_iota(jnp.int32, (1, PAIR), 1) < HEAD_DIM


def _pair_rms(x, g2, is_lo):
    x2 = x * x
    s_lo = jnp.sum(jnp.where(is_lo, x2, 0.0), axis=-1, keepdims=True)
    s_hi = jnp.sum(jnp.where(is_lo, 0.0, x2), axis=-1, keepdims=True)
    ms = jnp.where(is_lo, s_lo, s_hi) * (1.0 / HEAD_DIM)
    return x * lax.rsqrt(ms + EPS) * g2


def _split_halves(x, is_lo):
    return jnp.where(is_lo, x, 0.0).astype(BF16), jnp.where(is_lo, 0.0, x).astype(BF16)


def _pad_lo(x):
    return jnp.concatenate([x, jnp.zeros_like(x)], axis=1)


def _pad_hi(x):
    return jnp.concatenate([jnp.zeros_like(x), x], axis=1)


def _qk(q, k):
    return lax.dot_general(q, k, _NT, preferred_element_type=F32)


def _softmax_pv(scores, values):
    m = scores[0].max(axis=-1, keepdims=True)
    for s in scores[1:]:
        m = jnp.maximum(m, s.max(axis=-1, keepdims=True))
    acc = None
    den = None
    for s, v in zip(scores, values):
        p = jnp.exp(s - m)
        d = p.sum(axis=-1, keepdims=True)
        o = jnp.dot(p.astype(BF16), v, preferred_element_type=F32)
        acc = o if acc is None else acc + o
        den = d if den is None else den + d
    return acc * (1.0 / den)


CTX_NB = 2


def _attn_ctx_kernel(ha_ref, hb_ref, qg_ref, kg_ref, oab_ref, ak_ref, av_ref, nk_ref, nv_ref):
    is_lo = _is_lo_half()
    qg2 = qg_ref[...] * ATT_SCALE
    kg2 = kg_ref[...]
    for b in range(CTX_NB):
        rows = slice(b * SEQ, (b + 1) * SEQ)
        kp = _pair_rms(ha_ref[rows, A_Q:A_Q + PAIR], kg2, is_lo)
        vp = ha_ref[rows, A_Q + A_KV:A_Q + A_KV + PAIR]
        for kv in range(A_KV_HEADS):
            ak_ref[b, kv] = kp[:, kv * HEAD_DIM:(kv + 1) * HEAD_DIM]
            av_ref[b, kv] = vp[:, kv * HEAD_DIM:(kv + 1) * HEAD_DIM]
        kp_sw = pltpu.roll(kp, HEAD_DIM, axis=1)
        vp_sw = pltpu.roll(vp, HEAD_DIM, axis=1)
        qs = [_pair_rms(ha_ref[rows, c * PAIR:(c + 1) * PAIR], qg2, is_lo).astype(BF16)
              for c in range(A_HEADS // 2)]
        for kv in range(A_KV_HEADS):
            k_lo = jnp.where(is_lo, kp if kv == 0 else kp_sw, 0.0).astype(BF16)
            k_hi = jnp.where(is_lo, 0.0, kp_sw if kv == 0 else kp).astype(BF16)
            v_lo = jnp.where(is_lo, vp if kv == 0 else vp_sw, 0.0).astype(BF16)
            v_hi = jnp.where(is_lo, 0.0, vp_sw if kv == 0 else vp).astype(BF16)
            q_st = jnp.concatenate([qs[2 * kv], qs[2 * kv + 1]], axis=0)
            o = _softmax_pv([_qk(q_st, k_lo)], [v_lo]) + _softmax_pv([_qk(q_st, k_hi)], [v_hi])
            oab_ref[rows, (2 * kv) * PAIR:(2 * kv + 1) * PAIR] = o[0:SEQ].astype(BF16)
            oab_ref[rows, (2 * kv + 1) * PAIR:(2 * kv + 2) * PAIR] = o[SEQ:2 * SEQ].astype(BF16)
        for j in range(B_HEADS // 2):
            qp = (hb_ref[rows, j * PAIR:(j + 1) * PAIR] * ATT_SCALE).astype(BF16)
            kp_b = hb_ref[rows, B_W + j * PAIR:B_W + (j + 1) * PAIR]
            vp_b = hb_ref[rows, 2 * B_W + j * PAIR:2 * B_W + (j + 1) * PAIR]
            for u in range(2):
                nk_ref[b, 2 * j + u] = kp_b[:, u * HEAD_DIM:(u + 1) * HEAD_DIM]
                nv_ref[b, 2 * j + u] = vp_b[:, u * HEAD_DIM:(u + 1) * HEAD_DIM]
            k_lo, k_hi = _split_halves(kp_b, is_lo)
            v_lo, v_hi = _split_halves(vp_b, is_lo)
            o = _softmax_pv([_qk(qp, k_lo)], [v_lo]) + _softmax_pv([_qk(qp, k_hi)], [v_hi])
            oab_ref[rows, A_Q + j * PAIR:A_Q + (j + 1) * PAIR] = o.astype(BF16)


def _attn_ctx(h_a, h_b, qg2, kg2):
    rows = CTX_NB * SEQ
    kv_spec = lambda nh: pl.BlockSpec((CTX_NB, nh, SEQ, HEAD_DIM), lambda b: (b, 0, 0, 0))
    kv_shape = lambda nh: jax.ShapeDtypeStruct((BATCH, nh, SEQ, HEAD_DIM), F32)
    return pl.pallas_call(
        _attn_ctx_kernel,
        grid=(BATCH // CTX_NB,),
        in_specs=[
            pl.BlockSpec((rows, W_A), lambda b: (b, 0)),
            pl.BlockSpec((rows, W_B), lambda b: (b, 0)),
            pl.BlockSpec((1, PAIR), lambda b: (0, 0)),
            pl.BlockSpec((1, PAIR), lambda b: (0, 0)),
        ],
        out_specs=[
            pl.BlockSpec((rows, A_Q + B_W), lambda b: (b, 0)),
            kv_spec(A_KV_HEADS), kv_spec(A_KV_HEADS), kv_spec(B_HEADS), kv_spec(B_HEADS),
        ],
        out_shape=[
            jax.ShapeDtypeStruct((N_CTX, A_Q + B_W), BF16),
            kv_shape(A_KV_HEADS), kv_shape(A_KV_HEADS), kv_shape(B_HEADS), kv_shape(B_HEADS),
        ],
        compiler_params=_params(("arbitrary",), 40),
        name="attn_ctx",
    )(h_a, h_b, qg2, kg2)


LAT_TQ = 256
LAT_QROWS = LAT_TQ // GRID_W
LAT_NT = DEC_SEQ // LAT_TQ
NA_UNION_ROWS = 12
NA_UNION = NA_UNION_ROWS * GRID_W


def _na_union_row0(t):
    lo = min(max(r - NA_WIN_R // 2, 0) for r in range(t * LAT_QROWS, (t + 1) * LAT_QROWS))
    return min(min(lo, LAT_ROWS - NA_WIN_R), LAT_ROWS - NA_UNION_ROWS)


def _rope(x, cos, sin):
    lane = lax.broadcasted_iota(jnp.int32, x.shape, 1)
    nxt = pltpu.roll(x, x.shape[1] - 1, axis=1)
    prv = pltpu.roll(x, 1, axis=1)
    partner = jnp.where((lane & 1) == 0, nxt, prv)
    return x * cos + partner * sin


def _attn_lat_kernel(haq_ref, haf_ref, hbq_ref, hbf_ref, cak_ref, cav_ref, cnk_ref, cnv_ref,
                     qg_ref, kg_ref, cos_ref, sin_ref, bias_ref, oab_ref, kva_ref):
    t = pl.program_id(1)
    is_lo = _is_lo_half()
    row0 = pl.multiple_of(t * LAT_TQ, LAT_TQ)
    qg2 = qg_ref[...] * ATT_SCALE
    kg2 = kg_ref[...]

    @pl.when(t == 0)
    def _():
        kp = _rope(_pair_rms(haf_ref[:, A_Q:A_Q + PAIR].astype(F32), kg2, is_lo), cos_ref[...], sin_ref[...])
        vp = haf_ref[:, A_Q + A_KV:A_Q + A_KV + PAIR].astype(F32)
        kp_sw = pltpu.roll(kp, HEAD_DIM, axis=1)
        vp_sw = pltpu.roll(vp, HEAD_DIM, axis=1)
        for kv in range(A_KV_HEADS):
            kva_ref[4 * kv + 0] = jnp.where(is_lo, kp if kv == 0 else kp_sw, 0.0).astype(BF16)
            kva_ref[4 * kv + 1] = jnp.where(is_lo, 0.0, kp_sw if kv == 0 else kp).astype(BF16)
            kva_ref[4 * kv + 2] = jnp.where(is_lo, vp if kv == 0 else vp_sw, 0.0).astype(BF16)
            kva_ref[4 * kv + 3] = jnp.where(is_lo, 0.0, vp_sw if kv == 0 else vp).astype(BF16)

    cos_q = cos_ref[pl.ds(row0, LAT_TQ), :]
    sin_q = sin_ref[pl.ds(row0, LAT_TQ), :]
    qs = [_rope(_pair_rms(haq_ref[:, c * PAIR:(c + 1) * PAIR].astype(F32), qg2, is_lo), cos_q, sin_q).astype(BF16)
          for c in range(A_HEADS // 2)]
    for kv in range(A_KV_HEADS):
        kc = cak_ref[kv].astype(BF16)
        vc = cav_ref[kv].astype(BF16)
        q_st = jnp.concatenate([qs[2 * kv], qs[2 * kv + 1]], axis=0)
        o = (_softmax_pv([_qk(q_st, _pad_lo(kc)), _qk(q_st, kva_ref[4 * kv + 0])], [_pad_lo(vc), kva_ref[4 * kv + 2]])
             + _softmax_pv([_qk(q_st, _pad_hi(kc)), _qk(q_st, kva_ref[4 * kv + 1])], [_pad_hi(vc), kva_ref[4 * kv + 3]]))
        oab_ref[:, (2 * kv) * PAIR:(2 * kv + 1) * PAIR] = o[0:LAT_TQ].astype(BF16)
        oab_ref[:, (2 * kv + 1) * PAIR:(2 * kv + 2) * PAIR] = o[LAT_TQ:2 * LAT_TQ].astype(BF16)

    k0 = _na_union_row0(0) * GRID_W
    for tt in range(1, LAT_NT):
        k0 = jnp.where(t >= tt, _na_union_row0(tt) * GRID_W, k0)
    k0 = pl.multiple_of(k0, GRID_W)
    for j in range(B_HEADS // 2):
        qp = (hbq_ref[:, j * PAIR:(j + 1) * PAIR] * ATT_SCALE).astype(BF16)
        k_lo, k_hi = _split_halves(hbf_ref[pl.ds(k0, NA_UNION), B_W + j * PAIR:B_W + (j + 1) * PAIR], is_lo)
        v_lo, v_hi = _split_halves(hbf_ref[pl.ds(k0, NA_UNION), 2 * B_W + j * PAIR:2 * B_W + (j + 1) * PAIR], is_lo)
        kc_lo = _pad_lo(cnk_ref[2 * j].astype(BF16))
        kc_hi = _pad_hi(cnk_ref[2 * j + 1].astype(BF16))
        vc_lo = _pad_lo(cnv_ref[2 * j].astype(BF16))
        vc_hi = _pad_hi(cnv_ref[2 * j + 1].astype(BF16))
        o = (_softmax_pv([_qk(qp, kc_lo), _qk(qp, k_lo) + bias_ref[2 * j]], [vc_lo, v_lo])
             + _softmax_pv([_qk(qp, kc_hi), _qk(qp, k_hi) + bias_ref[2 * j + 1]], [vc_hi, v_hi]))
        oab_ref[:, A_Q + j * PAIR:A_Q + (j + 1) * PAIR] = o.astype(BF16)


def _attn_lat(h_a, h_b, cak, cav, cnk, cnv, layer, qg2, kg2, cos128, sin128, bias):
    nt = LAT_NT
    cache_spec = lambda nh: pl.BlockSpec((None, None, nh, PAST_LEN, HEAD_DIM), lambda b, t: (b, layer, 0, 0, 0))
    return pl.pallas_call(
        _attn_lat_kernel,
        grid=(DEC_BATCH, nt),
        in_specs=[
            pl.BlockSpec((LAT_TQ, W_A), lambda b, t: (b * nt + t, 0)),
            pl.BlockSpec((DEC_SEQ, W_A), lambda b, t: (b, 0)),
            pl.BlockSpec((LAT_TQ, W_B), lambda b, t: (b * nt + t, 0)),
            pl.BlockSpec((DEC_SEQ, W_B), lambda b, t: (b, 0)),
            cache_spec(A_KV_HEADS), cache_spec(A_KV_HEADS), cache_spec(B_HEADS), cache_spec(B_HEADS),
            pl.BlockSpec((1, PAIR), lambda b, t: (0, 0)),
            pl.BlockSpec((1, PAIR), lambda b, t: (0, 0)),
            pl.BlockSpec((DEC_SEQ, PAIR), lambda b, t: (0, 0)),
            pl.BlockSpec((DEC_SEQ, PAIR), lambda b, t: (0, 0)),
            pl.BlockSpec((B_HEADS, None, LAT_TQ, NA_UNION), lambda b, t: (0, t, 0, 0)),
        ],
        out_specs=pl.BlockSpec((LAT_TQ, A_Q + B_W), lambda b, t: (b * nt + t, 0)),
        out_shape=jax.ShapeDtypeStruct((N_LAT, A_Q + B_W), BF16),
        scratch_shapes=[pltpu.VMEM((4 * A_KV_HEADS, DEC_SEQ, PAIR), BF16)],
        compiler_params=_params(("arbitrary", "arbitrary"), 60),
        name="attn_lat",
    )(h_a, h_a, h_b, h_b, cak, cav, cnk, cnv, qg2, kg2, cos128, sin128, bias)


def _rope_tables():
    t = jnp.arange(DEC_SEQ)
    row = (t // GRID_W).astype(F32)
    col = (t % GRID_W).astype(F32)
    n_freq = HEAD_DIM // 4
    inv = ROPE_THETA ** (-jnp.arange(n_freq, dtype=F32) / n_freq)
    ang = jnp.concatenate([row[:, None] * inv, col[:, None] * inv], axis=-1)
    cos = jnp.repeat(jnp.cos(ang), 2, axis=-1)
    sign = jnp.tile(jnp.array([-1.0, 1.0], F32), HEAD_DIM // 2)
    sin = jnp.repeat(jnp.sin(ang), 2, axis=-1) * sign
    return jnp.tile(cos, (1, 2)), jnp.tile(sin, (1, 2))


def _na_bias(rpb):
    col = np.arange(GRID_W)
    dc = np.clip(col[None, :] - col[:, None] + NA_WIN_C - 1, 0, 2 * NA_WIN_C - 2)
    onehot = (dc[..., None] == np.arange(2 * NA_WIN_C - 1)).astype(np.float32)
    c0 = np.clip(col - NA_WIN_C // 2, 0, GRID_W - NA_WIN_C)
    in_win = (col[None, :] >= c0[:, None]) & (col[None, :] < c0[:, None] + NA_WIN_C)
    toep = jnp.einsum('hdm,ckm->hdck', rpb, onehot, precision=lax.Precision.HIGHEST)
    toep = jnp.where(in_win[None, None], toep, MASK_NEG)
    masked = jnp.full((B_HEADS, GRID_W, GRID_W), MASK_NEG, F32)
    tiles = []
    for t in range(LAT_NT):
        u0 = _na_union_row0(t)
        q_rows = []
        for r in range(t * LAT_QROWS, (t + 1) * LAT_QROWS):
            r0 = min(max(r - NA_WIN_R // 2, 0), LAT_ROWS - NA_WIN_R)
            q_rows.append(jnp.concatenate(
                [toep[:, kr - r + NA_WIN_R - 1] if r0 <= kr < r0 + NA_WIN_R else masked
                 for kr in range(u0, u0 + NA_UNION_ROWS)], axis=-1))
        tiles.append(jnp.concatenate(q_rows, axis=1))
    return jnp.stack(tiles, axis=1)


def _conv_kernel(hc_ref, w_ref, o_ref, *, seq):
    bg = hc_ref[:, 0:C_WIDTH].astype(F32)
    u = hc_ref[:, C_WIDTH:2 * C_WIDTH].astype(F32) * hc_ref[:, 2 * C_WIDTH:3 * C_WIDTH].astype(F32)
    rows = u.shape[0]
    assert seq & (seq - 1) == 0
    pos = lax.broadcasted_iota(jnp.int32, u.shape, 0) & (seq - 1)
    u_prev = jnp.where(pos == 0, 0.0, pltpu.roll(u, 1, axis=0))
    u_next = jnp.where(pos == seq - 1, 0.0, pltpu.roll(u, rows - 1, axis=0))
    y = u_prev * w_ref[0:1, :] + u * w_ref[1:2, :] + u_next * w_ref[2:3, :]
    o_ref[...] = (bg * y).astype(o_ref.dtype)


def _conv(h_c, conv_w, seq):
    n = h_c.shape[0]
    tm = 1024
    return pl.pallas_call(
        functools.partial(_conv_kernel, seq=seq),
        grid=(n // tm,),
        in_specs=[pl.BlockSpec((tm, W_C), lambda i: (i, 0)), pl.BlockSpec((3, C_WIDTH), lambda i: (0, 0))],
        out_specs=pl.BlockSpec((tm, C_WIDTH), lambda i: (i, 0)),
        out_shape=jax.ShapeDtypeStruct((n, C_WIDTH), BF16),
        compiler_params=_params(("arbitrary",), 40),
        name="short_conv",
    )(h_c, conv_w)


def _mix_kernel(x_ref, oab_ref, oc_ref, hg_ref, wa_ref, wb_ref, wc_ref, wo_ref, g1_ref, n2g_ref, sc2_ref, sh2_ref,
                wrt_ref, xo_ref, xn2_ref, pt_ref):
    ga = jax.nn.sigmoid(hg_ref[:, 0:D_MODEL].astype(F32))
    gb = jax.nn.sigmoid(hg_ref[:, D_MODEL:2 * D_MODEL].astype(F32))
    gc = jax.nn.sigmoid(hg_ref[:, 2 * D_MODEL:3 * D_MODEL].astype(F32))
    pa = jnp.dot(oab_ref[:, 0:A_Q], wa_ref[...], preferred_element_type=F32)
    pb = jnp.dot(oab_ref[:, A_Q:A_Q + B_W], wb_ref[...], preferred_element_type=F32)
    pc = jnp.dot(oc_ref[...], wc_ref[...], preferred_element_type=F32)
    merged = ga * pa + gb * pb + gc * pc
    xo = x_ref[...] + g1_ref[...] * jnp.dot(merged.astype(BF16), wo_ref[...], preferred_element_type=F32)
    xo_ref[...] = xo
    y = xo * lax.rsqrt(jnp.mean(xo * xo, axis=-1, keepdims=True) + EPS) * n2g_ref[...]
    xn2 = y * (1.0 + sc2_ref[...]) + sh2_ref[...]
    xn2_ref[...] = xn2
    logits_t = lax.dot_general(wrt_ref[...], xn2.astype(BF16), _NT, preferred_element_type=F32)
    z = jnp.exp(logits_t - logits_t.max(axis=0, keepdims=True))
    pt_ref[...] = z / z.sum(axis=0, keepdims=True)


def _mix(x, oab, oc, h_g, wa, wb, wc, wo, g1, n2g, sc2, sh2, wrt, seq):
    n = x.shape[0]
    tm = 512
    per_batch = g1.shape[0] > 1
    mod_map = (lambda i: ((i * tm) // seq, 0, 0)) if per_batch else (lambda i: (0, 0, 0))
    mod_spec = pl.BlockSpec((None, 1, D_MODEL), mod_map)
    full = lambda a: pl.BlockSpec(a.shape, lambda i: (0,) * a.ndim)
    row = lambda w: pl.BlockSpec((tm, w), lambda i: (i, 0))
    return pl.pallas_call(
        _mix_kernel,
        grid=(n // tm,),
        in_specs=[row(D_MODEL), row(A_Q + B_W), row(C_WIDTH), row(GATE_W), full(wa), full(wb), full(wc), full(wo),
                  mod_spec, full(n2g), mod_spec, mod_spec, full(wrt)],
        out_specs=[row(D_MODEL), row(D_MODEL), pl.BlockSpec((N_EXPERTS, tm), lambda i: (0, i))],
        out_shape=[jax.ShapeDtypeStruct((n, D_MODEL), F32), jax.ShapeDtypeStruct((n, D_MODEL), F32),
                   jax.ShapeDtypeStruct((N_EXPERTS, n), F32)],
        compiler_params=_params(("arbitrary",), 56),
        name="mix",
    )(x, oab, oc, h_g, wa, wb, wc, wo, g1, n2g, sc2, sh2, wrt)


MOE_TF = 512
MOE_ROW_CHUNKS = ((0, 512), (512, 512), (1024, 256))
MOE_ISSUE_UNROLL = 8


def _moe_issue_rows(idx_ref, idx0, src_hbm, dst_ref, dst_row0, count, sem):
    def body(i, carry):
        for u in range(MOE_ISSUE_UNROLL):
            r = i * MOE_ISSUE_UNROLL + u
            tok = idx_ref[idx0 + r]
            pltpu.make_async_copy(src_hbm.at[pl.ds(tok, 1)], dst_ref.at[pl.ds(dst_row0 + r, 1)], sem).start()
        return carry

    lax.fori_loop(0, count // MOE_ISSUE_UNROLL, body, 0)


def _moe_kernel(idx_ref, xc_hbm, xl_hbm, aff_ref, wg_ref, wu_ref, wd_ref, ye_ref, xe_ref, sem):
    e = pl.program_id(0)
    f = pl.program_id(1)
    slot = e % 2

    def gather(expert, s):
        base = expert * CAP_ALL
        _moe_issue_rows(idx_ref, base, xc_hbm, xe_ref.at[s], 0, CAP_CTX, sem.at[s])
        _moe_issue_rows(idx_ref, base + CAP_CTX, xl_hbm, xe_ref.at[s], CAP_CTX, CAP_LAT, sem.at[s])

    @pl.when(f == 0)
    def _():
        @pl.when(e == 0)
        def _():
            gather(0, 0)

        pltpu.make_async_copy(xc_hbm.at[pl.ds(0, CAP_ALL)], xe_ref.at[slot], sem.at[slot]).wait()

        @pl.when(e + 1 < N_EXPERTS)
        def _():
            gather(e + 1, 1 - slot)

    wg = wg_ref[...].astype(BF16)
    wu = wu_ref[...].astype(BF16)
    wd = wd_ref[...].astype(BF16)
    for r0, rn in MOE_ROW_CHUNKS:
        xe = xe_ref[slot, r0:r0 + rn, :].astype(BF16)
        hg = jnp.dot(xe, wg, preferred_element_type=F32)
        hu = jnp.dot(xe, wu, preferred_element_type=F32)
        act = (hg * jax.nn.sigmoid(hg) * hu).astype(BF16)
        part = jnp.dot(act, wd, preferred_element_type=F32) * aff_ref[r0:r0 + rn, :]

        @pl.when(f == 0)
        def _():
            ye_ref[r0:r0 + rn, :] = part

        @pl.when(f != 0)
        def _():
            ye_ref[r0:r0 + rn, :] += part


def _moe(idx_all, xn2_ctx, xn2_lat, aff_all, w_gate_e, w_up_e, w_down_e, layer):
    nf = EXPERT_FF // MOE_TF
    grid_spec = pltpu.PrefetchScalarGridSpec(
        num_scalar_prefetch=1,
        grid=(N_EXPERTS, nf),
        in_specs=[
            pl.BlockSpec(memory_space=pl.ANY),
            pl.BlockSpec(memory_space=pl.ANY),
            pl.BlockSpec((None, CAP_ALL, 1), lambda e, f, idx: (e, 0, 0)),
            pl.BlockSpec((None, None, D_MODEL, MOE_TF), lambda e, f, idx: (layer, e, 0, f)),
            pl.BlockSpec((None, None, D_MODEL, MOE_TF), lambda e, f, idx: (layer, e, 0, f)),
            pl.BlockSpec((None, None, MOE_TF, D_MODEL), lambda e, f, idx: (layer, e, f, 0)),
        ],
        out_specs=pl.BlockSpec((None, CAP_ALL, D_MODEL), lambda e, f, idx: (e, 0, 0)),
        scratch_shapes=[pltpu.VMEM((2, CAP_ALL, D_MODEL), F32), pltpu.SemaphoreType.DMA((2,))],
    )
    return pl.pallas_call(
        _moe_kernel,
        grid_spec=grid_spec,
        out_shape=jax.ShapeDtypeStruct((N_EXPERTS, CAP_ALL, D_MODEL), F32),
        compiler_params=_params(("arbitrary", "arbitrary"), 56),
        name="moe_experts",
    )(idx_all, xn2_ctx, xn2_lat, aff_all, w_gate_e, w_up_e, w_down_e)


def _route(probs_t, cap):
    aff, idx = lax.top_k(probs_t, cap)
    order = jnp.argsort(idx, axis=-1)
    return jnp.take_along_axis(idx, order, axis=-1), jnp.take_along_axis(aff, order, axis=-1)


CMB_TT = 1024
CMB_CH = 256
CMB_GROUP = 8
CMB_EPI_ROWS = 256


def _combine_kernel(idx_ref, st_ref, ye_hbm, x_ref, g2_ref, ng_ref, sc_ref, sh_ref, *rest, off, cap, nt, final):
    if final:
        y_ref, acc_ref, buf_ref, sem = rest
    else:
        xo_ref, xn_ref, acc_ref, buf_ref, sem = rest
    k = pl.program_id(0)
    t0 = k * CMB_TT
    acc_ref[...] = jnp.zeros_like(acc_ref)

    def slot_range(e):
        s0 = st_ref[e * (nt + 1) + k]
        s1 = st_ref[e * (nt + 1) + k + 1]
        return s0, s1, jnp.minimum(s0 & -CMB_GROUP, cap - CMB_CH)

    def chunk_copy(e, start):
        src = ye_hbm.at[e, pl.ds(pl.multiple_of(off + start, CMB_GROUP), CMB_CH)]
        return pltpu.make_async_copy(src, buf_ref.at[e], sem.at[e])

    def accumulate(e, start, lo, hi):
        def body(i, carry):
            r0 = pl.multiple_of(i * CMB_GROUP, CMB_GROUP)
            tile = buf_ref[e, pl.ds(r0, CMB_GROUP), :]
            dst = []
            for u in range(CMB_GROUP):
                slot = start + r0 + u
                tok = idx_ref[e * CAP_ALL + off + slot]
                dst.append(jnp.where((slot >= lo) & (slot < hi), tok - t0, CMB_TT))
            rows = [acc_ref[pl.ds(d, 1), :] for d in dst]
            for u in range(CMB_GROUP):
                acc_ref[pl.ds(dst[u], 1), :] = rows[u] + tile[u:u + 1, :]
            return carry

        lax.fori_loop((lo - start) // CMB_GROUP, (hi - start + CMB_GROUP - 1) // CMB_GROUP, body, 0)

    for e in range(N_EXPERTS):
        chunk_copy(e, slot_range(e)[2]).start()
    for e in range(N_EXPERTS):
        s0, s1, a = slot_range(e)
        chunk_copy(e, a).wait()
        accumulate(e, a, s0, jnp.minimum(s1, a + CMB_CH))

        def extra(j, carry, e=e, s1=s1, a=a):
            lo = a + CMB_CH * (j + 1)
            start = jnp.minimum(lo, cap - CMB_CH)
            cp = chunk_copy(e, start)
            cp.start()
            cp.wait()
            accumulate(e, start, lo, jnp.minimum(s1, lo + CMB_CH))
            return carry

        lax.fori_loop(0, jnp.maximum(s1 - a - 1, 0) // CMB_CH, extra, 0)

    for r0 in range(0, CMB_TT, CMB_EPI_ROWS):
        rs = slice(r0, r0 + CMB_EPI_ROWS)
        xo = x_ref[rs, :] + g2_ref[...] * acc_ref[rs, :]
        y = xo * lax.rsqrt(jnp.mean(xo * xo, axis=-1, keepdims=True) + EPS) * ng_ref[...]
        if final:
            y_ref[rs, :] = y
        else:
            xo_ref[rs, :] = xo
            xn_ref[rs, :] = (y * (1.0 + sc_ref[...]) + sh_ref[...]).astype(xn_ref.dtype)


def _combine(idx_all, starts, ye, x, g2, ng, sc, sh, seq, off, cap, final):
    n = x.shape[0]
    nt = n // CMB_TT
    per_batch = g2.shape[0] > 1
    mod_map = ((lambda i, *_: ((i * CMB_TT) // seq, 0, 0)) if per_batch else (lambda i, *_: (0, 0, 0)))
    mod_spec = pl.BlockSpec((None, 1, D_MODEL), mod_map)
    row = pl.BlockSpec((CMB_TT, D_MODEL), lambda i, *_: (i, 0))
    vec = pl.BlockSpec((1, D_MODEL), lambda i, *_: (0, 0))
    if final:
        out_specs, out_shape = row, jax.ShapeDtypeStruct((n, D_MODEL), F32)
    else:
        out_specs = [row, row]
        out_shape = [jax.ShapeDtypeStruct((n, D_MODEL), F32), jax.ShapeDtypeStruct((n, D_MODEL), BF16)]
    grid_spec = pltpu.PrefetchScalarGridSpec(
        num_scalar_prefetch=2,
        grid=(nt,),
        in_specs=[pl.BlockSpec(memory_space=pl.ANY), row, mod_spec, vec, mod_spec, mod_spec],
        out_specs=out_specs,
        scratch_shapes=[pltpu.VMEM((CMB_TT + CMB_GROUP, D_MODEL), F32),
                        pltpu.VMEM((N_EXPERTS, CMB_CH, D_MODEL), F32),
                        pltpu.SemaphoreType.DMA((N_EXPERTS,))],
    )
    return pl.pallas_call(
        functools.partial(_combine_kernel, off=off, cap=cap, nt=nt, final=final),
        grid_spec=grid_spec,
        out_shape=out_shape,
        compiler_params=_params(("arbitrary",), 60),
        name="combine",
    )(idx_all, starts.reshape(-1), ye, x, g2, ng, sc, sh)


def _tile_starts(idx_sorted, n):
    bounds = jnp.arange(n // CMB_TT + 1, dtype=jnp.int32) * CMB_TT
    return jnp.sum(idx_sorted[:, :, None] < bounds[None, None, :], axis=1, dtype=jnp.int32)


def kernel(x_prompt, x_sample, cache_attn_k, cache_attn_v, cache_na_k, cache_na_v, c, c_ctx, w_ada, b_ada, norm1_g,
           w_in, q_norm_g, k_norm_g, na_rpb, conv_w, w_proj_a, w_proj_b, w_proj_c, w_out, norm2_g, w_router,
           w_gate_e, w_up_e, w_down_e, final_g):
    cond8 = jnp.zeros((8, D_MODEL), F32).at[0].set(c_ctx).at[1:1 + DEC_BATCH].set(c)
    mods = _adaln(cond8, w_ada, b_ada).reshape(DEPTH, 8, N_MOD, D_MODEL)
    cos128, sin128 = _rope_tables()

    hp = x_prompt.reshape(N_CTX, D_MODEL)
    hs = x_sample.reshape(N_LAT, D_MODEL)
    new_ak, new_av, new_nk, new_nv = [], [], [], []
    for l in range(DEPTH):
        mod_ctx = [mods[l, 0:1, i][:, None, :] for i in range(N_MOD)]
        mod_lat = [mods[l, 1:1 + DEC_BATCH, i][:, None, :] for i in range(N_MOD)]
        n1g = norm1_g[l][None]
        n2g = norm2_g[l][None]
        qg2 = jnp.tile(q_norm_g[l][None], (1, 2))
        kg2 = jnp.tile(k_norm_g[l][None], (1, 2))
        wa, wb, wc, wo = (w_proj_a[l].astype(BF16), w_proj_b[l].astype(BF16), w_proj_c[l].astype(BF16),
                          w_out[l].astype(BF16))
        wrt = w_router[l].T.astype(BF16)
        bias = _na_bias(na_rpb[l])

        def sublayer1(x, xn, mod, seq, is_ctx):
            _, _, g1, sh2, sc2, _ = mod
            kv_dtype = F32 if is_ctx else BF16
            h_a = _inproj(xn, w_in, l, COL_A, W_A, 768, kv_dtype)
            h_b = _inproj(xn, w_in, l, COL_B, W_B, 768, kv_dtype)
            h_c = _inproj(xn, w_in, l, COL_C, W_C, 768, BF16)
            h_g = _inproj(xn, w_in, l, COL_G, GATE_W, 768, BF16)
            kv = None
            if is_ctx:
                oab, ka, va, kb, vb = _attn_ctx(h_a, h_b, qg2, kg2)
                kv = (ka, va, kb, vb)
            else:
                oab = _attn_lat(h_a, h_b, cache_attn_k, cache_attn_v, cache_na_k, cache_na_v, l, qg2, kg2,
                                cos128, sin128, bias)
            oc = _conv(h_c, conv_w[l], seq)
            xo, xn2, probs_t = _mix(x, oab, oc, h_g, wa, wb, wc, wo, g1, n2g, sc2, sh2, wrt, seq)
            return xo, xn2, probs_t, kv

        if l == 0:
            xn_p = _norm_mod(hp, n1g, mod_ctx[1], mod_ctx[0], SEQ, BF16)
            xn_s = _norm_mod(hs, n1g, mod_lat[1], mod_lat[0], DEC_SEQ, BF16)
        hp, xn2_c, pt_c, (ka, va, kb, vb) = sublayer1(hp, xn_p, mod_ctx, SEQ, True)
        hs, xn2_l, pt_l, _ = sublayer1(hs, xn_s, mod_lat, DEC_SEQ, False)
        new_ak.append(ka)
        new_av.append(va)
        new_nk.append(kb)
        new_nv.append(vb)

        idx_c, aff_c = _route(pt_c, CAP_CTX)
        idx_l, aff_l = _route(pt_l, CAP_LAT)
        idx_all = jnp.concatenate([idx_c, idx_l], axis=1).reshape(-1).astype(jnp.int32)
        aff_all = jnp.concatenate([aff_c, aff_l], axis=1)[..., None]
        ye = _moe(idx_all, xn2_c, xn2_l, aff_all, w_gate_e, w_up_e, w_down_e, l)

        final = l == DEPTH - 1
        if final:
            ng, nsc_c, nsh_c, nsc_l, nsh_l = final_g[None], mod_ctx[1], mod_ctx[0], mod_lat[1], mod_lat[0]
        else:
            nxt_ctx = [mods[l + 1, 0:1, i][:, None, :] for i in range(2)]
            nxt_lat = [mods[l + 1, 1:1 + DEC_BATCH, i][:, None, :] for i in range(2)]
            ng, nsc_c, nsh_c, nsc_l, nsh_l = norm1_g[l + 1][None], nxt_ctx[1], nxt_ctx[0], nxt_lat[1], nxt_lat[0]
        out_c = _combine(idx_all, _tile_starts(idx_c, N_CTX), ye, hp, mod_ctx[5], ng, nsc_c, nsh_c,
                         SEQ, 0, CAP_CTX, final)
        out_l = _combine(idx_all, _tile_starts(idx_l, N_LAT), ye, hs, mod_lat[5], ng, nsc_l, nsh_l,
                         DEC_SEQ, CAP_CTX, CAP_LAT, final)
        if final:
            y_prompt = out_c.reshape(BATCH, SEQ, D_MODEL)
            y_sample = out_l.reshape(DEC_BATCH, DEC_SEQ, D_MODEL)
        else:
            (hp, xn_p), (hs, xn_s) = out_c, out_l

    return (y_prompt, y_sample, jnp.stack(new_ak, axis=1), jnp.stack(new_av, axis=1),
            jnp.stack(new_nk, axis=1), jnp.stack(new_nv, axis=1))
```

```python
import functools

import jax
import jax.numpy as jnp
import numpy as np
from jax import lax
from jax.experimental import pallas as pl
from jax.experimental.pallas import tpu as pltpu

D_MODEL = 1024
BATCH = 32
SEQ = 256
DEPTH = 2
DEC_BATCH = 2
DEC_SEQ = 1024
PAST_LEN = 256
GRID_W = 64
HEAD_DIM = 64
A_HEADS = 8
A_KV_HEADS = 2
A_GROUP = A_HEADS // A_KV_HEADS
B_HEADS = 8
C_WIDTH = 512
NA_WIN_R = 8
NA_WIN_C = 16
N_EXPERTS = 16
EXPERT_FF = 1024
CAP_FACTOR = 2
ROPE_THETA = 10000.0
EPS = 1e-6
N_MOD = 6

A_Q = A_HEADS * HEAD_DIM
A_KV = A_KV_HEADS * HEAD_DIM
B_W = B_HEADS * HEAD_DIM
GATE_W = 3 * D_MODEL
COL_A = 0
W_A = A_Q + 2 * A_KV
COL_B = COL_A + W_A
W_B = 3 * B_W
COL_C = COL_B + W_B
W_C = 3 * C_WIDTH
COL_G = COL_C + W_C

N_CTX = BATCH * SEQ
N_LAT = DEC_BATCH * DEC_SEQ
CAP_CTX = CAP_FACTOR * N_CTX // N_EXPERTS
CAP_LAT = CAP_FACTOR * N_LAT // N_EXPERTS
CAP_ALL = CAP_CTX + CAP_LAT
LAT_ROWS = DEC_SEQ // GRID_W
MASK_NEG = -1e30
ATT_SCALE = HEAD_DIM ** -0.5

F32 = jnp.float32
BF16 = jnp.bfloat16
_NT = (((1,), (1,)), ((), ()))

_MIB = 1024 * 1024


def _params(sem, vmem_mib):
    return pltpu.CompilerParams(dimension_semantics=sem, vmem_limit_bytes=vmem_mib * _MIB)


def _adaln_kernel(cond_ref, w_ref, b_ref, o_ref):
    c = cond_ref[...]
    s = c * jax.nn.sigmoid(c)
    o_ref[...] = jnp.dot(s.astype(BF16), w_ref[...].astype(BF16), preferred_element_type=F32) + b_ref[...]


def _adaln(cond8, w_ada, b_ada):
    tn = 1536
    ncol = N_MOD * D_MODEL
    return pl.pallas_call(
        _adaln_kernel,
        grid=(DEPTH, ncol // tn),
        in_specs=[
            pl.BlockSpec((8, D_MODEL), lambda l, j: (0, 0)),
            pl.BlockSpec((None, D_MODEL, tn), lambda l, j: (l, 0, j)),
            pl.BlockSpec((None, 1, tn), lambda l, j: (l, 0, j)),
        ],
        out_specs=pl.BlockSpec((None, 8, tn), lambda l, j: (l, 0, j)),
        out_shape=jax.ShapeDtypeStruct((DEPTH, 8, ncol), F32),
        compiler_params=_params(("arbitrary", "arbitrary"), 40),
        name="adaln",
    )(cond8, w_ada, b_ada.reshape(DEPTH, 1, ncol))


def _norm_mod_kernel(x_ref, g_ref, sc_ref, sh_ref, o_ref):
    x = x_ref[...]
    y = x * lax.rsqrt(jnp.mean(x * x, axis=-1, keepdims=True) + EPS) * g_ref[...]
    o_ref[...] = (y * (1.0 + sc_ref[...]) + sh_ref[...]).astype(o_ref.dtype)


def _norm_mod(x, g, sc, sh, seq, out_dtype):
    n = x.shape[0]
    tm = 512
    per_batch = sc.shape[0] > 1
    mod_map = (lambda i: ((i * tm) // seq, 0, 0)) if per_batch else (lambda i: (0, 0, 0))
    return pl.pallas_call(
        _norm_mod_kernel,
        grid=(n // tm,),
        in_specs=[
            pl.BlockSpec((tm, D_MODEL), lambda i: (i, 0)),
            pl.BlockSpec((1, D_MODEL), lambda i: (0, 0)),
            pl.BlockSpec((None, 1, D_MODEL), mod_map),
            pl.BlockSpec((None, 1, D_MODEL), mod_map),
        ],
        out_specs=pl.BlockSpec((tm, D_MODEL), lambda i: (i, 0)),
        out_shape=jax.ShapeDtypeStruct((n, D_MODEL), out_dtype),
        compiler_params=_params(("arbitrary",), 32),
        name="norm_mod",
    )(x, g, sc, sh)


def _inproj_kernel(a_ref, w_ref, o_ref, wbf_ref):
    @pl.when(pl.program_id(1) == 0)
    def _():
        wbf_ref[...] = w_ref[...].astype(BF16)

    o_ref[...] = jnp.dot(a_ref[...], wbf_ref[...], preferred_element_type=F32).astype(o_ref.dtype)


def _inproj(xn, w_in, layer, col0, width, tn, out_dtype):
    n = xn.shape[0]
    tm = 2048
    joff = col0 // tn
    assert col0 % tn == 0 and width % tn == 0
    return pl.pallas_call(
        _inproj_kernel,
        grid=(width // tn, n // tm),
        in_specs=[
            pl.BlockSpec((tm, D_MODEL), lambda j, i: (i, 0)),
            pl.BlockSpec((None, D_MODEL, tn), lambda j, i: (layer, 0, j + joff)),
        ],
        out_specs=pl.BlockSpec((tm, tn), lambda j, i: (i, j)),
        out_shape=jax.ShapeDtypeStruct((n, width), out_dtype),
        scratch_shapes=[pltpu.VMEM((D_MODEL, tn), BF16)],
        compiler_params=_params(("arbitrary", "arbitrary"), 48),
        name="inproj",
    )(xn, w_in)


PAIR = 2 * HEAD_DIM


def _is_lo_half():
    return lax.broadcasted_iota(jnp.int32, (1, PAIR), 1) < HEAD_DIM


def _pair_rms(x, g2, is_lo):
    x2 = x * x
    s_lo = jnp.sum(jnp.where(is_lo, x2, 0.0), axis=-1, keepdims=True)
    s_hi = jnp.sum(jnp.where(is_lo, 0.0, x2), axis=-1, keepdims=True)
    ms = jnp.where(is_lo, s_lo, s_hi) * (1.0 / HEAD_DIM)
    return x * lax.rsqrt(ms + EPS) * g2


def _split_halves(x, is_lo):
    return jnp.where(is_lo, x, 0.0).astype(BF16), jnp.where(is_lo, 0.0, x).astype(BF16)


def _pad_lo(x):
    return jnp.concatenate([x, jnp.zeros_like(x)], axis=1)


def _pad_hi(x):
    return jnp.concatenate([jnp.zeros_like(x), x], axis=1)


def _qk(q, k):
    return lax.dot_general(q, k, _NT, preferred_element_type=F32)


def _softmax_pv(scores, values):
    m = scores[0].max(axis=-1, keepdims=True)
    for s in scores[1:]:
        m = jnp.maximum(m, s.max(axis=-1, keepdims=True))
    acc = None
    den = None
    for s, v in zip(scores, values):
        p = jnp.exp(s - m)
        d = p.sum(axis=-1, keepdims=True)
        o = jnp.dot(p.astype(BF16), v, preferred_element_type=F32)
        acc = o if acc is None else acc + o
        den = d if den is None else den + d
    return acc * (1.0 / den)


CTX_NB = 2


def _attn_ctx_kernel(ha_ref, hb_ref, qg_ref, kg_ref, oab_ref, ak_ref, av_ref, nk_ref, nv_ref):
    is_lo = _is_lo_half()
    qg2 = qg_ref[...] * ATT_SCALE
    kg2 = kg_ref[...]
    for b in range(CTX_NB):
        rows = slice(b * SEQ, (b + 1) * SEQ)
        kp = _pair_rms(ha_ref[rows, A_Q:A_Q + PAIR], kg2, is_lo)
        vp = ha_ref[rows, A_Q + A_KV:A_Q + A_KV + PAIR]
        for kv in range(A_KV_HEADS):
            ak_ref[b, kv] = kp[:, kv * HEAD_DIM:(kv + 1) * HEAD_DIM]
            av_ref[b, kv] = vp[:, kv * HEAD_DIM:(kv + 1) * HEAD_DIM]
        kp_sw = pltpu.roll(kp, HEAD_DIM, axis=1)
        vp_sw = pltpu.roll(vp, HEAD_DIM, axis=1)
        qs = [_pair_rms(ha_ref[rows, c * PAIR:(c + 1) * PAIR], qg2, is_lo).astype(BF16)
              for c in range(A_HEADS // 2)]
        for kv in range(A_KV_HEADS):
            k_lo = jnp.where(is_lo, kp if kv == 0 else kp_sw, 0.0).astype(BF16)
            k_hi = jnp.where(is_lo, 0.0, kp_sw if kv == 0 else kp).astype(BF16)
            v_lo = jnp.where(is_lo, vp if kv == 0 else vp_sw, 0.0).astype(BF16)
            v_hi = jnp.where(is_lo, 0.0, vp_sw if kv == 0 else vp).astype(BF16)
            q_st = jnp.concatenate([qs[2 * kv], qs[2 * kv + 1]], axis=0)
            o = _softmax_pv([_qk(q_st, k_lo)], [v_lo]) + _softmax_pv([_qk(q_st, k_hi)], [v_hi])
            oab_ref[rows, (2 * kv) * PAIR:(2 * kv + 1) * PAIR] = o[0:SEQ].astype(BF16)
            oab_ref[rows, (2 * kv + 1) * PAIR:(2 * kv + 2) * PAIR] = o[SEQ:2 * SEQ].astype(BF16)
        for j in range(B_HEADS // 2):
            qp = (hb_ref[rows, j * PAIR:(j + 1) * PAIR] * ATT_SCALE).astype(BF16)
            kp_b = hb_ref[rows, B_W + j * PAIR:B_W + (j + 1) * PAIR]
            vp_b = hb_ref[rows, 2 * B_W + j * PAIR:2 * B_W + (j + 1) * PAIR]
            for u in range(2):
                nk_ref[b, 2 * j + u] = kp_b[:, u * HEAD_DIM:(u + 1) * HEAD_DIM]
                nv_ref[b, 2 * j + u] = vp_b[:, u * HEAD_DIM:(u + 1) * HEAD_DIM]
            k_lo, k_hi = _split_halves(kp_b, is_lo)
            v_lo, v_hi = _split_halves(vp_b, is_lo)
            o = _softmax_pv([_qk(qp, k_lo)], [v_lo]) + _softmax_pv([_qk(qp, k_hi)], [v_hi])
            oab_ref[rows, A_Q + j * PAIR:A_Q + (j + 1) * PAIR] = o.astype(BF16)


def _attn_ctx(h_a, h_b, qg2, kg2):
    rows = CTX_NB * SEQ
    kv_spec = lambda nh: pl.BlockSpec((CTX_NB, nh, SEQ, HEAD_DIM), lambda b: (b, 0, 0, 0))
    kv_shape = lambda nh: jax.ShapeDtypeStruct((BATCH, nh, SEQ, HEAD_DIM), F32)
    return pl.pallas_call(
        _attn_ctx_kernel,
        grid=(BATCH // CTX_NB,),
        in_specs=[
            pl.BlockSpec((rows, W_A), lambda b: (b, 0)),
            pl.BlockSpec((rows, W_B), lambda b: (b, 0)),
            pl.BlockSpec((1, PAIR), lambda b: (0, 0)),
            pl.BlockSpec((1, PAIR), lambda b: (0, 0)),
        ],
        out_specs=[
            pl.BlockSpec((rows, A_Q + B_W), lambda b: (b, 0)),
            kv_spec(A_KV_HEADS), kv_spec(A_KV_HEADS), kv_spec(B_HEADS), kv_spec(B_HEADS),
        ],
        out_shape=[
            jax.ShapeDtypeStruct((N_CTX, A_Q + B_W), BF16),
            kv_shape(A_KV_HEADS), kv_shape(A_KV_HEADS), kv_shape(B_HEADS), kv_shape(B_HEADS),
        ],
        compiler_params=_params(("arbitrary",), 40),
        name="attn_ctx",
    )(h_a, h_b, qg2, kg2)


LAT_TQ = 256
LAT_QROWS = LAT_TQ // GRID_W
LAT_NT = DEC_SEQ // LAT_TQ
NA_UNION_ROWS = 12
NA_UNION = NA_UNION_ROWS * GRID_W


def _na_union_row0(t):
    lo = min(max(r - NA_WIN_R // 2, 0) for r in range(t * LAT_QROWS, (t + 1) * LAT_QROWS))
    return min(min(lo, LAT_ROWS - NA_WIN_R), LAT_ROWS - NA_UNION_ROWS)


def _rope(x, cos, sin):
    lane = lax.broadcasted_iota(jnp.int32, x.shape, 1)
    nxt = pltpu.roll(x, x.shape[1] - 1, axis=1)
    prv = pltpu.roll(x, 1, axis=1)
    partner = jnp.where((lane & 1) == 0, nxt, prv)
    return x * cos + partner * sin


def _attn_lat_kernel(haq_ref, haf_ref, hbq_ref, hbf_ref, cak_ref, cav_ref, cnk_ref, cnv_ref,
                     qg_ref, kg_ref, cos_ref, sin_ref, bias_ref, oab_ref, kva_ref):
    t = pl.program_id(1)
    is_lo = _is_lo_half()
    row0 = pl.multiple_of(t * LAT_TQ, LAT_TQ)
    qg2 = qg_ref[...] * ATT_SCALE
    kg2 = kg_ref[...]

    @pl.when(t == 0)
    def _():
        kp = _rope(_pair_rms(haf_ref[:, A_Q:A_Q + PAIR].astype(F32), kg2, is_lo), cos_ref[...], sin_ref[...])
        vp = haf_ref[:, A_Q + A_KV:A_Q + A_KV + PAIR].astype(F32)
        kp_sw = pltpu.roll(kp, HEAD_DIM, axis=1)
        vp_sw = pltpu.roll(vp, HEAD_DIM, axis=1)
        for kv in range(A_KV_HEADS):
            kva_ref[4 * kv + 0] = jnp.where(is_lo, kp if kv == 0 else kp_sw, 0.0).astype(BF16)
            kva_ref[4 * kv + 1] = jnp.where(is_lo, 0.0, kp_sw if kv == 0 else kp).astype(BF16)
            kva_ref[4 * kv + 2] = jnp.where(is_lo, vp if kv == 0 else vp_sw, 0.0).astype(BF16)
            kva_ref[4 * kv + 3] = jnp.where(is_lo, 0.0, vp_sw if kv == 0 else vp).astype(BF16)

    cos_q = cos_ref[pl.ds(row0, LAT_TQ), :]
    sin_q = sin_ref[pl.ds(row0, LAT_TQ), :]
    qs = [_rope(_pair_rms(haq_ref[:, c * PAIR:(c + 1) * PAIR].astype(F32), qg2, is_lo), cos_q, sin_q).astype(BF16)
          for c in range(A_HEADS // 2)]
    for kv in range(A_KV_HEADS):
        kc = cak_ref[kv].astype(BF16)
        vc = cav_ref[kv].astype(BF16)
        q_st = jnp.concatenate([qs[2 * kv], qs[2 * kv + 1]], axis=0)
        o = (_softmax_pv([_qk(q_st, _pad_lo(kc)), _qk(q_st, kva_ref[4 * kv + 0])], [_pad_lo(vc), kva_ref[4 * kv + 2]])
             + _softmax_pv([_qk(q_st, _pad_hi(kc)), _qk(q_st, kva_ref[4 * kv + 1])], [_pad_hi(vc), kva_ref[4 * kv + 3]]))
        oab_ref[:, (2 * kv) * PAIR:(2 * kv + 1) * PAIR] = o[0:LAT_TQ].astype(BF16)
        oab_ref[:, (2 * kv + 1) * PAIR:(2 * kv + 2) * PAIR] = o[LAT_TQ:2 * LAT_TQ].astype(BF16)

    k0 = _na_union_row0(0) * GRID_W
    for tt in range(1, LAT_NT):
        k0 = jnp.where(t >= tt, _na_union_row0(tt) * GRID_W, k0)
    k0 = pl.multiple_of(k0, GRID_W)
    for j in range(B_HEADS // 2):
        qp = (hbq_ref[:, j * PAIR:(j + 1) * PAIR] * ATT_SCALE).astype(BF16)
        k_lo, k_hi = _split_halves(hbf_ref[pl.ds(k0, NA_UNION), B_W + j * PAIR:B_W + (j + 1) * PAIR], is_lo)
        v_lo, v_hi = _split_halves(hbf_ref[pl.ds(k0, NA_UNION), 2 * B_W + j * PAIR:2 * B_W + (j + 1) * PAIR], is_lo)
        kc_lo = _pad_lo(cnk_ref[2 * j].astype(BF16))
        kc_hi = _pad_hi(cnk_ref[2 * j + 1].astype(BF16))
        vc_lo = _pad_lo(cnv_ref[2 * j].astype(BF16))
        vc_hi = _pad_hi(cnv_ref[2 * j + 1].astype(BF16))
        o = (_softmax_pv([_qk(qp, kc_lo), _qk(qp, k_lo) + bias_ref[2 * j]], [vc_lo, v_lo])
             + _softmax_pv([_qk(qp, kc_hi), _qk(qp, k_hi) + bias_ref[2 * j + 1]], [vc_hi, v_hi]))
        oab_ref[:, A_Q + j * PAIR:A_Q + (j + 1) * PAIR] = o.astype(BF16)


def _attn_lat(h_a, h_b, cak, cav, cnk, cnv, layer, qg2, kg2, cos128, sin128, bias):
    nt = LAT_NT
    cache_spec = lambda nh: pl.BlockSpec((None, None, nh, PAST_LEN, HEAD_DIM), lambda b, t: (b, layer, 0, 0, 0))
    return pl.pallas_call(
        _attn_lat_kernel,
        grid=(DEC_BATCH, nt),
        in_specs=[
            pl.BlockSpec((LAT_TQ, W_A), lambda b, t: (b * nt + t, 0)),
            pl.BlockSpec((DEC_SEQ, W_A), lambda b, t: (b, 0)),
            pl.BlockSpec((LAT_TQ, W_B), lambda b, t: (b * nt + t, 0)),
            pl.BlockSpec((DEC_SEQ, W_B), lambda b, t: (b, 0)),
            cache_spec(A_KV_HEADS), cache_spec(A_KV_HEADS), cache_spec(B_HEADS), cache_spec(B_HEADS),
            pl.BlockSpec((1, PAIR), lambda b, t: (0, 0)),
            pl.BlockSpec((1, PAIR), lambda b, t: (0, 0)),
            pl.BlockSpec((DEC_SEQ, PAIR), lambda b, t: (0, 0)),
            pl.BlockSpec((DEC_SEQ, PAIR), lambda b, t: (0, 0)),
            pl.BlockSpec((B_HEADS, None, LAT_TQ, NA_UNION), lambda b, t: (0, t, 0, 0)),
        ],
        out_specs=pl.BlockSpec((LAT_TQ, A_Q + B_W), lambda b, t: (b * nt + t, 0)),
        out_shape=jax.ShapeDtypeStruct((N_LAT, A_Q + B_W), BF16),
        scratch_shapes=[pltpu.VMEM((4 * A_KV_HEADS, DEC_SEQ, PAIR), BF16)],
        compiler_params=_params(("arbitrary", "arbitrary"), 60),
        name="attn_lat",
    )(h_a, h_a, h_b, h_b, cak, cav, cnk, cnv, qg2, kg2, cos128, sin128, bias)


def _rope_tables():
    t = jnp.arange(DEC_SEQ)
    row = (t // GRID_W).astype(F32)
    col = (t % GRID_W).astype(F32)
    n_freq = HEAD_DIM // 4
    inv = ROPE_THETA ** (-jnp.arange(n_freq, dtype=F32) / n_freq)
    ang = jnp.concatenate([row[:, None] * inv, col[:, None] * inv], axis=-1)
    cos = jnp.repeat(jnp.cos(ang), 2, axis=-1)
    sign = jnp.tile(jnp.array([-1.0, 1.0], F32), HEAD_DIM // 2)
    sin = jnp.repeat(jnp.sin(ang), 2, axis=-1) * sign
    return jnp.tile(cos, (1, 2)), jnp.tile(sin, (1, 2))


def _na_bias(rpb):
    col = np.arange(GRID_W)
    dc = np.clip(col[None, :] - col[:, None] + NA_WIN_C - 1, 0, 2 * NA_WIN_C - 2)
    onehot = (dc[..., None] == np.arange(2 * NA_WIN_C - 1)).astype(np.float32)
    c0 = np.clip(col - NA_WIN_C // 2, 0, GRID_W - NA_WIN_C)
    in_win = (col[None, :] >= c0[:, None]) & (col[None, :] < c0[:, None] + NA_WIN_C)
    toep = jnp.einsum('hdm,ckm->hdck', rpb, onehot, precision=lax.Precision.HIGHEST)
    toep = jnp.where(in_win[None, None], toep, MASK_NEG)
    masked = jnp.full((B_HEADS, GRID_W, GRID_W), MASK_NEG, F32)
    tiles = []
    for t in range(LAT_NT):
        u0 = _na_union_row0(t)
        q_rows = []
        for r in range(t * LAT_QROWS, (t + 1) * LAT_QROWS):
            r0 = min(max(r - NA_WIN_R // 2, 0), LAT_ROWS - NA_WIN_R)
            q_rows.append(jnp.concatenate(
                [toep[:, kr - r + NA_WIN_R - 1] if r0 <= kr < r0 + NA_WIN_R else masked
                 for kr in range(u0, u0 + NA_UNION_ROWS)], axis=-1))
        tiles.append(jnp.concatenate(q_rows, axis=1))
    return jnp.stack(tiles, axis=1)


def _conv_kernel(hc_ref, w_ref, o_ref, *, seq):
    bg = hc_ref[:, 0:C_WIDTH].astype(F32)
    u = hc_ref[:, C_WIDTH:2 * C_WIDTH].astype(F32) * hc_ref[:, 2 * C_WIDTH:3 * C_WIDTH].astype(F32)
    rows = u.shape[0]
    assert seq & (seq - 1) == 0
    pos = lax.broadcasted_iota(jnp.int32, u.shape, 0) & (seq - 1)
    u_prev = jnp.where(pos == 0, 0.0, pltpu.roll(u, 1, axis=0))
    u_next = jnp.where(pos == seq - 1, 0.0, pltpu.roll(u, rows - 1, axis=0))
    y = u_prev * w_ref[0:1, :] + u * w_ref[1:2, :] + u_next * w_ref[2:3, :]
    o_ref[...] = (bg * y).astype(o_ref.dtype)


def _conv(h_c, conv_w, seq):
    n = h_c.shape[0]
    tm = 1024
    return pl.pallas_call(
        functools.partial(_conv_kernel, seq=seq),
        grid=(n // tm,),
        in_specs=[pl.BlockSpec((tm, W_C), lambda i: (i, 0)), pl.BlockSpec((3, C_WIDTH), lambda i: (0, 0))],
        out_specs=pl.BlockSpec((tm, C_WIDTH), lambda i: (i, 0)),
        out_shape=jax.ShapeDtypeStruct((n, C_WIDTH), BF16),
        compiler_params=_params(("arbitrary",), 40),
        name="short_conv",
    )(h_c, conv_w)


def _mix_kernel(x_ref, oab_ref, oc_ref, hg_ref, wa_ref, wb_ref, wc_ref, wo_ref, g1_ref, n2g_ref, sc2_ref, sh2_ref,
                wrt_ref, xo_ref, xn2_ref, pt_ref):
    ga = jax.nn.sigmoid(hg_ref[:, 0:D_MODEL].astype(F32))
    gb = jax.nn.sigmoid(hg_ref[:, D_MODEL:2 * D_MODEL].astype(F32))
    gc = jax.nn.sigmoid(hg_ref[:, 2 * D_MODEL:3 * D_MODEL].astype(F32))
    pa = jnp.dot(oab_ref[:, 0:A_Q], wa_ref[...], preferred_element_type=F32)
    pb = jnp.dot(oab_ref[:, A_Q:A_Q + B_W], wb_ref[...], preferred_element_type=F32)
    pc = jnp.dot(oc_ref[...], wc_ref[...], preferred_element_type=F32)
    merged = ga * pa + gb * pb + gc * pc
    xo = x_ref[...] + g1_ref[...] * jnp.dot(merged.astype(BF16), wo_ref[...], preferred_element_type=F32)
    xo_ref[...] = xo
    y = xo * lax.rsqrt(jnp.mean(xo * xo, axis=-1, keepdims=True) + EPS) * n2g_ref[...]
    xn2 = y * (1.0 + sc2_ref[...]) + sh2_ref[...]
    xn2_ref[...] = xn2
    logits_t = lax.dot_general(wrt_ref[...], xn2.astype(BF16), _NT, preferred_element_type=F32)
    z = jnp.exp(logits_t - logits_t.max(axis=0, keepdims=True))
    pt_ref[...] = z / z.sum(axis=0, keepdims=True)


def _mix(x, oab, oc, h_g, wa, wb, wc, wo, g1, n2g, sc2, sh2, wrt, seq):
    n = x.shape[0]
    tm = 512
    per_batch = g1.shape[0] > 1
    mod_map = (lambda i: ((i * tm) // seq, 0, 0)) if per_batch else (lambda i: (0, 0, 0))
    mod_spec = pl.BlockSpec((None, 1, D_MODEL), mod_map)
    full = lambda a: pl.BlockSpec(a.shape, lambda i: (0,) * a.ndim)
    row = lambda w: pl.BlockSpec((tm, w), lambda i: (i, 0))
    return pl.pallas_call(
        _mix_kernel,
        grid=(n // tm,),
        in_specs=[row(D_MODEL), row(A_Q + B_W), row(C_WIDTH), row(GATE_W), full(wa), full(wb), full(wc), full(wo),
                  mod_spec, full(n2g), mod_spec, mod_spec, full(wrt)],
        out_specs=[row(D_MODEL), row(D_MODEL), pl.BlockSpec((N_EXPERTS, tm), lambda i: (0, i))],
        out_shape=[jax.ShapeDtypeStruct((n, D_MODEL), F32), jax.ShapeDtypeStruct((n, D_MODEL), F32),
                   jax.ShapeDtypeStruct((N_EXPERTS, n), F32)],
        compiler_params=_params(("arbitrary",), 56),
        name="mix",
    )(x, oab, oc, h_g, wa, wb, wc, wo, g1, n2g, sc2, sh2, wrt)


MOE_TF = 512
MOE_ROW_CHUNKS = ((0, 512), (512, 512), (1024, 256))
MOE_ISSUE_UNROLL = 8


def _moe_issue_rows(idx_ref, idx0, src_hbm, dst_ref, dst_row0, count, sem):
    def body(i, carry):
        for u in range(MOE_ISSUE_UNROLL):
            r = i * MOE_ISSUE_UNROLL + u
            tok = idx_ref[idx0 + r]
            pltpu.make_async_copy(src_hbm.at[pl.ds(tok, 1)], dst_ref.at[pl.ds(dst_row0 + r, 1)], sem).start()
        return carry

    lax.fori_loop(0, count // MOE_ISSUE_UNROLL, body, 0)


def _moe_kernel(idx_ref, xc_hbm, xl_hbm, aff_ref, wg_ref, wu_ref, wd_ref, ye_ref, xe_ref, sem):
    e = pl.program_id(0)
    f = pl.program_id(1)
    slot = e % 2

    def gather(expert, s):
        base = expert * CAP_ALL
        _moe_issue_rows(idx_ref, base, xc_hbm, xe_ref.at[s], 0, CAP_CTX, sem.at[s])
        _moe_issue_rows(idx_ref, base + CAP_CTX, xl_hbm, xe_ref.at[s], CAP_CTX, CAP_LAT, sem.at[s])

    @pl.when(f == 0)
    def _():
        @pl.when(e == 0)
        def _():
            gather(0, 0)

        pltpu.make_async_copy(xc_hbm.at[pl.ds(0, CAP_ALL)], xe_ref.at[slot], sem.at[slot]).wait()

        @pl.when(e + 1 < N_EXPERTS)
        def _():
            gather(e + 1, 1 - slot)

    wg = wg_ref[...].astype(BF16)
    wu = wu_ref[...].astype(BF16)
    wd = wd_ref[...].astype(BF16)
    for r0, rn in MOE_ROW_CHUNKS:
        xe = xe_ref[slot, r0:r0 + rn, :].astype(BF16)
        hg = jnp.dot(xe, wg, preferred_element_type=F32)
        hu = jnp.dot(xe, wu, preferred_element_type=F32)
        act = (hg * jax.nn.sigmoid(hg) * hu).astype(BF16)
        part = jnp.dot(act, wd, preferred_element_type=F32) * aff_ref[r0:r0 + rn, :]

        @pl.when(f == 0)
        def _():
            ye_ref[r0:r0 + rn, :] = part

        @pl.when(f != 0)
        def _():
            ye_ref[r0:r0 + rn, :] += part


def _moe(idx_all, xn2_ctx, xn2_lat, aff_all, w_gate_e, w_up_e, w_down_e, layer):
    nf = EXPERT_FF // MOE_TF
    grid_spec = pltpu.PrefetchScalarGridSpec(
        num_scalar_prefetch=1,
        grid=(N_EXPERTS, nf),
        in_specs=[
            pl.BlockSpec(memory_space=pl.ANY),
            pl.BlockSpec(memory_space=pl.ANY),
            pl.BlockSpec((None, CAP_ALL, 1), lambda e, f, idx: (e, 0, 0)),
            pl.BlockSpec((None, None, D_MODEL, MOE_TF), lambda e, f, idx: (layer, e, 0, f)),
            pl.BlockSpec((None, None, D_MODEL, MOE_TF), lambda e, f, idx: (layer, e, 0, f)),
            pl.BlockSpec((None, None, MOE_TF, D_MODEL), lambda e, f, idx: (layer, e, f, 0)),
        ],
        out_specs=pl.BlockSpec((None, CAP_ALL, D_MODEL), lambda e, f, idx: (e, 0, 0)),
        scratch_shapes=[pltpu.VMEM((2, CAP_ALL, D_MODEL), F32), pltpu.SemaphoreType.DMA((2,))],
    )
    return pl.pallas_call(
        _moe_kernel,
        grid_spec=grid_spec,
        out_shape=jax.ShapeDtypeStruct((N_EXPERTS, CAP_ALL, D_MODEL), F32),
        compiler_params=_params(("arbitrary", "arbitrary"), 56),
        name="moe_experts",
    )(idx_all, xn2_ctx, xn2_lat, aff_all, w_gate_e, w_up_e, w_down_e)


LANES = 128
POS_SPLIT = 32


def _prefix_count(mask_bf, tri_lane, tri_blk_strict):
    within = jnp.dot(mask_bf, tri_lane, preferred_element_type=F32)
    bs = jnp.broadcast_to(within[:, LANES - 1:LANES], within.shape).astype(BF16)
    return within + jnp.dot(tri_blk_strict, bs, preferred_element_type=F32)


def _route_kernel(p_ref, idx_ref, aff_ref, *, cap):
    p = p_ref[...]
    nb = p.shape[0]
    capf = float(cap)

    def as_float(word):
        return lax.bitcast_convert_type(word, F32)

    def bisect(_, c):
        lo, hi = c
        mid = lo + ((hi - lo) >> 1)
        ok = jnp.sum(jnp.where(p >= as_float(mid), 1.0, 0.0), keepdims=True) >= capf
        return jnp.where(ok, mid, lo), jnp.where(ok, hi, mid)

    lo0 = jnp.zeros((1, 1), jnp.int32)
    hi0 = jnp.full((1, 1), 0x7F800000, jnp.int32)
    thr, _ = lax.fori_loop(0, 31, bisect, (lo0, hi0))

    li = lax.broadcasted_iota(jnp.int32, (LANES, LANES), 0)
    lj = lax.broadcasted_iota(jnp.int32, (LANES, LANES), 1)
    tri_lane = jnp.where(li <= lj, 1.0, 0.0).astype(BF16)
    bi = lax.broadcasted_iota(jnp.int32, (nb, nb), 0)
    bj = lax.broadcasted_iota(jnp.int32, (nb, nb), 1)
    tri_blk_strict = jnp.where(bj < bi, 1.0, 0.0).astype(BF16)
    tri_blk_incl = jnp.where(bi <= bj, 1.0, 0.0).astype(BF16)

    gt = p > as_float(thr)
    eq = p == as_float(thr)
    need = capf - jnp.sum(jnp.where(gt, 1.0, 0.0), keepdims=True)
    tie_rank = _prefix_count(jnp.where(eq, 1.0, 0.0).astype(BF16), tri_lane, tri_blk_strict)
    sel = jnp.where(gt | (eq & (tie_rank <= need)), 1.0, 0.0).astype(BF16)
    pos = _prefix_count(sel, tri_lane, tri_blk_strict)

    s_col = lax.broadcasted_iota(jnp.int32, (cap, 1), 0).astype(F32)
    bs_row = lax.dot_general(jnp.ones((8, LANES), BF16), sel, _NT, preferred_element_type=F32)
    bp_row = jnp.dot(bs_row.astype(BF16), tri_blk_incl, preferred_element_type=F32)[0:1, :]
    blk = jnp.sum(jnp.where(bp_row <= s_col, 1.0, 0.0), axis=-1, keepdims=True)
    onehot = jnp.where(blk == lax.broadcasted_iota(jnp.int32, (1, nb), 1).astype(F32), 1.0, 0.0).astype(BF16)
    pos_hi = jnp.floor(pos * (1.0 / POS_SPLIT))
    pos_lo = pos - pos_hi * POS_SPLIT
    pos_row = (jnp.dot(onehot, pos_hi.astype(BF16), preferred_element_type=F32) * POS_SPLIT
               + jnp.dot(onehot, pos_lo.astype(BF16), preferred_element_type=F32))
    lane = jnp.sum(jnp.where(pos_row <= s_col, 1.0, 0.0), axis=-1, keepdims=True)
    idx_ref[...] = (blk * LANES + lane).astype(jnp.int32)

    p1 = p.astype(BF16)
    r1 = p - p1.astype(F32)
    p2 = r1.astype(BF16)
    p3 = (r1 - p2.astype(F32)).astype(BF16)
    p_row = (jnp.dot(onehot, p1, preferred_element_type=F32) + jnp.dot(onehot, p2, preferred_element_type=F32)
             + jnp.dot(onehot, p3, preferred_element_type=F32))
    lane_id = lax.broadcasted_iota(jnp.int32, (1, LANES), 1).astype(F32)
    aff_ref[...] = jnp.sum(jnp.where(lane_id == lane, p_row, 0.0), axis=-1, keepdims=True)


def _route(probs_t, cap):
    nb = probs_t.shape[1] // LANES
    slot = pl.BlockSpec((None, cap, 1), lambda e: (e, 0, 0))
    return pl.pallas_call(
        functools.partial(_route_kernel, cap=cap),
        grid=(N_EXPERTS,),
        in_specs=[pl.BlockSpec((None, nb, LANES), lambda e: (e, 0, 0))],
        out_specs=[slot, slot],
        out_shape=[jax.ShapeDtypeStruct((N_EXPERTS, cap, 1), jnp.int32),
                   jax.ShapeDtypeStruct((N_EXPERTS, cap, 1), F32)],
        compiler_params=_params(("arbitrary",), 32),
        name="route",
    )(probs_t.reshape(N_EXPERTS, nb, LANES))


CMB_TT = 1024
CMB_CH = 256
CMB_GROUP = 8
CMB_EPI_ROWS = 256


def _combine_kernel(idx_ref, st_ref, ye_hbm, x_ref, g2_ref, ng_ref, sc_ref, sh_ref, *rest, off, cap, nt, final):
    if final:
        y_ref, acc_ref, buf_ref, sem = rest
    else:
        xo_ref, xn_ref, acc_ref, buf_ref, sem = rest
    k = pl.program_id(0)
    t0 = k * CMB_TT
    acc_ref[...] = jnp.zeros_like(acc_ref)

    def slot_range(e):
        s0 = st_ref[e * (nt + 1) + k]
        s1 = st_ref[e * (nt + 1) + k + 1]
        return s0, s1, jnp.minimum(s0 & -CMB_GROUP, cap - CMB_CH)

    def chunk_copy(e, start):
        src = ye_hbm.at[e, pl.ds(pl.multiple_of(off + start, CMB_GROUP), CMB_CH)]
        return pltpu.make_async_copy(src, buf_ref.at[e], sem.at[e])

    def accumulate(e, start, lo, hi):
        def body(i, carry):
            r0 = pl.multiple_of(i * CMB_GROUP, CMB_GROUP)
            tile = buf_ref[e, pl.ds(r0, CMB_GROUP), :]
            dst = []
            for u in range(CMB_GROUP):
                slot = start + r0 + u
                tok = idx_ref[e * CAP_ALL + off + slot]
                dst.append(jnp.where((slot >= lo) & (slot < hi), tok - t0, CMB_TT))
            rows = [acc_ref[pl.ds(d, 1), :] for d in dst]
            for u in range(CMB_GROUP):
                acc_ref[pl.ds(dst[u], 1), :] = rows[u] + tile[u:u + 1, :]
            return carry

        lax.fori_loop((lo - start) // CMB_GROUP, (hi - start + CMB_GROUP - 1) // CMB_GROUP, body, 0)

    for e in range(N_EXPERTS):
        chunk_copy(e, slot_range(e)[2]).start()
    for e in range(N_EXPERTS):
        s0, s1, a = slot_range(e)
        chunk_copy(e, a).wait()
        accumulate(e, a, s0, jnp.minimum(s1, a + CMB_CH))

        def extra(j, carry, e=e, s1=s1, a=a):
            lo = a + CMB_CH * (j + 1)
            start = jnp.minimum(lo, cap - CMB_CH)
            cp = chunk_copy(e, start)
            cp.start()
            cp.wait()
            accumulate(e, start, lo, jnp.minimum(s1, lo + CMB_CH))
            return carry

        lax.fori_loop(0, jnp.maximum(s1 - a - 1, 0) // CMB_CH, extra, 0)

    for r0 in range(0, CMB_TT, CMB_EPI_ROWS):
        rs = slice(r0, r0 + CMB_EPI_ROWS)
        xo = x_ref[rs, :] + g2_ref[...] * acc_ref[rs, :]
        y = xo * lax.rsqrt(jnp.mean(xo * xo, axis=-1, keepdims=True) + EPS) * ng_ref[...]
        if final:
            y_ref[rs, :] = y
        else:
            xo_ref[rs, :] = xo
            xn_ref[rs, :] = (y * (1.0 + sc_ref[...]) + sh_ref[...]).astype(xn_ref.dtype)


def _combine(idx_all, starts, ye, x, g2, ng, sc, sh, seq, off, cap, final):
    n = x.shape[0]
    nt = n // CMB_TT
    per_batch = g2.shape[0] > 1
    mod_map = ((lambda i, *_: ((i * CMB_TT) // seq, 0, 0)) if per_batch else (lambda i, *_: (0, 0, 0)))
    mod_spec = pl.BlockSpec((None, 1, D_MODEL), mod_map)
    row = pl.BlockSpec((CMB_TT, D_MODEL), lambda i, *_: (i, 0))
    vec = pl.BlockSpec((1, D_MODEL), lambda i, *_: (0, 0))
    if final:
        out_specs, out_shape = row, jax.ShapeDtypeStruct((n, D_MODEL), F32)
    else:
        out_specs = [row, row]
        out_shape = [jax.ShapeDtypeStruct((n, D_MODEL), F32), jax.ShapeDtypeStruct((n, D_MODEL), BF16)]
    grid_spec = pltpu.PrefetchScalarGridSpec(
        num_scalar_prefetch=2,
        grid=(nt,),
        in_specs=[pl.BlockSpec(memory_space=pl.ANY), row, mod_spec, vec, mod_spec, mod_spec],
        out_specs=out_specs,
        scratch_shapes=[pltpu.VMEM((CMB_TT + CMB_GROUP, D_MODEL), F32),
                        pltpu.VMEM((N_EXPERTS, CMB_CH, D_MODEL), F32),
                        pltpu.SemaphoreType.DMA((N_EXPERTS,))],
    )
    return pl.pallas_call(
        functools.partial(_combine_kernel, off=off, cap=cap, nt=nt, final=final),
        grid_spec=grid_spec,
        out_shape=out_shape,
        compiler_params=_params(("arbitrary",), 60),
        name="combine",
    )(idx_all, starts.reshape(-1), ye, x, g2, ng, sc, sh)


def _tile_starts(idx_sorted, n):
    bounds = jnp.arange(n // CMB_TT + 1, dtype=jnp.int32) * CMB_TT
    return jnp.sum(idx_sorted[:, :, None] < bounds[None, None, :], axis=1, dtype=jnp.int32)


def kernel(x_prompt, x_sample, cache_attn_k, cache_attn_v, cache_na_k, cache_na_v, c, c_ctx, w_ada, b_ada, norm1_g,
           w_in, q_norm_g, k_norm_g, na_rpb, conv_w, w_proj_a, w_proj_b, w_proj_c, w_out, norm2_g, w_router,
           w_gate_e, w_up_e, w_down_e, final_g):
    cond8 = jnp.zeros((8, D_MODEL), F32).at[0].set(c_ctx).at[1:1 + DEC_BATCH].set(c)
    mods = _adaln(cond8, w_ada, b_ada).reshape(DEPTH, 8, N_MOD, D_MODEL)
    cos128, sin128 = _rope_tables()

    hp = x_prompt.reshape(N_CTX, D_MODEL)
    hs = x_sample.reshape(N_LAT, D_MODEL)
    new_ak, new_av, new_nk, new_nv = [], [], [], []
    for l in range(DEPTH):
        mod_ctx = [mods[l, 0:1, i][:, None, :] for i in range(N_MOD)]
        mod_lat = [mods[l, 1:1 + DEC_BATCH, i][:, None, :] for i in range(N_MOD)]
        n1g = norm1_g[l][None]
        n2g = norm2_g[l][None]
        qg2 = jnp.tile(q_norm_g[l][None], (1, 2))
        kg2 = jnp.tile(k_norm_g[l][None], (1, 2))
        wa, wb, wc, wo = (w_proj_a[l].astype(BF16), w_proj_b[l].astype(BF16), w_proj_c[l].astype(BF16),
                          w_out[l].astype(BF16))
        wrt = w_router[l].T.astype(BF16)
        bias = _na_bias(na_rpb[l])

        def sublayer1(x, xn, mod, seq, is_ctx):
            _, _, g1, sh2, sc2, _ = mod
            kv_dtype = F32 if is_ctx else BF16
            h_a = _inproj(xn, w_in, l, COL_A, W_A, 768, kv_dtype)
            h_b = _inproj(xn, w_in, l, COL_B, W_B, 768, kv_dtype)
            h_c = _inproj(xn, w_in, l, COL_C, W_C, 768, BF16)
            h_g = _inproj(xn, w_in, l, COL_G, GATE_W, 768, BF16)
            kv = None
            if is_ctx:
                oab, ka, va, kb, vb = _attn_ctx(h_a, h_b, qg2, kg2)
                kv = (ka, va, kb, vb)
            else:
                oab = _attn_lat(h_a, h_b, cache_attn_k, cache_attn_v, cache_na_k, cache_na_v, l, qg2, kg2,
                                cos128, sin128, bias)
            oc = _conv(h_c, conv_w[l], seq)
            xo, xn2, probs_t = _mix(x, oab, oc, h_g, wa, wb, wc, wo, g1, n2g, sc2, sh2, wrt, seq)
            return xo, xn2, probs_t, kv

        if l == 0:
            xn_p = _norm_mod(hp, n1g, mod_ctx[1], mod_ctx[0], SEQ, BF16)
            xn_s = _norm_mod(hs, n1g, mod_lat[1], mod_lat[0], DEC_SEQ, BF16)
        hp, xn2_c, pt_c, (ka, va, kb, vb) = sublayer1(hp, xn_p, mod_ctx, SEQ, True)
        hs, xn2_l, pt_l, _ = sublayer1(hs, xn_s, mod_lat, DEC_SEQ, False)
        new_ak.append(ka)
        new_av.append(va)
        new_nk.append(kb)
        new_nv.append(vb)

        idx_c, aff_c = _route(pt_c, CAP_CTX)
        idx_l, aff_l = _route(pt_l, CAP_LAT)
        idx_all = jnp.concatenate([idx_c, idx_l], axis=1).reshape(-1)
        aff_all = jnp.concatenate([aff_c, aff_l], axis=1)
        ye = _moe(idx_all, xn2_c, xn2_l, aff_all, w_gate_e, w_up_e, w_down_e, l)

        final = l == DEPTH - 1
        if final:
            ng, nsc_c, nsh_c, nsc_l, nsh_l = final_g[None], mod_ctx[1], mod_ctx[0], mod_lat[1], mod_lat[0]
        else:
            nxt_ctx = [mods[l + 1, 0:1, i][:, None, :] for i in range(2)]
            nxt_lat = [mods[l + 1, 1:1 + DEC_BATCH, i][:, None, :] for i in range(2)]
            ng, nsc_c, nsh_c, nsc_l, nsh_l = norm1_g[l + 1][None], nxt_ctx[1], nxt_ctx[0], nxt_lat[1], nxt_lat[0]
        out_c = _combine(idx_all, _tile_starts(idx_c[..., 0], N_CTX), ye, hp, mod_ctx[5], ng, nsc_c, nsh_c,
                         SEQ, 0, CAP_CTX, final)
        out_l = _combine(idx_all, _tile_starts(idx_l[..., 0], N_LAT), ye, hs, mod_lat[5], ng, nsc_l, nsh_l,
                         DEC_SEQ, CAP_CTX, CAP_LAT, final)
        if final:
            y_prompt = out_c.reshape(BATCH, SEQ, D_MODEL)
            y_sample = out_l.reshape(DEC_BATCH, DEC_SEQ, D_MODEL)
        else:
            (hp, xn_p), (hs, xn_s) = out_c, out_l

    return (y_prompt, y_sample, jnp.stack(new_ak, axis=1), jnp.stack(new_av, axis=1),
            jnp.stack(new_nk, axis=1), jnp.stack(new_nv, axis=1))
```

```python
import functools

import jax
import jax.numpy as jnp
import numpy as np
from jax import lax
from jax.experimental import pallas as pl
from jax.experimental.pallas import tpu as pltpu

D_MODEL = 1024
BATCH = 32
SEQ = 256
DEPTH = 2
DEC_BATCH = 2
DEC_SEQ = 1024
PAST_LEN = 256
GRID_W = 64
HEAD_DIM = 64
A_HEADS = 8
A_KV_HEADS = 2
A_GROUP = A_HEADS // A_KV_HEADS
B_HEADS = 8
C_WIDTH = 512
NA_WIN_R = 8
NA_WIN_C = 16
N_EXPERTS = 16
EXPERT_FF = 1024
CAP_FACTOR = 2
ROPE_THETA = 10000.0
EPS = 1e-6
N_MOD = 6

A_Q = A_HEADS * HEAD_DIM
A_KV = A_KV_HEADS * HEAD_DIM
B_W = B_HEADS * HEAD_DIM
GATE_W = 3 * D_MODEL
COL_A = 0
W_A = A_Q + 2 * A_KV
COL_B = COL_A + W_A
W_B = 3 * B_W
COL_C = COL_B + W_B
W_C = 3 * C_WIDTH
COL_G = COL_C + W_C

N_CTX = BATCH * SEQ
N_LAT = DEC_BATCH * DEC_SEQ
CAP_CTX = CAP_FACTOR * N_CTX // N_EXPERTS
CAP_LAT = CAP_FACTOR * N_LAT // N_EXPERTS
CAP_ALL = CAP_CTX + CAP_LAT
LAT_ROWS = DEC_SEQ // GRID_W
MASK_NEG = -1e30
ATT_SCALE = HEAD_DIM ** -0.5

F32 = jnp.float32
BF16 = jnp.bfloat16
_NT = (((1,), (1,)), ((), ()))

_MIB = 1024 * 1024


def _params(sem, vmem_mib):
    return pltpu.CompilerParams(dimension_semantics=sem, vmem_limit_bytes=vmem_mib * _MIB)


def _adaln_kernel(cond_ref, w_ref, b_ref, o_ref):
    c = cond_ref[...]
    s = c * jax.nn.sigmoid(c)
    o_ref[...] = jnp.dot(s.astype(BF16), w_ref[...].astype(BF16), preferred_element_type=F32) + b_ref[...]


def _adaln(cond8, w_ada, b_ada):
    tn = 1536
    ncol = N_MOD * D_MODEL
    return pl.pallas_call(
        _adaln_kernel,
        grid=(DEPTH, ncol // tn),
        in_specs=[
            pl.BlockSpec((8, D_MODEL), lambda l, j: (0, 0)),
            pl.BlockSpec((None, D_MODEL, tn), lambda l, j: (l, 0, j)),
            pl.BlockSpec((None, 1, tn), lambda l, j: (l, 0, j)),
        ],
        out_specs=pl.BlockSpec((None, 8, tn), lambda l, j: (l, 0, j)),
        out_shape=jax.ShapeDtypeStruct((DEPTH, 8, ncol), F32),
        compiler_params=_params(("arbitrary", "arbitrary"), 40),
        name="adaln",
    )(cond8, w_ada, b_ada.reshape(DEPTH, 1, ncol))


def _norm_mod_kernel(x_ref, g_ref, sc_ref, sh_ref, o_ref):
    x = x_ref[...]
    y = x * lax.rsqrt(jnp.mean(x * x, axis=-1, keepdims=True) + EPS) * g_ref[...]
    o_ref[...] = (y * (1.0 + sc_ref[...]) + sh_ref[...]).astype(o_ref.dtype)


def _norm_mod(x, g, sc, sh, seq, out_dtype):
    n = x.shape[0]
    tm = 512
    per_batch = sc.shape[0] > 1
    mod_map = (lambda i: ((i * tm) // seq, 0, 0)) if per_batch else (lambda i: (0, 0, 0))
    return pl.pallas_call(
        _norm_mod_kernel,
        grid=(n // tm,),
        in_specs=[
            pl.BlockSpec((tm, D_MODEL), lambda i: (i, 0)),
            pl.BlockSpec((1, D_MODEL), lambda i: (0, 0)),
            pl.BlockSpec((None, 1, D_MODEL), mod_map),
            pl.BlockSpec((None, 1, D_MODEL), mod_map),
        ],
        out_specs=pl.BlockSpec((tm, D_MODEL), lambda i: (i, 0)),
        out_shape=jax.ShapeDtypeStruct((n, D_MODEL), out_dtype),
        compiler_params=_params(("arbitrary",), 32),
        name="norm_mod",
    )(x, g, sc, sh)


def _inproj_kernel(a_ref, w_ref, o_ref, wbf_ref):
    @pl.when(pl.program_id(1) == 0)
    def _():
        wbf_ref[...] = w_ref[...].astype(BF16)

    o_ref[...] = jnp.dot(a_ref[...], wbf_ref[...], preferred_element_type=F32).astype(o_ref.dtype)


def _inproj(xn, w_in, layer, col0, width, tn, out_dtype):
    n = xn.shape[0]
    tm = 2048
    joff = col0 // tn
    assert col0 % tn == 0 and width % tn == 0
    return pl.pallas_call(
        _inproj_kernel,
        grid=(width // tn, n // tm),
        in_specs=[
            pl.BlockSpec((tm, D_MODEL), lambda j, i: (i, 0)),
            pl.BlockSpec((None, D_MODEL, tn), lambda j, i: (layer, 0, j + joff)),
        ],
        out_specs=pl.BlockSpec((tm, tn), lambda j, i: (i, j)),
        out_shape=jax.ShapeDtypeStruct((n, width), out_dtype),
        scratch_shapes=[pltpu.VMEM((D_MODEL, tn), BF16)],
        compiler_params=_params(("arbitrary", "arbitrary"), 48),
        name="inproj",
    )(xn, w_in)


PAIR = 2 * HEAD_DIM


def _is_lo_half():
    return lax.broadcasted_iota(jnp.int32, (1, PAIR), 1) < HEAD_DIM


def _pair_rms(x, g2, is_lo):
    x2 = x * x
    s_lo = jnp.sum(jnp.where(is_lo, x2, 0.0), axis=-1, keepdims=True)
    s_hi = jnp.sum(jnp.where(is_lo, 0.0, x2), axis=-1, keepdims=True)
    ms = jnp.where(is_lo, s_lo, s_hi) * (1.0 / HEAD_DIM)
    return x * lax.rsqrt(ms + EPS) * g2


def _split_halves(x, is_lo):
    return jnp.where(is_lo, x, 0.0).astype(BF16), jnp.where(is_lo, 0.0, x).astype(BF16)


def _pad_lo(x):
    return jnp.concatenate([x, jnp.zeros_like(x)], axis=1)


def _pad_hi(x):
    return jnp.concatenate([jnp.zeros_like(x), x], axis=1)


def _qk(q, k):
    return lax.dot_general(q, k, _NT, preferred_element_type=F32)


def _softmax_pv(scores, values):
    m = scores[0].max(axis=-1, keepdims=True)
    for s in scores[1:]:
        m = jnp.maximum(m, s.max(axis=-1, keepdims=True))
    acc = None
    den = None
    for s, v in zip(scores, values):
        p = jnp.exp(s - m)
        d = p.sum(axis=-1, keepdims=True)
        o = jnp.dot(p.astype(BF16), v, preferred_element_type=F32)
        acc = o if acc is None else acc + o
        den = d if den is None else den + d
    return acc * (1.0 / den)


CTX_NB = 2


def _attn_ctx_kernel(ha_ref, hb_ref, qg_ref, kg_ref, oab_ref, ak_ref, av_ref, nk_ref, nv_ref):
    is_lo = _is_lo_half()
    qg2 = qg_ref[...] * ATT_SCALE
    kg2 = kg_ref[...]
    for b in range(CTX_NB):
        rows = slice(b * SEQ, (b + 1) * SEQ)
        kp = _pair_rms(ha_ref[rows, A_Q:A_Q + PAIR], kg2, is_lo)
        vp = ha_ref[rows, A_Q + A_KV:A_Q + A_KV + PAIR]
        for kv in range(A_KV_HEADS):
            ak_ref[b, kv] = kp[:, kv * HEAD_DIM:(kv + 1) * HEAD_DIM]
            av_ref[b, kv] = vp[:, kv * HEAD_DIM:(kv + 1) * HEAD_DIM]
        kp_sw = pltpu.roll(kp, HEAD_DIM, axis=1)
        vp_sw = pltpu.roll(vp, HEAD_DIM, axis=1)
        qs = [_pair_rms(ha_ref[rows, c * PAIR:(c + 1) * PAIR], qg2, is_lo).astype(BF16)
              for c in range(A_HEADS // 2)]
        for kv in range(A_KV_HEADS):
            k_lo = jnp.where(is_lo, kp if kv == 0 else kp_sw, 0.0).astype(BF16)
            k_hi = jnp.where(is_lo, 0.0, kp_sw if kv == 0 else kp).astype(BF16)
            v_lo = jnp.where(is_lo, vp if kv == 0 else vp_sw, 0.0).astype(BF16)
            v_hi = jnp.where(is_lo, 0.0, vp_sw if kv == 0 else vp).astype(BF16)
            q_st = jnp.concatenate([qs[2 * kv], qs[2 * kv + 1]], axis=0)
            o = _softmax_pv([_qk(q_st, k_lo)], [v_lo]) + _softmax_pv([_qk(q_st, k_hi)], [v_hi])
            oab_ref[rows, (2 * kv) * PAIR:(2 * kv + 1) * PAIR] = o[0:SEQ].astype(BF16)
            oab_ref[rows, (2 * kv + 1) * PAIR:(2 * kv + 2) * PAIR] = o[SEQ:2 * SEQ].astype(BF16)
        for j in range(B_HEADS // 2):
            qp = (hb_ref[rows, j * PAIR:(j + 1) * PAIR] * ATT_SCALE).astype(BF16)
            kp_b = hb_ref[rows, B_W + j * PAIR:B_W + (j + 1) * PAIR]
            vp_b = hb_ref[rows, 2 * B_W + j * PAIR:2 * B_W + (j + 1) * PAIR]
            for u in range(2):
                nk_ref[b, 2 * j + u] = kp_b[:, u * HEAD_DIM:(u + 1) * HEAD_DIM]
                nv_ref[b, 2 * j + u] = vp_b[:, u * HEAD_DIM:(u + 1) * HEAD_DIM]
            k_lo, k_hi = _split_halves(kp_b, is_lo)
            v_lo, v_hi = _split_halves(vp_b, is_lo)
            o = _softmax_pv([_qk(qp, k_lo)], [v_lo]) + _softmax_pv([_qk(qp, k_hi)], [v_hi])
            oab_ref[rows, A_Q + j * PAIR:A_Q + (j + 1) * PAIR] = o.astype(BF16)


def _attn_ctx(h_a, h_b, qg2, kg2):
    rows = CTX_NB * SEQ
    kv_spec = lambda nh: pl.BlockSpec((CTX_NB, nh, SEQ, HEAD_DIM), lambda b: (b, 0, 0, 0))
    kv_shape = lambda nh: jax.ShapeDtypeStruct((BATCH, nh, SEQ, HEAD_DIM), F32)
    return pl.pallas_call(
        _attn_ctx_kernel,
        grid=(BATCH // CTX_NB,),
        in_specs=[
            pl.BlockSpec((rows, W_A), lambda b: (b, 0)),
            pl.BlockSpec((rows, W_B), lambda b: (b, 0)),
            pl.BlockSpec((1, PAIR), lambda b: (0, 0)),
            pl.BlockSpec((1, PAIR), lambda b: (0, 0)),
        ],
        out_specs=[
            pl.BlockSpec((rows, A_Q + B_W), lambda b: (b, 0)),
            kv_spec(A_KV_HEADS), kv_spec(A_KV_HEADS), kv_spec(B_HEADS), kv_spec(B_HEADS),
        ],
        out_shape=[
            jax.ShapeDtypeStruct((N_CTX, A_Q + B_W), BF16),
            kv_shape(A_KV_HEADS), kv_shape(A_KV_HEADS), kv_shape(B_HEADS), kv_shape(B_HEADS),
        ],
        compiler_params=_params(("arbitrary",), 40),
        name="attn_ctx",
    )(h_a, h_b, qg2, kg2)


LAT_TQ = 256
LAT_QROWS = LAT_TQ // GRID_W
LAT_NT = DEC_SEQ // LAT_TQ
NA_UNION_ROWS = 12
NA_UNION = NA_UNION_ROWS * GRID_W


def _na_union_row0(t):
    lo = min(max(r - NA_WIN_R // 2, 0) for r in range(t * LAT_QROWS, (t + 1) * LAT_QROWS))
    return min(min(lo, LAT_ROWS - NA_WIN_R), LAT_ROWS - NA_UNION_ROWS)


def _rope(x, cos, sin):
    lane = lax.broadcasted_iota(jnp.int32, x.shape, 1)
    nxt = pltpu.roll(x, x.shape[1] - 1, axis=1)
    prv = pltpu.roll(x, 1, axis=1)
    partner = jnp.where((lane & 1) == 0, nxt, prv)
    return x * cos + partner * sin


def _attn_lat_kernel(haq_ref, haf_ref, hbq_ref, hbf_ref, cak_ref, cav_ref, cnk_ref, cnv_ref,
                     qg_ref, kg_ref, cos_ref, sin_ref, bias_ref, oab_ref, kva_ref):
    t = pl.program_id(1)
    is_lo = _is_lo_half()
    row0 = pl.multiple_of(t * LAT_TQ, LAT_TQ)
    qg2 = qg_ref[...] * ATT_SCALE
    kg2 = kg_ref[...]

    @pl.when(t == 0)
    def _():
        kp = _rope(_pair_rms(haf_ref[:, A_Q:A_Q + PAIR].astype(F32), kg2, is_lo), cos_ref[...], sin_ref[...])
        vp = haf_ref[:, A_Q + A_KV:A_Q + A_KV + PAIR].astype(F32)
        kp_sw = pltpu.roll(kp, HEAD_DIM, axis=1)
        vp_sw = pltpu.roll(vp, HEAD_DIM, axis=1)
        for kv in range(A_KV_HEADS):
            kva_ref[4 * kv + 0] = jnp.where(is_lo, kp if kv == 0 else kp_sw, 0.0).astype(BF16)
            kva_ref[4 * kv + 1] = jnp.where(is_lo, 0.0, kp_sw if kv == 0 else kp).astype(BF16)
            kva_ref[4 * kv + 2] = jnp.where(is_lo, vp if kv == 0 else vp_sw, 0.0).astype(BF16)
            kva_ref[4 * kv + 3] = jnp.where(is_lo, 0.0, vp_sw if kv == 0 else vp).astype(BF16)

    cos_q = cos_ref[pl.ds(row0, LAT_TQ), :]
    sin_q = sin_ref[pl.ds(row0, LAT_TQ), :]
    qs = [_rope(_pair_rms(haq_ref[:, c * PAIR:(c + 1) * PAIR].astype(F32), qg2, is_lo), cos_q, sin_q).astype(BF16)
          for c in range(A_HEADS // 2)]
    for kv in range(A_KV_HEADS):
        kc = cak_ref[kv].astype(BF16)
        vc = cav_ref[kv].astype(BF16)
        q_st = jnp.concatenate([qs[2 * kv], qs[2 * kv + 1]], axis=0)
        o = (_softmax_pv([_qk(q_st, _pad_lo(kc)), _qk(q_st, kva_ref[4 * kv + 0])], [_pad_lo(vc), kva_ref[4 * kv + 2]])
             + _softmax_pv([_qk(q_st, _pad_hi(kc)), _qk(q_st, kva_ref[4 * kv + 1])], [_pad_hi(vc), kva_ref[4 * kv + 3]]))
        oab_ref[:, (2 * kv) * PAIR:(2 * kv + 1) * PAIR] = o[0:LAT_TQ].astype(BF16)
        oab_ref[:, (2 * kv + 1) * PAIR:(2 * kv + 2) * PAIR] = o[LAT_TQ:2 * LAT_TQ].astype(BF16)

    k0 = _na_union_row0(0) * GRID_W
    for tt in range(1, LAT_NT):
        k0 = jnp.where(t >= tt, _na_union_row0(tt) * GRID_W, k0)
    k0 = pl.multiple_of(k0, GRID_W)
    for j in range(B_HEADS // 2):
        qp = (hbq_ref[:, j * PAIR:(j + 1) * PAIR] * ATT_SCALE).astype(BF16)
        k_lo, k_hi = _split_halves(hbf_ref[pl.ds(k0, NA_UNION), B_W + j * PAIR:B_W + (j + 1) * PAIR], is_lo)
        v_lo, v_hi = _split_halves(hbf_ref[pl.ds(k0, NA_UNION), 2 * B_W + j * PAIR:2 * B_W + (j + 1) * PAIR], is_lo)
        kc_lo = _pad_lo(cnk_ref[2 * j].astype(BF16))
        kc_hi = _pad_hi(cnk_ref[2 * j + 1].astype(BF16))
        vc_lo = _pad_lo(cnv_ref[2 * j].astype(BF16))
        vc_hi = _pad_hi(cnv_ref[2 * j + 1].astype(BF16))
        o = (_softmax_pv([_qk(qp, kc_lo), _qk(qp, k_lo) + bias_ref[2 * j]], [vc_lo, v_lo])
             + _softmax_pv([_qk(qp, kc_hi), _qk(qp, k_hi) + bias_ref[2 * j + 1]], [vc_hi, v_hi]))
        oab_ref[:, A_Q + j * PAIR:A_Q + (j + 1) * PAIR] = o.astype(BF16)


def _attn_lat(h_a, h_b, cak, cav, cnk, cnv, layer, qg2, kg2, cos128, sin128, bias):
    nt = LAT_NT
    cache_spec = lambda nh: pl.BlockSpec((None, None, nh, PAST_LEN, HEAD_DIM), lambda b, t: (b, layer, 0, 0, 0))
    return pl.pallas_call(
        _attn_lat_kernel,
        grid=(DEC_BATCH, nt),
        in_specs=[
            pl.BlockSpec((LAT_TQ, W_A), lambda b, t: (b * nt + t, 0)),
            pl.BlockSpec((DEC_SEQ, W_A), lambda b, t: (b, 0)),
            pl.BlockSpec((LAT_TQ, W_B), lambda b, t: (b * nt + t, 0)),
            pl.BlockSpec((DEC_SEQ, W_B), lambda b, t: (b, 0)),
            cache_spec(A_KV_HEADS), cache_spec(A_KV_HEADS), cache_spec(B_HEADS), cache_spec(B_HEADS),
            pl.BlockSpec((1, PAIR), lambda b, t: (0, 0)),
            pl.BlockSpec((1, PAIR), lambda b, t: (0, 0)),
            pl.BlockSpec((DEC_SEQ, PAIR), lambda b, t: (0, 0)),
            pl.BlockSpec((DEC_SEQ, PAIR), lambda b, t: (0, 0)),
            pl.BlockSpec((B_HEADS, None, LAT_TQ, NA_UNION), lambda b, t: (0, t, 0, 0)),
        ],
        out_specs=pl.BlockSpec((LAT_TQ, A_Q + B_W), lambda b, t: (b * nt + t, 0)),
        out_shape=jax.ShapeDtypeStruct((N_LAT, A_Q + B_W), BF16),
        scratch_shapes=[pltpu.VMEM((4 * A_KV_HEADS, DEC_SEQ, PAIR), BF16)],
        compiler_params=_params(("arbitrary", "arbitrary"), 60),
        name="attn_lat",
    )(h_a, h_a, h_b, h_b, cak, cav, cnk, cnv, qg2, kg2, cos128, sin128, bias)


def _rope_tables():
    t = jnp.arange(DEC_SEQ)
    row = (t // GRID_W).astype(F32)
    col = (t % GRID_W).astype(F32)
    n_freq = HEAD_DIM // 4
    inv = ROPE_THETA ** (-jnp.arange(n_freq, dtype=F32) / n_freq)
    ang = jnp.concatenate([row[:, None] * inv, col[:, None] * inv], axis=-1)
    cos = jnp.repeat(jnp.cos(ang), 2, axis=-1)
    sign = jnp.tile(jnp.array([-1.0, 1.0], F32), HEAD_DIM // 2)
    sin = jnp.repeat(jnp.sin(ang), 2, axis=-1) * sign
    return jnp.tile(cos, (1, 2)), jnp.tile(sin, (1, 2))


def _na_bias(rpb):
    col = np.arange(GRID_W)
    dc = np.clip(col[None, :] - col[:, None] + NA_WIN_C - 1, 0, 2 * NA_WIN_C - 2)
    onehot = (dc[..., None] == np.arange(2 * NA_WIN_C - 1)).astype(np.float32)
    c0 = np.clip(col - NA_WIN_C // 2, 0, GRID_W - NA_WIN_C)
    in_win = (col[None, :] >= c0[:, None]) & (col[None, :] < c0[:, None] + NA_WIN_C)
    toep = jnp.einsum('hdm,ckm->hdck', rpb, onehot, precision=lax.Precision.HIGHEST)
    toep = jnp.where(in_win[None, None], toep, MASK_NEG)
    masked = jnp.full((B_HEADS, GRID_W, GRID_W), MASK_NEG, F32)
    tiles = []
    for t in range(LAT_NT):
        u0 = _na_union_row0(t)
        q_rows = []
        for r in range(t * LAT_QROWS, (t + 1) * LAT_QROWS):
            r0 = min(max(r - NA_WIN_R // 2, 0), LAT_ROWS - NA_WIN_R)
            q_rows.append(jnp.concatenate(
                [toep[:, kr - r + NA_WIN_R - 1] if r0 <= kr < r0 + NA_WIN_R else masked
                 for kr in range(u0, u0 + NA_UNION_ROWS)], axis=-1))
        tiles.append(jnp.concatenate(q_rows, axis=1))
    return jnp.stack(tiles, axis=1)


def _conv_kernel(hc_ref, w_ref, o_ref, *, seq):
    bg = hc_ref[:, 0:C_WIDTH].astype(F32)
    u = hc_ref[:, C_WIDTH:2 * C_WIDTH].astype(F32) * hc_ref[:, 2 * C_WIDTH:3 * C_WIDTH].astype(F32)
    rows = u.shape[0]
    assert seq & (seq - 1) == 0
    pos = lax.broadcasted_iota(jnp.int32, u.shape, 0) & (seq - 1)
    u_prev = jnp.where(pos == 0, 0.0, pltpu.roll(u, 1, axis=0))
    u_next = jnp.where(pos == seq - 1, 0.0, pltpu.roll(u, rows - 1, axis=0))
    y = u_prev * w_ref[0:1, :] + u * w_ref[1:2, :] + u_next * w_ref[2:3, :]
    o_ref[...] = (bg * y).astype(o_ref.dtype)


def _conv(h_c, conv_w, seq):
    n = h_c.shape[0]
    tm = 1024
    return pl.pallas_call(
        functools.partial(_conv_kernel, seq=seq),
        grid=(n // tm,),
        in_specs=[pl.BlockSpec((tm, W_C), lambda i: (i, 0)), pl.BlockSpec((3, C_WIDTH), lambda i: (0, 0))],
        out_specs=pl.BlockSpec((tm, C_WIDTH), lambda i: (i, 0)),
        out_shape=jax.ShapeDtypeStruct((n, C_WIDTH), BF16),
        compiler_params=_params(("arbitrary",), 40),
        name="short_conv",
    )(h_c, conv_w)


def _mix_kernel(x_ref, oab_ref, oc_ref, hg_ref, wa_ref, wb_ref, wc_ref, wo_ref, g1_ref, n2g_ref, sc2_ref, sh2_ref,
                wrt_ref, xo_ref, xn2_ref, pt_ref):
    ga = jax.nn.sigmoid(hg_ref[:, 0:D_MODEL].astype(F32))
    gb = jax.nn.sigmoid(hg_ref[:, D_MODEL:2 * D_MODEL].astype(F32))
    gc = jax.nn.sigmoid(hg_ref[:, 2 * D_MODEL:3 * D_MODEL].astype(F32))
    pa = jnp.dot(oab_ref[:, 0:A_Q], wa_ref[...], preferred_element_type=F32)
    pb = jnp.dot(oab_ref[:, A_Q:A_Q + B_W], wb_ref[...], preferred_element_type=F32)
    pc = jnp.dot(oc_ref[...], wc_ref[...], preferred_element_type=F32)
    merged = ga * pa + gb * pb + gc * pc
    xo = x_ref[...] + g1_ref[...] * jnp.dot(merged.astype(BF16), wo_ref[...], preferred_element_type=F32)
    xo_ref[...] = xo
    y = xo * lax.rsqrt(jnp.mean(xo * xo, axis=-1, keepdims=True) + EPS) * n2g_ref[...]
    xn2 = y * (1.0 + sc2_ref[...]) + sh2_ref[...]
    xn2_ref[...] = xn2
    logits_t = lax.dot_general(wrt_ref[...], xn2.astype(BF16), _NT, preferred_element_type=F32)
    z = jnp.exp(logits_t - logits_t.max(axis=0, keepdims=True))
    pt_ref[...] = z / z.sum(axis=0, keepdims=True)


def _mix(x, oab, oc, h_g, wa, wb, wc, wo, g1, n2g, sc2, sh2, wrt, seq):
    n = x.shape[0]
    tm = 512
    per_batch = g1.shape[0] > 1
    mod_map = (lambda i: ((i * tm) // seq, 0, 0)) if per_batch else (lambda i: (0, 0, 0))
    mod_spec = pl.BlockSpec((None, 1, D_MODEL), mod_map)
    full = lambda a: pl.BlockSpec(a.shape, lambda i: (0,) * a.ndim)
    row = lambda w: pl.BlockSpec((tm, w), lambda i: (i, 0))
    return pl.pallas_call(
        _mix_kernel,
        grid=(n // tm,),
        in_specs=[row(D_MODEL), row(A_Q + B_W), row(C_WIDTH), row(GATE_W), full(wa), full(wb), full(wc), full(wo),
                  mod_spec, full(n2g), mod_spec, mod_spec, full(wrt)],
        out_specs=[row(D_MODEL), row(D_MODEL), pl.BlockSpec((N_EXPERTS, tm), lambda i: (0, i))],
        out_shape=[jax.ShapeDtypeStruct((n, D_MODEL), F32), jax.ShapeDtypeStruct((n, D_MODEL), F32),
                   jax.ShapeDtypeStruct((N_EXPERTS, n), F32)],
        compiler_params=_params(("arbitrary",), 56),
        name="mix",
    )(x, oab, oc, h_g, wa, wb, wc, wo, g1, n2g, sc2, sh2, wrt)


MOE_TF = 512
MOE_ROW_CHUNKS = ((0, 512), (512, 512), (1024, 256))
MOE_ISSUE_UNROLL = 8


def _moe_issue_rows(idx_ref, idx0, src_hbm, dst_ref, dst_row0, count, sem):
    def body(i, carry):
        for u in range(MOE_ISSUE_UNROLL):
            r = i * MOE_ISSUE_UNROLL + u
            tok = idx_ref[idx0 + r]
            pltpu.make_async_copy(src_hbm.at[pl.ds(tok, 1)], dst_ref.at[pl.ds(dst_row0 + r, 1)], sem).start()
        return carry

    lax.fori_loop(0, count // MOE_ISSUE_UNROLL, body, 0)


def _moe_kernel(idx_ref, xc_hbm, xl_hbm, aff_ref, wg_ref, wu_ref, wd_ref, ye_ref, xe_ref, sem):
    e = pl.program_id(0)
    f = pl.program_id(1)
    slot = e % 2

    def gather(expert, s):
        base = expert * CAP_ALL
        _moe_issue_rows(idx_ref, base, xc_hbm, xe_ref.at[s], 0, CAP_CTX, sem.at[s])
        _moe_issue_rows(idx_ref, base + CAP_CTX, xl_hbm, xe_ref.at[s], CAP_CTX, CAP_LAT, sem.at[s])

    @pl.when(f == 0)
    def _():
        @pl.when(e == 0)
        def _():
            gather(0, 0)

        pltpu.make_async_copy(xc_hbm.at[pl.ds(0, CAP_ALL)], xe_ref.at[slot], sem.at[slot]).wait()

        @pl.when(e + 1 < N_EXPERTS)
        def _():
            gather(e + 1, 1 - slot)

    wg = wg_ref[...].astype(BF16)
    wu = wu_ref[...].astype(BF16)
    wd = wd_ref[...].astype(BF16)
    for r0, rn in MOE_ROW_CHUNKS:
        xe = xe_ref[slot, r0:r0 + rn, :].astype(BF16)
        hg = jnp.dot(xe, wg, preferred_element_type=F32)
        hu = jnp.dot(xe, wu, preferred_element_type=F32)
        act = (hg * jax.nn.sigmoid(hg) * hu).astype(BF16)
        part = jnp.dot(act, wd, preferred_element_type=F32) * aff_ref[r0:r0 + rn, :]

        @pl.when(f == 0)
        def _():
            ye_ref[r0:r0 + rn, :] = part

        @pl.when(f != 0)
        def _():
            ye_ref[r0:r0 + rn, :] += part


def _moe(idx_all, xn2_ctx, xn2_lat, aff_all, w_gate_e, w_up_e, w_down_e, layer):
    nf = EXPERT_FF // MOE_TF
    grid_spec = pltpu.PrefetchScalarGridSpec(
        num_scalar_prefetch=1,
        grid=(N_EXPERTS, nf),
        in_specs=[
            pl.BlockSpec(memory_space=pl.ANY),
            pl.BlockSpec(memory_space=pl.ANY),
            pl.BlockSpec((None, CAP_ALL, 1), lambda e, f, idx: (e, 0, 0)),
            pl.BlockSpec((None, None, D_MODEL, MOE_TF), lambda e, f, idx: (layer, e, 0, f)),
            pl.BlockSpec((None, None, D_MODEL, MOE_TF), lambda e, f, idx: (layer, e, 0, f)),
            pl.BlockSpec((None, None, MOE_TF, D_MODEL), lambda e, f, idx: (layer, e, f, 0)),
        ],
        out_specs=pl.BlockSpec((None, CAP_ALL, D_MODEL), lambda e, f, idx: (e, 0, 0)),
        scratch_shapes=[pltpu.VMEM((2, CAP_ALL, D_MODEL), F32), pltpu.SemaphoreType.DMA((2,))],
    )
    return pl.pallas_call(
        _moe_kernel,
        grid_spec=grid_spec,
        out_shape=jax.ShapeDtypeStruct((N_EXPERTS, CAP_ALL, D_MODEL), F32),
        compiler_params=_params(("arbitrary", "arbitrary"), 56),
        name="moe_experts",
    )(idx_all, xn2_ctx, xn2_lat, aff_all, w_gate_e, w_up_e, w_down_e)


LANES = 128
POS_SPLIT = 32


def _prefix_count(mask_bf, tri_lane, tri_blk_strict):
    within = jnp.dot(mask_bf, tri_lane, preferred_element_type=F32)
    bs = jnp.broadcast_to(within[:, LANES - 1:LANES], within.shape).astype(BF16)
    return within + jnp.dot(tri_blk_strict, bs, preferred_element_type=F32)


def _route_kernel(p_ref, idx_ref, aff_ref, *, cap):
    p_all = p_ref[...]
    nb = p_all.shape[1]
    capf = float(cap)

    def as_float(word):
        return lax.bitcast_convert_type(word, F32)

    def bisect(_, c):
        lo, hi = c
        mid = lo + ((hi - lo) >> 1)
        ok = jnp.sum(jnp.where(p_all >= as_float(mid), 1.0, 0.0), axis=(1, 2), keepdims=True) >= capf
        return jnp.where(ok, mid, lo), jnp.where(ok, hi, mid)

    lo0 = jnp.zeros((N_EXPERTS, 1, 1), jnp.int32)
    hi0 = jnp.full((N_EXPERTS, 1, 1), 0x7F800000, jnp.int32)
    thr_all, _ = lax.fori_loop(0, 31, bisect, (lo0, hi0))
    thr_val = as_float(thr_all)

    li = lax.broadcasted_iota(jnp.int32, (LANES, LANES), 0)
    lj = lax.broadcasted_iota(jnp.int32, (LANES, LANES), 1)
    tri_lane = jnp.where(li <= lj, 1.0, 0.0).astype(BF16)
    bi = lax.broadcasted_iota(jnp.int32, (nb, nb), 0)
    bj = lax.broadcasted_iota(jnp.int32, (nb, nb), 1)
    tri_blk_strict = jnp.where(bj < bi, 1.0, 0.0).astype(BF16)
    tri_blk_incl = jnp.where(bi <= bj, 1.0, 0.0).astype(BF16)
    s_col = lax.broadcasted_iota(jnp.int32, (cap, 1), 0).astype(F32)
    blk_id = lax.broadcasted_iota(jnp.int32, (1, nb), 1).astype(F32)
    lane_id = lax.broadcasted_iota(jnp.int32, (1, LANES), 1).astype(F32)
    ones_rows = jnp.ones((8, LANES), BF16)

    for e in range(N_EXPERTS):
        p = p_all[e]
        thr = thr_val[e]
        gt = p > thr
        eq = p == thr
        need = capf - jnp.sum(jnp.where(gt, 1.0, 0.0), keepdims=True)
        tie_rank = _prefix_count(jnp.where(eq, 1.0, 0.0).astype(BF16), tri_lane, tri_blk_strict)
        sel = jnp.where(gt | (eq & (tie_rank <= need)), 1.0, 0.0).astype(BF16)
        pos = _prefix_count(sel, tri_lane, tri_blk_strict)

        bs_row = lax.dot_general(ones_rows, sel, _NT, preferred_element_type=F32)
        bp_row = jnp.dot(bs_row.astype(BF16), tri_blk_incl, preferred_element_type=F32)[0:1, :]
        blk = jnp.sum(jnp.where(bp_row <= s_col, 1.0, 0.0), axis=-1, keepdims=True)
        onehot = jnp.where(blk == blk_id, 1.0, 0.0).astype(BF16)
        pos_hi = jnp.floor(pos * (1.0 / POS_SPLIT))
        pos_lo = pos - pos_hi * POS_SPLIT
        pos_row = (jnp.dot(onehot, pos_hi.astype(BF16), preferred_element_type=F32) * POS_SPLIT
                   + jnp.dot(onehot, pos_lo.astype(BF16), preferred_element_type=F32))
        lane = jnp.sum(jnp.where(pos_row <= s_col, 1.0, 0.0), axis=-1, keepdims=True)
        idx_ref[e] = (blk * LANES + lane).astype(jnp.int32)

        p1 = p.astype(BF16)
        r1 = p - p1.astype(F32)
        p2 = r1.astype(BF16)
        p3 = (r1 - p2.astype(F32)).astype(BF16)
        p_row = (jnp.dot(onehot, p1, preferred_element_type=F32) + jnp.dot(onehot, p2, preferred_element_type=F32)
                 + jnp.dot(onehot, p3, preferred_element_type=F32))
        aff_ref[e] = jnp.sum(jnp.where(lane_id == lane, p_row, 0.0), axis=-1, keepdims=True)


def _route(probs_t, cap):
    nb = probs_t.shape[1] // LANES
    slot = pl.BlockSpec((N_EXPERTS, cap, 1), lambda i: (0, 0, 0))
    return pl.pallas_call(
        functools.partial(_route_kernel, cap=cap),
        grid=(1,),
        in_specs=[pl.BlockSpec((N_EXPERTS, nb, LANES), lambda i: (0, 0, 0))],
        out_specs=[slot, slot],
        out_shape=[jax.ShapeDtypeStruct((N_EXPERTS, cap, 1), jnp.int32),
                   jax.ShapeDtypeStruct((N_EXPERTS, cap, 1), F32)],
        compiler_params=_params(("arbitrary",), 56),
        name="route",
    )(probs_t.reshape(N_EXPERTS, nb, LANES))


CMB_TT = 1024
CMB_CH = 256
CMB_GROUP = 8
CMB_EPI_ROWS = 256


def _combine_kernel(idx_ref, st_ref, ye_hbm, x_ref, g2_ref, ng_ref, sc_ref, sh_ref, *rest, off, cap, nt, final):
    if final:
        y_ref, acc_ref, buf_ref, sem = rest
    else:
        xo_ref, xn_ref, acc_ref, buf_ref, sem = rest
    k = pl.program_id(0)
    t0 = k * CMB_TT
    acc_ref[...] = jnp.zeros_like(acc_ref)

    def slot_range(e):
        s0 = st_ref[e * (nt + 1) + k]
        s1 = st_ref[e * (nt + 1) + k + 1]
        return s0, s1, jnp.minimum(s0 & -CMB_GROUP, cap - CMB_CH)

    def chunk_copy(e, start):
        src = ye_hbm.at[e, pl.ds(pl.multiple_of(off + start, CMB_GROUP), CMB_CH)]
        return pltpu.make_async_copy(src, buf_ref.at[e], sem.at[e])

    def accumulate(e, start, lo, hi):
        def body(i, carry):
            r0 = pl.multiple_of(i * CMB_GROUP, CMB_GROUP)
            tile = buf_ref[e, pl.ds(r0, CMB_GROUP), :]
            dst = []
            for u in range(CMB_GROUP):
                slot = start + r0 + u
                tok = idx_ref[e * CAP_ALL + off + slot]
                dst.append(jnp.where((slot >= lo) & (slot < hi), tok - t0, CMB_TT))
            rows = [acc_ref[pl.ds(d, 1), :] for d in dst]
            for u in range(CMB_GROUP):
                acc_ref[pl.ds(dst[u], 1), :] = rows[u] + tile[u:u + 1, :]
            return carry

        lax.fori_loop((lo - start) // CMB_GROUP, (hi - start + CMB_GROUP - 1) // CMB_GROUP, body, 0)

    for e in range(N_EXPERTS):
        chunk_copy(e, slot_range(e)[2]).start()
    for e in range(N_EXPERTS):
        s0, s1, a = slot_range(e)
        chunk_copy(e, a).wait()
        accumulate(e, a, s0, jnp.minimum(s1, a + CMB_CH))

        def extra(j, carry, e=e, s1=s1, a=a):
            lo = a + CMB_CH * (j + 1)
            start = jnp.minimum(lo, cap - CMB_CH)
            cp = chunk_copy(e, start)
            cp.start()
            cp.wait()
            accumulate(e, start, lo, jnp.minimum(s1, lo + CMB_CH))
            return carry

        lax.fori_loop(0, jnp.maximum(s1 - a - 1, 0) // CMB_CH, extra, 0)

    for r0 in range(0, CMB_TT, CMB_EPI_ROWS):
        rs = slice(r0, r0 + CMB_EPI_ROWS)
        xo = x_ref[rs, :] + g2_ref[...] * acc_ref[rs, :]
        y = xo * lax.rsqrt(jnp.mean(xo * xo, axis=-1, keepdims=True) + EPS) * ng_ref[...]
        if final:
            y_ref[rs, :] = y
        else:
            xo_ref[rs, :] = xo
            xn_ref[rs, :] = (y * (1.0 + sc_ref[...]) + sh_ref[...]).astype(xn_ref.dtype)


def _combine(idx_all, starts, ye, x, g2, ng, sc, sh, seq, off, cap, final):
    n = x.shape[0]
    nt = n // CMB_TT
    per_batch = g2.shape[0] > 1
    mod_map = ((lambda i, *_: ((i * CMB_TT) // seq, 0, 0)) if per_batch else (lambda i, *_: (0, 0, 0)))
    mod_spec = pl.BlockSpec((None, 1, D_MODEL), mod_map)
    row = pl.BlockSpec((CMB_TT, D_MODEL), lambda i, *_: (i, 0))
    vec = pl.BlockSpec((1, D_MODEL), lambda i, *_: (0, 0))
    if final:
        out_specs, out_shape = row, jax.ShapeDtypeStruct((n, D_MODEL), F32)
    else:
        out_specs = [row, row]
        out_shape = [jax.ShapeDtypeStruct((n, D_MODEL), F32), jax.ShapeDtypeStruct((n, D_MODEL), BF16)]
    grid_spec = pltpu.PrefetchScalarGridSpec(
        num_scalar_prefetch=2,
        grid=(nt,),
        in_specs=[pl.BlockSpec(memory_space=pl.ANY), row, mod_spec, vec, mod_spec, mod_spec],
        out_specs=out_specs,
        scratch_shapes=[pltpu.VMEM((CMB_TT + CMB_GROUP, D_MODEL), F32),
                        pltpu.VMEM((N_EXPERTS, CMB_CH, D_MODEL), F32),
                        pltpu.SemaphoreType.DMA((N_EXPERTS,))],
    )
    return pl.pallas_call(
        functools.partial(_combine_kernel, off=off, cap=cap, nt=nt, final=final),
        grid_spec=grid_spec,
        out_shape=out_shape,
        compiler_params=_params(("arbitrary",), 60),
        name="combine",
    )(idx_all, starts.reshape(-1), ye, x, g2, ng, sc, sh)


def _tile_starts(idx_sorted, n):
    bounds = jnp.arange(n // CMB_TT + 1, dtype=jnp.int32) * CMB_TT
    return jnp.sum(idx_sorted[:, :, None] < bounds[None, None, :], axis=1, dtype=jnp.int32)


def kernel(x_prompt, x_sample, cache_attn_k, cache_attn_v, cache_na_k, cache_na_v, c, c_ctx, w_ada, b_ada, norm1_g,
           w_in, q_norm_g, k_norm_g, na_rpb, conv_w, w_proj_a, w_proj_b, w_proj_c, w_out, norm2_g, w_router,
           w_gate_e, w_up_e, w_down_e, final_g):
    cond8 = jnp.zeros((8, D_MODEL), F32).at[0].set(c_ctx).at[1:1 + DEC_BATCH].set(c)
    mods = _adaln(cond8, w_ada, b_ada).reshape(DEPTH, 8, N_MOD, D_MODEL)
    cos128, sin128 = _rope_tables()

    hp = x_prompt.reshape(N_CTX, D_MODEL)
    hs = x_sample.reshape(N_LAT, D_MODEL)
    new_ak, new_av, new_nk, new_nv = [], [], [], []
    for l in range(DEPTH):
        mod_ctx = [mods[l, 0:1, i][:, None, :] for i in range(N_MOD)]
        mod_lat = [mods[l, 1:1 + DEC_BATCH, i][:, None, :] for i in range(N_MOD)]
        n1g = norm1_g[l][None]
        n2g = norm2_g[l][None]
        qg2 = jnp.tile(q_norm_g[l][None], (1, 2))
        kg2 = jnp.tile(k_norm_g[l][None], (1, 2))
        wa, wb, wc, wo = (w_proj_a[l].astype(BF16), w_proj_b[l].astype(BF16), w_proj_c[l].astype(BF16),
                          w_out[l].astype(BF16))
        wrt = w_router[l].T.astype(BF16)
        bias = _na_bias(na_rpb[l])

        def sublayer1(x, xn, mod, seq, is_ctx):
            _, _, g1, sh2, sc2, _ = mod
            kv_dtype = F32 if is_ctx else BF16
            h_a = _inproj(xn, w_in, l, COL_A, W_A, 768, kv_dtype)
            h_b = _inproj(xn, w_in, l, COL_B, W_B, 768, kv_dtype)
            h_c = _inproj(xn, w_in, l, COL_C, W_C, 768, BF16)
            h_g = _inproj(xn, w_in, l, COL_G, GATE_W, 768, BF16)
            kv = None
            if is_ctx:
                oab, ka, va, kb, vb = _attn_ctx(h_a, h_b, qg2, kg2)
                kv = (ka, va, kb, vb)
            else:
                oab = _attn_lat(h_a, h_b, cache_attn_k, cache_attn_v, cache_na_k, cache_na_v, l, qg2, kg2,
                                cos128, sin128, bias)
            oc = _conv(h_c, conv_w[l], seq)
            xo, xn2, probs_t = _mix(x, oab, oc, h_g, wa, wb, wc, wo, g1, n2g, sc2, sh2, wrt, seq)
            return xo, xn2, probs_t, kv

        if l == 0:
            xn_p = _norm_mod(hp, n1g, mod_ctx[1], mod_ctx[0], SEQ, BF16)
            xn_s = _norm_mod(hs, n1g, mod_lat[1], mod_lat[0], DEC_SEQ, BF16)
        hp, xn2_c, pt_c, (ka, va, kb, vb) = sublayer1(hp, xn_p, mod_ctx, SEQ, True)
        hs, xn2_l, pt_l, _ = sublayer1(hs, xn_s, mod_lat, DEC_SEQ, False)
        new_ak.append(ka)
        new_av.append(va)
        new_nk.append(kb)
        new_nv.append(vb)

        idx_c, aff_c = _route(pt_c, CAP_CTX)
        idx_l, aff_l = _route(pt_l, CAP_LAT)
        idx_all = jnp.concatenate([idx_c, idx_l], axis=1).reshape(-1)
        aff_all = jnp.concatenate([aff_c, aff_l], axis=1)
        ye = _moe(idx_all, xn2_c, xn2_l, aff_all, w_gate_e, w_up_e, w_down_e, l)

        final = l == DEPTH - 1
        if final:
            ng, nsc_c, nsh_c, nsc_l, nsh_l = final_g[None], mod_ctx[1], mod_ctx[0], mod_lat[1], mod_lat[0]
        else:
            nxt_ctx = [mods[l + 1, 0:1, i][:, None, :] for i in range(2)]
            nxt_lat = [mods[l + 1, 1:1 + DEC_BATCH, i][:, None, :] for i in range(2)]
            ng, nsc_c, nsh_c, nsc_l, nsh_l = norm1_g[l + 1][None], nxt_ctx[1], nxt_ctx[0], nxt_lat[1], nxt_lat[0]
        out_c = _combine(idx_all, _tile_starts(idx_c[..., 0], N_CTX), ye, hp, mod_ctx[5], ng, nsc_c, nsh_c,
                         SEQ, 0, CAP_CTX, final)
        out_l = _combine(idx_all, _tile_starts(idx_l[..., 0], N_LAT), ye, hs, mod_lat[5], ng, nsc_l, nsh_l,
                         DEC_SEQ, CAP_CTX, CAP_LAT, final)
        if final:
            y_prompt = out_c.reshape(BATCH, SEQ, D_MODEL)
            y_sample = out_l.reshape(DEC_BATCH, DEC_SEQ, D_MODEL)
        else:
            (hp, xn_p), (hs, xn_s) = out_c, out_l

    return (y_prompt, y_sample, jnp.stack(new_ak, axis=1), jnp.stack(new_av, axis=1),
            jnp.stack(new_nk, axis=1), jnp.stack(new_nv, axis=1))
```

```python
import functools

import jax
import jax.numpy as jnp
import numpy as np
from jax import lax
from jax.experimental import pallas as pl
from jax.experimental.pallas import tpu as pltpu

D_MODEL = 1024
BATCH = 32
SEQ = 256
DEPTH = 2
DEC_BATCH = 2
DEC_SEQ = 1024
PAST_LEN = 256
GRID_W = 64
HEAD_DIM = 64
A_HEADS = 8
A_KV_HEADS = 2
A_GROUP = A_HEADS // A_KV_HEADS
B_HEADS = 8
C_WIDTH = 512
NA_WIN_R = 8
NA_WIN_C = 16
N_EXPERTS = 16
EXPERT_FF = 1024
CAP_FACTOR = 2
ROPE_THETA = 10000.0
EPS = 1e-6
N_MOD = 6

A_Q = A_HEADS * HEAD_DIM
A_KV = A_KV_HEADS * HEAD_DIM
B_W = B_HEADS * HEAD_DIM
GATE_W = 3 * D_MODEL
COL_A = 0
W_A = A_Q + 2 * A_KV
COL_B = COL_A + W_A
W_B = 3 * B_W
COL_C = COL_B + W_B
W_C = 3 * C_WIDTH
COL_G = COL_C + W_C

N_CTX = BATCH * SEQ
N_LAT = DEC_BATCH * DEC_SEQ
CAP_CTX = CAP_FACTOR * N_CTX // N_EXPERTS
CAP_LAT = CAP_FACTOR * N_LAT // N_EXPERTS
CAP_ALL = CAP_CTX + CAP_LAT
LAT_ROWS = DEC_SEQ // GRID_W
MASK_NEG = -1e30
ATT_SCALE = HEAD_DIM ** -0.5

LANES = 128
ROW_TILE = 8
assert D_MODEL == ROW_TILE * LANES

F32 = jnp.float32
BF16 = jnp.bfloat16
_NT = (((1,), (1,)), ((), ()))

_MIB = 1024 * 1024


def _params(sem, vmem_mib):
    return pltpu.CompilerParams(dimension_semantics=sem, vmem_limit_bytes=vmem_mib * _MIB)


def _adaln_kernel(cond_ref, w_ref, b_ref, o_ref):
    c = cond_ref[...]
    s = c * jax.nn.sigmoid(c)
    o_ref[...] = jnp.dot(s.astype(BF16), w_ref[...].astype(BF16), preferred_element_type=F32) + b_ref[...]


def _adaln(cond8, w_ada, b_ada):
    tn = 1536
    ncol = N_MOD * D_MODEL
    return pl.pallas_call(
        _adaln_kernel,
        grid=(DEPTH, ncol // tn),
        in_specs=[
            pl.BlockSpec((8, D_MODEL), lambda l, j: (0, 0)),
            pl.BlockSpec((None, D_MODEL, tn), lambda l, j: (l, 0, j)),
            pl.BlockSpec((None, 1, tn), lambda l, j: (l, 0, j)),
        ],
        out_specs=pl.BlockSpec((None, 8, tn), lambda l, j: (l, 0, j)),
        out_shape=jax.ShapeDtypeStruct((DEPTH, 8, ncol), F32),
        compiler_params=_params(("arbitrary", "arbitrary"), 40),
        name="adaln",
    )(cond8, w_ada, b_ada.reshape(DEPTH, 1, ncol))


def _norm_mod_kernel(x_ref, g_ref, sc_ref, sh_ref, o_ref):
    x = x_ref[...]
    y = x * lax.rsqrt(jnp.mean(x * x, axis=-1, keepdims=True) + EPS) * g_ref[...]
    o_ref[...] = (y * (1.0 + sc_ref[...]) + sh_ref[...]).astype(o_ref.dtype)


def _norm_mod(x, g, sc, sh, seq, out_dtype):
    n = x.shape[0]
    tm = 512
    per_batch = sc.shape[0] > 1
    mod_map = (lambda i: ((i * tm) // seq, 0, 0)) if per_batch else (lambda i: (0, 0, 0))
    return pl.pallas_call(
        _norm_mod_kernel,
        grid=(n // tm,),
        in_specs=[
            pl.BlockSpec((tm, D_MODEL), lambda i: (i, 0)),
            pl.BlockSpec((1, D_MODEL), lambda i: (0, 0)),
            pl.BlockSpec((None, 1, D_MODEL), mod_map),
            pl.BlockSpec((None, 1, D_MODEL), mod_map),
        ],
        out_specs=pl.BlockSpec((tm, D_MODEL), lambda i: (i, 0)),
        out_shape=jax.ShapeDtypeStruct((n, D_MODEL), out_dtype),
        compiler_params=_params(("arbitrary",), 32),
        name="norm_mod",
    )(x, g, sc, sh)


def _inproj_kernel(a_ref, w_ref, o_ref, wbf_ref):
    @pl.when(pl.program_id(1) == 0)
    def _():
        wbf_ref[...] = w_ref[...].astype(BF16)

    o_ref[...] = jnp.dot(a_ref[...], wbf_ref[...], preferred_element_type=F32).astype(o_ref.dtype)


def _inproj(xn, w_in, layer, col0, width, tn, out_dtype):
    n = xn.shape[0]
    tm = 2048
    joff = col0 // tn
    assert col0 % tn == 0 and width % tn == 0
    return pl.pallas_call(
        _inproj_kernel,
        grid=(width // tn, n // tm),
        in_specs=[
            pl.BlockSpec((tm, D_MODEL), lambda j, i: (i, 0)),
            pl.BlockSpec((None, D_MODEL, tn), lambda j, i: (layer, 0, j + joff)),
        ],
        out_specs=pl.BlockSpec((tm, tn), lambda j, i: (i, j)),
        out_shape=jax.ShapeDtypeStruct((n, width), out_dtype),
        scratch_shapes=[pltpu.VMEM((D_MODEL, tn), BF16)],
        compiler_params=_params(("arbitrary", "arbitrary"), 48),
        name="inproj",
    )(xn, w_in)


PAIR = 2 * HEAD_DIM


def _is_lo_half():
    return lax.broadcasted_iota(jnp.int32, (1, PAIR), 1) < HEAD_DIM


def _pair_rms(x, g2, is_lo):
    x2 = x * x
    s_lo = jnp.sum(jnp.where(is_lo, x2, 0.0), axis=-1, keepdims=True)
    s_hi = jnp.sum(jnp.where(is_lo, 0.0, x2), axis=-1, keepdims=True)
    ms = jnp.where(is_lo, s_lo, s_hi) * (1.0 / HEAD_DIM)
    return x * lax.rsqrt(ms + EPS) * g2


def _split_halves(x, is_lo):
    return jnp.where(is_lo, x, 0.0).astype(BF16), jnp.where(is_lo, 0.0, x).astype(BF16)


def _pad_lo(x):
    return jnp.concatenate([x, jnp.zeros_like(x)], axis=1)


def _pad_hi(x):
    return jnp.concatenate([jnp.zeros_like(x), x], axis=1)


def _qk(q, k):
    return lax.dot_general(q, k, _NT, preferred_element_type=F32)


def _softmax_pv(scores, values):
    m = scores[0].max(axis=-1, keepdims=True)
    for s in scores[1:]:
        m = jnp.maximum(m, s.max(axis=-1, keepdims=True))
    acc = None
    den = None
    for s, v in zip(scores, values):
        p = jnp.exp(s - m)
        d = p.sum(axis=-1, keepdims=True)
        o = jnp.dot(p.astype(BF16), v, preferred_element_type=F32)
        acc = o if acc is None else acc + o
        den = d if den is None else den + d
    return acc * (1.0 / den)


CTX_NB = 2


def _attn_ctx_kernel(ha_ref, hb_ref, qg_ref, kg_ref, oab_ref, ak_ref, av_ref, nk_ref, nv_ref):
    is_lo = _is_lo_half()
    qg2 = qg_ref[...] * ATT_SCALE
    kg2 = kg_ref[...]
    for b in range(CTX_NB):
        rows = slice(b * SEQ, (b + 1) * SEQ)
        kp = _pair_rms(ha_ref[rows, A_Q:A_Q + PAIR], kg2, is_lo)
        vp = ha_ref[rows, A_Q + A_KV:A_Q + A_KV + PAIR]
        for kv in range(A_KV_HEADS):
            ak_ref[b, kv] = kp[:, kv * HEAD_DIM:(kv + 1) * HEAD_DIM]
            av_ref[b, kv] = vp[:, kv * HEAD_DIM:(kv + 1) * HEAD_DIM]
        kp_sw = pltpu.roll(kp, HEAD_DIM, axis=1)
        vp_sw = pltpu.roll(vp, HEAD_DIM, axis=1)
        qs = [_pair_rms(ha_ref[rows, c * PAIR:(c + 1) * PAIR], qg2, is_lo).astype(BF16)
              for c in range(A_HEADS // 2)]
        for kv in range(A_KV_HEADS):
            k_lo = jnp.where(is_lo, kp if kv == 0 else kp_sw, 0.0).astype(BF16)
            k_hi = jnp.where(is_lo, 0.0, kp_sw if kv == 0 else kp).astype(BF16)
            v_lo = jnp.where(is_lo, vp if kv == 0 else vp_sw, 0.0).astype(BF16)
            v_hi = jnp.where(is_lo, 0.0, vp_sw if kv == 0 else vp).astype(BF16)
            q_st = jnp.concatenate([qs[2 * kv], qs[2 * kv + 1]], axis=0)
            o = _softmax_pv([_qk(q_st, k_lo)], [v_lo]) + _softmax_pv([_qk(q_st, k_hi)], [v_hi])
            oab_ref[rows, (2 * kv) * PAIR:(2 * kv + 1) * PAIR] = o[0:SEQ].astype(BF16)
            oab_ref[rows, (2 * kv + 1) * PAIR:(2 * kv + 2) * PAIR] = o[SEQ:2 * SEQ].astype(BF16)
        for j in range(B_HEADS // 2):
            qp = (hb_ref[rows, j * PAIR:(j + 1) * PAIR] * ATT_SCALE).astype(BF16)
            kp_b = hb_ref[rows, B_W + j * PAIR:B_W + (j + 1) * PAIR]
            vp_b = hb_ref[rows, 2 * B_W + j * PAIR:2 * B_W + (j + 1) * PAIR]
            for u in range(2):
                nk_ref[b, 2 * j + u] = kp_b[:, u * HEAD_DIM:(u + 1) * HEAD_DIM]
                nv_ref[b, 2 * j + u] = vp_b[:, u * HEAD_DIM:(u + 1) * HEAD_DIM]
            k_lo, k_hi = _split_halves(kp_b, is_lo)
            v_lo, v_hi = _split_halves(vp_b, is_lo)
            o = _softmax_pv([_qk(qp, k_lo)], [v_lo]) + _softmax_pv([_qk(qp, k_hi)], [v_hi])
            oab_ref[rows, A_Q + j * PAIR:A_Q + (j + 1) * PAIR] = o.astype(BF16)


def _attn_ctx(h_a, h_b, qg2, kg2):
    rows = CTX_NB * SEQ
    kv_spec = lambda nh: pl.BlockSpec((CTX_NB, nh, SEQ, HEAD_DIM), lambda b: (b, 0, 0, 0))
    kv_shape = lambda nh: jax.ShapeDtypeStruct((BATCH, nh, SEQ, HEAD_DIM), F32)
    return pl.pallas_call(
        _attn_ctx_kernel,
        grid=(BATCH // CTX_NB,),
        in_specs=[
            pl.BlockSpec((rows, W_A), lambda b: (b, 0)),
            pl.BlockSpec((rows, W_B), lambda b: (b, 0)),
            pl.BlockSpec((1, PAIR), lambda b: (0, 0)),
            pl.BlockSpec((1, PAIR), lambda b: (0, 0)),
        ],
        out_specs=[
            pl.BlockSpec((rows, A_Q + B_W), lambda b: (b, 0)),
            kv_spec(A_KV_HEADS), kv_spec(A_KV_HEADS), kv_spec(B_HEADS), kv_spec(B_HEADS),
        ],
        out_shape=[
            jax.ShapeDtypeStruct((N_CTX, A_Q + B_W), BF16),
            kv_shape(A_KV_HEADS), kv_shape(A_KV_HEADS), kv_shape(B_HEADS), kv_shape(B_HEADS),
        ],
        compiler_params=_params(("arbitrary",), 40),
        name="attn_ctx",
    )(h_a, h_b, qg2, kg2)


LAT_TQ = 256
LAT_QROWS = LAT_TQ // GRID_W
LAT_NT = DEC_SEQ // LAT_TQ
NA_UNION_ROWS = 12
NA_UNION = NA_UNION_ROWS * GRID_W


def _na_union_row0(t):
    lo = min(max(r - NA_WIN_R // 2, 0) for r in range(t * LAT_QROWS, (t + 1) * LAT_QROWS))
    return min(min(lo, LAT_ROWS - NA_WIN_R), LAT_ROWS - NA_UNION_ROWS)


def _rope(x, cos, sin):
    lane = lax.broadcasted_iota(jnp.int32, x.shape, 1)
    nxt = pltpu.roll(x, x.shape[1] - 1, axis=1)
    prv = pltpu.roll(x, 1, axis=1)
    partner = jnp.where((lane & 1) == 0, nxt, prv)
    return x * cos + partner * sin


def _attn_lat_kernel(haq_ref, haf_ref, hbq_ref, hbf_ref, cak_ref, cav_ref, cnk_ref, cnv_ref,
                     qg_ref, kg_ref, cos_ref, sin_ref, bias_ref, oab_ref, kva_ref):
    t = pl.program_id(1)
    is_lo = _is_lo_half()
    row0 = pl.multiple_of(t * LAT_TQ, LAT_TQ)
    qg2 = qg_ref[...] * ATT_SCALE
    kg2 = kg_ref[...]

    @pl.when(t == 0)
    def _():
        kp = _rope(_pair_rms(haf_ref[:, A_Q:A_Q + PAIR].astype(F32), kg2, is_lo), cos_ref[...], sin_ref[...])
        vp = haf_ref[:, A_Q + A_KV:A_Q + A_KV + PAIR].astype(F32)
        kp_sw = pltpu.roll(kp, HEAD_DIM, axis=1)
        vp_sw = pltpu.roll(vp, HEAD_DIM, axis=1)
        for kv in range(A_KV_HEADS):
            kva_ref[4 * kv + 0] = jnp.where(is_lo, kp if kv == 0 else kp_sw, 0.0).astype(BF16)
            kva_ref[4 * kv + 1] = jnp.where(is_lo, 0.0, kp_sw if kv == 0 else kp).astype(BF16)
            kva_ref[4 * kv + 2] = jnp.where(is_lo, vp if kv == 0 else vp_sw, 0.0).astype(BF16)
            kva_ref[4 * kv + 3] = jnp.where(is_lo, 0.0, vp_sw if kv == 0 else vp).astype(BF16)

    cos_q = cos_ref[pl.ds(row0, LAT_TQ), :]
    sin_q = sin_ref[pl.ds(row0, LAT_TQ), :]
    qs = [_rope(_pair_rms(haq_ref[:, c * PAIR:(c + 1) * PAIR].astype(F32), qg2, is_lo), cos_q, sin_q).astype(BF16)
          for c in range(A_HEADS // 2)]
    for kv in range(A_KV_HEADS):
        kc = cak_ref[kv].astype(BF16)
        vc = cav_ref[kv].astype(BF16)
        q_st = jnp.concatenate([qs[2 * kv], qs[2 * kv + 1]], axis=0)
        o = (_softmax_pv([_qk(q_st, _pad_lo(kc)), _qk(q_st, kva_ref[4 * kv + 0])], [_pad_lo(vc), kva_ref[4 * kv + 2]])
             + _softmax_pv([_qk(q_st, _pad_hi(kc)), _qk(q_st, kva_ref[4 * kv + 1])], [_pad_hi(vc), kva_ref[4 * kv + 3]]))
        oab_ref[:, (2 * kv) * PAIR:(2 * kv + 1) * PAIR] = o[0:LAT_TQ].astype(BF16)
        oab_ref[:, (2 * kv + 1) * PAIR:(2 * kv + 2) * PAIR] = o[LAT_TQ:2 * LAT_TQ].astype(BF16)

    k0 = _na_union_row0(0) * GRID_W
    for tt in range(1, LAT_NT):
        k0 = jnp.where(t >= tt, _na_union_row0(tt) * GRID_W, k0)
    k0 = pl.multiple_of(k0, GRID_W)
    for j in range(B_HEADS // 2):
        qp = (hbq_ref[:, j * PAIR:(j + 1) * PAIR] * ATT_SCALE).astype(BF16)
        k_lo, k_hi = _split_halves(hbf_ref[pl.ds(k0, NA_UNION), B_W + j * PAIR:B_W + (j + 1) * PAIR], is_lo)
        v_lo, v_hi = _split_halves(hbf_ref[pl.ds(k0, NA_UNION), 2 * B_W + j * PAIR:2 * B_W + (j + 1) * PAIR], is_lo)
        kc_lo = _pad_lo(cnk_ref[2 * j].astype(BF16))
        kc_hi = _pad_hi(cnk_ref[2 * j + 1].astype(BF16))
        vc_lo = _pad_lo(cnv_ref[2 * j].astype(BF16))
        vc_hi = _pad_hi(cnv_ref[2 * j + 1].astype(BF16))
        o = (_softmax_pv([_qk(qp, kc_lo), _qk(qp, k_lo) + bias_ref[2 * j]], [vc_lo, v_lo])
             + _softmax_pv([_qk(qp, kc_hi), _qk(qp, k_hi) + bias_ref[2 * j + 1]], [vc_hi, v_hi]))
        oab_ref[:, A_Q + j * PAIR:A_Q + (j + 1) * PAIR] = o.astype(BF16)


def _attn_lat(h_a, h_b, cak, cav, cnk, cnv, layer, qg2, kg2, cos128, sin128, bias):
    nt = LAT_NT
    cache_spec = lambda nh: pl.BlockSpec((None, None, nh, PAST_LEN, HEAD_DIM), lambda b, t: (b, layer, 0, 0, 0))
    return pl.pallas_call(
        _attn_lat_kernel,
        grid=(DEC_BATCH, nt),
        in_specs=[
            pl.BlockSpec((LAT_TQ, W_A), lambda b, t: (b * nt + t, 0)),
            pl.BlockSpec((DEC_SEQ, W_A), lambda b, t: (b, 0)),
            pl.BlockSpec((LAT_TQ, W_B), lambda b, t: (b * nt + t, 0)),
            pl.BlockSpec((DEC_SEQ, W_B), lambda b, t: (b, 0)),
            cache_spec(A_KV_HEADS), cache_spec(A_KV_HEADS), cache_spec(B_HEADS), cache_spec(B_HEADS),
            pl.BlockSpec((1, PAIR), lambda b, t: (0, 0)),
            pl.BlockSpec((1, PAIR), lambda b, t: (0, 0)),
            pl.BlockSpec((DEC_SEQ, PAIR), lambda b, t: (0, 0)),
            pl.BlockSpec((DEC_SEQ, PAIR), lambda b, t: (0, 0)),
            pl.BlockSpec((B_HEADS, None, LAT_TQ, NA_UNION), lambda b, t: (0, t, 0, 0)),
        ],
        out_specs=pl.BlockSpec((LAT_TQ, A_Q + B_W), lambda b, t: (b * nt + t, 0)),
        out_shape=jax.ShapeDtypeStruct((N_LAT, A_Q + B_W), BF16),
        scratch_shapes=[pltpu.VMEM((4 * A_KV_HEADS, DEC_SEQ, PAIR), BF16)],
        compiler_params=_params(("arbitrary", "arbitrary"), 60),
        name="attn_lat",
    )(h_a, h_a, h_b, h_b, cak, cav, cnk, cnv, qg2, kg2, cos128, sin128, bias)


def _rope_tables():
    t = jnp.arange(DEC_SEQ)
    row = (t // GRID_W).astype(F32)
    col = (t % GRID_W).astype(F32)
    n_freq = HEAD_DIM // 4
    inv = ROPE_THETA ** (-jnp.arange(n_freq, dtype=F32) / n_freq)
    ang = jnp.concatenate([row[:, None] * inv, col[:, None] * inv], axis=-1)
    cos = jnp.repeat(jnp.cos(ang), 2, axis=-1)
    sign = jnp.tile(jnp.array([-1.0, 1.0], F32), HEAD_DIM // 2)
    sin = jnp.repeat(jnp.sin(ang), 2, axis=-1) * sign
    return jnp.tile(cos, (1, 2)), jnp.tile(sin, (1, 2))


def _na_bias(rpb):
    col = np.arange(GRID_W)
    dc = np.clip(col[None, :] - col[:, None] + NA_WIN_C - 1, 0, 2 * NA_WIN_C - 2)
    onehot = (dc[..., None] == np.arange(2 * NA_WIN_C - 1)).astype(np.float32)
    c0 = np.clip(col - NA_WIN_C // 2, 0, GRID_W - NA_WIN_C)
    in_win = (col[None, :] >= c0[:, None]) & (col[None, :] < c0[:, None] + NA_WIN_C)
    toep = jnp.einsum('hdm,ckm->hdck', rpb, onehot, precision=lax.Precision.HIGHEST)
    toep = jnp.where(in_win[None, None], toep, MASK_NEG)
    n_off = 2 * NA_WIN_R - 1
    toep = jnp.concatenate([toep, jnp.full((B_HEADS, 1, GRID_W, GRID_W), MASK_NEG, F32)], axis=1)
    pick = np.full((LAT_NT, LAT_QROWS, NA_UNION_ROWS), n_off, np.int32)
    for t in range(LAT_NT):
        u0 = _na_union_row0(t)
        for i in range(LAT_QROWS):
            r = t * LAT_QROWS + i
            r0 = min(max(r - NA_WIN_R // 2, 0), LAT_ROWS - NA_WIN_R)
            for u in range(NA_UNION_ROWS):
                if r0 <= u0 + u < r0 + NA_WIN_R:
                    pick[t, i, u] = u0 + u - r + NA_WIN_R - 1
    tiles = jnp.take(toep, jnp.asarray(pick), axis=1)
    return tiles.transpose(0, 1, 2, 4, 3, 5).reshape(B_HEADS, LAT_NT, LAT_TQ, NA_UNION)


def _conv_kernel(hc_ref, w_ref, o_ref, *, seq):
    bg = hc_ref[:, 0:C_WIDTH].astype(F32)
    u = hc_ref[:, C_WIDTH:2 * C_WIDTH].astype(F32) * hc_ref[:, 2 * C_WIDTH:3 * C_WIDTH].astype(F32)
    rows = u.shape[0]
    assert seq & (seq - 1) == 0
    pos = lax.broadcasted_iota(jnp.int32, u.shape, 0) & (seq - 1)
    u_prev = jnp.where(pos == 0, 0.0, pltpu.roll(u, 1, axis=0))
    u_next = jnp.where(pos == seq - 1, 0.0, pltpu.roll(u, rows - 1, axis=0))
    y = u_prev * w_ref[0:1, :] + u * w_ref[1:2, :] + u_next * w_ref[2:3, :]
    o_ref[...] = (bg * y).astype(o_ref.dtype)


def _conv(h_c, conv_w, seq):
    n = h_c.shape[0]
    tm = 1024
    return pl.pallas_call(
        functools.partial(_conv_kernel, seq=seq),
        grid=(n // tm,),
        in_specs=[pl.BlockSpec((tm, W_C), lambda i: (i, 0)), pl.BlockSpec((3, C_WIDTH), lambda i: (0, 0))],
        out_specs=pl.BlockSpec((tm, C_WIDTH), lambda i: (i, 0)),
        out_shape=jax.ShapeDtypeStruct((n, C_WIDTH), BF16),
        compiler_params=_params(("arbitrary",), 40),
        name="short_conv",
    )(h_c, conv_w)


def _mix_kernel(x_ref, oab_ref, oc_ref, hg_ref, wa_ref, wb_ref, wc_ref, wo_ref, g1_ref, n2g_ref, sc2_ref, sh2_ref,
                wrt_ref, xo_ref, xn2_ref, pt_ref):
    ga = jax.nn.sigmoid(hg_ref[:, 0:D_MODEL].astype(F32))
    gb = jax.nn.sigmoid(hg_ref[:, D_MODEL:2 * D_MODEL].astype(F32))
    gc = jax.nn.sigmoid(hg_ref[:, 2 * D_MODEL:3 * D_MODEL].astype(F32))
    pa = jnp.dot(oab_ref[:, 0:A_Q], wa_ref[...], preferred_element_type=F32)
    pb = jnp.dot(oab_ref[:, A_Q:A_Q + B_W], wb_ref[...], preferred_element_type=F32)
    pc = jnp.dot(oc_ref[...], wc_ref[...], preferred_element_type=F32)
    merged = ga * pa + gb * pb + gc * pc
    xo = x_ref[...] + g1_ref[...] * jnp.dot(merged.astype(BF16), wo_ref[...], preferred_element_type=F32)
    xo_ref[...] = xo
    y = xo * lax.rsqrt(jnp.mean(xo * xo, axis=-1, keepdims=True) + EPS) * n2g_ref[...]
    xn2 = y * (1.0 + sc2_ref[...]) + sh2_ref[...]
    rows = xn2.shape[0]
    for j in range(ROW_TILE):
        xn2_ref[pl.ds(j, rows, stride=ROW_TILE), :] = xn2[:, j * LANES:(j + 1) * LANES]
    logits_t = lax.dot_general(wrt_ref[...], xn2.astype(BF16), _NT, preferred_element_type=F32)
    z = jnp.exp(logits_t - logits_t.max(axis=0, keepdims=True))
    pt_ref[...] = z / z.sum(axis=0, keepdims=True)


def _mix(x, oab, oc, h_g, wa, wb, wc, wo, g1, n2g, sc2, sh2, wrt, seq):
    n = x.shape[0]
    tm = 512
    per_batch = g1.shape[0] > 1
    mod_map = (lambda i: ((i * tm) // seq, 0, 0)) if per_batch else (lambda i: (0, 0, 0))
    mod_spec = pl.BlockSpec((None, 1, D_MODEL), mod_map)
    full = lambda a: pl.BlockSpec(a.shape, lambda i: (0,) * a.ndim)
    row = lambda w: pl.BlockSpec((tm, w), lambda i: (i, 0))
    return pl.pallas_call(
        _mix_kernel,
        grid=(n // tm,),
        in_specs=[row(D_MODEL), row(A_Q + B_W), row(C_WIDTH), row(GATE_W), full(wa), full(wb), full(wc), full(wo),
                  mod_spec, full(n2g), mod_spec, mod_spec, full(wrt)],
        out_specs=[row(D_MODEL), pl.BlockSpec((tm * ROW_TILE, LANES), lambda i: (i, 0)),
                   pl.BlockSpec((N_EXPERTS, tm), lambda i: (0, i))],
        out_shape=[jax.ShapeDtypeStruct((n, D_MODEL), F32), jax.ShapeDtypeStruct((n * ROW_TILE, LANES), F32),
                   jax.ShapeDtypeStruct((N_EXPERTS, n), F32)],
        compiler_params=_params(("arbitrary",), 56),
        name="mix",
    )(x, oab, oc, h_g, wa, wb, wc, wo, g1, n2g, sc2, sh2, wrt)


MOE_TF = 512
MOE_ROW_CHUNKS = ((0, 512), (512, 512), (1024, 256))
MOE_ISSUE_UNROLL = 8


def _moe_issue_rows(idx_ref, idx0, src_hbm, dst_ref, dst_row0, count, sem):
    def body(i, carry):
        for u in range(MOE_ISSUE_UNROLL):
            r = i * MOE_ISSUE_UNROLL + u
            tok = idx_ref[idx0 + r]
            src = src_hbm.at[pl.ds(pl.multiple_of(tok * ROW_TILE, ROW_TILE), ROW_TILE)]
            dst = dst_ref.at[pl.ds(pl.multiple_of((dst_row0 + r) * ROW_TILE, ROW_TILE), ROW_TILE)]
            pltpu.make_async_copy(src, dst, sem).start()
        return carry

    lax.fori_loop(0, count // MOE_ISSUE_UNROLL, body, 0)


def _moe_kernel(idx_ref, xc_hbm, xl_hbm, aff_ref, wg_ref, wu_ref, wd_ref, ye_ref, xe_ref, sem):
    e = pl.program_id(0)
    f = pl.program_id(1)
    slot = e % 2

    def gather(expert, s):
        base = expert * CAP_ALL
        _moe_issue_rows(idx_ref, base, xc_hbm, xe_ref.at[s], 0, CAP_CTX, sem.at[s])
        _moe_issue_rows(idx_ref, base + CAP_CTX, xl_hbm, xe_ref.at[s], CAP_CTX, CAP_LAT, sem.at[s])

    @pl.when(f == 0)
    def _():
        @pl.when(e == 0)
        def _():
            gather(0, 0)

        pltpu.make_async_copy(xc_hbm.at[pl.ds(0, CAP_ALL * ROW_TILE)], xe_ref.at[slot], sem.at[slot]).wait()

        @pl.when(e + 1 < N_EXPERTS)
        def _():
            gather(e + 1, 1 - slot)

    wg = wg_ref[...].astype(BF16)
    wu = wu_ref[...].astype(BF16)
    wd = wd_ref[...].astype(BF16)
    for r0, rn in MOE_ROW_CHUNKS:
        xe = jnp.concatenate([xe_ref[slot, pl.ds(r0 * ROW_TILE + j, rn, stride=ROW_TILE), :]
                              for j in range(ROW_TILE)], axis=1).astype(BF16)
        hg = jnp.dot(xe, wg, preferred_element_type=F32)
        hu = jnp.dot(xe, wu, preferred_element_type=F32)
        act = (hg * jax.nn.sigmoid(hg) * hu).astype(BF16)
        part = jnp.dot(act, wd, preferred_element_type=F32) * aff_ref[r0:r0 + rn, :]

        @pl.when(f == 0)
        def _():
            ye_ref[r0:r0 + rn, :] = part

        @pl.when(f != 0)
        def _():
            ye_ref[r0:r0 + rn, :] += part


def _moe(idx_all, xn2_ctx, xn2_lat, aff_all, w_gate_e, w_up_e, w_down_e, layer):
    nf = EXPERT_FF // MOE_TF
    grid_spec = pltpu.PrefetchScalarGridSpec(
        num_scalar_prefetch=1,
        grid=(N_EXPERTS, nf),
        in_specs=[
            pl.BlockSpec(memory_space=pl.ANY),
            pl.BlockSpec(memory_space=pl.ANY),
            pl.BlockSpec((None, CAP_ALL, 1), lambda e, f, idx: (e, 0, 0)),
            pl.BlockSpec((None, None, D_MODEL, MOE_TF), lambda e, f, idx: (layer, e, 0, f)),
            pl.BlockSpec((None, None, D_MODEL, MOE_TF), lambda e, f, idx: (layer, e, 0, f)),
            pl.BlockSpec((None, None, MOE_TF, D_MODEL), lambda e, f, idx: (layer, e, f, 0)),
        ],
        out_specs=pl.BlockSpec((None, CAP_ALL, D_MODEL), lambda e, f, idx: (e, 0, 0)),
        scratch_shapes=[pltpu.VMEM((2, CAP_ALL * ROW_TILE, LANES), F32), pltpu.SemaphoreType.DMA((2,))],
    )
    return pl.pallas_call(
        _moe_kernel,
        grid_spec=grid_spec,
        out_shape=jax.ShapeDtypeStruct((N_EXPERTS, CAP_ALL, D_MODEL), F32),
        compiler_params=_params(("arbitrary", "arbitrary"), 56),
        name="moe_experts",
    )(idx_all, xn2_ctx, xn2_lat, aff_all, w_gate_e, w_up_e, w_down_e)


POS_SPLIT = 32


def _prefix_count(mask_bf, tri_lane, tri_blk_strict):
    within = jnp.dot(mask_bf, tri_lane, preferred_element_type=F32)
    bs = jnp.broadcast_to(within[:, LANES - 1:LANES], within.shape).astype(BF16)
    return within + jnp.dot(tri_blk_strict, bs, preferred_element_type=F32)


def _route_kernel(p_ref, idx_ref, aff_ref, *, cap):
    p_all = p_ref[...]
    nb = p_all.shape[1]
    capf = float(cap)

    def as_float(word):
        return lax.bitcast_convert_type(word, F32)

    def bisect(_, c):
        lo, hi = c
        mid = lo + ((hi - lo) >> 1)
        ok = jnp.sum(jnp.where(p_all >= as_float(mid), 1.0, 0.0), axis=(1, 2), keepdims=True) >= capf
        return jnp.where(ok, mid, lo), jnp.where(ok, hi, mid)

    lo0 = jnp.zeros((N_EXPERTS, 1, 1), jnp.int32)
    hi0 = jnp.full((N_EXPERTS, 1, 1), 0x7F800000, jnp.int32)
    thr_all, _ = lax.fori_loop(0, 31, bisect, (lo0, hi0))
    thr_val = as_float(thr_all)

    li = lax.broadcasted_iota(jnp.int32, (LANES, LANES), 0)
    lj = lax.broadcasted_iota(jnp.int32, (LANES, LANES), 1)
    tri_lane = jnp.where(li <= lj, 1.0, 0.0).astype(BF16)
    bi = lax.broadcasted_iota(jnp.int32, (nb, nb), 0)
    bj = lax.broadcasted_iota(jnp.int32, (nb, nb), 1)
    tri_blk_strict = jnp.where(bj < bi, 1.0, 0.0).astype(BF16)
    tri_blk_incl = jnp.where(bi <= bj, 1.0, 0.0).astype(BF16)
    s_col = lax.broadcasted_iota(jnp.int32, (cap, 1), 0).astype(F32)
    blk_id = lax.broadcasted_iota(jnp.int32, (1, nb), 1).astype(F32)
    lane_id = lax.broadcasted_iota(jnp.int32, (1, LANES), 1).astype(F32)
    ones_rows = jnp.ones((8, LANES), BF16)

    for e in range(N_EXPERTS):
        p = p_all[e]
        thr = thr_val[e]
        gt = p > thr
        eq = p == thr
        need = capf - jnp.sum(jnp.where(gt, 1.0, 0.0), keepdims=True)
        tie_rank = _prefix_count(jnp.where(eq, 1.0, 0.0).astype(BF16), tri_lane, tri_blk_strict)
        sel = jnp.where(gt | (eq & (tie_rank <= need)), 1.0, 0.0).astype(BF16)
        pos = _prefix_count(sel, tri_lane, tri_blk_strict)

        bs_row = lax.dot_general(ones_rows, sel, _NT, preferred_element_type=F32)
        bp_row = jnp.dot(bs_row.astype(BF16), tri_blk_incl, preferred_element_type=F32)[0:1, :]
        blk = jnp.sum(jnp.where(bp_row <= s_col, 1.0, 0.0), axis=-1, keepdims=True)
        onehot = jnp.where(blk == blk_id, 1.0, 0.0).astype(BF16)
        pos_hi = jnp.floor(pos * (1.0 / POS_SPLIT))
        pos_lo = pos - pos_hi * POS_SPLIT
        pos_row = (jnp.dot(onehot, pos_hi.astype(BF16), preferred_element_type=F32) * POS_SPLIT
                   + jnp.dot(onehot, pos_lo.astype(BF16), preferred_element_type=F32))
        lane = jnp.sum(jnp.where(pos_row <= s_col, 1.0, 0.0), axis=-1, keepdims=True)
        idx_ref[e] = (blk * LANES + lane).astype(jnp.int32)

        p1 = p.astype(BF16)
        r1 = p - p1.astype(F32)
        p2 = r1.astype(BF16)
        p3 = (r1 - p2.astype(F32)).astype(BF16)
        p_row = (jnp.dot(onehot, p1, preferred_element_type=F32) + jnp.dot(onehot, p2, preferred_element_type=F32)
                 + jnp.dot(onehot, p3, preferred_element_type=F32))
        aff_ref[e] = jnp.sum(jnp.where(lane_id == lane, p_row, 0.0), axis=-1, keepdims=True)


def _route(probs_t, cap):
    nb = probs_t.shape[1] // LANES
    slot = pl.BlockSpec((N_EXPERTS, cap, 1), lambda i: (0, 0, 0))
    return pl.pallas_call(
        functools.partial(_route_kernel, cap=cap),
        grid=(1,),
        in_specs=[pl.BlockSpec((N_EXPERTS, nb, LANES), lambda i: (0, 0, 0))],
        out_specs=[slot, slot],
        out_shape=[jax.ShapeDtypeStruct((N_EXPERTS, cap, 1), jnp.int32),
                   jax.ShapeDtypeStruct((N_EXPERTS, cap, 1), F32)],
        compiler_params=_params(("arbitrary",), 56),
        name="route",
    )(probs_t.reshape(N_EXPERTS, nb, LANES))


CMB_TT = 1024
CMB_CH = 256
CMB_GROUP = 8
CMB_EPI_ROWS = 256


def _combine_kernel(idx_ref, st_ref, ye_hbm, x_ref, g2_ref, ng_ref, sc_ref, sh_ref, *rest, off, cap, nt, final):
    if final:
        y_ref, acc_ref, buf_ref, sem = rest
    else:
        xo_ref, xn_ref, acc_ref, buf_ref, sem = rest
    k = pl.program_id(0)
    t0 = k * CMB_TT
    acc_ref[...] = jnp.zeros_like(acc_ref)

    def slot_range(e):
        s0 = st_ref[e * (nt + 1) + k]
        s1 = st_ref[e * (nt + 1) + k + 1]
        return s0, s1, jnp.minimum(s0 & -CMB_GROUP, cap - CMB_CH)

    def chunk_copy(e, start):
        src = ye_hbm.at[e, pl.ds(pl.multiple_of(off + start, CMB_GROUP), CMB_CH)]
        return pltpu.make_async_copy(src, buf_ref.at[e], sem.at[e])

    def accumulate(e, start, lo, hi):
        def body(i, carry):
            r0 = pl.multiple_of(i * CMB_GROUP, CMB_GROUP)
            tile = buf_ref[e, pl.ds(r0, CMB_GROUP), :]
            dst = []
            for u in range(CMB_GROUP):
                slot = start + r0 + u
                tok = idx_ref[e * CAP_ALL + off + slot]
                dst.append(jnp.where((slot >= lo) & (slot < hi), tok - t0, CMB_TT))
            rows = [acc_ref[pl.ds(d, 1), :] for d in dst]
            for u in range(CMB_GROUP):
                acc_ref[pl.ds(dst[u], 1), :] = rows[u] + tile[u:u + 1, :]
            return carry

        lax.fori_loop((lo - start) // CMB_GROUP, (hi - start + CMB_GROUP - 1) // CMB_GROUP, body, 0)

    for e in range(N_EXPERTS):
        chunk_copy(e, slot_range(e)[2]).start()
    for e in range(N_EXPERTS):
        s0, s1, a = slot_range(e)
        chunk_copy(e, a).wait()
        accumulate(e, a, s0, jnp.minimum(s1, a + CMB_CH))

        def extra(j, carry, e=e, s1=s1, a=a):
            lo = a + CMB_CH * (j + 1)
            start = jnp.minimum(lo, cap - CMB_CH)
            cp = chunk_copy(e, start)
            cp.start()
            cp.wait()
            accumulate(e, start, lo, jnp.minimum(s1, lo + CMB_CH))
            return carry

        lax.fori_loop(0, jnp.maximum(s1 - a - 1, 0) // CMB_CH, extra, 0)

    for r0 in range(0, CMB_TT, CMB_EPI_ROWS):
        rs = slice(r0, r0 + CMB_EPI_ROWS)
        xo = x_ref[rs, :] + g2_ref[...] * acc_ref[rs, :]
        y = xo * lax.rsqrt(jnp.mean(xo * xo, axis=-1, keepdims=True) + EPS) * ng_ref[...]
        if final:
            y_ref[rs, :] = y
        else:
            xo_ref[rs, :] = xo
            xn_ref[rs, :] = (y * (1.0 + sc_ref[...]) + sh_ref[...]).astype(xn_ref.dtype)


def _combine(idx_all, starts, ye, x, g2, ng, sc, sh, seq, off, cap, final):
    n = x.shape[0]
    nt = n // CMB_TT
    per_batch = g2.shape[0] > 1
    mod_map = ((lambda i, *_: ((i * CMB_TT) // seq, 0, 0)) if per_batch else (lambda i, *_: (0, 0, 0)))
    mod_spec = pl.BlockSpec((None, 1, D_MODEL), mod_map)
    row = pl.BlockSpec((CMB_TT, D_MODEL), lambda i, *_: (i, 0))
    vec = pl.BlockSpec((1, D_MODEL), lambda i, *_: (0, 0))
    if final:
        out_specs, out_shape = row, jax.ShapeDtypeStruct((n, D_MODEL), F32)
    else:
        out_specs = [row, row]
        out_shape = [jax.ShapeDtypeStruct((n, D_MODEL), F32), jax.ShapeDtypeStruct((n, D_MODEL), BF16)]
    grid_spec = pltpu.PrefetchScalarGridSpec(
        num_scalar_prefetch=2,
        grid=(nt,),
        in_specs=[pl.BlockSpec(memory_space=pl.ANY), row, mod_spec, vec, mod_spec, mod_spec],
        out_specs=out_specs,
        scratch_shapes=[pltpu.VMEM((CMB_TT + CMB_GROUP, D_MODEL), F32),
                        pltpu.VMEM((N_EXPERTS, CMB_CH, D_MODEL), F32),
                        pltpu.SemaphoreType.DMA((N_EXPERTS,))],
    )
    return pl.pallas_call(
        functools.partial(_combine_kernel, off=off, cap=cap, nt=nt, final=final),
        grid_spec=grid_spec,
        out_shape=out_shape,
        compiler_params=_params(("arbitrary",), 60),
        name="combine",
    )(idx_all, starts.reshape(-1), ye, x, g2, ng, sc, sh)


def _tile_starts(idx_sorted, n):
    bounds = jnp.arange(n // CMB_TT + 1, dtype=jnp.int32) * CMB_TT
    return jnp.sum(idx_sorted[:, :, None] < bounds[None, None, :], axis=1, dtype=jnp.int32)


def kernel(x_prompt, x_sample, cache_attn_k, cache_attn_v, cache_na_k, cache_na_v, c, c_ctx, w_ada, b_ada, norm1_g,
           w_in, q_norm_g, k_norm_g, na_rpb, conv_w, w_proj_a, w_proj_b, w_proj_c, w_out, norm2_g, w_router,
           w_gate_e, w_up_e, w_down_e, final_g):
    cond8 = jnp.zeros((8, D_MODEL), F32).at[0].set(c_ctx).at[1:1 + DEC_BATCH].set(c)
    mods = _adaln(cond8, w_ada, b_ada).reshape(DEPTH, 8, N_MOD, D_MODEL)
    cos128, sin128 = _rope_tables()

    hp = x_prompt.reshape(N_CTX, D_MODEL)
    hs = x_sample.reshape(N_LAT, D_MODEL)
    new_ak, new_av, new_nk, new_nv = [], [], [], []
    for l in range(DEPTH):
        mod_ctx = [mods[l, 0:1, i][:, None, :] for i in range(N_MOD)]
        mod_lat = [mods[l, 1:1 + DEC_BATCH, i][:, None, :] for i in range(N_MOD)]
        n1g = norm1_g[l][None]
        n2g = norm2_g[l][None]
        qg2 = jnp.tile(q_norm_g[l][None], (1, 2))
        kg2 = jnp.tile(k_norm_g[l][None], (1, 2))
        wa, wb, wc, wo = (w_proj_a[l].astype(BF16), w_proj_b[l].astype(BF16), w_proj_c[l].astype(BF16),
                          w_out[l].astype(BF16))
        wrt = w_router[l].T.astype(BF16)
        bias = _na_bias(na_rpb[l])

        def sublayer1(x, xn, mod, seq, is_ctx):
            _, _, g1, sh2, sc2, _ = mod
            kv_dtype = F32 if is_ctx else BF16
            h_a = _inproj(xn, w_in, l, COL_A, W_A, 768, kv_dtype)
            h_b = _inproj(xn, w_in, l, COL_B, W_B, 768, kv_dtype)
            h_c = _inproj(xn, w_in, l, COL_C, W_C, 768, BF16)
            h_g = _inproj(xn, w_in, l, COL_G, GATE_W, 768, BF16)
            kv = None
            if is_ctx:
                oab, ka, va, kb, vb = _attn_ctx(h_a, h_b, qg2, kg2)
                kv = (ka, va, kb, vb)
            else:
                oab = _attn_lat(h_a, h_b, cache_attn_k, cache_attn_v, cache_na_k, cache_na_v, l, qg2, kg2,
                                cos128, sin128, bias)
            oc = _conv(h_c, conv_w[l], seq)
            xo, xn2, probs_t = _mix(x, oab, oc, h_g, wa, wb, wc, wo, g1, n2g, sc2, sh2, wrt, seq)
            return xo, xn2, probs_t, kv

        if l == 0:
            xn_p = _norm_mod(hp, n1g, mod_ctx[1], mod_ctx[0], SEQ, BF16)
            xn_s = _norm_mod(hs, n1g, mod_lat[1], mod_lat[0], DEC_SEQ, BF16)
        hp, xn2_c, pt_c, (ka, va, kb, vb) = sublayer1(hp, xn_p, mod_ctx, SEQ, True)
        hs, xn2_l, pt_l, _ = sublayer1(hs, xn_s, mod_lat, DEC_SEQ, False)
        new_ak.append(ka)
        new_av.append(va)
        new_nk.append(kb)
        new_nv.append(vb)

        idx_c, aff_c = _route(pt_c, CAP_CTX)
        idx_l, aff_l = _route(pt_l, CAP_LAT)
        idx_all = jnp.concatenate([idx_c, idx_l], axis=1).reshape(-1)
        aff_all = jnp.concatenate([aff_c, aff_l], axis=1)
        ye = _moe(idx_all, xn2_c, xn2_l, aff_all, w_gate_e, w_up_e, w_down_e, l)

        final = l == DEPTH - 1
        if final:
            ng, nsc_c, nsh_c, nsc_l, nsh_l = final_g[None], mod_ctx[1], mod_ctx[0], mod_lat[1], mod_lat[0]
        else:
            nxt_ctx = [mods[l + 1, 0:1, i][:, None, :] for i in range(2)]
            nxt_lat = [mods[l + 1, 1:1 + DEC_BATCH, i][:, None, :] for i in range(2)]
            ng, nsc_c, nsh_c, nsc_l, nsh_l = norm1_g[l + 1][None], nxt_ctx[1], nxt_ctx[0], nxt_lat[1], nxt_lat[0]
        out_c = _combine(idx_all, _tile_starts(idx_c[..., 0], N_CTX), ye, hp, mod_ctx[5], ng, nsc_c, nsh_c,
                         SEQ, 0, CAP_CTX, final)
        out_l = _combine(idx_all, _tile_starts(idx_l[..., 0], N_LAT), ye, hs, mod_lat[5], ng, nsc_l, nsh_l,
                         DEC_SEQ, CAP_CTX, CAP_LAT, final)
        if final:
            y_prompt = out_c.reshape(BATCH, SEQ, D_MODEL)
            y_sample = out_l.reshape(DEC_BATCH, DEC_SEQ, D_MODEL)
        else:
            (hp, xn_p), (hs, xn_s) = out_c, out_l

    return (y_prompt, y_sample, jnp.stack(new_ak, axis=1), jnp.stack(new_av, axis=1),
            jnp.stack(new_nk, axis=1), jnp.stack(new_nv, axis=1))
```

```python
import functools

import jax
import jax.numpy as jnp
import numpy as np
from jax import lax
from jax.experimental import pallas as pl
from jax.experimental.pallas import tpu as pltpu

D_MODEL = 1024
BATCH = 32
SEQ = 256
DEPTH = 2
DEC_BATCH = 2
DEC_SEQ = 1024
PAST_LEN = 256
GRID_W = 64
HEAD_DIM = 64
A_HEADS = 8
A_KV_HEADS = 2
A_GROUP = A_HEADS // A_KV_HEADS
B_HEADS = 8
C_WIDTH = 512
NA_WIN_R = 8
NA_WIN_C = 16
N_EXPERTS = 16
EXPERT_FF = 1024
CAP_FACTOR = 2
ROPE_THETA = 10000.0
EPS = 1e-6
N_MOD = 6

A_Q = A_HEADS * HEAD_DIM
A_KV = A_KV_HEADS * HEAD_DIM
B_W = B_HEADS * HEAD_DIM
GATE_W = 3 * D_MODEL
COL_A = 0
W_A = A_Q + 2 * A_KV
COL_B = COL_A + W_A
W_B = 3 * B_W
COL_C = COL_B + W_B
W_C = 3 * C_WIDTH
COL_G = COL_C + W_C

N_CTX = BATCH * SEQ
N_LAT = DEC_BATCH * DEC_SEQ
CAP_CTX = CAP_FACTOR * N_CTX // N_EXPERTS
CAP_LAT = CAP_FACTOR * N_LAT // N_EXPERTS
CAP_ALL = CAP_CTX + CAP_LAT
LAT_ROWS = DEC_SEQ // GRID_W
MASK_NEG = -1e30
ATT_SCALE = HEAD_DIM ** -0.5

LANES = 128
ROW_TILE = 8
assert D_MODEL == ROW_TILE * LANES

F32 = jnp.float32
BF16 = jnp.bfloat16
_NT = (((1,), (1,)), ((), ()))

_MIB = 1024 * 1024


def _params(sem, vmem_mib):
    return pltpu.CompilerParams(dimension_semantics=sem, vmem_limit_bytes=vmem_mib * _MIB)


def _adaln_kernel(cond_ref, w_ref, b_ref, o_ref):
    c = cond_ref[...]
    s = c * jax.nn.sigmoid(c)
    o_ref[...] = jnp.dot(s.astype(BF16), w_ref[...].astype(BF16), preferred_element_type=F32) + b_ref[...]


def _adaln(cond8, w_ada, b_ada):
    tn = 1536
    ncol = N_MOD * D_MODEL
    return pl.pallas_call(
        _adaln_kernel,
        grid=(DEPTH, ncol // tn),
        in_specs=[
            pl.BlockSpec((8, D_MODEL), lambda l, j: (0, 0)),
            pl.BlockSpec((None, D_MODEL, tn), lambda l, j: (l, 0, j)),
            pl.BlockSpec((None, 1, tn), lambda l, j: (l, 0, j)),
        ],
        out_specs=pl.BlockSpec((None, 8, tn), lambda l, j: (l, 0, j)),
        out_shape=jax.ShapeDtypeStruct((DEPTH, 8, ncol), F32),
        compiler_params=_params(("arbitrary", "arbitrary"), 40),
        name="adaln",
    )(cond8, w_ada, b_ada.reshape(DEPTH, 1, ncol))


def _norm_mod_kernel(x_ref, g_ref, sc_ref, sh_ref, o_ref):
    x = x_ref[...]
    y = x * lax.rsqrt(jnp.mean(x * x, axis=-1, keepdims=True) + EPS) * g_ref[...]
    o_ref[...] = (y * (1.0 + sc_ref[...]) + sh_ref[...]).astype(o_ref.dtype)


def _norm_mod(x, g, sc, sh, seq, out_dtype):
    n = x.shape[0]
    tm = 512
    per_batch = sc.shape[0] > 1
    mod_map = (lambda i: ((i * tm) // seq, 0, 0)) if per_batch else (lambda i: (0, 0, 0))
    return pl.pallas_call(
        _norm_mod_kernel,
        grid=(n // tm,),
        in_specs=[
            pl.BlockSpec((tm, D_MODEL), lambda i: (i, 0)),
            pl.BlockSpec((1, D_MODEL), lambda i: (0, 0)),
            pl.BlockSpec((None, 1, D_MODEL), mod_map),
            pl.BlockSpec((None, 1, D_MODEL), mod_map),
        ],
        out_specs=pl.BlockSpec((tm, D_MODEL), lambda i: (i, 0)),
        out_shape=jax.ShapeDtypeStruct((n, D_MODEL), out_dtype),
        compiler_params=_params(("arbitrary",), 32),
        name="norm_mod",
    )(x, g, sc, sh)


def _inproj_kernel(a_ref, w_ref, o_ref, wbf_ref):
    @pl.when(pl.program_id(1) == 0)
    def _():
        wbf_ref[...] = w_ref[...].astype(BF16)

    o_ref[...] = jnp.dot(a_ref[...], wbf_ref[...], preferred_element_type=F32).astype(o_ref.dtype)


def _inproj(xn, w_in, layer, col0, width, tn, out_dtype):
    n = xn.shape[0]
    tm = 2048
    joff = col0 // tn
    assert col0 % tn == 0 and width % tn == 0
    return pl.pallas_call(
        _inproj_kernel,
        grid=(width // tn, n // tm),
        in_specs=[
            pl.BlockSpec((tm, D_MODEL), lambda j, i: (i, 0)),
            pl.BlockSpec((None, D_MODEL, tn), lambda j, i: (layer, 0, j + joff)),
        ],
        out_specs=pl.BlockSpec((tm, tn), lambda j, i: (i, j)),
        out_shape=jax.ShapeDtypeStruct((n, width), out_dtype),
        scratch_shapes=[pltpu.VMEM((D_MODEL, tn), BF16)],
        compiler_params=_params(("arbitrary", "arbitrary"), 48),
        name="inproj",
    )(xn, w_in)


PAIR = 2 * HEAD_DIM


def _is_lo_half():
    return lax.broadcasted_iota(jnp.int32, (1, PAIR), 1) < HEAD_DIM


def _pair_rms(x, g2, is_lo):
    x2 = x * x
    s_lo = jnp.sum(jnp.where(is_lo, x2, 0.0), axis=-1, keepdims=True)
    s_hi = jnp.sum(jnp.where(is_lo, 0.0, x2), axis=-1, keepdims=True)
    ms = jnp.where(is_lo, s_lo, s_hi) * (1.0 / HEAD_DIM)
    return x * lax.rsqrt(ms + EPS) * g2


def _split_halves(x, is_lo):
    return jnp.where(is_lo, x, 0.0).astype(BF16), jnp.where(is_lo, 0.0, x).astype(BF16)


def _pad_lo(x):
    return jnp.concatenate([x, jnp.zeros_like(x)], axis=1)


def _pad_hi(x):
    return jnp.concatenate([jnp.zeros_like(x), x], axis=1)


def _qk(q, k):
    return lax.dot_general(q, k, _NT, preferred_element_type=F32)


def _softmax_pv(scores, values):
    m = scores[0].max(axis=-1, keepdims=True)
    for s in scores[1:]:
        m = jnp.maximum(m, s.max(axis=-1, keepdims=True))
    acc = None
    den = None
    for s, v in zip(scores, values):
        p = jnp.exp(s - m)
        d = p.sum(axis=-1, keepdims=True)
        o = jnp.dot(p.astype(BF16), v, preferred_element_type=F32)
        acc = o if acc is None else acc + o
        den = d if den is None else den + d
    return acc * (1.0 / den)


CTX_NB = 2


def _attn_ctx_kernel(ha_ref, hb_ref, qg_ref, kg_ref, *rest):
    oab_ref, *kv_refs = rest[-5:]
    prev_refs = rest[:-5]
    if prev_refs:
        for li in range(len(prev_refs) // 4):
            for dst, src in zip(kv_refs, prev_refs[4 * li:4 * li + 4]):
                dst[:, li] = src[...]
        ak_ref, av_ref, nk_ref, nv_ref = [r.at[:, DEPTH - 1] for r in kv_refs]
    else:
        ak_ref, av_ref, nk_ref, nv_ref = kv_refs
    is_lo = _is_lo_half()
    qg2 = qg_ref[...] * ATT_SCALE
    kg2 = kg_ref[...]
    for b in range(CTX_NB):
        rows = slice(b * SEQ, (b + 1) * SEQ)
        kp = _pair_rms(ha_ref[rows, A_Q:A_Q + PAIR], kg2, is_lo)
        vp = ha_ref[rows, A_Q + A_KV:A_Q + A_KV + PAIR]
        for kv in range(A_KV_HEADS):
            ak_ref[b, kv] = kp[:, kv * HEAD_DIM:(kv + 1) * HEAD_DIM]
            av_ref[b, kv] = vp[:, kv * HEAD_DIM:(kv + 1) * HEAD_DIM]
        kp_sw = pltpu.roll(kp, HEAD_DIM, axis=1)
        vp_sw = pltpu.roll(vp, HEAD_DIM, axis=1)
        qs = [_pair_rms(ha_ref[rows, c * PAIR:(c + 1) * PAIR], qg2, is_lo).astype(BF16)
              for c in range(A_HEADS // 2)]
        for kv in range(A_KV_HEADS):
            k_lo = jnp.where(is_lo, kp if kv == 0 else kp_sw, 0.0).astype(BF16)
            k_hi = jnp.where(is_lo, 0.0, kp_sw if kv == 0 else kp).astype(BF16)
            v_lo = jnp.where(is_lo, vp if kv == 0 else vp_sw, 0.0).astype(BF16)
            v_hi = jnp.where(is_lo, 0.0, vp_sw if kv == 0 else vp).astype(BF16)
            q_st = jnp.concatenate([qs[2 * kv], qs[2 * kv + 1]], axis=0)
            o = _softmax_pv([_qk(q_st, k_lo)], [v_lo]) + _softmax_pv([_qk(q_st, k_hi)], [v_hi])
            oab_ref[rows, (2 * kv) * PAIR:(2 * kv + 1) * PAIR] = o[0:SEQ].astype(BF16)
            oab_ref[rows, (2 * kv + 1) * PAIR:(2 * kv + 2) * PAIR] = o[SEQ:2 * SEQ].astype(BF16)
        for j in range(B_HEADS // 2):
            qp = (hb_ref[rows, j * PAIR:(j + 1) * PAIR] * ATT_SCALE).astype(BF16)
            kp_b = hb_ref[rows, B_W + j * PAIR:B_W + (j + 1) * PAIR]
            vp_b = hb_ref[rows, 2 * B_W + j * PAIR:2 * B_W + (j + 1) * PAIR]
            for u in range(2):
                nk_ref[b, 2 * j + u] = kp_b[:, u * HEAD_DIM:(u + 1) * HEAD_DIM]
                nv_ref[b, 2 * j + u] = vp_b[:, u * HEAD_DIM:(u + 1) * HEAD_DIM]
            k_lo, k_hi = _split_halves(kp_b, is_lo)
            v_lo, v_hi = _split_halves(vp_b, is_lo)
            o = _softmax_pv([_qk(qp, k_lo)], [v_lo]) + _softmax_pv([_qk(qp, k_hi)], [v_hi])
            oab_ref[rows, A_Q + j * PAIR:A_Q + (j + 1) * PAIR] = o.astype(BF16)


def _attn_ctx(h_a, h_b, qg2, kg2, prev_kv):
    rows = CTX_NB * SEQ
    layer_spec = lambda nh: pl.BlockSpec((CTX_NB, nh, SEQ, HEAD_DIM), lambda b: (b, 0, 0, 0))
    heads = (A_KV_HEADS, A_KV_HEADS, B_HEADS, B_HEADS)
    if prev_kv is None:
        prev_kv, prev_specs = (), []
        kv_specs = [layer_spec(nh) for nh in heads]
        kv_shapes = [jax.ShapeDtypeStruct((BATCH, nh, SEQ, HEAD_DIM), F32) for nh in heads]
    else:
        prev_specs = [layer_spec(heads[i % 4]) for i in range(len(prev_kv))]
        kv_specs = [pl.BlockSpec((CTX_NB, DEPTH, nh, SEQ, HEAD_DIM), lambda b: (b, 0, 0, 0, 0)) for nh in heads]
        kv_shapes = [jax.ShapeDtypeStruct((BATCH, DEPTH, nh, SEQ, HEAD_DIM), F32) for nh in heads]
    return pl.pallas_call(
        _attn_ctx_kernel,
        grid=(BATCH // CTX_NB,),
        in_specs=[
            pl.BlockSpec((rows, W_A), lambda b: (b, 0)),
            pl.BlockSpec((rows, W_B), lambda b: (b, 0)),
            pl.BlockSpec((1, PAIR), lambda b: (0, 0)),
            pl.BlockSpec((1, PAIR), lambda b: (0, 0)),
        ] + prev_specs,
        out_specs=[pl.BlockSpec((rows, A_Q + B_W), lambda b: (b, 0))] + kv_specs,
        out_shape=[jax.ShapeDtypeStruct((N_CTX, A_Q + B_W), BF16)] + kv_shapes,
        compiler_params=_params(("arbitrary",), 56),
        name="attn_ctx",
    )(h_a, h_b, qg2, kg2, *prev_kv)


LAT_TQ = 256
LAT_QROWS = LAT_TQ // GRID_W
LAT_NT = DEC_SEQ // LAT_TQ
NA_UNION_ROWS = 12
NA_UNION = NA_UNION_ROWS * GRID_W


def _na_union_row0(t):
    lo = min(max(r - NA_WIN_R // 2, 0) for r in range(t * LAT_QROWS, (t + 1) * LAT_QROWS))
    return min(min(lo, LAT_ROWS - NA_WIN_R), LAT_ROWS - NA_UNION_ROWS)


def _rope(x, cos, sin):
    lane = lax.broadcasted_iota(jnp.int32, x.shape, 1)
    nxt = pltpu.roll(x, x.shape[1] - 1, axis=1)
    prv = pltpu.roll(x, 1, axis=1)
    partner = jnp.where((lane & 1) == 0, nxt, prv)
    return x * cos + partner * sin


def _attn_lat_kernel(haq_ref, haf_ref, hbq_ref, hbf_ref, cak_ref, cav_ref, cnk_ref, cnv_ref,
                     qg_ref, kg_ref, cos_ref, sin_ref, bias_ref, oab_ref, kva_ref):
    t = pl.program_id(1)
    is_lo = _is_lo_half()
    row0 = pl.multiple_of(t * LAT_TQ, LAT_TQ)
    qg2 = qg_ref[...] * ATT_SCALE
    kg2 = kg_ref[...]

    @pl.when(t == 0)
    def _():
        kp = _rope(_pair_rms(haf_ref[:, A_Q:A_Q + PAIR].astype(F32), kg2, is_lo), cos_ref[...], sin_ref[...])
        vp = haf_ref[:, A_Q + A_KV:A_Q + A_KV + PAIR].astype(F32)
        kp_sw = pltpu.roll(kp, HEAD_DIM, axis=1)
        vp_sw = pltpu.roll(vp, HEAD_DIM, axis=1)
        for kv in range(A_KV_HEADS):
            kva_ref[4 * kv + 0] = jnp.where(is_lo, kp if kv == 0 else kp_sw, 0.0).astype(BF16)
            kva_ref[4 * kv + 1] = jnp.where(is_lo, 0.0, kp_sw if kv == 0 else kp).astype(BF16)
            kva_ref[4 * kv + 2] = jnp.where(is_lo, vp if kv == 0 else vp_sw, 0.0).astype(BF16)
            kva_ref[4 * kv + 3] = jnp.where(is_lo, 0.0, vp_sw if kv == 0 else vp).astype(BF16)

    cos_q = cos_ref[pl.ds(row0, LAT_TQ), :]
    sin_q = sin_ref[pl.ds(row0, LAT_TQ), :]
    qs = [_rope(_pair_rms(haq_ref[:, c * PAIR:(c + 1) * PAIR].astype(F32), qg2, is_lo), cos_q, sin_q).astype(BF16)
          for c in range(A_HEADS // 2)]
    for kv in range(A_KV_HEADS):
        kc = cak_ref[kv].astype(BF16)
        vc = cav_ref[kv].astype(BF16)
        q_st = jnp.concatenate([qs[2 * kv], qs[2 * kv + 1]], axis=0)
        o = (_softmax_pv([_qk(q_st, _pad_lo(kc)), _qk(q_st, kva_ref[4 * kv + 0])], [_pad_lo(vc), kva_ref[4 * kv + 2]])
             + _softmax_pv([_qk(q_st, _pad_hi(kc)), _qk(q_st, kva_ref[4 * kv + 1])], [_pad_hi(vc), kva_ref[4 * kv + 3]]))
        oab_ref[:, (2 * kv) * PAIR:(2 * kv + 1) * PAIR] = o[0:LAT_TQ].astype(BF16)
        oab_ref[:, (2 * kv + 1) * PAIR:(2 * kv + 2) * PAIR] = o[LAT_TQ:2 * LAT_TQ].astype(BF16)

    k0 = _na_union_row0(0) * GRID_W
    for tt in range(1, LAT_NT):
        k0 = jnp.where(t >= tt, _na_union_row0(tt) * GRID_W, k0)
    k0 = pl.multiple_of(k0, GRID_W)
    for j in range(B_HEADS // 2):
        qp = (hbq_ref[:, j * PAIR:(j + 1) * PAIR] * ATT_SCALE).astype(BF16)
        k_lo, k_hi = _split_halves(hbf_ref[pl.ds(k0, NA_UNION), B_W + j * PAIR:B_W + (j + 1) * PAIR], is_lo)
        v_lo, v_hi = _split_halves(hbf_ref[pl.ds(k0, NA_UNION), 2 * B_W + j * PAIR:2 * B_W + (j + 1) * PAIR], is_lo)
        kc_lo = _pad_lo(cnk_ref[2 * j].astype(BF16))
        kc_hi = _pad_hi(cnk_ref[2 * j + 1].astype(BF16))
        vc_lo = _pad_lo(cnv_ref[2 * j].astype(BF16))
        vc_hi = _pad_hi(cnv_ref[2 * j + 1].astype(BF16))
        o = (_softmax_pv([_qk(qp, kc_lo), _qk(qp, k_lo) + bias_ref[2 * j]], [vc_lo, v_lo])
             + _softmax_pv([_qk(qp, kc_hi), _qk(qp, k_hi) + bias_ref[2 * j + 1]], [vc_hi, v_hi]))
        oab_ref[:, A_Q + j * PAIR:A_Q + (j + 1) * PAIR] = o.astype(BF16)


def _attn_lat(h_a, h_b, cak, cav, cnk, cnv, layer, qg2, kg2, cos128, sin128, bias):
    nt = LAT_NT
    cache_spec = lambda nh: pl.BlockSpec((None, None, nh, PAST_LEN, HEAD_DIM), lambda b, t: (b, layer, 0, 0, 0))
    return pl.pallas_call(
        _attn_lat_kernel,
        grid=(DEC_BATCH, nt),
        in_specs=[
            pl.BlockSpec((LAT_TQ, W_A), lambda b, t: (b * nt + t, 0)),
            pl.BlockSpec((DEC_SEQ, W_A), lambda b, t: (b, 0)),
            pl.BlockSpec((LAT_TQ, W_B), lambda b, t: (b * nt + t, 0)),
            pl.BlockSpec((DEC_SEQ, W_B), lambda b, t: (b, 0)),
            cache_spec(A_KV_HEADS), cache_spec(A_KV_HEADS), cache_spec(B_HEADS), cache_spec(B_HEADS),
            pl.BlockSpec((1, PAIR), lambda b, t: (0, 0)),
            pl.BlockSpec((1, PAIR), lambda b, t: (0, 0)),
            pl.BlockSpec((DEC_SEQ, PAIR), lambda b, t: (0, 0)),
            pl.BlockSpec((DEC_SEQ, PAIR), lambda b, t: (0, 0)),
            pl.BlockSpec((B_HEADS, None, LAT_TQ, NA_UNION), lambda b, t: (0, t, 0, 0)),
        ],
        out_specs=pl.BlockSpec((LAT_TQ, A_Q + B_W), lambda b, t: (b * nt + t, 0)),
        out_shape=jax.ShapeDtypeStruct((N_LAT, A_Q + B_W), BF16),
        scratch_shapes=[pltpu.VMEM((4 * A_KV_HEADS, DEC_SEQ, PAIR), BF16)],
        compiler_params=_params(("arbitrary", "arbitrary"), 60),
        name="attn_lat",
    )(h_a, h_a, h_b, h_b, cak, cav, cnk, cnv, qg2, kg2, cos128, sin128, bias)


def _rope_tables():
    t = jnp.arange(DEC_SEQ)
    row = (t // GRID_W).astype(F32)
    col = (t % GRID_W).astype(F32)
    n_freq = HEAD_DIM // 4
    inv = ROPE_THETA ** (-jnp.arange(n_freq, dtype=F32) / n_freq)
    ang = jnp.concatenate([row[:, None] * inv, col[:, None] * inv], axis=-1)
    cos = jnp.repeat(jnp.cos(ang), 2, axis=-1)
    sign = jnp.tile(jnp.array([-1.0, 1.0], F32), HEAD_DIM // 2)
    sin = jnp.repeat(jnp.sin(ang), 2, axis=-1) * sign
    return jnp.tile(cos, (1, 2)), jnp.tile(sin, (1, 2))


def _na_bias(rpb):
    col = np.arange(GRID_W)
    dc = np.clip(col[None, :] - col[:, None] + NA_WIN_C - 1, 0, 2 * NA_WIN_C - 2)
    onehot = (dc[..., None] == np.arange(2 * NA_WIN_C - 1)).astype(np.float32)
    c0 = np.clip(col - NA_WIN_C // 2, 0, GRID_W - NA_WIN_C)
    in_win = (col[None, :] >= c0[:, None]) & (col[None, :] < c0[:, None] + NA_WIN_C)
    toep = jnp.einsum('hdm,ckm->hdck', rpb, onehot, precision=lax.Precision.HIGHEST)
    toep = jnp.where(in_win[None, None], toep, MASK_NEG)
    n_off = 2 * NA_WIN_R - 1
    toep = jnp.concatenate([toep, jnp.full((B_HEADS, 1, GRID_W, GRID_W), MASK_NEG, F32)], axis=1)
    pick = np.full((LAT_NT, LAT_QROWS, NA_UNION_ROWS), n_off, np.int32)
    for t in range(LAT_NT):
        u0 = _na_union_row0(t)
        for i in range(LAT_QROWS):
            r = t * LAT_QROWS + i
            r0 = min(max(r - NA_WIN_R // 2, 0), LAT_ROWS - NA_WIN_R)
            for u in range(NA_UNION_ROWS):
                if r0 <= u0 + u < r0 + NA_WIN_R:
                    pick[t, i, u] = u0 + u - r + NA_WIN_R - 1
    tiles = jnp.take(toep.transpose(0, 2, 1, 3), jnp.asarray(pick), axis=2)
    return tiles.transpose(0, 2, 3, 1, 4, 5).reshape(B_HEADS, LAT_NT, LAT_TQ, NA_UNION)


def _conv_kernel(hc_ref, w_ref, o_ref, *, seq):
    bg = hc_ref[:, 0:C_WIDTH].astype(F32)
    u = hc_ref[:, C_WIDTH:2 * C_WIDTH].astype(F32) * hc_ref[:, 2 * C_WIDTH:3 * C_WIDTH].astype(F32)
    rows = u.shape[0]
    assert seq & (seq - 1) == 0
    pos = lax.broadcasted_iota(jnp.int32, u.shape, 0) & (seq - 1)
    u_prev = jnp.where(pos == 0, 0.0, pltpu.roll(u, 1, axis=0))
    u_next = jnp.where(pos == seq - 1, 0.0, pltpu.roll(u, rows - 1, axis=0))
    y = u_prev * w_ref[0:1, :] + u * w_ref[1:2, :] + u_next * w_ref[2:3, :]
    o_ref[...] = (bg * y).astype(o_ref.dtype)


def _conv(h_c, conv_w, seq):
    n = h_c.shape[0]
    tm = 1024
    return pl.pallas_call(
        functools.partial(_conv_kernel, seq=seq),
        grid=(n // tm,),
        in_specs=[pl.BlockSpec((tm, W_C), lambda i: (i, 0)), pl.BlockSpec((3, C_WIDTH), lambda i: (0, 0))],
        out_specs=pl.BlockSpec((tm, C_WIDTH), lambda i: (i, 0)),
        out_shape=jax.ShapeDtypeStruct((n, C_WIDTH), BF16),
        compiler_params=_params(("arbitrary",), 40),
        name="short_conv",
    )(h_c, conv_w)


def _mix_kernel(x_ref, oab_ref, oc_ref, hg_ref, wa_ref, wb_ref, wc_ref, wo_ref, g1_ref, n2g_ref, sc2_ref, sh2_ref,
                wrt_ref, xo_ref, xn2_ref, pt_ref):
    ga = jax.nn.sigmoid(hg_ref[:, 0:D_MODEL].astype(F32))
    gb = jax.nn.sigmoid(hg_ref[:, D_MODEL:2 * D_MODEL].astype(F32))
    gc = jax.nn.sigmoid(hg_ref[:, 2 * D_MODEL:3 * D_MODEL].astype(F32))
    pa = jnp.dot(oab_ref[:, 0:A_Q], wa_ref[...], preferred_element_type=F32)
    pb = jnp.dot(oab_ref[:, A_Q:A_Q + B_W], wb_ref[...], preferred_element_type=F32)
    pc = jnp.dot(oc_ref[...], wc_ref[...], preferred_element_type=F32)
    merged = ga * pa + gb * pb + gc * pc
    xo = x_ref[...] + g1_ref[...] * jnp.dot(merged.astype(BF16), wo_ref[...], preferred_element_type=F32)
    xo_ref[...] = xo
    y = xo * lax.rsqrt(jnp.mean(xo * xo, axis=-1, keepdims=True) + EPS) * n2g_ref[...]
    xn2 = y * (1.0 + sc2_ref[...]) + sh2_ref[...]
    rows = xn2.shape[0]
    for j in range(ROW_TILE):
        xn2_ref[pl.ds(j, rows, stride=ROW_TILE), :] = xn2[:, j * LANES:(j + 1) * LANES]
    logits_t = lax.dot_general(wrt_ref[...], xn2.astype(BF16), _NT, preferred_element_type=F32)
    z = jnp.exp(logits_t - logits_t.max(axis=0, keepdims=True))
    pt_ref[...] = z / z.sum(axis=0, keepdims=True)


def _mix(x, oab, oc, h_g, wa, wb, wc, wo, g1, n2g, sc2, sh2, wrt, seq):
    n = x.shape[0]
    tm = 512
    per_batch = g1.shape[0] > 1
    mod_map = (lambda i: ((i * tm) // seq, 0, 0)) if per_batch else (lambda i: (0, 0, 0))
    mod_spec = pl.BlockSpec((None, 1, D_MODEL), mod_map)
    full = lambda a: pl.BlockSpec(a.shape, lambda i: (0,) * a.ndim)
    row = lambda w: pl.BlockSpec((tm, w), lambda i: (i, 0))
    return pl.pallas_call(
        _mix_kernel,
        grid=(n // tm,),
        in_specs=[row(D_MODEL), row(A_Q + B_W), row(C_WIDTH), row(GATE_W), full(wa), full(wb), full(wc), full(wo),
                  mod_spec, full(n2g), mod_spec, mod_spec, full(wrt)],
        out_specs=[row(D_MODEL), pl.BlockSpec((tm * ROW_TILE, LANES), lambda i: (i, 0)),
                   pl.BlockSpec((N_EXPERTS, tm), lambda i: (0, i))],
        out_shape=[jax.ShapeDtypeStruct((n, D_MODEL), F32), jax.ShapeDtypeStruct((n * ROW_TILE, LANES), F32),
                   jax.ShapeDtypeStruct((N_EXPERTS, n), F32)],
        compiler_params=_params(("arbitrary",), 56),
        name="mix",
    )(x, oab, oc, h_g, wa, wb, wc, wo, g1, n2g, sc2, sh2, wrt)


MOE_TF = 512
MOE_ROW_CHUNKS = ((0, 512), (512, 512), (1024, 256))
MOE_ISSUE_UNROLL = 8


def _moe_issue_rows(idx_ref, idx0, src_hbm, dst_ref, dst_row0, count, sem):
    def body(i, carry):
        for u in range(MOE_ISSUE_UNROLL):
            r = i * MOE_ISSUE_UNROLL + u
            tok = idx_ref[idx0 + r]
            src = src_hbm.at[pl.ds(pl.multiple_of(tok * ROW_TILE, ROW_TILE), ROW_TILE)]
            dst = dst_ref.at[pl.ds(pl.multiple_of((dst_row0 + r) * ROW_TILE, ROW_TILE), ROW_TILE)]
            pltpu.make_async_copy(src, dst, sem).start()
        return carry

    lax.fori_loop(0, count // MOE_ISSUE_UNROLL, body, 0)


def _moe_kernel(idx_ref, xc_hbm, xl_hbm, aff_ref, wg_ref, wu_ref, wd_ref, ye_ref, xe_ref, sem):
    e = pl.program_id(0)
    f = pl.program_id(1)
    slot = e % 2

    def gather(expert, s):
        base = expert * CAP_ALL
        _moe_issue_rows(idx_ref, base, xc_hbm, xe_ref.at[s], 0, CAP_CTX, sem.at[s])
        _moe_issue_rows(idx_ref, base + CAP_CTX, xl_hbm, xe_ref.at[s], CAP_CTX, CAP_LAT, sem.at[s])

    @pl.when(f == 0)
    def _():
        @pl.when(e == 0)
        def _():
            gather(0, 0)

        pltpu.make_async_copy(xc_hbm.at[pl.ds(0, CAP_ALL * ROW_TILE)], xe_ref.at[slot], sem.at[slot]).wait()

        @pl.when(e + 1 < N_EXPERTS)
        def _():
            gather(e + 1, 1 - slot)

    wg = wg_ref[...].astype(BF16)
    wu = wu_ref[...].astype(BF16)
    wd = wd_ref[...].astype(BF16)
    for r0, rn in MOE_ROW_CHUNKS:
        xe = jnp.concatenate([xe_ref[slot, pl.ds(r0 * ROW_TILE + j, rn, stride=ROW_TILE), :]
                              for j in range(ROW_TILE)], axis=1).astype(BF16)
        hg = jnp.dot(xe, wg, preferred_element_type=F32)
        hu = jnp.dot(xe, wu, preferred_element_type=F32)
        act = (hg * jax.nn.sigmoid(hg) * hu).astype(BF16)
        part = jnp.dot(act, wd, preferred_element_type=F32) * aff_ref[r0:r0 + rn, :]

        @pl.when(f == 0)
        def _():
            ye_ref[r0:r0 + rn, :] = part

        @pl.when(f != 0)
        def _():
            ye_ref[r0:r0 + rn, :] += part


def _moe(idx_all, xn2_ctx, xn2_lat, aff_all, w_gate_e, w_up_e, w_down_e, layer):
    nf = EXPERT_FF // MOE_TF
    grid_spec = pltpu.PrefetchScalarGridSpec(
        num_scalar_prefetch=1,
        grid=(N_EXPERTS, nf),
        in_specs=[
            pl.BlockSpec(memory_space=pl.ANY),
            pl.BlockSpec(memory_space=pl.ANY),
            pl.BlockSpec((None, CAP_ALL, 1), lambda e, f, idx: (e, 0, 0)),
            pl.BlockSpec((None, None, D_MODEL, MOE_TF), lambda e, f, idx: (layer, e, 0, f)),
            pl.BlockSpec((None, None, D_MODEL, MOE_TF), lambda e, f, idx: (layer, e, 0, f)),
            pl.BlockSpec((None, None, MOE_TF, D_MODEL), lambda e, f, idx: (layer, e, f, 0)),
        ],
        out_specs=pl.BlockSpec((None, CAP_ALL, D_MODEL), lambda e, f, idx: (e, 0, 0)),
        scratch_shapes=[pltpu.VMEM((2, CAP_ALL * ROW_TILE, LANES), F32), pltpu.SemaphoreType.DMA((2,))],
    )
    return pl.pallas_call(
        _moe_kernel,
        grid_spec=grid_spec,
        out_shape=jax.ShapeDtypeStruct((N_EXPERTS, CAP_ALL, D_MODEL), F32),
        compiler_params=_params(("arbitrary", "arbitrary"), 56),
        name="moe_experts",
    )(idx_all, xn2_ctx, xn2_lat, aff_all, w_gate_e, w_up_e, w_down_e)


POS_SPLIT = 32


def _prefix_count(mask_bf, tri_lane, tri_blk_strict):
    within = jnp.dot(mask_bf, tri_lane, preferred_element_type=F32)
    bs = jnp.broadcast_to(within[:, LANES - 1:LANES], within.shape).astype(BF16)
    return within + jnp.dot(tri_blk_strict, bs, preferred_element_type=F32)


def _route_kernel(p_ref, idx_ref, aff_ref, *, cap):
    p_all = p_ref[...]
    nb = p_all.shape[1]
    capf = float(cap)

    def as_float(word):
        return lax.bitcast_convert_type(word, F32)

    def bisect(_, c):
        lo, hi = c
        mid = lo + ((hi - lo) >> 1)
        ok = jnp.sum(jnp.where(p_all >= as_float(mid), 1.0, 0.0), axis=(1, 2), keepdims=True) >= capf
        return jnp.where(ok, mid, lo), jnp.where(ok, hi, mid)

    lo0 = jnp.zeros((N_EXPERTS, 1, 1), jnp.int32)
    hi0 = jnp.full((N_EXPERTS, 1, 1), 0x7F800000, jnp.int32)
    thr_all, _ = lax.fori_loop(0, 31, bisect, (lo0, hi0))
    thr_val = as_float(thr_all)

    li = lax.broadcasted_iota(jnp.int32, (LANES, LANES), 0)
    lj = lax.broadcasted_iota(jnp.int32, (LANES, LANES), 1)
    tri_lane = jnp.where(li <= lj, 1.0, 0.0).astype(BF16)
    bi = lax.broadcasted_iota(jnp.int32, (nb, nb), 0)
    bj = lax.broadcasted_iota(jnp.int32, (nb, nb), 1)
    tri_blk_strict = jnp.where(bj < bi, 1.0, 0.0).astype(BF16)
    tri_blk_incl = jnp.where(bi <= bj, 1.0, 0.0).astype(BF16)
    s_col = lax.broadcasted_iota(jnp.int32, (cap, 1), 0).astype(F32)
    blk_id = lax.broadcasted_iota(jnp.int32, (1, nb), 1).astype(F32)
    lane_id = lax.broadcasted_iota(jnp.int32, (1, LANES), 1).astype(F32)
    ones_rows = jnp.ones((8, LANES), BF16)

    for e in range(N_EXPERTS):
        p = p_all[e]
        thr = thr_val[e]
        gt = p > thr
        eq = p == thr
        need = capf - jnp.sum(jnp.where(gt, 1.0, 0.0), keepdims=True)
        tie_rank = _prefix_count(jnp.where(eq, 1.0, 0.0).astype(BF16), tri_lane, tri_blk_strict)
        sel = jnp.where(gt | (eq & (tie_rank <= need)), 1.0, 0.0).astype(BF16)
        pos = _prefix_count(sel, tri_lane, tri_blk_strict)

        bs_row = lax.dot_general(ones_rows, sel, _NT, preferred_element_type=F32)
        bp_row = jnp.dot(bs_row.astype(BF16), tri_blk_incl, preferred_element_type=F32)[0:1, :]
        blk = jnp.sum(jnp.where(bp_row <= s_col, 1.0, 0.0), axis=-1, keepdims=True)
        onehot = jnp.where(blk == blk_id, 1.0, 0.0).astype(BF16)
        pos_hi = jnp.floor(pos * (1.0 / POS_SPLIT))
        pos_lo = pos - pos_hi * POS_SPLIT
        pos_row = (jnp.dot(onehot, pos_hi.astype(BF16), preferred_element_type=F32) * POS_SPLIT
                   + jnp.dot(onehot, pos_lo.astype(BF16), preferred_element_type=F32))
        lane = jnp.sum(jnp.where(pos_row <= s_col, 1.0, 0.0), axis=-1, keepdims=True)
        idx_ref[e] = (blk * LANES + lane).astype(jnp.int32)

        p1 = p.astype(BF16)
        r1 = p - p1.astype(F32)
        p2 = r1.astype(BF16)
        p3 = (r1 - p2.astype(F32)).astype(BF16)
        p_row = (jnp.dot(onehot, p1, preferred_element_type=F32) + jnp.dot(onehot, p2, preferred_element_type=F32)
                 + jnp.dot(onehot, p3, preferred_element_type=F32))
        aff_ref[e] = jnp.sum(jnp.where(lane_id == lane, p_row, 0.0), axis=-1, keepdims=True)


def _route(probs_t, cap):
    nb = probs_t.shape[1] // LANES
    slot = pl.BlockSpec((N_EXPERTS, cap, 1), lambda i: (0, 0, 0))
    return pl.pallas_call(
        functools.partial(_route_kernel, cap=cap),
        grid=(1,),
        in_specs=[pl.BlockSpec((N_EXPERTS, nb, LANES), lambda i: (0, 0, 0))],
        out_specs=[slot, slot],
        out_shape=[jax.ShapeDtypeStruct((N_EXPERTS, cap, 1), jnp.int32),
                   jax.ShapeDtypeStruct((N_EXPERTS, cap, 1), F32)],
        compiler_params=_params(("arbitrary",), 56),
        name="route",
    )(probs_t.reshape(N_EXPERTS, nb, LANES))


CMB_TT = 1024
CMB_CH = 256
CMB_GROUP = 8
CMB_EPI_ROWS = 256


def _combine_kernel(idx_ref, st_ref, ye_hbm, x_ref, g2_ref, ng_ref, sc_ref, sh_ref, *rest, off, cap, nt, final):
    if final:
        y_ref, acc_ref, buf_ref, sem = rest
    else:
        xo_ref, xn_ref, acc_ref, buf_ref, sem = rest
    k = pl.program_id(0)
    t0 = k * CMB_TT
    acc_ref[...] = jnp.zeros_like(acc_ref)

    def slot_range(e):
        s0 = st_ref[e * (nt + 1) + k]
        s1 = st_ref[e * (nt + 1) + k + 1]
        return s0, s1, jnp.minimum(s0 & -CMB_GROUP, cap - CMB_CH)

    def chunk_copy(e, start):
        src = ye_hbm.at[e, pl.ds(pl.multiple_of(off + start, CMB_GROUP), CMB_CH)]
        return pltpu.make_async_copy(src, buf_ref.at[e], sem.at[e])

    def accumulate(e, start, lo, hi):
        def body(i, carry):
            r0 = pl.multiple_of(i * CMB_GROUP, CMB_GROUP)
            tile = buf_ref[e, pl.ds(r0, CMB_GROUP), :]
            dst = []
            for u in range(CMB_GROUP):
                slot = start + r0 + u
                tok = idx_ref[e * CAP_ALL + off + slot]
                dst.append(jnp.where((slot >= lo) & (slot < hi), tok - t0, CMB_TT))
            rows = [acc_ref[pl.ds(d, 1), :] for d in dst]
            for u in range(CMB_GROUP):
                acc_ref[pl.ds(dst[u], 1), :] = rows[u] + tile[u:u + 1, :]
            return carry

        lax.fori_loop((lo - start) // CMB_GROUP, (hi - start + CMB_GROUP - 1) // CMB_GROUP, body, 0)

    for e in range(N_EXPERTS):
        chunk_copy(e, slot_range(e)[2]).start()
    for e in range(N_EXPERTS):
        s0, s1, a = slot_range(e)
        chunk_copy(e, a).wait()
        accumulate(e, a, s0, jnp.minimum(s1, a + CMB_CH))

        def extra(j, carry, e=e, s1=s1, a=a):
            lo = a + CMB_CH * (j + 1)
            start = jnp.minimum(lo, cap - CMB_CH)
            cp = chunk_copy(e, start)
            cp.start()
            cp.wait()
            accumulate(e, start, lo, jnp.minimum(s1, lo + CMB_CH))
            return carry

        lax.fori_loop(0, jnp.maximum(s1 - a - 1, 0) // CMB_CH, extra, 0)

    for r0 in range(0, CMB_TT, CMB_EPI_ROWS):
        rs = slice(r0, r0 + CMB_EPI_ROWS)
        xo = x_ref[rs, :] + g2_ref[...] * acc_ref[rs, :]
        y = xo * lax.rsqrt(jnp.mean(xo * xo, axis=-1, keepdims=True) + EPS) * ng_ref[...]
        if final:
            y_ref[rs, :] = y
        else:
            xo_ref[rs, :] = xo
            xn_ref[rs, :] = (y * (1.0 + sc_ref[...]) + sh_ref[...]).astype(xn_ref.dtype)


def _combine(idx_all, starts, ye, x, g2, ng, sc, sh, seq, off, cap, final):
    n = x.shape[0]
    nt = n // CMB_TT
    per_batch = g2.shape[0] > 1
    mod_map = ((lambda i, *_: ((i * CMB_TT) // seq, 0, 0)) if per_batch else (lambda i, *_: (0, 0, 0)))
    mod_spec = pl.BlockSpec((None, 1, D_MODEL), mod_map)
    row = pl.BlockSpec((CMB_TT, D_MODEL), lambda i, *_: (i, 0))
    vec = pl.BlockSpec((1, D_MODEL), lambda i, *_: (0, 0))
    if final:
        out_specs, out_shape = row, jax.ShapeDtypeStruct((n, D_MODEL), F32)
    else:
        out_specs = [row, row]
        out_shape = [jax.ShapeDtypeStruct((n, D_MODEL), F32), jax.ShapeDtypeStruct((n, D_MODEL), BF16)]
    grid_spec = pltpu.PrefetchScalarGridSpec(
        num_scalar_prefetch=2,
        grid=(nt,),
        in_specs=[pl.BlockSpec(memory_space=pl.ANY), row, mod_spec, vec, mod_spec, mod_spec],
        out_specs=out_specs,
        scratch_shapes=[pltpu.VMEM((CMB_TT + CMB_GROUP, D_MODEL), F32),
                        pltpu.VMEM((N_EXPERTS, CMB_CH, D_MODEL), F32),
                        pltpu.SemaphoreType.DMA((N_EXPERTS,))],
    )
    return pl.pallas_call(
        functools.partial(_combine_kernel, off=off, cap=cap, nt=nt, final=final),
        grid_spec=grid_spec,
        out_shape=out_shape,
        compiler_params=_params(("arbitrary",), 60),
        name="combine",
    )(idx_all, starts.reshape(-1), ye, x, g2, ng, sc, sh)


def _tile_starts(idx_sorted, n):
    bounds = jnp.arange(n // CMB_TT + 1, dtype=jnp.int32) * CMB_TT
    return jnp.sum(idx_sorted[:, :, None] < bounds[None, None, :], axis=1, dtype=jnp.int32)


def kernel(x_prompt, x_sample, cache_attn_k, cache_attn_v, cache_na_k, cache_na_v, c, c_ctx, w_ada, b_ada, norm1_g,
           w_in, q_norm_g, k_norm_g, na_rpb, conv_w, w_proj_a, w_proj_b, w_proj_c, w_out, norm2_g, w_router,
           w_gate_e, w_up_e, w_down_e, final_g):
    cond8 = jnp.zeros((8, D_MODEL), F32).at[0].set(c_ctx).at[1:1 + DEC_BATCH].set(c)
    mods = _adaln(cond8, w_ada, b_ada).reshape(DEPTH, 8, N_MOD, D_MODEL)
    cos128, sin128 = _rope_tables()

    hp = x_prompt.reshape(N_CTX, D_MODEL)
    hs = x_sample.reshape(N_LAT, D_MODEL)
    kv_layers = []
    for l in range(DEPTH):
        mod_ctx = [mods[l, 0:1, i][:, None, :] for i in range(N_MOD)]
        mod_lat = [mods[l, 1:1 + DEC_BATCH, i][:, None, :] for i in range(N_MOD)]
        n1g = norm1_g[l][None]
        n2g = norm2_g[l][None]
        qg2 = jnp.tile(q_norm_g[l][None], (1, 2))
        kg2 = jnp.tile(k_norm_g[l][None], (1, 2))
        wa, wb, wc, wo = (w_proj_a[l].astype(BF16), w_proj_b[l].astype(BF16), w_proj_c[l].astype(BF16),
                          w_out[l].astype(BF16))
        wrt = w_router[l].T.astype(BF16)
        bias = _na_bias(na_rpb[l])

        def sublayer1(x, xn, mod, seq, is_ctx):
            _, _, g1, sh2, sc2, _ = mod
            kv_dtype = F32 if is_ctx else BF16
            h_a = _inproj(xn, w_in, l, COL_A, W_A, 768, kv_dtype)
            h_b = _inproj(xn, w_in, l, COL_B, W_B, 768, kv_dtype)
            h_c = _inproj(xn, w_in, l, COL_C, W_C, 768, BF16)
            h_g = _inproj(xn, w_in, l, COL_G, GATE_W, 768, BF16)
            kv = None
            if is_ctx:
                oab, *kv = _attn_ctx(h_a, h_b, qg2, kg2, kv_layers if l == DEPTH - 1 else None)
            else:
                oab = _attn_lat(h_a, h_b, cache_attn_k, cache_attn_v, cache_na_k, cache_na_v, l, qg2, kg2,
                                cos128, sin128, bias)
            oc = _conv(h_c, conv_w[l], seq)
            xo, xn2, probs_t = _mix(x, oab, oc, h_g, wa, wb, wc, wo, g1, n2g, sc2, sh2, wrt, seq)
            return xo, xn2, probs_t, kv

        if l == 0:
            xn_p = _norm_mod(hp, n1g, mod_ctx[1], mod_ctx[0], SEQ, BF16)
            xn_s = _norm_mod(hs, n1g, mod_lat[1], mod_lat[0], DEC_SEQ, BF16)
        hp, xn2_c, pt_c, kv_out = sublayer1(hp, xn_p, mod_ctx, SEQ, True)
        kv_layers = kv_layers + list(kv_out)
        hs, xn2_l, pt_l, _ = sublayer1(hs, xn_s, mod_lat, DEC_SEQ, False)

        idx_c, aff_c = _route(pt_c, CAP_CTX)
        idx_l, aff_l = _route(pt_l, CAP_LAT)
        idx_all = jnp.concatenate([idx_c, idx_l], axis=1).reshape(-1)
        aff_all = jnp.concatenate([aff_c, aff_l], axis=1)
        ye = _moe(idx_all, xn2_c, xn2_l, aff_all, w_gate_e, w_up_e, w_down_e, l)

        final = l == DEPTH - 1
        if final:
            ng, nsc_c, nsh_c, nsc_l, nsh_l = final_g[None], mod_ctx[1], mod_ctx[0], mod_lat[1], mod_lat[0]
        else:
            nxt_ctx = [mods[l + 1, 0:1, i][:, None, :] for i in range(2)]
            nxt_lat = [mods[l + 1, 1:1 + DEC_BATCH, i][:, None, :] for i in range(2)]
            ng, nsc_c, nsh_c, nsc_l, nsh_l = norm1_g[l + 1][None], nxt_ctx[1], nxt_ctx[0], nxt_lat[1], nxt_lat[0]
        out_c = _combine(idx_all, _tile_starts(idx_c[..., 0], N_CTX), ye, hp, mod_ctx[5], ng, nsc_c, nsh_c,
                         SEQ, 0, CAP_CTX, final)
        out_l = _combine(idx_all, _tile_starts(idx_l[..., 0], N_LAT), ye, hs, mod_lat[5], ng, nsc_l, nsh_l,
                         DEC_SEQ, CAP_CTX, CAP_LAT, final)
        if final:
            y_prompt = out_c.reshape(BATCH, SEQ, D_MODEL)
            y_sample = out_l.reshape(DEC_BATCH, DEC_SEQ, D_MODEL)
        else:
            (hp, xn_p), (hs, xn_s) = out_c, out_l

    return (y_prompt, y_sample, *kv_out)
```

```python
import functools

import jax
import jax.numpy as jnp
import numpy as np
from jax import lax
from jax.experimental import pallas as pl
from jax.experimental.pallas import tpu as pltpu

D_MODEL = 1024
BATCH = 32
SEQ = 256
DEPTH = 2
DEC_BATCH = 2
DEC_SEQ = 1024
PAST_LEN = 256
GRID_W = 64
HEAD_DIM = 64
A_HEADS = 8
A_KV_HEADS = 2
A_GROUP = A_HEADS // A_KV_HEADS
B_HEADS = 8
C_WIDTH = 512
NA_WIN_R = 8
NA_WIN_C = 16
N_EXPERTS = 16
EXPERT_FF = 1024
CAP_FACTOR = 2
ROPE_THETA = 10000.0
EPS = 1e-6
N_MOD = 6

A_Q = A_HEADS * HEAD_DIM
A_KV = A_KV_HEADS * HEAD_DIM
B_W = B_HEADS * HEAD_DIM
GATE_W = 3 * D_MODEL
COL_A = 0
W_A = A_Q + 2 * A_KV
COL_B = COL_A + W_A
W_B = 3 * B_W
COL_C = COL_B + W_B
W_C = 3 * C_WIDTH
COL_G = COL_C + W_C

N_CTX = BATCH * SEQ
N_LAT = DEC_BATCH * DEC_SEQ
CAP_CTX = CAP_FACTOR * N_CTX // N_EXPERTS
CAP_LAT = CAP_FACTOR * N_LAT // N_EXPERTS
CAP_ALL = CAP_CTX + CAP_LAT
LAT_ROWS = DEC_SEQ // GRID_W
MASK_NEG = -1e30
ATT_SCALE = HEAD_DIM ** -0.5

LANES = 128
ROW_TILE = 8
assert D_MODEL == ROW_TILE * LANES

F32 = jnp.float32
BF16 = jnp.bfloat16
_NT = (((1,), (1,)), ((), ()))

_MIB = 1024 * 1024


def _params(sem, vmem_mib):
    return pltpu.CompilerParams(dimension_semantics=sem, vmem_limit_bytes=vmem_mib * _MIB)


def _adaln_kernel(cond_ref, w_ref, b_ref, o_ref):
    c = cond_ref[...]
    s = c * jax.nn.sigmoid(c)
    o_ref[...] = jnp.dot(s.astype(BF16), w_ref[...].astype(BF16), preferred_element_type=F32) + b_ref[...]


def _adaln(cond8, w_ada, b_ada):
    tn = 1536
    ncol = N_MOD * D_MODEL
    return pl.pallas_call(
        _adaln_kernel,
        grid=(DEPTH, ncol // tn),
        in_specs=[
            pl.BlockSpec((8, D_MODEL), lambda l, j: (0, 0)),
            pl.BlockSpec((None, D_MODEL, tn), lambda l, j: (l, 0, j)),
            pl.BlockSpec((None, 1, tn), lambda l, j: (l, 0, j)),
        ],
        out_specs=pl.BlockSpec((None, 8, tn), lambda l, j: (l, 0, j)),
        out_shape=jax.ShapeDtypeStruct((DEPTH, 8, ncol), F32),
        compiler_params=_params(("arbitrary", "arbitrary"), 40),
        name="adaln",
    )(cond8, w_ada, b_ada.reshape(DEPTH, 1, ncol))


def _norm_mod_kernel(x_ref, g_ref, sc_ref, sh_ref, o_ref):
    x = x_ref[...]
    y = x * lax.rsqrt(jnp.mean(x * x, axis=-1, keepdims=True) + EPS) * g_ref[...]
    o_ref[...] = (y * (1.0 + sc_ref[...]) + sh_ref[...]).astype(o_ref.dtype)


def _norm_mod(x, g, sc, sh, seq, out_dtype):
    n = x.shape[0]
    tm = 512
    per_batch = sc.shape[0] > 1
    mod_map = (lambda i: ((i * tm) // seq, 0, 0)) if per_batch else (lambda i: (0, 0, 0))
    return pl.pallas_call(
        _norm_mod_kernel,
        grid=(n // tm,),
        in_specs=[
            pl.BlockSpec((tm, D_MODEL), lambda i: (i, 0)),
            pl.BlockSpec((1, D_MODEL), lambda i: (0, 0)),
            pl.BlockSpec((None, 1, D_MODEL), mod_map),
            pl.BlockSpec((None, 1, D_MODEL), mod_map),
        ],
        out_specs=pl.BlockSpec((tm, D_MODEL), lambda i: (i, 0)),
        out_shape=jax.ShapeDtypeStruct((n, D_MODEL), out_dtype),
        compiler_params=_params(("arbitrary",), 32),
        name="norm_mod",
    )(x, g, sc, sh)


def _inproj_kernel(a_ref, w_ref, o_ref, wbf_ref):
    @pl.when(pl.program_id(1) == 0)
    def _():
        wbf_ref[...] = w_ref[...].astype(BF16)

    o_ref[...] = jnp.dot(a_ref[...], wbf_ref[...], preferred_element_type=F32).astype(o_ref.dtype)


def _inproj(xn, w_in, layer, col0, width, tn, out_dtype):
    n = xn.shape[0]
    tm = 2048
    joff = col0 // tn
    assert col0 % tn == 0 and width % tn == 0
    return pl.pallas_call(
        _inproj_kernel,
        grid=(width // tn, n // tm),
        in_specs=[
            pl.BlockSpec((tm, D_MODEL), lambda j, i: (i, 0)),
            pl.BlockSpec((None, D_MODEL, tn), lambda j, i: (layer, 0, j + joff)),
        ],
        out_specs=pl.BlockSpec((tm, tn), lambda j, i: (i, j)),
        out_shape=jax.ShapeDtypeStruct((n, width), out_dtype),
        scratch_shapes=[pltpu.VMEM((D_MODEL, tn), BF16)],
        compiler_params=_params(("arbitrary", "arbitrary"), 48),
        name="inproj",
    )(xn, w_in)


PAIR = 2 * HEAD_DIM


def _is_lo_half():
    return lax.broadcasted_iota(jnp.int32, (1, PAIR), 1) < HEAD_DIM


def _pair_rms(x, g2, is_lo):
    x2 = x * x
    s_lo = jnp.sum(jnp.where(is_lo, x2, 0.0), axis=-1, keepdims=True)
    s_hi = jnp.sum(jnp.where(is_lo, 0.0, x2), axis=-1, keepdims=True)
    ms = jnp.where(is_lo, s_lo, s_hi) * (1.0 / HEAD_DIM)
    return x * lax.rsqrt(ms + EPS) * g2


def _split_halves(x, is_lo):
    return jnp.where(is_lo, x, 0.0).astype(BF16), jnp.where(is_lo, 0.0, x).astype(BF16)


def _pad_lo(x):
    return jnp.concatenate([x, jnp.zeros_like(x)], axis=1)


def _pad_hi(x):
    return jnp.concatenate([jnp.zeros_like(x), x], axis=1)


def _qk(q, k):
    return lax.dot_general(q, k, _NT, preferred_element_type=F32)


def _softmax_pv(scores, values):
    m = scores[0].max(axis=-1, keepdims=True)
    for s in scores[1:]:
        m = jnp.maximum(m, s.max(axis=-1, keepdims=True))
    acc = None
    den = None
    for s, v in zip(scores, values):
        p = jnp.exp(s - m)
        d = p.sum(axis=-1, keepdims=True)
        o = jnp.dot(p.astype(BF16), v, preferred_element_type=F32)
        acc = o if acc is None else acc + o
        den = d if den is None else den + d
    return acc * (1.0 / den)


CTX_NB = 2


def _attn_ctx_kernel(ha_ref, hb_ref, qg_ref, kg_ref, *rest):
    oab_ref, *kv_refs = rest[-5:]
    prev_refs = rest[:-5]
    if prev_refs:
        for li in range(len(prev_refs) // 4):
            for dst, src in zip(kv_refs, prev_refs[4 * li:4 * li + 4]):
                dst[:, li] = src[...]
        ak_ref, av_ref, nk_ref, nv_ref = [r.at[:, DEPTH - 1] for r in kv_refs]
    else:
        ak_ref, av_ref, nk_ref, nv_ref = kv_refs
    is_lo = _is_lo_half()
    qg2 = qg_ref[...] * ATT_SCALE
    kg2 = kg_ref[...]
    for b in range(CTX_NB):
        rows = slice(b * SEQ, (b + 1) * SEQ)
        kp = _pair_rms(ha_ref[rows, A_Q:A_Q + PAIR], kg2, is_lo)
        vp = ha_ref[rows, A_Q + A_KV:A_Q + A_KV + PAIR]
        for kv in range(A_KV_HEADS):
            ak_ref[b, kv] = kp[:, kv * HEAD_DIM:(kv + 1) * HEAD_DIM]
            av_ref[b, kv] = vp[:, kv * HEAD_DIM:(kv + 1) * HEAD_DIM]
        kp_sw = pltpu.roll(kp, HEAD_DIM, axis=1)
        vp_sw = pltpu.roll(vp, HEAD_DIM, axis=1)
        qs = [_pair_rms(ha_ref[rows, c * PAIR:(c + 1) * PAIR], qg2, is_lo).astype(BF16)
              for c in range(A_HEADS // 2)]
        for kv in range(A_KV_HEADS):
            k_lo = jnp.where(is_lo, kp if kv == 0 else kp_sw, 0.0).astype(BF16)
            k_hi = jnp.where(is_lo, 0.0, kp_sw if kv == 0 else kp).astype(BF16)
            v_lo = jnp.where(is_lo, vp if kv == 0 else vp_sw, 0.0).astype(BF16)
            v_hi = jnp.where(is_lo, 0.0, vp_sw if kv == 0 else vp).astype(BF16)
            q_st = jnp.concatenate([qs[2 * kv], qs[2 * kv + 1]], axis=0)
            o = _softmax_pv([_qk(q_st, k_lo)], [v_lo]) + _softmax_pv([_qk(q_st, k_hi)], [v_hi])
            oab_ref[rows, (2 * kv) * PAIR:(2 * kv + 1) * PAIR] = o[0:SEQ].astype(BF16)
            oab_ref[rows, (2 * kv + 1) * PAIR:(2 * kv + 2) * PAIR] = o[SEQ:2 * SEQ].astype(BF16)
        for j in range(B_HEADS // 2):
            qp = (hb_ref[rows, j * PAIR:(j + 1) * PAIR] * ATT_SCALE).astype(BF16)
            kp_b = hb_ref[rows, B_W + j * PAIR:B_W + (j + 1) * PAIR]
            vp_b = hb_ref[rows, 2 * B_W + j * PAIR:2 * B_W + (j + 1) * PAIR]
            for u in range(2):
                nk_ref[b, 2 * j + u] = kp_b[:, u * HEAD_DIM:(u + 1) * HEAD_DIM]
                nv_ref[b, 2 * j + u] = vp_b[:, u * HEAD_DIM:(u + 1) * HEAD_DIM]
            k_lo, k_hi = _split_halves(kp_b, is_lo)
            v_lo, v_hi = _split_halves(vp_b, is_lo)
            o = _softmax_pv([_qk(qp, k_lo)], [v_lo]) + _softmax_pv([_qk(qp, k_hi)], [v_hi])
            oab_ref[rows, A_Q + j * PAIR:A_Q + (j + 1) * PAIR] = o.astype(BF16)


def _attn_ctx(h_a, h_b, qg2, kg2, prev_kv):
    rows = CTX_NB * SEQ
    layer_spec = lambda nh: pl.BlockSpec((CTX_NB, nh, SEQ, HEAD_DIM), lambda b: (b, 0, 0, 0))
    heads = (A_KV_HEADS, A_KV_HEADS, B_HEADS, B_HEADS)
    if prev_kv is None:
        prev_kv, prev_specs = (), []
        kv_specs = [layer_spec(nh) for nh in heads]
        kv_shapes = [jax.ShapeDtypeStruct((BATCH, nh, SEQ, HEAD_DIM), F32) for nh in heads]
    else:
        prev_specs = [layer_spec(heads[i % 4]) for i in range(len(prev_kv))]
        kv_specs = [pl.BlockSpec((CTX_NB, DEPTH, nh, SEQ, HEAD_DIM), lambda b: (b, 0, 0, 0, 0)) for nh in heads]
        kv_shapes = [jax.ShapeDtypeStruct((BATCH, DEPTH, nh, SEQ, HEAD_DIM), F32) for nh in heads]
    return pl.pallas_call(
        _attn_ctx_kernel,
        grid=(BATCH // CTX_NB,),
        in_specs=[
            pl.BlockSpec((rows, W_A), lambda b: (b, 0)),
            pl.BlockSpec((rows, W_B), lambda b: (b, 0)),
            pl.BlockSpec((1, PAIR), lambda b: (0, 0)),
            pl.BlockSpec((1, PAIR), lambda b: (0, 0)),
        ] + prev_specs,
        out_specs=[pl.BlockSpec((rows, A_Q + B_W), lambda b: (b, 0))] + kv_specs,
        out_shape=[jax.ShapeDtypeStruct((N_CTX, A_Q + B_W), BF16)] + kv_shapes,
        compiler_params=_params(("arbitrary",), 56),
        name="attn_ctx",
    )(h_a, h_b, qg2, kg2, *prev_kv)


LAT_TQ = 256
LAT_QROWS = LAT_TQ // GRID_W
LAT_NT = DEC_SEQ // LAT_TQ
NA_UNION_ROWS = 12
NA_UNION = NA_UNION_ROWS * GRID_W


def _na_union_row0(t):
    lo = min(max(r - NA_WIN_R // 2, 0) for r in range(t * LAT_QROWS, (t + 1) * LAT_QROWS))
    return min(min(lo, LAT_ROWS - NA_WIN_R), LAT_ROWS - NA_UNION_ROWS)


def _rope(x, cos, sin):
    lane = lax.broadcasted_iota(jnp.int32, x.shape, 1)
    nxt = pltpu.roll(x, x.shape[1] - 1, axis=1)
    prv = pltpu.roll(x, 1, axis=1)
    partner = jnp.where((lane & 1) == 0, nxt, prv)
    return x * cos + partner * sin


def _attn_lat_kernel(pick_ref, haq_ref, haf_ref, hbq_ref, hbf_ref, cak_ref, cav_ref, cnk_ref, cnv_ref,
                     qg_ref, kg_ref, cos_ref, sin_ref, toep_ref, oab_ref, kva_ref):
    t = pl.program_id(1)
    is_lo = _is_lo_half()
    row0 = pl.multiple_of(t * LAT_TQ, LAT_TQ)
    qg2 = qg_ref[...] * ATT_SCALE
    kg2 = kg_ref[...]

    @pl.when(t == 0)
    def _():
        kp = _rope(_pair_rms(haf_ref[:, A_Q:A_Q + PAIR].astype(F32), kg2, is_lo), cos_ref[...], sin_ref[...])
        vp = haf_ref[:, A_Q + A_KV:A_Q + A_KV + PAIR].astype(F32)
        kp_sw = pltpu.roll(kp, HEAD_DIM, axis=1)
        vp_sw = pltpu.roll(vp, HEAD_DIM, axis=1)
        for kv in range(A_KV_HEADS):
            kva_ref[4 * kv + 0] = jnp.where(is_lo, kp if kv == 0 else kp_sw, 0.0).astype(BF16)
            kva_ref[4 * kv + 1] = jnp.where(is_lo, 0.0, kp_sw if kv == 0 else kp).astype(BF16)
            kva_ref[4 * kv + 2] = jnp.where(is_lo, vp if kv == 0 else vp_sw, 0.0).astype(BF16)
            kva_ref[4 * kv + 3] = jnp.where(is_lo, 0.0, vp_sw if kv == 0 else vp).astype(BF16)

    cos_q = cos_ref[pl.ds(row0, LAT_TQ), :]
    sin_q = sin_ref[pl.ds(row0, LAT_TQ), :]
    qs = [_rope(_pair_rms(haq_ref[:, c * PAIR:(c + 1) * PAIR].astype(F32), qg2, is_lo), cos_q, sin_q).astype(BF16)
          for c in range(A_HEADS // 2)]
    for kv in range(A_KV_HEADS):
        kc = cak_ref[kv].astype(BF16)
        vc = cav_ref[kv].astype(BF16)
        q_st = jnp.concatenate([qs[2 * kv], qs[2 * kv + 1]], axis=0)
        o = (_softmax_pv([_qk(q_st, _pad_lo(kc)), _qk(q_st, kva_ref[4 * kv + 0])], [_pad_lo(vc), kva_ref[4 * kv + 2]])
             + _softmax_pv([_qk(q_st, _pad_hi(kc)), _qk(q_st, kva_ref[4 * kv + 1])], [_pad_hi(vc), kva_ref[4 * kv + 3]]))
        oab_ref[:, (2 * kv) * PAIR:(2 * kv + 1) * PAIR] = o[0:LAT_TQ].astype(BF16)
        oab_ref[:, (2 * kv + 1) * PAIR:(2 * kv + 2) * PAIR] = o[LAT_TQ:2 * LAT_TQ].astype(BF16)

    k0 = _na_union_row0(0) * GRID_W
    for tt in range(1, LAT_NT):
        k0 = jnp.where(t >= tt, _na_union_row0(tt) * GRID_W, k0)
    k0 = pl.multiple_of(k0, GRID_W)

    def bias(h):
        base = t * (LAT_QROWS * NA_UNION_ROWS)
        return jnp.concatenate(
            [jnp.concatenate([toep_ref[h, pick_ref[base + i * NA_UNION_ROWS + u]] for u in range(NA_UNION_ROWS)],
                             axis=1) for i in range(LAT_QROWS)], axis=0)

    for j in range(B_HEADS // 2):
        qp = (hbq_ref[:, j * PAIR:(j + 1) * PAIR] * ATT_SCALE).astype(BF16)
        k_lo, k_hi = _split_halves(hbf_ref[pl.ds(k0, NA_UNION), B_W + j * PAIR:B_W + (j + 1) * PAIR], is_lo)
        v_lo, v_hi = _split_halves(hbf_ref[pl.ds(k0, NA_UNION), 2 * B_W + j * PAIR:2 * B_W + (j + 1) * PAIR], is_lo)
        kc_lo = _pad_lo(cnk_ref[2 * j].astype(BF16))
        kc_hi = _pad_hi(cnk_ref[2 * j + 1].astype(BF16))
        vc_lo = _pad_lo(cnv_ref[2 * j].astype(BF16))
        vc_hi = _pad_hi(cnv_ref[2 * j + 1].astype(BF16))
        o = (_softmax_pv([_qk(qp, kc_lo), _qk(qp, k_lo) + bias(2 * j)], [vc_lo, v_lo])
             + _softmax_pv([_qk(qp, kc_hi), _qk(qp, k_hi) + bias(2 * j + 1)], [vc_hi, v_hi]))
        oab_ref[:, A_Q + j * PAIR:A_Q + (j + 1) * PAIR] = o.astype(BF16)


def _attn_lat(h_a, h_b, cak, cav, cnk, cnv, layer, qg2, kg2, cos128, sin128, toep):
    nt = LAT_NT
    cache_spec = lambda nh: pl.BlockSpec((None, None, nh, PAST_LEN, HEAD_DIM), lambda b, t, _: (b, layer, 0, 0, 0))
    const = lambda shape: pl.BlockSpec(shape, lambda b, t, _: (0,) * len(shape))
    grid_spec = pltpu.PrefetchScalarGridSpec(
        num_scalar_prefetch=1,
        grid=(DEC_BATCH, nt),
        in_specs=[
            pl.BlockSpec((LAT_TQ, W_A), lambda b, t, _: (b * nt + t, 0)),
            pl.BlockSpec((DEC_SEQ, W_A), lambda b, t, _: (b, 0)),
            pl.BlockSpec((LAT_TQ, W_B), lambda b, t, _: (b * nt + t, 0)),
            pl.BlockSpec((DEC_SEQ, W_B), lambda b, t, _: (b, 0)),
            cache_spec(A_KV_HEADS), cache_spec(A_KV_HEADS), cache_spec(B_HEADS), cache_spec(B_HEADS),
            const((1, PAIR)), const((1, PAIR)), const((DEC_SEQ, PAIR)), const((DEC_SEQ, PAIR)),
            const(toep.shape),
        ],
        out_specs=pl.BlockSpec((LAT_TQ, A_Q + B_W), lambda b, t, _: (b * nt + t, 0)),
        scratch_shapes=[pltpu.VMEM((4 * A_KV_HEADS, DEC_SEQ, PAIR), BF16)],
    )
    return pl.pallas_call(
        _attn_lat_kernel,
        grid_spec=grid_spec,
        out_shape=jax.ShapeDtypeStruct((N_LAT, A_Q + B_W), BF16),
        compiler_params=_params(("arbitrary", "arbitrary"), 56),
        name="attn_lat",
    )(jnp.asarray(_na_tile_picks().reshape(-1)), h_a, h_a, h_b, h_b, cak, cav, cnk, cnv, qg2, kg2, cos128, sin128,
      toep)


def _rope_tables():
    t = jnp.arange(DEC_SEQ)
    row = (t // GRID_W).astype(F32)
    col = (t % GRID_W).astype(F32)
    n_freq = HEAD_DIM // 4
    inv = ROPE_THETA ** (-jnp.arange(n_freq, dtype=F32) / n_freq)
    ang = jnp.concatenate([row[:, None] * inv, col[:, None] * inv], axis=-1)
    cos = jnp.repeat(jnp.cos(ang), 2, axis=-1)
    sign = jnp.tile(jnp.array([-1.0, 1.0], F32), HEAD_DIM // 2)
    sin = jnp.repeat(jnp.sin(ang), 2, axis=-1) * sign
    return jnp.tile(cos, (1, 2)), jnp.tile(sin, (1, 2))


NA_MASKED_TILE = 2 * NA_WIN_R - 1


def _na_toeplitz(rpb):
    col = np.arange(GRID_W)
    dc = np.clip(col[None, :] - col[:, None] + NA_WIN_C - 1, 0, 2 * NA_WIN_C - 2)
    onehot = (dc[..., None] == np.arange(2 * NA_WIN_C - 1)).astype(np.float32)
    c0 = np.clip(col - NA_WIN_C // 2, 0, GRID_W - NA_WIN_C)
    in_win = (col[None, :] >= c0[:, None]) & (col[None, :] < c0[:, None] + NA_WIN_C)
    toep = jnp.einsum('hdm,ckm->hdck', rpb, onehot, precision=lax.Precision.HIGHEST)
    toep = jnp.where(in_win[None, None], toep, MASK_NEG)
    return jnp.concatenate([toep, jnp.full((B_HEADS, 1, GRID_W, GRID_W), MASK_NEG, F32)], axis=1)


def _na_tile_picks():
    pick = np.full((LAT_NT, LAT_QROWS, NA_UNION_ROWS), NA_MASKED_TILE, np.int32)
    for t in range(LAT_NT):
        u0 = _na_union_row0(t)
        for i in range(LAT_QROWS):
            r = t * LAT_QROWS + i
            r0 = min(max(r - NA_WIN_R // 2, 0), LAT_ROWS - NA_WIN_R)
            for u in range(NA_UNION_ROWS):
                if r0 <= u0 + u < r0 + NA_WIN_R:
                    pick[t, i, u] = u0 + u - r + NA_WIN_R - 1
    return pick


def _conv_kernel(hc_ref, w_ref, o_ref, *, seq):
    bg = hc_ref[:, 0:C_WIDTH].astype(F32)
    u = hc_ref[:, C_WIDTH:2 * C_WIDTH].astype(F32) * hc_ref[:, 2 * C_WIDTH:3 * C_WIDTH].astype(F32)
    rows = u.shape[0]
    assert seq & (seq - 1) == 0
    pos = lax.broadcasted_iota(jnp.int32, u.shape, 0) & (seq - 1)
    u_prev = jnp.where(pos == 0, 0.0, pltpu.roll(u, 1, axis=0))
    u_next = jnp.where(pos == seq - 1, 0.0, pltpu.roll(u, rows - 1, axis=0))
    y = u_prev * w_ref[0:1, :] + u * w_ref[1:2, :] + u_next * w_ref[2:3, :]
    o_ref[...] = (bg * y).astype(o_ref.dtype)


def _conv(h_c, conv_w, seq):
    n = h_c.shape[0]
    tm = 1024
    return pl.pallas_call(
        functools.partial(_conv_kernel, seq=seq),
        grid=(n // tm,),
        in_specs=[pl.BlockSpec((tm, W_C), lambda i: (i, 0)), pl.BlockSpec((3, C_WIDTH), lambda i: (0, 0))],
        out_specs=pl.BlockSpec((tm, C_WIDTH), lambda i: (i, 0)),
        out_shape=jax.ShapeDtypeStruct((n, C_WIDTH), BF16),
        compiler_params=_params(("arbitrary",), 40),
        name="short_conv",
    )(h_c, conv_w)


def _mix_kernel(x_ref, oab_ref, oc_ref, hg_ref, wa_ref, wb_ref, wc_ref, wo_ref, g1_ref, n2g_ref, sc2_ref, sh2_ref,
                wrt_ref, xo_ref, xn2_ref, pt_ref):
    ga = jax.nn.sigmoid(hg_ref[:, 0:D_MODEL].astype(F32))
    gb = jax.nn.sigmoid(hg_ref[:, D_MODEL:2 * D_MODEL].astype(F32))
    gc = jax.nn.sigmoid(hg_ref[:, 2 * D_MODEL:3 * D_MODEL].astype(F32))
    pa = jnp.dot(oab_ref[:, 0:A_Q], wa_ref[...], preferred_element_type=F32)
    pb = jnp.dot(oab_ref[:, A_Q:A_Q + B_W], wb_ref[...], preferred_element_type=F32)
    pc = jnp.dot(oc_ref[...], wc_ref[...], preferred_element_type=F32)
    merged = ga * pa + gb * pb + gc * pc
    xo = x_ref[...] + g1_ref[...] * jnp.dot(merged.astype(BF16), wo_ref[...], preferred_element_type=F32)
    xo_ref[...] = xo
    y = xo * lax.rsqrt(jnp.mean(xo * xo, axis=-1, keepdims=True) + EPS) * n2g_ref[...]
    xn2 = y * (1.0 + sc2_ref[...]) + sh2_ref[...]
    rows = xn2.shape[0]
    for j in range(ROW_TILE):
        xn2_ref[pl.ds(j, rows, stride=ROW_TILE), :] = xn2[:, j * LANES:(j + 1) * LANES]
    logits_t = lax.dot_general(wrt_ref[...], xn2.astype(BF16), _NT, preferred_element_type=F32)
    z = jnp.exp(logits_t - logits_t.max(axis=0, keepdims=True))
    pt_ref[...] = z / z.sum(axis=0, keepdims=True)


def _mix(x, oab, oc, h_g, wa, wb, wc, wo, g1, n2g, sc2, sh2, wrt, seq):
    n = x.shape[0]
    tm = 512
    per_batch = g1.shape[0] > 1
    mod_map = (lambda i: ((i * tm) // seq, 0, 0)) if per_batch else (lambda i: (0, 0, 0))
    mod_spec = pl.BlockSpec((None, 1, D_MODEL), mod_map)
    full = lambda a: pl.BlockSpec(a.shape, lambda i: (0,) * a.ndim)
    row = lambda w: pl.BlockSpec((tm, w), lambda i: (i, 0))
    return pl.pallas_call(
        _mix_kernel,
        grid=(n // tm,),
        in_specs=[row(D_MODEL), row(A_Q + B_W), row(C_WIDTH), row(GATE_W), full(wa), full(wb), full(wc), full(wo),
                  mod_spec, full(n2g), mod_spec, mod_spec, full(wrt)],
        out_specs=[row(D_MODEL), pl.BlockSpec((tm * ROW_TILE, LANES), lambda i: (i, 0)),
                   pl.BlockSpec((N_EXPERTS, tm), lambda i: (0, i))],
        out_shape=[jax.ShapeDtypeStruct((n, D_MODEL), F32), jax.ShapeDtypeStruct((n * ROW_TILE, LANES), F32),
                   jax.ShapeDtypeStruct((N_EXPERTS, n), F32)],
        compiler_params=_params(("arbitrary",), 56),
        name="mix",
    )(x, oab, oc, h_g, wa, wb, wc, wo, g1, n2g, sc2, sh2, wrt)


MOE_TF = 512
MOE_ROW_CHUNKS = ((0, 512), (512, 512), (1024, 256))
MOE_ISSUE_UNROLL = 8


def _moe_issue_rows(idx_ref, idx0, src_hbm, dst_ref, dst_row0, count, sem):
    def body(i, carry):
        for u in range(MOE_ISSUE_UNROLL):
            r = i * MOE_ISSUE_UNROLL + u
            tok = idx_ref[idx0 + r]
            src = src_hbm.at[pl.ds(pl.multiple_of(tok * ROW_TILE, ROW_TILE), ROW_TILE)]
            dst = dst_ref.at[pl.ds(pl.multiple_of((dst_row0 + r) * ROW_TILE, ROW_TILE), ROW_TILE)]
            pltpu.make_async_copy(src, dst, sem).start()
        return carry

    lax.fori_loop(0, count // MOE_ISSUE_UNROLL, body, 0)


def _moe_kernel(idx_ref, xc_hbm, xl_hbm, aff_ref, wg_ref, wu_ref, wd_ref, ye_ref, xe_ref, sem):
    e = pl.program_id(0)
    f = pl.program_id(1)
    slot = e % 2

    def gather(expert, s):
        base = expert * CAP_ALL
        _moe_issue_rows(idx_ref, base, xc_hbm, xe_ref.at[s], 0, CAP_CTX, sem.at[s])
        _moe_issue_rows(idx_ref, base + CAP_CTX, xl_hbm, xe_ref.at[s], CAP_CTX, CAP_LAT, sem.at[s])

    @pl.when(f == 0)
    def _():
        @pl.when(e == 0)
        def _():
            gather(0, 0)

        pltpu.make_async_copy(xc_hbm.at[pl.ds(0, CAP_ALL * ROW_TILE)], xe_ref.at[slot], sem.at[slot]).wait()

        @pl.when(e + 1 < N_EXPERTS)
        def _():
            gather(e + 1, 1 - slot)

    wg = wg_ref[...].astype(BF16)
    wu = wu_ref[...].astype(BF16)
    wd = wd_ref[...].astype(BF16)
    for r0, rn in MOE_ROW_CHUNKS:
        xe = jnp.concatenate([xe_ref[slot, pl.ds(r0 * ROW_TILE + j, rn, stride=ROW_TILE), :]
                              for j in range(ROW_TILE)], axis=1).astype(BF16)
        hg = jnp.dot(xe, wg, preferred_element_type=F32)
        hu = jnp.dot(xe, wu, preferred_element_type=F32)
        act = (hg * jax.nn.sigmoid(hg) * hu).astype(BF16)
        part = jnp.dot(act, wd, preferred_element_type=F32) * aff_ref[r0:r0 + rn, :]

        @pl.when(f == 0)
        def _():
            ye_ref[r0:r0 + rn, :] = part

        @pl.when(f != 0)
        def _():
            ye_ref[r0:r0 + rn, :] += part


def _moe(idx_all, xn2_ctx, xn2_lat, aff_all, w_gate_e, w_up_e, w_down_e, layer):
    nf = EXPERT_FF // MOE_TF
    grid_spec = pltpu.PrefetchScalarGridSpec(
        num_scalar_prefetch=1,
        grid=(N_EXPERTS, nf),
        in_specs=[
            pl.BlockSpec(memory_space=pl.ANY),
            pl.BlockSpec(memory_space=pl.ANY),
            pl.BlockSpec((None, CAP_ALL, 1), lambda e, f, idx: (e, 0, 0)),
            pl.BlockSpec((None, None, D_MODEL, MOE_TF), lambda e, f, idx: (layer, e, 0, f)),
            pl.BlockSpec((None, None, D_MODEL, MOE_TF), lambda e, f, idx: (layer, e, 0, f)),
            pl.BlockSpec((None, None, MOE_TF, D_MODEL), lambda e, f, idx: (layer, e, f, 0)),
        ],
        out_specs=pl.BlockSpec((None, CAP_ALL, D_MODEL), lambda e, f, idx: (e, 0, 0)),
        scratch_shapes=[pltpu.VMEM((2, CAP_ALL * ROW_TILE, LANES), F32), pltpu.SemaphoreType.DMA((2,))],
    )
    return pl.pallas_call(
        _moe_kernel,
        grid_spec=grid_spec,
        out_shape=jax.ShapeDtypeStruct((N_EXPERTS, CAP_ALL, D_MODEL), F32),
        compiler_params=_params(("arbitrary", "arbitrary"), 56),
        name="moe_experts",
    )(idx_all, xn2_ctx, xn2_lat, aff_all, w_gate_e, w_up_e, w_down_e)


POS_SPLIT = 32


def _prefix_count(mask_bf, tri_lane, tri_blk_strict):
    within = jnp.dot(mask_bf, tri_lane, preferred_element_type=F32)
    bs = jnp.broadcast_to(within[:, LANES - 1:LANES], within.shape).astype(BF16)
    return within + jnp.dot(tri_blk_strict, bs, preferred_element_type=F32)


def _route_kernel(p_ref, idx_ref, aff_ref, *, cap):
    p_all = p_ref[...]
    nb = p_all.shape[1]
    capf = float(cap)

    def as_float(word):
        return lax.bitcast_convert_type(word, F32)

    def bisect(_, c):
        lo, hi = c
        mid = lo + ((hi - lo) >> 1)
        ok = jnp.sum(jnp.where(p_all >= as_float(mid), 1.0, 0.0), axis=(1, 2), keepdims=True) >= capf
        return jnp.where(ok, mid, lo), jnp.where(ok, hi, mid)

    lo0 = jnp.zeros((N_EXPERTS, 1, 1), jnp.int32)
    hi0 = jnp.full((N_EXPERTS, 1, 1), 0x7F800000, jnp.int32)
    thr_all, _ = lax.fori_loop(0, 31, bisect, (lo0, hi0))
    thr_val = as_float(thr_all)

    li = lax.broadcasted_iota(jnp.int32, (LANES, LANES), 0)
    lj = lax.broadcasted_iota(jnp.int32, (LANES, LANES), 1)
    tri_lane = jnp.where(li <= lj, 1.0, 0.0).astype(BF16)
    bi = lax.broadcasted_iota(jnp.int32, (nb, nb), 0)
    bj = lax.broadcasted_iota(jnp.int32, (nb, nb), 1)
    tri_blk_strict = jnp.where(bj < bi, 1.0, 0.0).astype(BF16)
    tri_blk_incl = jnp.where(bi <= bj, 1.0, 0.0).astype(BF16)
    s_col = lax.broadcasted_iota(jnp.int32, (cap, 1), 0).astype(F32)
    blk_id = lax.broadcasted_iota(jnp.int32, (1, nb), 1).astype(F32)
    lane_id = lax.broadcasted_iota(jnp.int32, (1, LANES), 1).astype(F32)
    ones_rows = jnp.ones((8, LANES), BF16)

    for e in range(N_EXPERTS):
        p = p_all[e]
        thr = thr_val[e]
        gt = p > thr
        eq = p == thr
        need = capf - jnp.sum(jnp.where(gt, 1.0, 0.0), keepdims=True)
        tie_rank = _prefix_count(jnp.where(eq, 1.0, 0.0).astype(BF16), tri_lane, tri_blk_strict)
        sel = jnp.where(gt | (eq & (tie_rank <= need)), 1.0, 0.0).astype(BF16)
        pos = _prefix_count(sel, tri_lane, tri_blk_strict)

        bs_row = lax.dot_general(ones_rows, sel, _NT, preferred_element_type=F32)
        bp_row = jnp.dot(bs_row.astype(BF16), tri_blk_incl, preferred_element_type=F32)[0:1, :]
        blk = jnp.sum(jnp.where(bp_row <= s_col, 1.0, 0.0), axis=-1, keepdims=True)
        onehot = jnp.where(blk == blk_id, 1.0, 0.0).astype(BF16)
        pos_hi = jnp.floor(pos * (1.0 / POS_SPLIT))
        pos_lo = pos - pos_hi * POS_SPLIT
        pos_row = (jnp.dot(onehot, pos_hi.astype(BF16), preferred_element_type=F32) * POS_SPLIT
                   + jnp.dot(onehot, pos_lo.astype(BF16), preferred_element_type=F32))
        lane = jnp.sum(jnp.where(pos_row <= s_col, 1.0, 0.0), axis=-1, keepdims=True)
        idx_ref[e] = (blk * LANES + lane).astype(jnp.int32)

        p1 = p.astype(BF16)
        r1 = p - p1.astype(F32)
        p2 = r1.astype(BF16)
        p3 = (r1 - p2.astype(F32)).astype(BF16)
        p_row = (jnp.dot(onehot, p1, preferred_element_type=F32) + jnp.dot(onehot, p2, preferred_element_type=F32)
                 + jnp.dot(onehot, p3, preferred_element_type=F32))
        aff_ref[e] = jnp.sum(jnp.where(lane_id == lane, p_row, 0.0), axis=-1, keepdims=True)


def _route(probs_t, cap):
    nb = probs_t.shape[1] // LANES
    slot = pl.BlockSpec((N_EXPERTS, cap, 1), lambda i: (0, 0, 0))
    return pl.pallas_call(
        functools.partial(_route_kernel, cap=cap),
        grid=(1,),
        in_specs=[pl.BlockSpec((N_EXPERTS, nb, LANES), lambda i: (0, 0, 0))],
        out_specs=[slot, slot],
        out_shape=[jax.ShapeDtypeStruct((N_EXPERTS, cap, 1), jnp.int32),
                   jax.ShapeDtypeStruct((N_EXPERTS, cap, 1), F32)],
        compiler_params=_params(("arbitrary",), 56),
        name="route",
    )(probs_t.reshape(N_EXPERTS, nb, LANES))


CMB_TT = 1024
CMB_CH = 256
CMB_GROUP = 8
CMB_EPI_ROWS = 256


def _combine_kernel(idx_ref, st_ref, ye_hbm, x_ref, g2_ref, ng_ref, sc_ref, sh_ref, *rest, off, cap, nt, final):
    if final:
        y_ref, acc_ref, buf_ref, sem = rest
    else:
        xo_ref, xn_ref, acc_ref, buf_ref, sem = rest
    k = pl.program_id(0)
    t0 = k * CMB_TT
    acc_ref[...] = jnp.zeros_like(acc_ref)

    def slot_range(e):
        s0 = st_ref[e * (nt + 1) + k]
        s1 = st_ref[e * (nt + 1) + k + 1]
        return s0, s1, jnp.minimum(s0 & -CMB_GROUP, cap - CMB_CH)

    def chunk_copy(e, start):
        src = ye_hbm.at[e, pl.ds(pl.multiple_of(off + start, CMB_GROUP), CMB_CH)]
        return pltpu.make_async_copy(src, buf_ref.at[e], sem.at[e])

    def accumulate(e, start, lo, hi):
        def body(i, carry):
            r0 = pl.multiple_of(i * CMB_GROUP, CMB_GROUP)
            tile = buf_ref[e, pl.ds(r0, CMB_GROUP), :]
            dst = []
            for u in range(CMB_GROUP):
                slot = start + r0 + u
                tok = idx_ref[e * CAP_ALL + off + slot]
                dst.append(jnp.where((slot >= lo) & (slot < hi), tok - t0, CMB_TT))
            rows = [acc_ref[pl.ds(d, 1), :] for d in dst]
            for u in range(CMB_GROUP):
                acc_ref[pl.ds(dst[u], 1), :] = rows[u] + tile[u:u + 1, :]
            return carry

        lax.fori_loop((lo - start) // CMB_GROUP, (hi - start + CMB_GROUP - 1) // CMB_GROUP, body, 0)

    for e in range(N_EXPERTS):
        chunk_copy(e, slot_range(e)[2]).start()
    for e in range(N_EXPERTS):
        s0, s1, a = slot_range(e)
        chunk_copy(e, a).wait()
        accumulate(e, a, s0, jnp.minimum(s1, a + CMB_CH))

        def extra(j, carry, e=e, s1=s1, a=a):
            lo = a + CMB_CH * (j + 1)
            start = jnp.minimum(lo, cap - CMB_CH)
            cp = chunk_copy(e, start)
            cp.start()
            cp.wait()
            accumulate(e, start, lo, jnp.minimum(s1, lo + CMB_CH))
            return carry

        lax.fori_loop(0, jnp.maximum(s1 - a - 1, 0) // CMB_CH, extra, 0)

    for r0 in range(0, CMB_TT, CMB_EPI_ROWS):
        rs = slice(r0, r0 + CMB_EPI_ROWS)
        xo = x_ref[rs, :] + g2_ref[...] * acc_ref[rs, :]
        y = xo * lax.rsqrt(jnp.mean(xo * xo, axis=-1, keepdims=True) + EPS) * ng_ref[...]
        if final:
            y_ref[rs, :] = y
        else:
            xo_ref[rs, :] = xo
            xn_ref[rs, :] = (y * (1.0 + sc_ref[...]) + sh_ref[...]).astype(xn_ref.dtype)


def _combine(idx_all, starts, ye, x, g2, ng, sc, sh, seq, off, cap, final):
    n = x.shape[0]
    nt = n // CMB_TT
    per_batch = g2.shape[0] > 1
    mod_map = ((lambda i, *_: ((i * CMB_TT) // seq, 0, 0)) if per_batch else (lambda i, *_: (0, 0, 0)))
    mod_spec = pl.BlockSpec((None, 1, D_MODEL), mod_map)
    row = pl.BlockSpec((CMB_TT, D_MODEL), lambda i, *_: (i, 0))
    vec = pl.BlockSpec((1, D_MODEL), lambda i, *_: (0, 0))
    if final:
        out_specs, out_shape = row, jax.ShapeDtypeStruct((n, D_MODEL), F32)
    else:
        out_specs = [row, row]
        out_shape = [jax.ShapeDtypeStruct((n, D_MODEL), F32), jax.ShapeDtypeStruct((n, D_MODEL), BF16)]
    grid_spec = pltpu.PrefetchScalarGridSpec(
        num_scalar_prefetch=2,
        grid=(nt,),
        in_specs=[pl.BlockSpec(memory_space=pl.ANY), row, mod_spec, vec, mod_spec, mod_spec],
        out_specs=out_specs,
        scratch_shapes=[pltpu.VMEM((CMB_TT + CMB_GROUP, D_MODEL), F32),
                        pltpu.VMEM((N_EXPERTS, CMB_CH, D_MODEL), F32),
                        pltpu.SemaphoreType.DMA((N_EXPERTS,))],
    )
    return pl.pallas_call(
        functools.partial(_combine_kernel, off=off, cap=cap, nt=nt, final=final),
        grid_spec=grid_spec,
        out_shape=out_shape,
        compiler_params=_params(("arbitrary",), 60),
        name="combine",
    )(idx_all, starts.reshape(-1), ye, x, g2, ng, sc, sh)


def _tile_starts(idx_sorted, n):
    bounds = jnp.arange(n // CMB_TT + 1, dtype=jnp.int32) * CMB_TT
    return jnp.sum(idx_sorted[:, :, None] < bounds[None, None, :], axis=1, dtype=jnp.int32)


def kernel(x_prompt, x_sample, cache_attn_k, cache_attn_v, cache_na_k, cache_na_v, c, c_ctx, w_ada, b_ada, norm1_g,
           w_in, q_norm_g, k_norm_g, na_rpb, conv_w, w_proj_a, w_proj_b, w_proj_c, w_out, norm2_g, w_router,
           w_gate_e, w_up_e, w_down_e, final_g):
    cond8 = jnp.zeros((8, D_MODEL), F32).at[0].set(c_ctx).at[1:1 + DEC_BATCH].set(c)
    mods = _adaln(cond8, w_ada, b_ada).reshape(DEPTH, 8, N_MOD, D_MODEL)
    cos128, sin128 = _rope_tables()

    hp = x_prompt.reshape(N_CTX, D_MODEL)
    hs = x_sample.reshape(N_LAT, D_MODEL)
    kv_layers = []
    for l in range(DEPTH):
        mod_ctx = [mods[l, 0:1, i][:, None, :] for i in range(N_MOD)]
        mod_lat = [mods[l, 1:1 + DEC_BATCH, i][:, None, :] for i in range(N_MOD)]
        n1g = norm1_g[l][None]
        n2g = norm2_g[l][None]
        qg2 = jnp.tile(q_norm_g[l][None], (1, 2))
        kg2 = jnp.tile(k_norm_g[l][None], (1, 2))
        wa, wb, wc, wo = (w_proj_a[l].astype(BF16), w_proj_b[l].astype(BF16), w_proj_c[l].astype(BF16),
                          w_out[l].astype(BF16))
        wrt = w_router[l].T.astype(BF16)
        toep = _na_toeplitz(na_rpb[l])

        def sublayer1(x, xn, mod, seq, is_ctx):
            _, _, g1, sh2, sc2, _ = mod
            kv_dtype = F32 if is_ctx else BF16
            h_a = _inproj(xn, w_in, l, COL_A, W_A, 768, kv_dtype)
            h_b = _inproj(xn, w_in, l, COL_B, W_B, 768, kv_dtype)
            h_c = _inproj(xn, w_in, l, COL_C, W_C, 768, BF16)
            h_g = _inproj(xn, w_in, l, COL_G, GATE_W, 768, BF16)
            kv = None
            if is_ctx:
                oab, *kv = _attn_ctx(h_a, h_b, qg2, kg2, kv_layers if l == DEPTH - 1 else None)
            else:
                oab = _attn_lat(h_a, h_b, cache_attn_k, cache_attn_v, cache_na_k, cache_na_v, l, qg2, kg2,
                                cos128, sin128, toep)
            oc = _conv(h_c, conv_w[l], seq)
            xo, xn2, probs_t = _mix(x, oab, oc, h_g, wa, wb, wc, wo, g1, n2g, sc2, sh2, wrt, seq)
            return xo, xn2, probs_t, kv

        if l == 0:
            xn_p = _norm_mod(hp, n1g, mod_ctx[1], mod_ctx[0], SEQ, BF16)
            xn_s = _norm_mod(hs, n1g, mod_lat[1], mod_lat[0], DEC_SEQ, BF16)
        hp, xn2_c, pt_c, kv_out = sublayer1(hp, xn_p, mod_ctx, SEQ, True)
        kv_layers = kv_layers + list(kv_out)
        hs, xn2_l, pt_l, _ = sublayer1(hs, xn_s, mod_lat, DEC_SEQ, False)

        idx_c, aff_c = _route(pt_c, CAP_CTX)
        idx_l, aff_l = _route(pt_l, CAP_LAT)
        idx_all = jnp.concatenate([idx_c, idx_l], axis=1).reshape(-1)
        aff_all = jnp.concatenate([aff_c, aff_l], axis=1)
        ye = _moe(idx_all, xn2_c, xn2_l, aff_all, w_gate_e, w_up_e, w_down_e, l)

        final = l == DEPTH - 1
        if final:
            ng, nsc_c, nsh_c, nsc_l, nsh_l = final_g[None], mod_ctx[1], mod_ctx[0], mod_lat[1], mod_lat[0]
        else:
            nxt_ctx = [mods[l + 1, 0:1, i][:, None, :] for i in range(2)]
            nxt_lat = [mods[l + 1, 1:1 + DEC_BATCH, i][:, None, :] for i in range(2)]
            ng, nsc_c, nsh_c, nsc_l, nsh_l = norm1_g[l + 1][None], nxt_ctx[1], nxt_ctx[0], nxt_lat[1], nxt_lat[0]
        out_c = _combine(idx_all, _tile_starts(idx_c[..., 0], N_CTX), ye, hp, mod_ctx[5], ng, nsc_c, nsh_c,
                         SEQ, 0, CAP_CTX, final)
        out_l = _combine(idx_all, _tile_starts(idx_l[..., 0], N_LAT), ye, hs, mod_lat[5], ng, nsc_l, nsh_l,
                         DEC_SEQ, CAP_CTX, CAP_LAT, final)
        if final:
            y_prompt = out_c.reshape(BATCH, SEQ, D_MODEL)
            y_sample = out_l.reshape(DEC_BATCH, DEC_SEQ, D_MODEL)
        else:
            (hp, xn_p), (hs, xn_s) = out_c, out_l

    return (y_prompt, y_sample, *kv_out)
```

```python
import functools

import jax
import jax.numpy as jnp
import numpy as np
from jax import lax
from jax.experimental import pallas as pl
from jax.experimental.pallas import tpu as pltpu

D_MODEL = 1024
BATCH = 32
SEQ = 256
DEPTH = 2
DEC_BATCH = 2
DEC_SEQ = 1024
PAST_LEN = 256
GRID_W = 64
HEAD_DIM = 64
A_HEADS = 8
A_KV_HEADS = 2
A_GROUP = A_HEADS // A_KV_HEADS
B_HEADS = 8
C_WIDTH = 512
NA_WIN_R = 8
NA_WIN_C = 16
N_EXPERTS = 16
EXPERT_FF = 1024
CAP_FACTOR = 2
ROPE_THETA = 10000.0
EPS = 1e-6
N_MOD = 6

A_Q = A_HEADS * HEAD_DIM
A_KV = A_KV_HEADS * HEAD_DIM
B_W = B_HEADS * HEAD_DIM
GATE_W = 3 * D_MODEL
COL_A = 0
W_A = A_Q + 2 * A_KV
COL_B = COL_A + W_A
W_B = 3 * B_W
COL_C = COL_B + W_B
W_C = 3 * C_WIDTH
COL_G = COL_C + W_C

N_CTX = BATCH * SEQ
N_LAT = DEC_BATCH * DEC_SEQ
CAP_CTX = CAP_FACTOR * N_CTX // N_EXPERTS
CAP_LAT = CAP_FACTOR * N_LAT // N_EXPERTS
CAP_ALL = CAP_CTX + CAP_LAT
LAT_ROWS = DEC_SEQ // GRID_W
MASK_NEG = -1e30
ATT_SCALE = HEAD_DIM ** -0.5

LANES = 128
ROW_TILE = 8
assert D_MODEL == ROW_TILE * LANES

F32 = jnp.float32
BF16 = jnp.bfloat16
_NT = (((1,), (1,)), ((), ()))

_MIB = 1024 * 1024


def _params(sem, vmem_mib):
    return pltpu.CompilerParams(dimension_semantics=sem, vmem_limit_bytes=vmem_mib * _MIB)


def _adaln_kernel(cond_ref, w_ref, b_ref, o_ref):
    c = cond_ref[...]
    s = c * jax.nn.sigmoid(c)
    o_ref[...] = jnp.dot(s.astype(BF16), w_ref[...].astype(BF16), preferred_element_type=F32) + b_ref[...]


def _adaln(cond8, w_ada, b_ada):
    tn = 1536
    ncol = N_MOD * D_MODEL
    return pl.pallas_call(
        _adaln_kernel,
        grid=(DEPTH, ncol // tn),
        in_specs=[
            pl.BlockSpec((8, D_MODEL), lambda l, j: (0, 0)),
            pl.BlockSpec((None, D_MODEL, tn), lambda l, j: (l, 0, j)),
            pl.BlockSpec((None, 1, tn), lambda l, j: (l, 0, j)),
        ],
        out_specs=pl.BlockSpec((None, 8, tn), lambda l, j: (l, 0, j)),
        out_shape=jax.ShapeDtypeStruct((DEPTH, 8, ncol), F32),
        compiler_params=_params(("arbitrary", "arbitrary"), 40),
        name="adaln",
    )(cond8, w_ada, b_ada.reshape(DEPTH, 1, ncol))


def _norm_mod_kernel(x_ref, g_ref, sc_ref, sh_ref, o_ref):
    x = x_ref[...]
    y = x * lax.rsqrt(jnp.mean(x * x, axis=-1, keepdims=True) + EPS) * g_ref[...]
    o_ref[...] = (y * (1.0 + sc_ref[...]) + sh_ref[...]).astype(o_ref.dtype)


def _norm_mod(x, g, sc, sh, seq, out_dtype):
    n = x.shape[0]
    tm = 512
    per_batch = sc.shape[0] > 1
    mod_map = (lambda i: ((i * tm) // seq, 0, 0)) if per_batch else (lambda i: (0, 0, 0))
    return pl.pallas_call(
        _norm_mod_kernel,
        grid=(n // tm,),
        in_specs=[
            pl.BlockSpec((tm, D_MODEL), lambda i: (i, 0)),
            pl.BlockSpec((1, D_MODEL), lambda i: (0, 0)),
            pl.BlockSpec((None, 1, D_MODEL), mod_map),
            pl.BlockSpec((None, 1, D_MODEL), mod_map),
        ],
        out_specs=pl.BlockSpec((tm, D_MODEL), lambda i: (i, 0)),
        out_shape=jax.ShapeDtypeStruct((n, D_MODEL), out_dtype),
        compiler_params=_params(("arbitrary",), 32),
        name="norm_mod",
    )(x, g, sc, sh)


def _inproj_kernel(a_ref, w_ref, o_ref, wbf_ref):
    @pl.when(pl.program_id(1) == 0)
    def _():
        wbf_ref[...] = w_ref[...].astype(BF16)

    o_ref[...] = jnp.dot(a_ref[...], wbf_ref[...], preferred_element_type=F32).astype(o_ref.dtype)


def _inproj(xn, w_in, layer, col0, width, tn, out_dtype):
    n = xn.shape[0]
    tm = 2048
    joff = col0 // tn
    assert col0 % tn == 0 and width % tn == 0
    return pl.pallas_call(
        _inproj_kernel,
        grid=(width // tn, n // tm),
        in_specs=[
            pl.BlockSpec((tm, D_MODEL), lambda j, i: (i, 0)),
            pl.BlockSpec((None, D_MODEL, tn), lambda j, i: (layer, 0, j + joff)),
        ],
        out_specs=pl.BlockSpec((tm, tn), lambda j, i: (i, j)),
        out_shape=jax.ShapeDtypeStruct((n, width), out_dtype),
        scratch_shapes=[pltpu.VMEM((D_MODEL, tn), BF16)],
        compiler_params=_params(("arbitrary", "arbitrary"), 48),
        name="inproj",
    )(xn, w_in)


PAIR = 2 * HEAD_DIM


def _is_lo_half():
    return lax.broadcasted_iota(jnp.int32, (1, PAIR), 1) < HEAD_DIM


def _pair_rms(x, g2, is_lo):
    x2 = x * x
    s_lo = jnp.sum(jnp.where(is_lo, x2, 0.0), axis=-1, keepdims=True)
    s_hi = jnp.sum(jnp.where(is_lo, 0.0, x2), axis=-1, keepdims=True)
    ms = jnp.where(is_lo, s_lo, s_hi) * (1.0 / HEAD_DIM)
    return x * lax.rsqrt(ms + EPS) * g2


def _split_halves(x, is_lo):
    return jnp.where(is_lo, x, 0.0).astype(BF16), jnp.where(is_lo, 0.0, x).astype(BF16)


def _pad_lo(x):
    return jnp.concatenate([x, jnp.zeros_like(x)], axis=1)


def _pad_hi(x):
    return jnp.concatenate([jnp.zeros_like(x), x], axis=1)


def _qk(q, k):
    return lax.dot_general(q, k, _NT, preferred_element_type=F32)


def _softmax_pv(scores, values):
    m = scores[0].max(axis=-1, keepdims=True)
    for s in scores[1:]:
        m = jnp.maximum(m, s.max(axis=-1, keepdims=True))
    acc = None
    den = None
    for s, v in zip(scores, values):
        p = jnp.exp(s - m)
        d = p.sum(axis=-1, keepdims=True)
        o = jnp.dot(p.astype(BF16), v, preferred_element_type=F32)
        acc = o if acc is None else acc + o
        den = d if den is None else den + d
    return acc * (1.0 / den)


CTX_NB = 2


def _attn_ctx_kernel(ha_ref, hb_ref, qg_ref, kg_ref, *rest):
    oab_ref, *kv_refs = rest[-5:]
    prev_refs = rest[:-5]
    if prev_refs:
        for li in range(len(prev_refs) // 4):
            for dst, src in zip(kv_refs, prev_refs[4 * li:4 * li + 4]):
                dst[:, li] = src[...]
        ak_ref, av_ref, nk_ref, nv_ref = [r.at[:, DEPTH - 1] for r in kv_refs]
    else:
        ak_ref, av_ref, nk_ref, nv_ref = kv_refs
    is_lo = _is_lo_half()
    qg2 = qg_ref[...] * ATT_SCALE
    kg2 = kg_ref[...]
    for b in range(CTX_NB):
        rows = slice(b * SEQ, (b + 1) * SEQ)
        kp = _pair_rms(ha_ref[rows, A_Q:A_Q + PAIR], kg2, is_lo)
        vp = ha_ref[rows, A_Q + A_KV:A_Q + A_KV + PAIR]
        for kv in range(A_KV_HEADS):
            ak_ref[b, kv] = kp[:, kv * HEAD_DIM:(kv + 1) * HEAD_DIM]
            av_ref[b, kv] = vp[:, kv * HEAD_DIM:(kv + 1) * HEAD_DIM]
        kp_sw = pltpu.roll(kp, HEAD_DIM, axis=1)
        vp_sw = pltpu.roll(vp, HEAD_DIM, axis=1)
        qs = [_pair_rms(ha_ref[rows, c * PAIR:(c + 1) * PAIR], qg2, is_lo).astype(BF16)
              for c in range(A_HEADS // 2)]
        for kv in range(A_KV_HEADS):
            k_lo = jnp.where(is_lo, kp if kv == 0 else kp_sw, 0.0).astype(BF16)
            k_hi = jnp.where(is_lo, 0.0, kp_sw if kv == 0 else kp).astype(BF16)
            v_lo = jnp.where(is_lo, vp if kv == 0 else vp_sw, 0.0).astype(BF16)
            v_hi = jnp.where(is_lo, 0.0, vp_sw if kv == 0 else vp).astype(BF16)
            q_st = jnp.concatenate([qs[2 * kv], qs[2 * kv + 1]], axis=0)
            o = _softmax_pv([_qk(q_st, k_lo)], [v_lo]) + _softmax_pv([_qk(q_st, k_hi)], [v_hi])
            oab_ref[rows, (2 * kv) * PAIR:(2 * kv + 1) * PAIR] = o[0:SEQ].astype(BF16)
            oab_ref[rows, (2 * kv + 1) * PAIR:(2 * kv + 2) * PAIR] = o[SEQ:2 * SEQ].astype(BF16)
        for j in range(B_HEADS // 2):
            qp = (hb_ref[rows, j * PAIR:(j + 1) * PAIR] * ATT_SCALE).astype(BF16)
            kp_b = hb_ref[rows, B_W + j * PAIR:B_W + (j + 1) * PAIR]
            vp_b = hb_ref[rows, 2 * B_W + j * PAIR:2 * B_W + (j + 1) * PAIR]
            for u in range(2):
                nk_ref[b, 2 * j + u] = kp_b[:, u * HEAD_DIM:(u + 1) * HEAD_DIM]
                nv_ref[b, 2 * j + u] = vp_b[:, u * HEAD_DIM:(u + 1) * HEAD_DIM]
            k_lo, k_hi = _split_halves(kp_b, is_lo)
            v_lo, v_hi = _split_halves(vp_b, is_lo)
            o = _softmax_pv([_qk(qp, k_lo)], [v_lo]) + _softmax_pv([_qk(qp, k_hi)], [v_hi])
            oab_ref[rows, A_Q + j * PAIR:A_Q + (j + 1) * PAIR] = o.astype(BF16)


def _attn_ctx(h_a, h_b, qg2, kg2, prev_kv):
    rows = CTX_NB * SEQ
    layer_spec = lambda nh: pl.BlockSpec((CTX_NB, nh, SEQ, HEAD_DIM), lambda b: (b, 0, 0, 0))
    heads = (A_KV_HEADS, A_KV_HEADS, B_HEADS, B_HEADS)
    if prev_kv is None:
        prev_kv, prev_specs = (), []
        kv_specs = [layer_spec(nh) for nh in heads]
        kv_shapes = [jax.ShapeDtypeStruct((BATCH, nh, SEQ, HEAD_DIM), F32) for nh in heads]
    else:
        prev_specs = [layer_spec(heads[i % 4]) for i in range(len(prev_kv))]
        kv_specs = [pl.BlockSpec((CTX_NB, DEPTH, nh, SEQ, HEAD_DIM), lambda b: (b, 0, 0, 0, 0)) for nh in heads]
        kv_shapes = [jax.ShapeDtypeStruct((BATCH, DEPTH, nh, SEQ, HEAD_DIM), F32) for nh in heads]
    return pl.pallas_call(
        _attn_ctx_kernel,
        grid=(BATCH // CTX_NB,),
        in_specs=[
            pl.BlockSpec((rows, W_A), lambda b: (b, 0)),
            pl.BlockSpec((rows, W_B), lambda b: (b, 0)),
            pl.BlockSpec((1, PAIR), lambda b: (0, 0)),
            pl.BlockSpec((1, PAIR), lambda b: (0, 0)),
        ] + prev_specs,
        out_specs=[pl.BlockSpec((rows, A_Q + B_W), lambda b: (b, 0))] + kv_specs,
        out_shape=[jax.ShapeDtypeStruct((N_CTX, A_Q + B_W), BF16)] + kv_shapes,
        compiler_params=_params(("arbitrary",), 56),
        name="attn_ctx",
    )(h_a, h_b, qg2, kg2, *prev_kv)


LAT_TQ = 256
LAT_QROWS = LAT_TQ // GRID_W
LAT_NT = DEC_SEQ // LAT_TQ
NA_UNION_ROWS = 12
NA_UNION = NA_UNION_ROWS * GRID_W


def _na_union_row0(t):
    lo = min(max(r - NA_WIN_R // 2, 0) for r in range(t * LAT_QROWS, (t + 1) * LAT_QROWS))
    return min(min(lo, LAT_ROWS - NA_WIN_R), LAT_ROWS - NA_UNION_ROWS)


def _rope(x, cos, sin):
    lane = lax.broadcasted_iota(jnp.int32, x.shape, 1)
    nxt = pltpu.roll(x, x.shape[1] - 1, axis=1)
    prv = pltpu.roll(x, 1, axis=1)
    partner = jnp.where((lane & 1) == 0, nxt, prv)
    return x * cos + partner * sin


def _attn_lat_kernel(pick_ref, haq_ref, haf_ref, hbq_ref, hbf_ref, cak_ref, cav_ref, cnk_ref, cnv_ref,
                     qg_ref, kg_ref, cos_ref, sin_ref, toep_ref, oab_ref, kva_ref):
    t = pl.program_id(1)
    is_lo = _is_lo_half()
    row0 = pl.multiple_of(t * LAT_TQ, LAT_TQ)
    qg2 = qg_ref[...] * ATT_SCALE
    kg2 = kg_ref[...]

    @pl.when(t == 0)
    def _():
        kp = _rope(_pair_rms(haf_ref[:, A_Q:A_Q + PAIR].astype(F32), kg2, is_lo), cos_ref[...], sin_ref[...])
        vp = haf_ref[:, A_Q + A_KV:A_Q + A_KV + PAIR].astype(F32)
        kp_sw = pltpu.roll(kp, HEAD_DIM, axis=1)
        vp_sw = pltpu.roll(vp, HEAD_DIM, axis=1)
        for kv in range(A_KV_HEADS):
            kva_ref[4 * kv + 0] = jnp.where(is_lo, kp if kv == 0 else kp_sw, 0.0).astype(BF16)
            kva_ref[4 * kv + 1] = jnp.where(is_lo, 0.0, kp_sw if kv == 0 else kp).astype(BF16)
            kva_ref[4 * kv + 2] = jnp.where(is_lo, vp if kv == 0 else vp_sw, 0.0).astype(BF16)
            kva_ref[4 * kv + 3] = jnp.where(is_lo, 0.0, vp_sw if kv == 0 else vp).astype(BF16)

    cos_q = cos_ref[pl.ds(row0, LAT_TQ), :]
    sin_q = sin_ref[pl.ds(row0, LAT_TQ), :]
    qs = [_rope(_pair_rms(haq_ref[:, c * PAIR:(c + 1) * PAIR].astype(F32), qg2, is_lo), cos_q, sin_q).astype(BF16)
          for c in range(A_HEADS // 2)]
    for kv in range(A_KV_HEADS):
        kc = cak_ref[kv].astype(BF16)
        vc = cav_ref[kv].astype(BF16)
        q_st = jnp.concatenate([qs[2 * kv], qs[2 * kv + 1]], axis=0)
        o = (_softmax_pv([_qk(q_st, _pad_lo(kc)), _qk(q_st, kva_ref[4 * kv + 0])], [_pad_lo(vc), kva_ref[4 * kv + 2]])
             + _softmax_pv([_qk(q_st, _pad_hi(kc)), _qk(q_st, kva_ref[4 * kv + 1])], [_pad_hi(vc), kva_ref[4 * kv + 3]]))
        oab_ref[:, (2 * kv) * PAIR:(2 * kv + 1) * PAIR] = o[0:LAT_TQ].astype(BF16)
        oab_ref[:, (2 * kv + 1) * PAIR:(2 * kv + 2) * PAIR] = o[LAT_TQ:2 * LAT_TQ].astype(BF16)

    k0 = _na_union_row0(0) * GRID_W
    for tt in range(1, LAT_NT):
        k0 = jnp.where(t >= tt, _na_union_row0(tt) * GRID_W, k0)
    k0 = pl.multiple_of(k0, GRID_W)

    def bias(h):
        base = t * (LAT_QROWS * NA_UNION_ROWS)
        return jnp.concatenate(
            [jnp.concatenate([toep_ref[h, pick_ref[base + i * NA_UNION_ROWS + u]] for u in range(NA_UNION_ROWS)],
                             axis=1) for i in range(LAT_QROWS)], axis=0)

    for j in range(B_HEADS // 2):
        qp = (hbq_ref[:, j * PAIR:(j + 1) * PAIR] * ATT_SCALE).astype(BF16)
        k_lo, k_hi = _split_halves(hbf_ref[pl.ds(k0, NA_UNION), B_W + j * PAIR:B_W + (j + 1) * PAIR], is_lo)
        v_lo, v_hi = _split_halves(hbf_ref[pl.ds(k0, NA_UNION), 2 * B_W + j * PAIR:2 * B_W + (j + 1) * PAIR], is_lo)
        kc_lo = _pad_lo(cnk_ref[2 * j].astype(BF16))
        kc_hi = _pad_hi(cnk_ref[2 * j + 1].astype(BF16))
        vc_lo = _pad_lo(cnv_ref[2 * j].astype(BF16))
        vc_hi = _pad_hi(cnv_ref[2 * j + 1].astype(BF16))
        o = (_softmax_pv([_qk(qp, kc_lo), _qk(qp, k_lo) + bias(2 * j)], [vc_lo, v_lo])
             + _softmax_pv([_qk(qp, kc_hi), _qk(qp, k_hi) + bias(2 * j + 1)], [vc_hi, v_hi]))
        oab_ref[:, A_Q + j * PAIR:A_Q + (j + 1) * PAIR] = o.astype(BF16)


def _attn_lat(h_a, h_b, cak, cav, cnk, cnv, layer, qg2, kg2, cos128, sin128, toep):
    nt = LAT_NT
    cache_spec = lambda nh: pl.BlockSpec((None, None, nh, PAST_LEN, HEAD_DIM), lambda b, t, _: (b, layer, 0, 0, 0))
    const = lambda shape: pl.BlockSpec(shape, lambda b, t, _: (0,) * len(shape))
    grid_spec = pltpu.PrefetchScalarGridSpec(
        num_scalar_prefetch=1,
        grid=(DEC_BATCH, nt),
        in_specs=[
            pl.BlockSpec((LAT_TQ, W_A), lambda b, t, _: (b * nt + t, 0)),
            pl.BlockSpec((DEC_SEQ, W_A), lambda b, t, _: (b, 0)),
            pl.BlockSpec((LAT_TQ, W_B), lambda b, t, _: (b * nt + t, 0)),
            pl.BlockSpec((DEC_SEQ, W_B), lambda b, t, _: (b, 0)),
            cache_spec(A_KV_HEADS), cache_spec(A_KV_HEADS), cache_spec(B_HEADS), cache_spec(B_HEADS),
            const((1, PAIR)), const((1, PAIR)), const((DEC_SEQ, PAIR)), const((DEC_SEQ, PAIR)),
            const(toep.shape),
        ],
        out_specs=pl.BlockSpec((LAT_TQ, A_Q + B_W), lambda b, t, _: (b * nt + t, 0)),
        scratch_shapes=[pltpu.VMEM((4 * A_KV_HEADS, DEC_SEQ, PAIR), BF16)],
    )
    return pl.pallas_call(
        _attn_lat_kernel,
        grid_spec=grid_spec,
        out_shape=jax.ShapeDtypeStruct((N_LAT, A_Q + B_W), BF16),
        compiler_params=_params(("arbitrary", "arbitrary"), 56),
        name="attn_lat",
    )(jnp.asarray(_na_tile_picks().reshape(-1)), h_a, h_a, h_b, h_b, cak, cav, cnk, cnv, qg2, kg2, cos128, sin128,
      toep)


def _rope_tables():
    t = jnp.arange(DEC_SEQ)
    row = (t // GRID_W).astype(F32)
    col = (t % GRID_W).astype(F32)
    n_freq = HEAD_DIM // 4
    inv = ROPE_THETA ** (-jnp.arange(n_freq, dtype=F32) / n_freq)
    ang = jnp.concatenate([row[:, None] * inv, col[:, None] * inv], axis=-1)
    cos = jnp.repeat(jnp.cos(ang), 2, axis=-1)
    sign = jnp.tile(jnp.array([-1.0, 1.0], F32), HEAD_DIM // 2)
    sin = jnp.repeat(jnp.sin(ang), 2, axis=-1) * sign
    return jnp.tile(cos, (1, 2)), jnp.tile(sin, (1, 2))


NA_MASKED_TILE = 2 * NA_WIN_R - 1


def _na_toeplitz(rpb):
    col = np.arange(GRID_W)
    dc = np.clip(col[None, :] - col[:, None] + NA_WIN_C - 1, 0, 2 * NA_WIN_C - 2)
    onehot = (dc[..., None] == np.arange(2 * NA_WIN_C - 1)).astype(np.float32)
    c0 = np.clip(col - NA_WIN_C // 2, 0, GRID_W - NA_WIN_C)
    in_win = (col[None, :] >= c0[:, None]) & (col[None, :] < c0[:, None] + NA_WIN_C)
    toep = jnp.einsum('hdm,ckm->hdck', rpb, onehot, precision=lax.Precision.HIGHEST)
    toep = jnp.where(in_win[None, None], toep, MASK_NEG)
    return jnp.concatenate([toep, jnp.full((B_HEADS, 1, GRID_W, GRID_W), MASK_NEG, F32)], axis=1)


def _na_tile_picks():
    pick = np.full((LAT_NT, LAT_QROWS, NA_UNION_ROWS), NA_MASKED_TILE, np.int32)
    for t in range(LAT_NT):
        u0 = _na_union_row0(t)
        for i in range(LAT_QROWS):
            r = t * LAT_QROWS + i
            r0 = min(max(r - NA_WIN_R // 2, 0), LAT_ROWS - NA_WIN_R)
            for u in range(NA_UNION_ROWS):
                if r0 <= u0 + u < r0 + NA_WIN_R:
                    pick[t, i, u] = u0 + u - r + NA_WIN_R - 1
    return pick


def _conv_kernel(hc_ref, w_ref, o_ref, *, seq):
    bg = hc_ref[:, 0:C_WIDTH].astype(F32)
    u = hc_ref[:, C_WIDTH:2 * C_WIDTH].astype(F32) * hc_ref[:, 2 * C_WIDTH:3 * C_WIDTH].astype(F32)
    rows = u.shape[0]
    assert seq & (seq - 1) == 0
    pos = lax.broadcasted_iota(jnp.int32, u.shape, 0) & (seq - 1)
    u_prev = jnp.where(pos == 0, 0.0, pltpu.roll(u, 1, axis=0))
    u_next = jnp.where(pos == seq - 1, 0.0, pltpu.roll(u, rows - 1, axis=0))
    y = u_prev * w_ref[0:1, :] + u * w_ref[1:2, :] + u_next * w_ref[2:3, :]
    o_ref[...] = (bg * y).astype(o_ref.dtype)


def _conv(h_c, conv_w, seq):
    n = h_c.shape[0]
    tm = 1024
    return pl.pallas_call(
        functools.partial(_conv_kernel, seq=seq),
        grid=(n // tm,),
        in_specs=[pl.BlockSpec((tm, W_C), lambda i: (i, 0)), pl.BlockSpec((3, C_WIDTH), lambda i: (0, 0))],
        out_specs=pl.BlockSpec((tm, C_WIDTH), lambda i: (i, 0)),
        out_shape=jax.ShapeDtypeStruct((n, C_WIDTH), BF16),
        compiler_params=_params(("arbitrary",), 40),
        name="short_conv",
    )(h_c, conv_w)


def _mix_kernel(x_ref, oab_ref, oc_ref, hg_ref, wa_ref, wb_ref, wc_ref, wo_ref, g1_ref, n2g_ref, sc2_ref, sh2_ref,
                wrt_ref, xo_ref, xn2_ref, pt_ref):
    ga = jax.nn.sigmoid(hg_ref[:, 0:D_MODEL].astype(F32))
    gb = jax.nn.sigmoid(hg_ref[:, D_MODEL:2 * D_MODEL].astype(F32))
    gc = jax.nn.sigmoid(hg_ref[:, 2 * D_MODEL:3 * D_MODEL].astype(F32))
    pa = jnp.dot(oab_ref[:, 0:A_Q], wa_ref[...], preferred_element_type=F32)
    pb = jnp.dot(oab_ref[:, A_Q:A_Q + B_W], wb_ref[...], preferred_element_type=F32)
    pc = jnp.dot(oc_ref[...], wc_ref[...], preferred_element_type=F32)
    merged = ga * pa + gb * pb + gc * pc
    xo = x_ref[...] + g1_ref[...] * jnp.dot(merged.astype(BF16), wo_ref[...], preferred_element_type=F32)
    xo_ref[...] = xo
    y = xo * lax.rsqrt(jnp.mean(xo * xo, axis=-1, keepdims=True) + EPS) * n2g_ref[...]
    xn2 = y * (1.0 + sc2_ref[...]) + sh2_ref[...]
    rows = xn2.shape[0]
    for j in range(ROW_TILE):
        xn2_ref[pl.ds(j, rows, stride=ROW_TILE), :] = xn2[:, j * LANES:(j + 1) * LANES]
    logits_t = lax.dot_general(wrt_ref[...], xn2.astype(BF16), _NT, preferred_element_type=F32)
    z = jnp.exp(logits_t - logits_t.max(axis=0, keepdims=True))
    pt_ref[...] = z / z.sum(axis=0, keepdims=True)


def _mix(x, oab, oc, h_g, wa, wb, wc, wo, g1, n2g, sc2, sh2, wrt, seq):
    n = x.shape[0]
    tm = 512
    per_batch = g1.shape[0] > 1
    mod_map = (lambda i: ((i * tm) // seq, 0, 0)) if per_batch else (lambda i: (0, 0, 0))
    mod_spec = pl.BlockSpec((None, 1, D_MODEL), mod_map)
    full = lambda a: pl.BlockSpec(a.shape, lambda i: (0,) * a.ndim)
    row = lambda w: pl.BlockSpec((tm, w), lambda i: (i, 0))
    return pl.pallas_call(
        _mix_kernel,
        grid=(n // tm,),
        in_specs=[row(D_MODEL), row(A_Q + B_W), row(C_WIDTH), row(GATE_W), full(wa), full(wb), full(wc), full(wo),
                  mod_spec, full(n2g), mod_spec, mod_spec, full(wrt)],
        out_specs=[row(D_MODEL), pl.BlockSpec((tm * ROW_TILE, LANES), lambda i: (i, 0)),
                   pl.BlockSpec((N_EXPERTS, tm), lambda i: (0, i))],
        out_shape=[jax.ShapeDtypeStruct((n, D_MODEL), F32), jax.ShapeDtypeStruct((n * ROW_TILE, LANES), F32),
                   jax.ShapeDtypeStruct((N_EXPERTS, n), F32)],
        compiler_params=_params(("arbitrary",), 56),
        name="mix",
    )(x, oab, oc, h_g, wa, wb, wc, wo, g1, n2g, sc2, sh2, wrt)


MOE_TF = 512
MOE_ROW_CHUNKS = ((0, 640), (640, 640))
MOE_ISSUE_UNROLL = 8


def _moe_issue_rows(idx_ref, idx0, src_hbm, dst_ref, dst_row0, count, sem):
    def body(i, carry):
        for u in range(MOE_ISSUE_UNROLL):
            r = i * MOE_ISSUE_UNROLL + u
            tok = idx_ref[idx0 + r]
            src = src_hbm.at[pl.ds(pl.multiple_of(tok * ROW_TILE, ROW_TILE), ROW_TILE)]
            dst = dst_ref.at[pl.ds(pl.multiple_of((dst_row0 + r) * ROW_TILE, ROW_TILE), ROW_TILE)]
            pltpu.make_async_copy(src, dst, sem).start(priority=u % 2)
        return carry

    lax.fori_loop(0, count // MOE_ISSUE_UNROLL, body, 0)


def _moe_kernel(idx_ref, xc_hbm, xl_hbm, aff_ref, wg_ref, wu_ref, wd_ref, ye_ref, xe_ref, sem):
    e = pl.program_id(0)
    f = pl.program_id(1)
    slot = e % 2

    def gather(expert, s):
        base = expert * CAP_ALL
        _moe_issue_rows(idx_ref, base, xc_hbm, xe_ref.at[s], 0, CAP_CTX, sem.at[s])
        _moe_issue_rows(idx_ref, base + CAP_CTX, xl_hbm, xe_ref.at[s], CAP_CTX, CAP_LAT, sem.at[s])

    @pl.when(f == 0)
    def _():
        @pl.when(e == 0)
        def _():
            gather(0, 0)

        pltpu.make_async_copy(xc_hbm.at[pl.ds(0, CAP_ALL * ROW_TILE)], xe_ref.at[slot], sem.at[slot]).wait()

        @pl.when(e + 1 < N_EXPERTS)
        def _():
            gather(e + 1, 1 - slot)

    wg = wg_ref[...].astype(BF16)
    wu = wu_ref[...].astype(BF16)
    wd = wd_ref[...].astype(BF16)
    for r0, rn in MOE_ROW_CHUNKS:
        xe = jnp.concatenate([xe_ref[slot, pl.ds(r0 * ROW_TILE + j, rn, stride=ROW_TILE), :]
                              for j in range(ROW_TILE)], axis=1).astype(BF16)
        hg = jnp.dot(xe, wg, preferred_element_type=F32)
        hu = jnp.dot(xe, wu, preferred_element_type=F32)
        act = (hg * jax.nn.sigmoid(hg) * hu).astype(BF16)
        part = jnp.dot(act, wd, preferred_element_type=F32) * aff_ref[r0:r0 + rn, :]

        @pl.when(f == 0)
        def _():
            ye_ref[r0:r0 + rn, :] = part

        @pl.when(f != 0)
        def _():
            ye_ref[r0:r0 + rn, :] += part


def _moe(idx_all, xn2_ctx, xn2_lat, aff_all, w_gate_e, w_up_e, w_down_e, layer):
    nf = EXPERT_FF // MOE_TF
    grid_spec = pltpu.PrefetchScalarGridSpec(
        num_scalar_prefetch=1,
        grid=(N_EXPERTS, nf),
        in_specs=[
            pl.BlockSpec(memory_space=pl.ANY),
            pl.BlockSpec(memory_space=pl.ANY),
            pl.BlockSpec((None, CAP_ALL, 1), lambda e, f, idx: (e, 0, 0)),
            pl.BlockSpec((None, None, D_MODEL, MOE_TF), lambda e, f, idx: (layer, e, 0, f)),
            pl.BlockSpec((None, None, D_MODEL, MOE_TF), lambda e, f, idx: (layer, e, 0, f)),
            pl.BlockSpec((None, None, MOE_TF, D_MODEL), lambda e, f, idx: (layer, e, f, 0)),
        ],
        out_specs=pl.BlockSpec((None, CAP_ALL, D_MODEL), lambda e, f, idx: (e, 0, 0)),
        scratch_shapes=[pltpu.VMEM((2, CAP_ALL * ROW_TILE, LANES), F32), pltpu.SemaphoreType.DMA((2,))],
    )
    return pl.pallas_call(
        _moe_kernel,
        grid_spec=grid_spec,
        out_shape=jax.ShapeDtypeStruct((N_EXPERTS, CAP_ALL, D_MODEL), F32),
        compiler_params=_params(("arbitrary", "arbitrary"), 56),
        name="moe_experts",
    )(idx_all, xn2_ctx, xn2_lat, aff_all, w_gate_e, w_up_e, w_down_e)


POS_SPLIT = 32


def _prefix_count(mask_bf, tri_lane, tri_blk_strict):
    within = jnp.dot(mask_bf, tri_lane, preferred_element_type=F32)
    bs = jnp.broadcast_to(within[:, LANES - 1:LANES], within.shape).astype(BF16)
    return within + jnp.dot(tri_blk_strict, bs, preferred_element_type=F32)


def _route_kernel(p_ref, idx_ref, aff_ref, *, cap):
    p_all = p_ref[...]
    nb = p_all.shape[1]
    capf = float(cap)

    def as_float(word):
        return lax.bitcast_convert_type(word, F32)

    def bisect(_, c):
        lo, hi = c
        mid = lo + ((hi - lo) >> 1)
        ok = jnp.sum(jnp.where(p_all >= as_float(mid), 1.0, 0.0), axis=(1, 2), keepdims=True) >= capf
        return jnp.where(ok, mid, lo), jnp.where(ok, hi, mid)

    lo0 = jnp.zeros((N_EXPERTS, 1, 1), jnp.int32)
    hi0 = jnp.full((N_EXPERTS, 1, 1), 0x7F800000, jnp.int32)
    thr_all, _ = lax.fori_loop(0, 31, bisect, (lo0, hi0))
    thr_val = as_float(thr_all)

    li = lax.broadcasted_iota(jnp.int32, (LANES, LANES), 0)
    lj = lax.broadcasted_iota(jnp.int32, (LANES, LANES), 1)
    tri_lane = jnp.where(li <= lj, 1.0, 0.0).astype(BF16)
    bi = lax.broadcasted_iota(jnp.int32, (nb, nb), 0)
    bj = lax.broadcasted_iota(jnp.int32, (nb, nb), 1)
    tri_blk_strict = jnp.where(bj < bi, 1.0, 0.0).astype(BF16)
    tri_blk_incl = jnp.where(bi <= bj, 1.0, 0.0).astype(BF16)
    s_col = lax.broadcasted_iota(jnp.int32, (cap, 1), 0).astype(F32)
    blk_id = lax.broadcasted_iota(jnp.int32, (1, nb), 1).astype(F32)
    lane_id = lax.broadcasted_iota(jnp.int32, (1, LANES), 1).astype(F32)
    ones_rows = jnp.ones((8, LANES), BF16)

    for e in range(N_EXPERTS):
        p = p_all[e]
        thr = thr_val[e]
        gt = p > thr
        eq = p == thr
        need = capf - jnp.sum(jnp.where(gt, 1.0, 0.0), keepdims=True)
        tie_rank = _prefix_count(jnp.where(eq, 1.0, 0.0).astype(BF16), tri_lane, tri_blk_strict)
        sel = jnp.where(gt | (eq & (tie_rank <= need)), 1.0, 0.0).astype(BF16)
        pos = _prefix_count(sel, tri_lane, tri_blk_strict)

        bs_row = lax.dot_general(ones_rows, sel, _NT, preferred_element_type=F32)
        bp_row = jnp.dot(bs_row.astype(BF16), tri_blk_incl, preferred_element_type=F32)[0:1, :]
        blk = jnp.sum(jnp.where(bp_row <= s_col, 1.0, 0.0), axis=-1, keepdims=True)
        onehot = jnp.where(blk == blk_id, 1.0, 0.0).astype(BF16)
        pos_hi = jnp.floor(pos * (1.0 / POS_SPLIT))
        pos_lo = pos - pos_hi * POS_SPLIT
        pos_row = (jnp.dot(onehot, pos_hi.astype(BF16), preferred_element_type=F32) * POS_SPLIT
                   + jnp.dot(onehot, pos_lo.astype(BF16), preferred_element_type=F32))
        lane = jnp.sum(jnp.where(pos_row <= s_col, 1.0, 0.0), axis=-1, keepdims=True)
        idx_ref[e] = (blk * LANES + lane).astype(jnp.int32)

        p1 = p.astype(BF16)
        r1 = p - p1.astype(F32)
        p2 = r1.astype(BF16)
        p3 = (r1 - p2.astype(F32)).astype(BF16)
        p_row = (jnp.dot(onehot, p1, preferred_element_type=F32) + jnp.dot(onehot, p2, preferred_element_type=F32)
                 + jnp.dot(onehot, p3, preferred_element_type=F32))
        aff_ref[e] = jnp.sum(jnp.where(lane_id == lane, p_row, 0.0), axis=-1, keepdims=True)


def _route(probs_t, cap):
    nb = probs_t.shape[1] // LANES
    slot = pl.BlockSpec((N_EXPERTS, cap, 1), lambda i: (0, 0, 0))
    return pl.pallas_call(
        functools.partial(_route_kernel, cap=cap),
        grid=(1,),
        in_specs=[pl.BlockSpec((N_EXPERTS, nb, LANES), lambda i: (0, 0, 0))],
        out_specs=[slot, slot],
        out_shape=[jax.ShapeDtypeStruct((N_EXPERTS, cap, 1), jnp.int32),
                   jax.ShapeDtypeStruct((N_EXPERTS, cap, 1), F32)],
        compiler_params=_params(("arbitrary",), 56),
        name="route",
    )(probs_t.reshape(N_EXPERTS, nb, LANES))


CMB_TT = 1024
CMB_CH = 256
CMB_GROUP = 8
CMB_EPI_ROWS = 256


def _combine_kernel(idx_ref, st_ref, ye_hbm, x_ref, g2_ref, ng_ref, sc_ref, sh_ref, *rest, off, cap, nt, final):
    if final:
        y_ref, acc_ref, buf_ref, sem = rest
    else:
        xo_ref, xn_ref, acc_ref, buf_ref, sem = rest
    k = pl.program_id(0)
    t0 = k * CMB_TT
    acc_ref[...] = jnp.zeros_like(acc_ref)

    def slot_range(e):
        s0 = st_ref[e * (nt + 1) + k]
        s1 = st_ref[e * (nt + 1) + k + 1]
        return s0, s1, jnp.minimum(s0 & -CMB_GROUP, cap - CMB_CH)

    def chunk_copy(e, start):
        src = ye_hbm.at[e, pl.ds(pl.multiple_of(off + start, CMB_GROUP), CMB_CH)]
        return pltpu.make_async_copy(src, buf_ref.at[e], sem.at[e])

    def accumulate(e, start, lo, hi):
        def body(i, carry):
            r0 = pl.multiple_of(i * CMB_GROUP, CMB_GROUP)
            tile = buf_ref[e, pl.ds(r0, CMB_GROUP), :]
            dst = []
            for u in range(CMB_GROUP):
                slot = start + r0 + u
                tok = idx_ref[e * CAP_ALL + off + slot]
                dst.append(jnp.where((slot >= lo) & (slot < hi), tok - t0, CMB_TT))
            rows = [acc_ref[pl.ds(d, 1), :] for d in dst]
            for u in range(CMB_GROUP):
                acc_ref[pl.ds(dst[u], 1), :] = rows[u] + tile[u:u + 1, :]
            return carry

        lax.fori_loop((lo - start) // CMB_GROUP, (hi - start + CMB_GROUP - 1) // CMB_GROUP, body, 0)

    for e in range(N_EXPERTS):
        chunk_copy(e, slot_range(e)[2]).start()
    for e in range(N_EXPERTS):
        s0, s1, a = slot_range(e)
        chunk_copy(e, a).wait()
        accumulate(e, a, s0, jnp.minimum(s1, a + CMB_CH))

        def extra(j, carry, e=e, s1=s1, a=a):
            lo = a + CMB_CH * (j + 1)
            start = jnp.minimum(lo, cap - CMB_CH)
            cp = chunk_copy(e, start)
            cp.start()
            cp.wait()
            accumulate(e, start, lo, jnp.minimum(s1, lo + CMB_CH))
            return carry

        lax.fori_loop(0, jnp.maximum(s1 - a - 1, 0) // CMB_CH, extra, 0)

    for r0 in range(0, CMB_TT, CMB_EPI_ROWS):
        rs = slice(r0, r0 + CMB_EPI_ROWS)
        xo = x_ref[rs, :] + g2_ref[...] * acc_ref[rs, :]
        y = xo * lax.rsqrt(jnp.mean(xo * xo, axis=-1, keepdims=True) + EPS) * ng_ref[...]
        if final:
            y_ref[rs, :] = y
        else:
            xo_ref[rs, :] = xo
            xn_ref[rs, :] = (y * (1.0 + sc_ref[...]) + sh_ref[...]).astype(xn_ref.dtype)


def _combine(idx_all, starts, ye, x, g2, ng, sc, sh, seq, off, cap, final):
    n = x.shape[0]
    nt = n // CMB_TT
    per_batch = g2.shape[0] > 1
    mod_map = ((lambda i, *_: ((i * CMB_TT) // seq, 0, 0)) if per_batch else (lambda i, *_: (0, 0, 0)))
    mod_spec = pl.BlockSpec((None, 1, D_MODEL), mod_map)
    row = pl.BlockSpec((CMB_TT, D_MODEL), lambda i, *_: (i, 0))
    vec = pl.BlockSpec((1, D_MODEL), lambda i, *_: (0, 0))
    if final:
        out_specs, out_shape = row, jax.ShapeDtypeStruct((n, D_MODEL), F32)
    else:
        out_specs = [row, row]
        out_shape = [jax.ShapeDtypeStruct((n, D_MODEL), F32), jax.ShapeDtypeStruct((n, D_MODEL), BF16)]
    grid_spec = pltpu.PrefetchScalarGridSpec(
        num_scalar_prefetch=2,
        grid=(nt,),
        in_specs=[pl.BlockSpec(memory_space=pl.ANY), row, mod_spec, vec, mod_spec, mod_spec],
        out_specs=out_specs,
        scratch_shapes=[pltpu.VMEM((CMB_TT + CMB_GROUP, D_MODEL), F32),
                        pltpu.VMEM((N_EXPERTS, CMB_CH, D_MODEL), F32),
                        pltpu.SemaphoreType.DMA((N_EXPERTS,))],
    )
    return pl.pallas_call(
        functools.partial(_combine_kernel, off=off, cap=cap, nt=nt, final=final),
        grid_spec=grid_spec,
        out_shape=out_shape,
        compiler_params=_params(("arbitrary",), 60),
        name="combine",
    )(idx_all, starts.reshape(-1), ye, x, g2, ng, sc, sh)


def _tile_starts(idx_sorted, n):
    bounds = jnp.arange(n // CMB_TT + 1, dtype=jnp.int32) * CMB_TT
    return jnp.sum(idx_sorted[:, :, None] < bounds[None, None, :], axis=1, dtype=jnp.int32)


def kernel(x_prompt, x_sample, cache_attn_k, cache_attn_v, cache_na_k, cache_na_v, c, c_ctx, w_ada, b_ada, norm1_g,
           w_in, q_norm_g, k_norm_g, na_rpb, conv_w, w_proj_a, w_proj_b, w_proj_c, w_out, norm2_g, w_router,
           w_gate_e, w_up_e, w_down_e, final_g):
    cond8 = jnp.zeros((8, D_MODEL), F32).at[0].set(c_ctx).at[1:1 + DEC_BATCH].set(c)
    mods = _adaln(cond8, w_ada, b_ada).reshape(DEPTH, 8, N_MOD, D_MODEL)
    cos128, sin128 = _rope_tables()

    hp = x_prompt.reshape(N_CTX, D_MODEL)
    hs = x_sample.reshape(N_LAT, D_MODEL)
    kv_layers = []
    for l in range(DEPTH):
        mod_ctx = [mods[l, 0:1, i][:, None, :] for i in range(N_MOD)]
        mod_lat = [mods[l, 1:1 + DEC_BATCH, i][:, None, :] for i in range(N_MOD)]
        n1g = norm1_g[l][None]
        n2g = norm2_g[l][None]
        qg2 = jnp.tile(q_norm_g[l][None], (1, 2))
        kg2 = jnp.tile(k_norm_g[l][None], (1, 2))
        wa, wb, wc, wo = (w_proj_a[l].astype(BF16), w_proj_b[l].astype(BF16), w_proj_c[l].astype(BF16),
                          w_out[l].astype(BF16))
        wrt = w_router[l].T.astype(BF16)
        toep = _na_toeplitz(na_rpb[l])

        def sublayer1(x, xn, mod, seq, is_ctx):
            _, _, g1, sh2, sc2, _ = mod
            kv_dtype = F32 if is_ctx else BF16
            h_a = _inproj(xn, w_in, l, COL_A, W_A, 768, kv_dtype)
            h_b = _inproj(xn, w_in, l, COL_B, W_B, 768, kv_dtype)
            h_c = _inproj(xn, w_in, l, COL_C, W_C, 768, BF16)
            h_g = _inproj(xn, w_in, l, COL_G, GATE_W, 768, BF16)
            kv = None
            if is_ctx:
                oab, *kv = _attn_ctx(h_a, h_b, qg2, kg2, kv_layers if l == DEPTH - 1 else None)
            else:
                oab = _attn_lat(h_a, h_b, cache_attn_k, cache_attn_v, cache_na_k, cache_na_v, l, qg2, kg2,
                                cos128, sin128, toep)
            oc = _conv(h_c, conv_w[l], seq)
            xo, xn2, probs_t = _mix(x, oab, oc, h_g, wa, wb, wc, wo, g1, n2g, sc2, sh2, wrt, seq)
            return xo, xn2, probs_t, kv

        if l == 0:
            xn_p = _norm_mod(hp, n1g, mod_ctx[1], mod_ctx[0], SEQ, BF16)
            xn_s = _norm_mod(hs, n1g, mod_lat[1], mod_lat[0], DEC_SEQ, BF16)
        hp, xn2_c, pt_c, kv_out = sublayer1(hp, xn_p, mod_ctx, SEQ, True)
        kv_layers = kv_layers + list(kv_out)
        hs, xn2_l, pt_l, _ = sublayer1(hs, xn_s, mod_lat, DEC_SEQ, False)

        idx_c, aff_c = _route(pt_c, CAP_CTX)
        idx_l, aff_l = _route(pt_l, CAP_LAT)
        idx_all = jnp.concatenate([idx_c, idx_l], axis=1).reshape(-1)
        aff_all = jnp.concatenate([aff_c, aff_l], axis=1)
        ye = _moe(idx_all, xn2_c, xn2_l, aff_all, w_gate_e, w_up_e, w_down_e, l)

        final = l == DEPTH - 1
        if final:
            ng, nsc_c, nsh_c, nsc_l, nsh_l = final_g[None], mod_ctx[1], mod_ctx[0], mod_lat[1], mod_lat[0]
        else:
            nxt_ctx = [mods[l + 1, 0:1, i][:, None, :] for i in range(2)]
            nxt_lat = [mods[l + 1, 1:1 + DEC_BATCH, i][:, None, :] for i in range(2)]
            ng, nsc_c, nsh_c, nsc_l, nsh_l = norm1_g[l + 1][None], nxt_ctx[1], nxt_ctx[0], nxt_lat[1], nxt_lat[0]
        out_c = _combine(idx_all, _tile_starts(idx_c[..., 0], N_CTX), ye, hp, mod_ctx[5], ng, nsc_c, nsh_c,
                         SEQ, 0, CAP_CTX, final)
        out_l = _combine(idx_all, _tile_starts(idx_l[..., 0], N_LAT), ye, hs, mod_lat[5], ng, nsc_l, nsh_l,
                         DEC_SEQ, CAP_CTX, CAP_LAT, final)
        if final:
            y_prompt = out_c.reshape(BATCH, SEQ, D_MODEL)
            y_sample = out_l.reshape(DEC_BATCH, DEC_SEQ, D_MODEL)
        else:
            (hp, xn_p), (hs, xn_s) = out_c, out_l

    return (y_prompt, y_sample, *kv_out)
```

```python
import functools

import jax
import jax.numpy as jnp
import numpy as np
from jax import lax
from jax.experimental import pallas as pl
from jax.experimental.pallas import tpu as pltpu

D_MODEL = 1024
BATCH = 32
SEQ = 256
DEPTH = 2
DEC_BATCH = 2
DEC_SEQ = 1024
PAST_LEN = 256
GRID_W = 64
HEAD_DIM = 64
A_HEADS = 8
A_KV_HEADS = 2
A_GROUP = A_HEADS // A_KV_HEADS
B_HEADS = 8
C_WIDTH = 512
NA_WIN_R = 8
NA_WIN_C = 16
N_EXPERTS = 16
EXPERT_FF = 1024
CAP_FACTOR = 2
ROPE_THETA = 10000.0
EPS = 1e-6
N_MOD = 6

A_Q = A_HEADS * HEAD_DIM
A_KV = A_KV_HEADS * HEAD_DIM
B_W = B_HEADS * HEAD_DIM
GATE_W = 3 * D_MODEL
COL_A = 0
W_A = A_Q + 2 * A_KV
COL_B = COL_A + W_A
W_B = 3 * B_W
COL_C = COL_B + W_B
W_C = 3 * C_WIDTH
COL_G = COL_C + W_C

N_CTX = BATCH * SEQ
N_LAT = DEC_BATCH * DEC_SEQ
CAP_CTX = CAP_FACTOR * N_CTX // N_EXPERTS
CAP_LAT = CAP_FACTOR * N_LAT // N_EXPERTS
CAP_ALL = CAP_CTX + CAP_LAT
LAT_ROWS = DEC_SEQ // GRID_W
MASK_NEG = -1e30
ATT_SCALE = HEAD_DIM ** -0.5

LANES = 128
ROW_TILE = 8
assert D_MODEL == ROW_TILE * LANES

F32 = jnp.float32
BF16 = jnp.bfloat16
_NT = (((1,), (1,)), ((), ()))

_MIB = 1024 * 1024


def _params(sem, vmem_mib):
    return pltpu.CompilerParams(dimension_semantics=sem, vmem_limit_bytes=vmem_mib * _MIB)


def _adaln_kernel(cond_ref, w_ref, b_ref, o_ref):
    c = cond_ref[...]
    s = c * jax.nn.sigmoid(c)
    o_ref[...] = jnp.dot(s.astype(BF16), w_ref[...].astype(BF16), preferred_element_type=F32) + b_ref[...]


def _adaln(cond8, w_ada, b_ada):
    tn = 1536
    ncol = N_MOD * D_MODEL
    return pl.pallas_call(
        _adaln_kernel,
        grid=(DEPTH, ncol // tn),
        in_specs=[
            pl.BlockSpec((8, D_MODEL), lambda l, j: (0, 0)),
            pl.BlockSpec((None, D_MODEL, tn), lambda l, j: (l, 0, j)),
            pl.BlockSpec((None, 1, tn), lambda l, j: (l, 0, j)),
        ],
        out_specs=pl.BlockSpec((None, 8, tn), lambda l, j: (l, 0, j)),
        out_shape=jax.ShapeDtypeStruct((DEPTH, 8, ncol), F32),
        compiler_params=_params(("arbitrary", "arbitrary"), 40),
        name="adaln",
    )(cond8, w_ada, b_ada.reshape(DEPTH, 1, ncol))


def _norm_mod_kernel(x_ref, g_ref, sc_ref, sh_ref, o_ref):
    x = x_ref[...]
    y = x * lax.rsqrt(jnp.mean(x * x, axis=-1, keepdims=True) + EPS) * g_ref[...]
    o_ref[...] = (y * (1.0 + sc_ref[...]) + sh_ref[...]).astype(o_ref.dtype)


def _norm_mod(x, g, sc, sh, seq, out_dtype):
    n = x.shape[0]
    tm = 512
    per_batch = sc.shape[0] > 1
    mod_map = (lambda i: ((i * tm) // seq, 0, 0)) if per_batch else (lambda i: (0, 0, 0))
    return pl.pallas_call(
        _norm_mod_kernel,
        grid=(n // tm,),
        in_specs=[
            pl.BlockSpec((tm, D_MODEL), lambda i: (i, 0)),
            pl.BlockSpec((1, D_MODEL), lambda i: (0, 0)),
            pl.BlockSpec((None, 1, D_MODEL), mod_map),
            pl.BlockSpec((None, 1, D_MODEL), mod_map),
        ],
        out_specs=pl.BlockSpec((tm, D_MODEL), lambda i: (i, 0)),
        out_shape=jax.ShapeDtypeStruct((n, D_MODEL), out_dtype),
        compiler_params=_params(("arbitrary",), 32),
        name="norm_mod",
    )(x, g, sc, sh)


def _inproj_kernel(a_ref, w_ref, o_ref, wbf_ref):
    @pl.when(pl.program_id(1) == 0)
    def _():
        wbf_ref[...] = w_ref[...].astype(BF16)

    o_ref[...] = jnp.dot(a_ref[...], wbf_ref[...], preferred_element_type=F32).astype(o_ref.dtype)


def _inproj(xn, w_in, layer, col0, width, tn, out_dtype):
    n = xn.shape[0]
    tm = 2048
    joff = col0 // tn
    assert col0 % tn == 0 and width % tn == 0
    return pl.pallas_call(
        _inproj_kernel,
        grid=(width // tn, n // tm),
        in_specs=[
            pl.BlockSpec((tm, D_MODEL), lambda j, i: (i, 0)),
            pl.BlockSpec((None, D_MODEL, tn), lambda j, i: (layer, 0, j + joff)),
        ],
        out_specs=pl.BlockSpec((tm, tn), lambda j, i: (i, j)),
        out_shape=jax.ShapeDtypeStruct((n, width), out_dtype),
        scratch_shapes=[pltpu.VMEM((D_MODEL, tn), BF16)],
        compiler_params=_params(("arbitrary", "arbitrary"), 48),
        name="inproj",
    )(xn, w_in)


PAIR = 2 * HEAD_DIM


def _is_lo_half():
    return lax.broadcasted_iota(jnp.int32, (1, PAIR), 1) < HEAD_DIM


def _pair_rms(x, g2, is_lo):
    x2 = x * x
    s_lo = jnp.sum(jnp.where(is_lo, x2, 0.0), axis=-1, keepdims=True)
    s_hi = jnp.sum(jnp.where(is_lo, 0.0, x2), axis=-1, keepdims=True)
    ms = jnp.where(is_lo, s_lo, s_hi) * (1.0 / HEAD_DIM)
    return x * lax.rsqrt(ms + EPS) * g2


def _split_halves(x, is_lo):
    return jnp.where(is_lo, x, 0.0).astype(BF16), jnp.where(is_lo, 0.0, x).astype(BF16)


def _pad_lo(x):
    return jnp.concatenate([x, jnp.zeros_like(x)], axis=1)


def _pad_hi(x):
    return jnp.concatenate([jnp.zeros_like(x), x], axis=1)


def _qk(q, k):
    return lax.dot_general(q, k, _NT, preferred_element_type=F32)


def _softmax_pv(scores, values):
    m = scores[0].max(axis=-1, keepdims=True)
    for s in scores[1:]:
        m = jnp.maximum(m, s.max(axis=-1, keepdims=True))
    acc = None
    den = None
    for s, v in zip(scores, values):
        p = jnp.exp(s - m)
        d = p.sum(axis=-1, keepdims=True)
        o = jnp.dot(p.astype(BF16), v, preferred_element_type=F32)
        acc = o if acc is None else acc + o
        den = d if den is None else den + d
    return acc * (1.0 / den)


CTX_NB = 2


def _attn_ctx_kernel(ha_ref, hb_ref, qg_ref, kg_ref, *rest):
    oab_ref, *kv_refs = rest[-5:]
    prev_refs = rest[:-5]
    if prev_refs:
        for li in range(len(prev_refs) // 4):
            for dst, src in zip(kv_refs, prev_refs[4 * li:4 * li + 4]):
                dst[:, li] = src[...]
        ak_ref, av_ref, nk_ref, nv_ref = [r.at[:, DEPTH - 1] for r in kv_refs]
    else:
        ak_ref, av_ref, nk_ref, nv_ref = kv_refs
    is_lo = _is_lo_half()
    qg2 = qg_ref[...] * ATT_SCALE
    kg2 = kg_ref[...]
    for b in range(CTX_NB):
        rows = slice(b * SEQ, (b + 1) * SEQ)
        kp = _pair_rms(ha_ref[rows, A_Q:A_Q + PAIR], kg2, is_lo)
        vp = ha_ref[rows, A_Q + A_KV:A_Q + A_KV + PAIR]
        for kv in range(A_KV_HEADS):
            ak_ref[b, kv] = kp[:, kv * HEAD_DIM:(kv + 1) * HEAD_DIM]
            av_ref[b, kv] = vp[:, kv * HEAD_DIM:(kv + 1) * HEAD_DIM]
        kp_sw = pltpu.roll(kp, HEAD_DIM, axis=1)
        vp_sw = pltpu.roll(vp, HEAD_DIM, axis=1)
        qs = [_pair_rms(ha_ref[rows, c * PAIR:(c + 1) * PAIR], qg2, is_lo).astype(BF16)
              for c in range(A_HEADS // 2)]
        for kv in range(A_KV_HEADS):
            k_lo = jnp.where(is_lo, kp if kv == 0 else kp_sw, 0.0).astype(BF16)
            k_hi = jnp.where(is_lo, 0.0, kp_sw if kv == 0 else kp).astype(BF16)
            v_lo = jnp.where(is_lo, vp if kv == 0 else vp_sw, 0.0).astype(BF16)
            v_hi = jnp.where(is_lo, 0.0, vp_sw if kv == 0 else vp).astype(BF16)
            q_st = jnp.concatenate([qs[2 * kv], qs[2 * kv + 1]], axis=0)
            o = _softmax_pv([_qk(q_st, k_lo)], [v_lo]) + _softmax_pv([_qk(q_st, k_hi)], [v_hi])
            oab_ref[rows, (2 * kv) * PAIR:(2 * kv + 1) * PAIR] = o[0:SEQ].astype(BF16)
            oab_ref[rows, (2 * kv + 1) * PAIR:(2 * kv + 2) * PAIR] = o[SEQ:2 * SEQ].astype(BF16)
        for j in range(B_HEADS // 2):
            qp = (hb_ref[rows, j * PAIR:(j + 1) * PAIR] * ATT_SCALE).astype(BF16)
            kp_b = hb_ref[rows, B_W + j * PAIR:B_W + (j + 1) * PAIR]
            vp_b = hb_ref[rows, 2 * B_W + j * PAIR:2 * B_W + (j + 1) * PAIR]
            for u in range(2):
                nk_ref[b, 2 * j + u] = kp_b[:, u * HEAD_DIM:(u + 1) * HEAD_DIM]
                nv_ref[b, 2 * j + u] = vp_b[:, u * HEAD_DIM:(u + 1) * HEAD_DIM]
            k_lo, k_hi = _split_halves(kp_b, is_lo)
            v_lo, v_hi = _split_halves(vp_b, is_lo)
            o = _softmax_pv([_qk(qp, k_lo)], [v_lo]) + _softmax_pv([_qk(qp, k_hi)], [v_hi])
            oab_ref[rows, A_Q + j * PAIR:A_Q + (j + 1) * PAIR] = o.astype(BF16)


def _attn_ctx(h_a, h_b, qg2, kg2, prev_kv):
    rows = CTX_NB * SEQ
    layer_spec = lambda nh: pl.BlockSpec((CTX_NB, nh, SEQ, HEAD_DIM), lambda b: (b, 0, 0, 0))
    heads = (A_KV_HEADS, A_KV_HEADS, B_HEADS, B_HEADS)
    if prev_kv is None:
        prev_kv, prev_specs = (), []
        kv_specs = [layer_spec(nh) for nh in heads]
        kv_shapes = [jax.ShapeDtypeStruct((BATCH, nh, SEQ, HEAD_DIM), F32) for nh in heads]
    else:
        prev_specs = [layer_spec(heads[i % 4]) for i in range(len(prev_kv))]
        kv_specs = [pl.BlockSpec((CTX_NB, DEPTH, nh, SEQ, HEAD_DIM), lambda b: (b, 0, 0, 0, 0)) for nh in heads]
        kv_shapes = [jax.ShapeDtypeStruct((BATCH, DEPTH, nh, SEQ, HEAD_DIM), F32) for nh in heads]
    return pl.pallas_call(
        _attn_ctx_kernel,
        grid=(BATCH // CTX_NB,),
        in_specs=[
            pl.BlockSpec((rows, W_A), lambda b: (b, 0)),
            pl.BlockSpec((rows, W_B), lambda b: (b, 0)),
            pl.BlockSpec((1, PAIR), lambda b: (0, 0)),
            pl.BlockSpec((1, PAIR), lambda b: (0, 0)),
        ] + prev_specs,
        out_specs=[pl.BlockSpec((rows, A_Q + B_W), lambda b: (b, 0))] + kv_specs,
        out_shape=[jax.ShapeDtypeStruct((N_CTX, A_Q + B_W), BF16)] + kv_shapes,
        compiler_params=_params(("arbitrary",), 56),
        name="attn_ctx",
    )(h_a, h_b, qg2, kg2, *prev_kv)


LAT_TQ = 256
LAT_QROWS = LAT_TQ // GRID_W
LAT_NT = DEC_SEQ // LAT_TQ
NA_UNION_ROWS = 12
NA_UNION = NA_UNION_ROWS * GRID_W


def _na_union_row0(t):
    lo = min(max(r - NA_WIN_R // 2, 0) for r in range(t * LAT_QROWS, (t + 1) * LAT_QROWS))
    return min(min(lo, LAT_ROWS - NA_WIN_R), LAT_ROWS - NA_UNION_ROWS)


def _rope(x, cos, sin):
    lane = lax.broadcasted_iota(jnp.int32, x.shape, 1)
    nxt = pltpu.roll(x, x.shape[1] - 1, axis=1)
    prv = pltpu.roll(x, 1, axis=1)
    partner = jnp.where((lane & 1) == 0, nxt, prv)
    return x * cos + partner * sin


def _attn_lat_kernel(pick_ref, haq_ref, haf_ref, hbq_ref, hbf_ref, cak_ref, cav_ref, cnk_ref, cnv_ref,
                     qg_ref, kg_ref, cos_ref, sin_ref, toep_ref, oab_ref, kva_ref):
    t = pl.program_id(1)
    is_lo = _is_lo_half()
    row0 = pl.multiple_of(t * LAT_TQ, LAT_TQ)
    qg2 = qg_ref[...] * ATT_SCALE
    kg2 = kg_ref[...]

    @pl.when(t == 0)
    def _():
        kp = _rope(_pair_rms(haf_ref[:, A_Q:A_Q + PAIR].astype(F32), kg2, is_lo), cos_ref[...], sin_ref[...])
        vp = haf_ref[:, A_Q + A_KV:A_Q + A_KV + PAIR].astype(F32)
        kp_sw = pltpu.roll(kp, HEAD_DIM, axis=1)
        vp_sw = pltpu.roll(vp, HEAD_DIM, axis=1)
        for kv in range(A_KV_HEADS):
            kva_ref[4 * kv + 0] = jnp.where(is_lo, kp if kv == 0 else kp_sw, 0.0).astype(BF16)
            kva_ref[4 * kv + 1] = jnp.where(is_lo, 0.0, kp_sw if kv == 0 else kp).astype(BF16)
            kva_ref[4 * kv + 2] = jnp.where(is_lo, vp if kv == 0 else vp_sw, 0.0).astype(BF16)
            kva_ref[4 * kv + 3] = jnp.where(is_lo, 0.0, vp_sw if kv == 0 else vp).astype(BF16)

    cos_q = cos_ref[pl.ds(row0, LAT_TQ), :]
    sin_q = sin_ref[pl.ds(row0, LAT_TQ), :]
    qs = [_rope(_pair_rms(haq_ref[:, c * PAIR:(c + 1) * PAIR].astype(F32), qg2, is_lo), cos_q, sin_q).astype(BF16)
          for c in range(A_HEADS // 2)]
    for kv in range(A_KV_HEADS):
        kc = cak_ref[kv].astype(BF16)
        vc = cav_ref[kv].astype(BF16)
        q_st = jnp.concatenate([qs[2 * kv], qs[2 * kv + 1]], axis=0)
        o = (_softmax_pv([_qk(q_st, _pad_lo(kc)), _qk(q_st, kva_ref[4 * kv + 0])], [_pad_lo(vc), kva_ref[4 * kv + 2]])
             + _softmax_pv([_qk(q_st, _pad_hi(kc)), _qk(q_st, kva_ref[4 * kv + 1])], [_pad_hi(vc), kva_ref[4 * kv + 3]]))
        oab_ref[:, (2 * kv) * PAIR:(2 * kv + 1) * PAIR] = o[0:LAT_TQ].astype(BF16)
        oab_ref[:, (2 * kv + 1) * PAIR:(2 * kv + 2) * PAIR] = o[LAT_TQ:2 * LAT_TQ].astype(BF16)

    k0 = _na_union_row0(0) * GRID_W
    for tt in range(1, LAT_NT):
        k0 = jnp.where(t >= tt, _na_union_row0(tt) * GRID_W, k0)
    k0 = pl.multiple_of(k0, GRID_W)

    def bias(h):
        base = t * (LAT_QROWS * NA_UNION_ROWS)
        return jnp.concatenate(
            [jnp.concatenate([toep_ref[h, pick_ref[base + i * NA_UNION_ROWS + u]] for u in range(NA_UNION_ROWS)],
                             axis=1) for i in range(LAT_QROWS)], axis=0)

    for j in range(B_HEADS // 2):
        qp = (hbq_ref[:, j * PAIR:(j + 1) * PAIR] * ATT_SCALE).astype(BF16)
        k_lo, k_hi = _split_halves(hbf_ref[pl.ds(k0, NA_UNION), B_W + j * PAIR:B_W + (j + 1) * PAIR], is_lo)
        v_lo, v_hi = _split_halves(hbf_ref[pl.ds(k0, NA_UNION), 2 * B_W + j * PAIR:2 * B_W + (j + 1) * PAIR], is_lo)
        kc_lo = _pad_lo(cnk_ref[2 * j].astype(BF16))
        kc_hi = _pad_hi(cnk_ref[2 * j + 1].astype(BF16))
        vc_lo = _pad_lo(cnv_ref[2 * j].astype(BF16))
        vc_hi = _pad_hi(cnv_ref[2 * j + 1].astype(BF16))
        o = (_softmax_pv([_qk(qp, kc_lo), _qk(qp, k_lo) + bias(2 * j)], [vc_lo, v_lo])
             + _softmax_pv([_qk(qp, kc_hi), _qk(qp, k_hi) + bias(2 * j + 1)], [vc_hi, v_hi]))
        oab_ref[:, A_Q + j * PAIR:A_Q + (j + 1) * PAIR] = o.astype(BF16)


def _attn_lat(h_a, h_b, cak, cav, cnk, cnv, layer, qg2, kg2, cos128, sin128, toep):
    nt = LAT_NT
    cache_spec = lambda nh: pl.BlockSpec((None, None, nh, PAST_LEN, HEAD_DIM), lambda b, t, _: (b, layer, 0, 0, 0))
    const = lambda shape: pl.BlockSpec(shape, lambda b, t, _: (0,) * len(shape))
    grid_spec = pltpu.PrefetchScalarGridSpec(
        num_scalar_prefetch=1,
        grid=(DEC_BATCH, nt),
        in_specs=[
            pl.BlockSpec((LAT_TQ, W_A), lambda b, t, _: (b * nt + t, 0)),
            pl.BlockSpec((DEC_SEQ, W_A), lambda b, t, _: (b, 0)),
            pl.BlockSpec((LAT_TQ, W_B), lambda b, t, _: (b * nt + t, 0)),
            pl.BlockSpec((DEC_SEQ, W_B), lambda b, t, _: (b, 0)),
            cache_spec(A_KV_HEADS), cache_spec(A_KV_HEADS), cache_spec(B_HEADS), cache_spec(B_HEADS),
            const((1, PAIR)), const((1, PAIR)), const((DEC_SEQ, PAIR)), const((DEC_SEQ, PAIR)),
            const(toep.shape),
        ],
        out_specs=pl.BlockSpec((LAT_TQ, A_Q + B_W), lambda b, t, _: (b * nt + t, 0)),
        scratch_shapes=[pltpu.VMEM((4 * A_KV_HEADS, DEC_SEQ, PAIR), BF16)],
    )
    return pl.pallas_call(
        _attn_lat_kernel,
        grid_spec=grid_spec,
        out_shape=jax.ShapeDtypeStruct((N_LAT, A_Q + B_W), BF16),
        compiler_params=_params(("arbitrary", "arbitrary"), 56),
        name="attn_lat",
    )(jnp.asarray(_na_tile_picks().reshape(-1)), h_a, h_a, h_b, h_b, cak, cav, cnk, cnv, qg2, kg2, cos128, sin128,
      toep)


def _rope_tables():
    t = jnp.arange(DEC_SEQ)
    row = (t // GRID_W).astype(F32)
    col = (t % GRID_W).astype(F32)
    n_freq = HEAD_DIM // 4
    inv = ROPE_THETA ** (-jnp.arange(n_freq, dtype=F32) / n_freq)
    ang = jnp.concatenate([row[:, None] * inv, col[:, None] * inv], axis=-1)
    cos = jnp.repeat(jnp.cos(ang), 2, axis=-1)
    sign = jnp.tile(jnp.array([-1.0, 1.0], F32), HEAD_DIM // 2)
    sin = jnp.repeat(jnp.sin(ang), 2, axis=-1) * sign
    return jnp.tile(cos, (1, 2)), jnp.tile(sin, (1, 2))


NA_MASKED_TILE = 2 * NA_WIN_R - 1


def _na_toeplitz(rpb):
    col = np.arange(GRID_W)
    dc = np.clip(col[None, :] - col[:, None] + NA_WIN_C - 1, 0, 2 * NA_WIN_C - 2)
    onehot = (dc[..., None] == np.arange(2 * NA_WIN_C - 1)).astype(np.float32)
    c0 = np.clip(col - NA_WIN_C // 2, 0, GRID_W - NA_WIN_C)
    in_win = (col[None, :] >= c0[:, None]) & (col[None, :] < c0[:, None] + NA_WIN_C)
    toep = jnp.einsum('hdm,ckm->hdck', rpb, onehot, precision=lax.Precision.HIGHEST)
    toep = jnp.where(in_win[None, None], toep, MASK_NEG)
    return jnp.concatenate([toep, jnp.full((B_HEADS, 1, GRID_W, GRID_W), MASK_NEG, F32)], axis=1)


def _na_tile_picks():
    pick = np.full((LAT_NT, LAT_QROWS, NA_UNION_ROWS), NA_MASKED_TILE, np.int32)
    for t in range(LAT_NT):
        u0 = _na_union_row0(t)
        for i in range(LAT_QROWS):
            r = t * LAT_QROWS + i
            r0 = min(max(r - NA_WIN_R // 2, 0), LAT_ROWS - NA_WIN_R)
            for u in range(NA_UNION_ROWS):
                if r0 <= u0 + u < r0 + NA_WIN_R:
                    pick[t, i, u] = u0 + u - r + NA_WIN_R - 1
    return pick


def _conv_kernel(hc_ref, w_ref, o_ref, *, seq):
    bg = hc_ref[:, 0:C_WIDTH].astype(F32)
    u = hc_ref[:, C_WIDTH:2 * C_WIDTH].astype(F32) * hc_ref[:, 2 * C_WIDTH:3 * C_WIDTH].astype(F32)
    rows = u.shape[0]
    assert seq & (seq - 1) == 0
    pos = lax.broadcasted_iota(jnp.int32, u.shape, 0) & (seq - 1)
    u_prev = jnp.where(pos == 0, 0.0, pltpu.roll(u, 1, axis=0))
    u_next = jnp.where(pos == seq - 1, 0.0, pltpu.roll(u, rows - 1, axis=0))
    y = u_prev * w_ref[0:1, :] + u * w_ref[1:2, :] + u_next * w_ref[2:3, :]
    o_ref[...] = (bg * y).astype(o_ref.dtype)


def _conv(h_c, conv_w, seq):
    n = h_c.shape[0]
    tm = 1024
    return pl.pallas_call(
        functools.partial(_conv_kernel, seq=seq),
        grid=(n // tm,),
        in_specs=[pl.BlockSpec((tm, W_C), lambda i: (i, 0)), pl.BlockSpec((3, C_WIDTH), lambda i: (0, 0))],
        out_specs=pl.BlockSpec((tm, C_WIDTH), lambda i: (i, 0)),
        out_shape=jax.ShapeDtypeStruct((n, C_WIDTH), BF16),
        compiler_params=_params(("arbitrary",), 40),
        name="short_conv",
    )(h_c, conv_w)


def _mix_kernel(x_ref, oab_ref, oc_ref, hg_ref, wa_ref, wb_ref, wc_ref, wo_ref, g1_ref, n2g_ref, sc2_ref, sh2_ref,
                wrt_ref, xo_ref, xn2_ref, pt_ref):
    ga = jax.nn.sigmoid(hg_ref[:, 0:D_MODEL].astype(F32))
    gb = jax.nn.sigmoid(hg_ref[:, D_MODEL:2 * D_MODEL].astype(F32))
    gc = jax.nn.sigmoid(hg_ref[:, 2 * D_MODEL:3 * D_MODEL].astype(F32))
    pa = jnp.dot(oab_ref[:, 0:A_Q], wa_ref[...], preferred_element_type=F32)
    pb = jnp.dot(oab_ref[:, A_Q:A_Q + B_W], wb_ref[...], preferred_element_type=F32)
    pc = jnp.dot(oc_ref[...], wc_ref[...], preferred_element_type=F32)
    merged = ga * pa + gb * pb + gc * pc
    xo = x_ref[...] + g1_ref[...] * jnp.dot(merged.astype(BF16), wo_ref[...], preferred_element_type=F32)
    xo_ref[...] = xo
    y = xo * lax.rsqrt(jnp.mean(xo * xo, axis=-1, keepdims=True) + EPS) * n2g_ref[...]
    xn2 = y * (1.0 + sc2_ref[...]) + sh2_ref[...]
    rows = xn2.shape[0]
    for j in range(ROW_TILE):
        xn2_ref[pl.ds(j, rows, stride=ROW_TILE), :] = xn2[:, j * LANES:(j + 1) * LANES]
    logits_t = lax.dot_general(wrt_ref[...], xn2.astype(BF16), _NT, preferred_element_type=F32)
    z = jnp.exp(logits_t - logits_t.max(axis=0, keepdims=True))
    pt_ref[...] = z / z.sum(axis=0, keepdims=True)


def _mix(x, oab, oc, h_g, wa, wb, wc, wo, g1, n2g, sc2, sh2, wrt, seq):
    n = x.shape[0]
    tm = 512
    per_batch = g1.shape[0] > 1
    mod_map = (lambda i: ((i * tm) // seq, 0, 0)) if per_batch else (lambda i: (0, 0, 0))
    mod_spec = pl.BlockSpec((None, 1, D_MODEL), mod_map)
    full = lambda a: pl.BlockSpec(a.shape, lambda i: (0,) * a.ndim)
    row = lambda w: pl.BlockSpec((tm, w), lambda i: (i, 0))
    return pl.pallas_call(
        _mix_kernel,
        grid=(n // tm,),
        in_specs=[row(D_MODEL), row(A_Q + B_W), row(C_WIDTH), row(GATE_W), full(wa), full(wb), full(wc), full(wo),
                  mod_spec, full(n2g), mod_spec, mod_spec, full(wrt)],
        out_specs=[row(D_MODEL), pl.BlockSpec((tm * ROW_TILE, LANES), lambda i: (i, 0)),
                   pl.BlockSpec((N_EXPERTS, tm), lambda i: (0, i))],
        out_shape=[jax.ShapeDtypeStruct((n, D_MODEL), F32), jax.ShapeDtypeStruct((n * ROW_TILE, LANES), F32),
                   jax.ShapeDtypeStruct((N_EXPERTS, n), F32)],
        compiler_params=_params(("arbitrary",), 56),
        name="mix",
    )(x, oab, oc, h_g, wa, wb, wc, wo, g1, n2g, sc2, sh2, wrt)


MOE_TF = 512
MOE_ROW_CHUNKS = ((0, 640), (640, 640))
MOE_ISSUE_UNROLL = 8


def _moe_issue_rows(idx_ref, idx0, src_hbm, dst_ref, dst_row0, count, sem):
    def body(i, carry):
        for u in range(MOE_ISSUE_UNROLL):
            r = i * MOE_ISSUE_UNROLL + u
            tok = idx_ref[idx0 + r]
            src = src_hbm.at[pl.ds(pl.multiple_of(tok * ROW_TILE, ROW_TILE), ROW_TILE)]
            dst = dst_ref.at[pl.ds(pl.multiple_of((dst_row0 + r) * ROW_TILE, ROW_TILE), ROW_TILE)]
            pltpu.make_async_copy(src, dst, sem).start(priority=u % 2)
        return carry

    lax.fori_loop(0, count // MOE_ISSUE_UNROLL, body, 0)


def _moe_kernel(idx_ref, xc_hbm, xl_hbm, aff_ref, wg_ref, wu_ref, wd_ref, ye_ref, xe_ref, sem):
    e = pl.program_id(0)
    f = pl.program_id(1)
    slot = e % 2

    def gather(expert, s):
        base = expert * CAP_ALL
        _moe_issue_rows(idx_ref, base, xc_hbm, xe_ref.at[s], 0, CAP_CTX, sem.at[s])
        _moe_issue_rows(idx_ref, base + CAP_CTX, xl_hbm, xe_ref.at[s], CAP_CTX, CAP_LAT, sem.at[s])

    @pl.when(f == 0)
    def _():
        @pl.when(e == 0)
        def _():
            gather(0, 0)

        pltpu.make_async_copy(xc_hbm.at[pl.ds(0, CAP_ALL * ROW_TILE)], xe_ref.at[slot], sem.at[slot]).wait()

        @pl.when(e + 1 < N_EXPERTS)
        def _():
            gather(e + 1, 1 - slot)

    wg = wg_ref[...].astype(BF16)
    wu = wu_ref[...].astype(BF16)
    wd = wd_ref[...].astype(BF16)
    for r0, rn in MOE_ROW_CHUNKS:
        xe = jnp.concatenate([xe_ref[slot, pl.ds(r0 * ROW_TILE + j, rn, stride=ROW_TILE), :]
                              for j in range(ROW_TILE)], axis=1).astype(BF16)
        hg = jnp.dot(xe, wg, preferred_element_type=F32)
        hu = jnp.dot(xe, wu, preferred_element_type=F32)
        act = (hg * jax.nn.sigmoid(hg) * hu).astype(BF16)
        part = jnp.dot(act, wd, preferred_element_type=F32) * aff_ref[r0:r0 + rn, :]

        for j in range(ROW_TILE):
            rows_j = pl.ds(r0 * ROW_TILE + j, rn, stride=ROW_TILE)
            piece = part[:, j * LANES:(j + 1) * LANES]

            @pl.when(f == 0)
            def _():
                ye_ref[rows_j, :] = piece

            @pl.when(f != 0)
            def _():
                ye_ref[rows_j, :] += piece


def _moe(idx_all, xn2_ctx, xn2_lat, aff_all, w_gate_e, w_up_e, w_down_e, layer):
    nf = EXPERT_FF // MOE_TF
    grid_spec = pltpu.PrefetchScalarGridSpec(
        num_scalar_prefetch=1,
        grid=(N_EXPERTS, nf),
        in_specs=[
            pl.BlockSpec(memory_space=pl.ANY),
            pl.BlockSpec(memory_space=pl.ANY),
            pl.BlockSpec((None, CAP_ALL, 1), lambda e, f, idx: (e, 0, 0)),
            pl.BlockSpec((None, None, D_MODEL, MOE_TF), lambda e, f, idx: (layer, e, 0, f)),
            pl.BlockSpec((None, None, D_MODEL, MOE_TF), lambda e, f, idx: (layer, e, 0, f)),
            pl.BlockSpec((None, None, MOE_TF, D_MODEL), lambda e, f, idx: (layer, e, f, 0)),
        ],
        out_specs=pl.BlockSpec((None, CAP_ALL * ROW_TILE, LANES), lambda e, f, idx: (e, 0, 0)),
        scratch_shapes=[pltpu.VMEM((2, CAP_ALL * ROW_TILE, LANES), F32), pltpu.SemaphoreType.DMA((2,))],
    )
    return pl.pallas_call(
        _moe_kernel,
        grid_spec=grid_spec,
        out_shape=jax.ShapeDtypeStruct((N_EXPERTS, CAP_ALL * ROW_TILE, LANES), F32),
        compiler_params=_params(("arbitrary", "arbitrary"), 56),
        name="moe_experts",
    )(idx_all, xn2_ctx, xn2_lat, aff_all, w_gate_e, w_up_e, w_down_e)


POS_SPLIT = 32


def _prefix_count(mask_bf, tri_lane, tri_blk_strict):
    within = jnp.dot(mask_bf, tri_lane, preferred_element_type=F32)
    bs = jnp.broadcast_to(within[:, LANES - 1:LANES], within.shape).astype(BF16)
    return within + jnp.dot(tri_blk_strict, bs, preferred_element_type=F32)


def _route_kernel(p_ref, idx_ref, aff_ref, *, cap):
    p_all = p_ref[...]
    nb = p_all.shape[1]
    capf = float(cap)

    def as_float(word):
        return lax.bitcast_convert_type(word, F32)

    def bisect(_, c):
        lo, hi = c
        mid = lo + ((hi - lo) >> 1)
        ok = jnp.sum(jnp.where(p_all >= as_float(mid), 1.0, 0.0), axis=(1, 2), keepdims=True) >= capf
        return jnp.where(ok, mid, lo), jnp.where(ok, hi, mid)

    lo0 = jnp.zeros((N_EXPERTS, 1, 1), jnp.int32)
    hi0 = jnp.full((N_EXPERTS, 1, 1), 0x7F800000, jnp.int32)
    thr_all, _ = lax.fori_loop(0, 31, bisect, (lo0, hi0))
    thr_val = as_float(thr_all)

    li = lax.broadcasted_iota(jnp.int32, (LANES, LANES), 0)
    lj = lax.broadcasted_iota(jnp.int32, (LANES, LANES), 1)
    tri_lane = jnp.where(li <= lj, 1.0, 0.0).astype(BF16)
    bi = lax.broadcasted_iota(jnp.int32, (nb, nb), 0)
    bj = lax.broadcasted_iota(jnp.int32, (nb, nb), 1)
    tri_blk_strict = jnp.where(bj < bi, 1.0, 0.0).astype(BF16)
    tri_blk_incl = jnp.where(bi <= bj, 1.0, 0.0).astype(BF16)
    s_col = lax.broadcasted_iota(jnp.int32, (cap, 1), 0).astype(F32)
    blk_id = lax.broadcasted_iota(jnp.int32, (1, nb), 1).astype(F32)
    lane_id = lax.broadcasted_iota(jnp.int32, (1, LANES), 1).astype(F32)
    ones_rows = jnp.ones((8, LANES), BF16)

    for e in range(N_EXPERTS):
        p = p_all[e]
        thr = thr_val[e]
        gt = p > thr
        eq = p == thr
        need = capf - jnp.sum(jnp.where(gt, 1.0, 0.0), keepdims=True)
        tie_rank = _prefix_count(jnp.where(eq, 1.0, 0.0).astype(BF16), tri_lane, tri_blk_strict)
        sel = jnp.where(gt | (eq & (tie_rank <= need)), 1.0, 0.0).astype(BF16)
        pos = _prefix_count(sel, tri_lane, tri_blk_strict)

        bs_row = lax.dot_general(ones_rows, sel, _NT, preferred_element_type=F32)
        bp_row = jnp.dot(bs_row.astype(BF16), tri_blk_incl, preferred_element_type=F32)[0:1, :]
        blk = jnp.sum(jnp.where(bp_row <= s_col, 1.0, 0.0), axis=-1, keepdims=True)
        onehot = jnp.where(blk == blk_id, 1.0, 0.0).astype(BF16)
        pos_hi = jnp.floor(pos * (1.0 / POS_SPLIT))
        pos_lo = pos - pos_hi * POS_SPLIT
        pos_row = (jnp.dot(onehot, pos_hi.astype(BF16), preferred_element_type=F32) * POS_SPLIT
                   + jnp.dot(onehot, pos_lo.astype(BF16), preferred_element_type=F32))
        lane = jnp.sum(jnp.where(pos_row <= s_col, 1.0, 0.0), axis=-1, keepdims=True)
        idx_ref[e] = (blk * LANES + lane).astype(jnp.int32)

        p1 = p.astype(BF16)
        r1 = p - p1.astype(F32)
        p2 = r1.astype(BF16)
        p3 = (r1 - p2.astype(F32)).astype(BF16)
        p_row = (jnp.dot(onehot, p1, preferred_element_type=F32) + jnp.dot(onehot, p2, preferred_element_type=F32)
                 + jnp.dot(onehot, p3, preferred_element_type=F32))
        aff_ref[e] = jnp.sum(jnp.where(lane_id == lane, p_row, 0.0), axis=-1, keepdims=True)


def _route(probs_t, cap):
    nb = probs_t.shape[1] // LANES
    slot = pl.BlockSpec((N_EXPERTS, cap, 1), lambda i: (0, 0, 0))
    return pl.pallas_call(
        functools.partial(_route_kernel, cap=cap),
        grid=(1,),
        in_specs=[pl.BlockSpec((N_EXPERTS, nb, LANES), lambda i: (0, 0, 0))],
        out_specs=[slot, slot],
        out_shape=[jax.ShapeDtypeStruct((N_EXPERTS, cap, 1), jnp.int32),
                   jax.ShapeDtypeStruct((N_EXPERTS, cap, 1), F32)],
        compiler_params=_params(("arbitrary",), 56),
        name="route",
    )(probs_t.reshape(N_EXPERTS, nb, LANES))


CMB_TT = 1024
CMB_CH = 256
CMB_GROUP = 8
CMB_EPI_ROWS = 256


def _combine_kernel(idx_ref, st_ref, ye_hbm, x_ref, g2_ref, ng_ref, sc_ref, sh_ref, *rest, off, cap, nt, final):
    if final:
        y_ref, acc_ref, buf_ref, sem = rest
    else:
        xo_ref, xn_ref, acc_ref, buf_ref, sem = rest
    k = pl.program_id(0)
    t0 = k * CMB_TT
    acc_ref[...] = jnp.zeros_like(acc_ref)

    def slot_range(e):
        s0 = st_ref[e * (nt + 1) + k]
        s1 = st_ref[e * (nt + 1) + k + 1]
        return s0, s1, jnp.minimum(s0 & -CMB_GROUP, cap - CMB_CH)

    def chunk_copy(e, start):
        src = ye_hbm.at[e, pl.ds(pl.multiple_of((off + start) * ROW_TILE, ROW_TILE), CMB_CH * ROW_TILE)]
        return pltpu.make_async_copy(src, buf_ref.at[e], sem.at[e])

    def tile_rows(row):
        return pl.ds(pl.multiple_of(row * ROW_TILE, ROW_TILE), ROW_TILE)

    def accumulate(e, start, lo, hi):
        def body(i, carry):
            r0 = i * CMB_GROUP
            dst = []
            for u in range(CMB_GROUP):
                slot = start + r0 + u
                tok = idx_ref[e * CAP_ALL + off + slot]
                dst.append(jnp.where((slot >= lo) & (slot < hi), tok - t0, CMB_TT))
            rows = [acc_ref[tile_rows(d), :] for d in dst]
            for u in range(CMB_GROUP):
                acc_ref[tile_rows(dst[u]), :] = rows[u] + buf_ref[e, tile_rows(r0 + u), :]
            return carry

        lax.fori_loop((lo - start) // CMB_GROUP, (hi - start + CMB_GROUP - 1) // CMB_GROUP, body, 0)

    for e in range(N_EXPERTS):
        chunk_copy(e, slot_range(e)[2]).start()
    for e in range(N_EXPERTS):
        s0, s1, a = slot_range(e)
        chunk_copy(e, a).wait()
        accumulate(e, a, s0, jnp.minimum(s1, a + CMB_CH))

        def extra(j, carry, e=e, s1=s1, a=a):
            lo = a + CMB_CH * (j + 1)
            start = jnp.minimum(lo, cap - CMB_CH)
            cp = chunk_copy(e, start)
            cp.start()
            cp.wait()
            accumulate(e, start, lo, jnp.minimum(s1, lo + CMB_CH))
            return carry

        lax.fori_loop(0, jnp.maximum(s1 - a - 1, 0) // CMB_CH, extra, 0)

    for r0 in range(0, CMB_TT, CMB_EPI_ROWS):
        rs = slice(r0, r0 + CMB_EPI_ROWS)
        y_moe = jnp.concatenate([acc_ref[pl.ds(r0 * ROW_TILE + j, CMB_EPI_ROWS, stride=ROW_TILE), :]
                                 for j in range(ROW_TILE)], axis=1)
        xo = x_ref[rs, :] + g2_ref[...] * y_moe
        y = xo * lax.rsqrt(jnp.mean(xo * xo, axis=-1, keepdims=True) + EPS) * ng_ref[...]
        if final:
            y_ref[rs, :] = y
        else:
            xo_ref[rs, :] = xo
            xn_ref[rs, :] = (y * (1.0 + sc_ref[...]) + sh_ref[...]).astype(xn_ref.dtype)


def _combine(idx_all, starts, ye, x, g2, ng, sc, sh, seq, off, cap, final):
    n = x.shape[0]
    nt = n // CMB_TT
    per_batch = g2.shape[0] > 1
    mod_map = ((lambda i, *_: ((i * CMB_TT) // seq, 0, 0)) if per_batch else (lambda i, *_: (0, 0, 0)))
    mod_spec = pl.BlockSpec((None, 1, D_MODEL), mod_map)
    row = pl.BlockSpec((CMB_TT, D_MODEL), lambda i, *_: (i, 0))
    vec = pl.BlockSpec((1, D_MODEL), lambda i, *_: (0, 0))
    if final:
        out_specs, out_shape = row, jax.ShapeDtypeStruct((n, D_MODEL), F32)
    else:
        out_specs = [row, row]
        out_shape = [jax.ShapeDtypeStruct((n, D_MODEL), F32), jax.ShapeDtypeStruct((n, D_MODEL), BF16)]
    grid_spec = pltpu.PrefetchScalarGridSpec(
        num_scalar_prefetch=2,
        grid=(nt,),
        in_specs=[pl.BlockSpec(memory_space=pl.ANY), row, mod_spec, vec, mod_spec, mod_spec],
        out_specs=out_specs,
        scratch_shapes=[pltpu.VMEM(((CMB_TT + 1) * ROW_TILE, LANES), F32),
                        pltpu.VMEM((N_EXPERTS, CMB_CH * ROW_TILE, LANES), F32),
                        pltpu.SemaphoreType.DMA((N_EXPERTS,))],
    )
    return pl.pallas_call(
        functools.partial(_combine_kernel, off=off, cap=cap, nt=nt, final=final),
        grid_spec=grid_spec,
        out_shape=out_shape,
        compiler_params=_params(("arbitrary",), 60),
        name="combine",
    )(idx_all, starts.reshape(-1), ye, x, g2, ng, sc, sh)


def _tile_starts(idx_sorted, n):
    bounds = jnp.arange(n // CMB_TT + 1, dtype=jnp.int32) * CMB_TT
    return jnp.sum(idx_sorted[:, :, None] < bounds[None, None, :], axis=1, dtype=jnp.int32)


def kernel(x_prompt, x_sample, cache_attn_k, cache_attn_v, cache_na_k, cache_na_v, c, c_ctx, w_ada, b_ada, norm1_g,
           w_in, q_norm_g, k_norm_g, na_rpb, conv_w, w_proj_a, w_proj_b, w_proj_c, w_out, norm2_g, w_router,
           w_gate_e, w_up_e, w_down_e, final_g):
    cond8 = jnp.zeros((8, D_MODEL), F32).at[0].set(c_ctx).at[1:1 + DEC_BATCH].set(c)
    mods = _adaln(cond8, w_ada, b_ada).reshape(DEPTH, 8, N_MOD, D_MODEL)
    cos128, sin128 = _rope_tables()

    hp = x_prompt.reshape(N_CTX, D_MODEL)
    hs = x_sample.reshape(N_LAT, D_MODEL)
    kv_layers = []
    for l in range(DEPTH):
        mod_ctx = [mods[l, 0:1, i][:, None, :] for i in range(N_MOD)]
        mod_lat = [mods[l, 1:1 + DEC_BATCH, i][:, None, :] for i in range(N_MOD)]
        n1g = norm1_g[l][None]
        n2g = norm2_g[l][None]
        qg2 = jnp.tile(q_norm_g[l][None], (1, 2))
        kg2 = jnp.tile(k_norm_g[l][None], (1, 2))
        wa, wb, wc, wo = (w_proj_a[l].astype(BF16), w_proj_b[l].astype(BF16), w_proj_c[l].astype(BF16),
                          w_out[l].astype(BF16))
        wrt = w_router[l].T.astype(BF16)
        toep = _na_toeplitz(na_rpb[l])

        def sublayer1(x, xn, mod, seq, is_ctx):
            _, _, g1, sh2, sc2, _ = mod
            kv_dtype = F32 if is_ctx else BF16
            h_a = _inproj(xn, w_in, l, COL_A, W_A, 768, kv_dtype)
            h_b = _inproj(xn, w_in, l, COL_B, W_B, 768, kv_dtype)
            h_c = _inproj(xn, w_in, l, COL_C, W_C, 768, BF16)
            h_g = _inproj(xn, w_in, l, COL_G, GATE_W, 768, BF16)
            kv = None
            if is_ctx:
                oab, *kv = _attn_ctx(h_a, h_b, qg2, kg2, kv_layers if l == DEPTH - 1 else None)
            else:
                oab = _attn_lat(h_a, h_b, cache_attn_k, cache_attn_v, cache_na_k, cache_na_v, l, qg2, kg2,
                                cos128, sin128, toep)
            oc = _conv(h_c, conv_w[l], seq)
            xo, xn2, probs_t = _mix(x, oab, oc, h_g, wa, wb, wc, wo, g1, n2g, sc2, sh2, wrt, seq)
            return xo, xn2, probs_t, kv

        if l == 0:
            xn_p = _norm_mod(hp, n1g, mod_ctx[1], mod_ctx[0], SEQ, BF16)
            xn_s = _norm_mod(hs, n1g, mod_lat[1], mod_lat[0], DEC_SEQ, BF16)
        hp, xn2_c, pt_c, kv_out = sublayer1(hp, xn_p, mod_ctx, SEQ, True)
        kv_layers = kv_layers + list(kv_out)
        hs, xn2_l, pt_l, _ = sublayer1(hs, xn_s, mod_lat, DEC_SEQ, False)

        idx_c, aff_c = _route(pt_c, CAP_CTX)
        idx_l, aff_l = _route(pt_l, CAP_LAT)
        idx_all = jnp.concatenate([idx_c, idx_l], axis=1).reshape(-1)
        aff_all = jnp.concatenate([aff_c, aff_l], axis=1)
        ye = _moe(idx_all, xn2_c, xn2_l, aff_all, w_gate_e, w_up_e, w_down_e, l)

        final = l == DEPTH - 1
        if final:
            ng, nsc_c, nsh_c, nsc_l, nsh_l = final_g[None], mod_ctx[1], mod_ctx[0], mod_lat[1], mod_lat[0]
        else:
            nxt_ctx = [mods[l + 1, 0:1, i][:, None, :] for i in range(2)]
            nxt_lat = [mods[l + 1, 1:1 + DEC_BATCH, i][:, None, :] for i in range(2)]
            ng, nsc_c, nsh_c, nsc_l, nsh_l = norm1_g[l + 1][None], nxt_ctx[1], nxt_ctx[0], nxt_lat[1], nxt_lat[0]
        out_c = _combine(idx_all, _tile_starts(idx_c[..., 0], N_CTX), ye, hp, mod_ctx[5], ng, nsc_c, nsh_c,
                         SEQ, 0, CAP_CTX, final)
        out_l = _combine(idx_all, _tile_starts(idx_l[..., 0], N_LAT), ye, hs, mod_lat[5], ng, nsc_l, nsh_l,
                         DEC_SEQ, CAP_CTX, CAP_LAT, final)
        if final:
            y_prompt = out_c.reshape(BATCH, SEQ, D_MODEL)
            y_sample = out_l.reshape(DEC_BATCH, DEC_SEQ, D_MODEL)
        else:
            (hp, xn_p), (hs, xn_s) = out_c, out_l

    return (y_prompt, y_sample, *kv_out)
```

```python
import functools

import jax
import jax.numpy as jnp
import numpy as np
from jax import lax
from jax.experimental import pallas as pl
from jax.experimental.pallas import tpu as pltpu

D_MODEL = 1024
BATCH = 32
SEQ = 256
DEPTH = 2
DEC_BATCH = 2
DEC_SEQ = 1024
PAST_LEN = 256
GRID_W = 64
HEAD_DIM = 64
A_HEADS = 8
A_KV_HEADS = 2
A_GROUP = A_HEADS // A_KV_HEADS
B_HEADS = 8
C_WIDTH = 512
NA_WIN_R = 8
NA_WIN_C = 16
N_EXPERTS = 16
EXPERT_FF = 1024
CAP_FACTOR = 2
ROPE_THETA = 10000.0
EPS = 1e-6
N_MOD = 6

A_Q = A_HEADS * HEAD_DIM
A_KV = A_KV_HEADS * HEAD_DIM
B_W = B_HEADS * HEAD_DIM
GATE_W = 3 * D_MODEL
COL_A = 0
W_A = A_Q + 2 * A_KV
COL_B = COL_A + W_A
W_B = 3 * B_W
COL_C = COL_B + W_B
W_C = 3 * C_WIDTH
COL_G = COL_C + W_C

N_CTX = BATCH * SEQ
N_LAT = DEC_BATCH * DEC_SEQ
CAP_CTX = CAP_FACTOR * N_CTX // N_EXPERTS
CAP_LAT = CAP_FACTOR * N_LAT // N_EXPERTS
CAP_ALL = CAP_CTX + CAP_LAT
LAT_ROWS = DEC_SEQ // GRID_W
MASK_NEG = -1e30
ATT_SCALE = HEAD_DIM ** -0.5

LANES = 128
ROW_TILE = 8
assert D_MODEL == ROW_TILE * LANES

F32 = jnp.float32
BF16 = jnp.bfloat16
_NT = (((1,), (1,)), ((), ()))

_MIB = 1024 * 1024


def _params(sem, vmem_mib):
    return pltpu.CompilerParams(dimension_semantics=sem, vmem_limit_bytes=vmem_mib * _MIB)


def _adaln_kernel(cond_ref, w_ref, b_ref, o_ref):
    c = cond_ref[...]
    s = c * jax.nn.sigmoid(c)
    o_ref[...] = jnp.dot(s.astype(BF16), w_ref[...].astype(BF16), preferred_element_type=F32) + b_ref[...]


def _adaln(cond8, w_ada, b_ada):
    tn = 1536
    ncol = N_MOD * D_MODEL
    return pl.pallas_call(
        _adaln_kernel,
        grid=(DEPTH, ncol // tn),
        in_specs=[
            pl.BlockSpec((8, D_MODEL), lambda l, j: (0, 0)),
            pl.BlockSpec((None, D_MODEL, tn), lambda l, j: (l, 0, j)),
            pl.BlockSpec((None, 1, tn), lambda l, j: (l, 0, j)),
        ],
        out_specs=pl.BlockSpec((None, 8, tn), lambda l, j: (l, 0, j)),
        out_shape=jax.ShapeDtypeStruct((DEPTH, 8, ncol), F32),
        compiler_params=_params(("arbitrary", "arbitrary"), 40),
        name="adaln",
    )(cond8, w_ada, b_ada.reshape(DEPTH, 1, ncol))


def _norm_mod_kernel(x_ref, g_ref, sc_ref, sh_ref, o_ref):
    x = x_ref[...]
    y = x * lax.rsqrt(jnp.mean(x * x, axis=-1, keepdims=True) + EPS) * g_ref[...]
    o_ref[...] = (y * (1.0 + sc_ref[...]) + sh_ref[...]).astype(o_ref.dtype)


def _norm_mod(x, g, sc, sh, seq, out_dtype):
    n = x.shape[0]
    tm = 512
    per_batch = sc.shape[0] > 1
    mod_map = (lambda i: ((i * tm) // seq, 0, 0)) if per_batch else (lambda i: (0, 0, 0))
    return pl.pallas_call(
        _norm_mod_kernel,
        grid=(n // tm,),
        in_specs=[
            pl.BlockSpec((tm, D_MODEL), lambda i: (i, 0)),
            pl.BlockSpec((1, D_MODEL), lambda i: (0, 0)),
            pl.BlockSpec((None, 1, D_MODEL), mod_map),
            pl.BlockSpec((None, 1, D_MODEL), mod_map),
        ],
        out_specs=pl.BlockSpec((tm, D_MODEL), lambda i: (i, 0)),
        out_shape=jax.ShapeDtypeStruct((n, D_MODEL), out_dtype),
        compiler_params=_params(("arbitrary",), 32),
        name="norm_mod",
    )(x, g, sc, sh)


def _inproj_kernel(a_ref, w_ref, o_ref, wbf_ref):
    @pl.when(pl.program_id(1) == 0)
    def _():
        wbf_ref[...] = w_ref[...].astype(BF16)

    o_ref[...] = jnp.dot(a_ref[...], wbf_ref[...], preferred_element_type=F32).astype(o_ref.dtype)


def _inproj(xn, w_in, layer, col0, width, tn, out_dtype):
    n = xn.shape[0]
    tm = min(n, 4096 if out_dtype == BF16 else 2048)
    joff = col0 // tn
    assert col0 % tn == 0 and width % tn == 0
    return pl.pallas_call(
        _inproj_kernel,
        grid=(width // tn, n // tm),
        in_specs=[
            pl.BlockSpec((tm, D_MODEL), lambda j, i: (i, 0)),
            pl.BlockSpec((None, D_MODEL, tn), lambda j, i: (layer, 0, j + joff)),
        ],
        out_specs=pl.BlockSpec((tm, tn), lambda j, i: (i, j)),
        out_shape=jax.ShapeDtypeStruct((n, width), out_dtype),
        scratch_shapes=[pltpu.VMEM((D_MODEL, tn), BF16)],
        compiler_params=_params(("arbitrary", "arbitrary"), 48),
        name="inproj",
    )(xn, w_in)


PAIR = 2 * HEAD_DIM


def _is_lo_half():
    return lax.broadcasted_iota(jnp.int32, (1, PAIR), 1) < HEAD_DIM


def _pair_rms(x, g2, is_lo):
    x2 = x * x
    s_lo = jnp.sum(jnp.where(is_lo, x2, 0.0), axis=-1, keepdims=True)
    s_hi = jnp.sum(jnp.where(is_lo, 0.0, x2), axis=-1, keepdims=True)
    ms = jnp.where(is_lo, s_lo, s_hi) * (1.0 / HEAD_DIM)
    return x * lax.rsqrt(ms + EPS) * g2


def _split_halves(x, is_lo):
    return jnp.where(is_lo, x, 0.0).astype(BF16), jnp.where(is_lo, 0.0, x).astype(BF16)


def _pad_lo(x):
    return jnp.concatenate([x, jnp.zeros_like(x)], axis=1)


def _pad_hi(x):
    return jnp.concatenate([jnp.zeros_like(x), x], axis=1)


def _qk(q, k):
    return lax.dot_general(q, k, _NT, preferred_element_type=F32)


def _softmax_pv(scores, values):
    m = scores[0].max(axis=-1, keepdims=True)
    for s in scores[1:]:
        m = jnp.maximum(m, s.max(axis=-1, keepdims=True))
    acc = None
    den = None
    for s, v in zip(scores, values):
        p = jnp.exp(s - m)
        d = p.sum(axis=-1, keepdims=True)
        o = jnp.dot(p.astype(BF16), v, preferred_element_type=F32)
        acc = o if acc is None else acc + o
        den = d if den is None else den + d
    return acc * (1.0 / den)


CTX_NB = 2


def _attn_ctx_kernel(ha_ref, hb_ref, qg_ref, kg_ref, *rest):
    oab_ref, *kv_refs = rest[-5:]
    prev_refs = rest[:-5]
    if prev_refs:
        for li in range(len(prev_refs) // 4):
            for dst, src in zip(kv_refs, prev_refs[4 * li:4 * li + 4]):
                dst[:, li] = src[...]
        ak_ref, av_ref, nk_ref, nv_ref = [r.at[:, DEPTH - 1] for r in kv_refs]
    else:
        ak_ref, av_ref, nk_ref, nv_ref = kv_refs
    is_lo = _is_lo_half()
    qg2 = qg_ref[...] * ATT_SCALE
    kg2 = kg_ref[...]
    for b in range(CTX_NB):
        rows = slice(b * SEQ, (b + 1) * SEQ)
        kp = _pair_rms(ha_ref[rows, A_Q:A_Q + PAIR], kg2, is_lo)
        vp = ha_ref[rows, A_Q + A_KV:A_Q + A_KV + PAIR]
        for kv in range(A_KV_HEADS):
            ak_ref[b, kv] = kp[:, kv * HEAD_DIM:(kv + 1) * HEAD_DIM]
            av_ref[b, kv] = vp[:, kv * HEAD_DIM:(kv + 1) * HEAD_DIM]
        kp_sw = pltpu.roll(kp, HEAD_DIM, axis=1)
        vp_sw = pltpu.roll(vp, HEAD_DIM, axis=1)
        qs = [_pair_rms(ha_ref[rows, c * PAIR:(c + 1) * PAIR], qg2, is_lo).astype(BF16)
              for c in range(A_HEADS // 2)]
        for kv in range(A_KV_HEADS):
            k_lo = jnp.where(is_lo, kp if kv == 0 else kp_sw, 0.0).astype(BF16)
            k_hi = jnp.where(is_lo, 0.0, kp_sw if kv == 0 else kp).astype(BF16)
            v_lo = jnp.where(is_lo, vp if kv == 0 else vp_sw, 0.0).astype(BF16)
            v_hi = jnp.where(is_lo, 0.0, vp_sw if kv == 0 else vp).astype(BF16)
            q_st = jnp.concatenate([qs[2 * kv], qs[2 * kv + 1]], axis=0)
            o = _softmax_pv([_qk(q_st, k_lo)], [v_lo]) + _softmax_pv([_qk(q_st, k_hi)], [v_hi])
            oab_ref[rows, (2 * kv) * PAIR:(2 * kv + 1) * PAIR] = o[0:SEQ].astype(BF16)
            oab_ref[rows, (2 * kv + 1) * PAIR:(2 * kv + 2) * PAIR] = o[SEQ:2 * SEQ].astype(BF16)
        for j in range(B_HEADS // 2):
            qp = (hb_ref[rows, j * PAIR:(j + 1) * PAIR] * ATT_SCALE).astype(BF16)
            kp_b = hb_ref[rows, B_W + j * PAIR:B_W + (j + 1) * PAIR]
            vp_b = hb_ref[rows, 2 * B_W + j * PAIR:2 * B_W + (j + 1) * PAIR]
            for u in range(2):
                nk_ref[b, 2 * j + u] = kp_b[:, u * HEAD_DIM:(u + 1) * HEAD_DIM]
                nv_ref[b, 2 * j + u] = vp_b[:, u * HEAD_DIM:(u + 1) * HEAD_DIM]
            k_lo, k_hi = _split_halves(kp_b, is_lo)
            v_lo, v_hi = _split_halves(vp_b, is_lo)
            o = _softmax_pv([_qk(qp, k_lo)], [v_lo]) + _softmax_pv([_qk(qp, k_hi)], [v_hi])
            oab_ref[rows, A_Q + j * PAIR:A_Q + (j + 1) * PAIR] = o.astype(BF16)


def _attn_ctx(h_a, h_b, qg2, kg2, prev_kv):
    rows = CTX_NB * SEQ
    layer_spec = lambda nh: pl.BlockSpec((CTX_NB, nh, SEQ, HEAD_DIM), lambda b: (b, 0, 0, 0))
    heads = (A_KV_HEADS, A_KV_HEADS, B_HEADS, B_HEADS)
    if prev_kv is None:
        prev_kv, prev_specs = (), []
        kv_specs = [layer_spec(nh) for nh in heads]
        kv_shapes = [jax.ShapeDtypeStruct((BATCH, nh, SEQ, HEAD_DIM), F32) for nh in heads]
    else:
        prev_specs = [layer_spec(heads[i % 4]) for i in range(len(prev_kv))]
        kv_specs = [pl.BlockSpec((CTX_NB, DEPTH, nh, SEQ, HEAD_DIM), lambda b: (b, 0, 0, 0, 0)) for nh in heads]
        kv_shapes = [jax.ShapeDtypeStruct((BATCH, DEPTH, nh, SEQ, HEAD_DIM), F32) for nh in heads]
    return pl.pallas_call(
        _attn_ctx_kernel,
        grid=(BATCH // CTX_NB,),
        in_specs=[
            pl.BlockSpec((rows, W_A), lambda b: (b, 0)),
            pl.BlockSpec((rows, W_B), lambda b: (b, 0)),
            pl.BlockSpec((1, PAIR), lambda b: (0, 0)),
            pl.BlockSpec((1, PAIR), lambda b: (0, 0)),
        ] + prev_specs,
        out_specs=[pl.BlockSpec((rows, A_Q + B_W), lambda b: (b, 0))] + kv_specs,
        out_shape=[jax.ShapeDtypeStruct((N_CTX, A_Q + B_W), BF16)] + kv_shapes,
        compiler_params=_params(("arbitrary",), 56),
        name="attn_ctx",
    )(h_a, h_b, qg2, kg2, *prev_kv)


LAT_TQ = 256
LAT_QROWS = LAT_TQ // GRID_W
LAT_NT = DEC_SEQ // LAT_TQ
NA_UNION_ROWS = 12
NA_UNION = NA_UNION_ROWS * GRID_W


def _na_union_row0(t):
    lo = min(max(r - NA_WIN_R // 2, 0) for r in range(t * LAT_QROWS, (t + 1) * LAT_QROWS))
    return min(min(lo, LAT_ROWS - NA_WIN_R), LAT_ROWS - NA_UNION_ROWS)


def _rope(x, cos, sin):
    lane = lax.broadcasted_iota(jnp.int32, x.shape, 1)
    nxt = pltpu.roll(x, x.shape[1] - 1, axis=1)
    prv = pltpu.roll(x, 1, axis=1)
    partner = jnp.where((lane & 1) == 0, nxt, prv)
    return x * cos + partner * sin


def _attn_lat_kernel(pick_ref, haq_ref, haf_ref, hbq_ref, hbf_ref, cak_ref, cav_ref, cnk_ref, cnv_ref,
                     qg_ref, kg_ref, cos_ref, sin_ref, toep_ref, oab_ref, kva_ref):
    t = pl.program_id(1)
    is_lo = _is_lo_half()
    row0 = pl.multiple_of(t * LAT_TQ, LAT_TQ)
    qg2 = qg_ref[...] * ATT_SCALE
    kg2 = kg_ref[...]

    @pl.when(t == 0)
    def _():
        kp = _rope(_pair_rms(haf_ref[:, A_Q:A_Q + PAIR].astype(F32), kg2, is_lo), cos_ref[...], sin_ref[...])
        vp = haf_ref[:, A_Q + A_KV:A_Q + A_KV + PAIR].astype(F32)
        kp_sw = pltpu.roll(kp, HEAD_DIM, axis=1)
        vp_sw = pltpu.roll(vp, HEAD_DIM, axis=1)
        for kv in range(A_KV_HEADS):
            kva_ref[4 * kv + 0] = jnp.where(is_lo, kp if kv == 0 else kp_sw, 0.0).astype(BF16)
            kva_ref[4 * kv + 1] = jnp.where(is_lo, 0.0, kp_sw if kv == 0 else kp).astype(BF16)
            kva_ref[4 * kv + 2] = jnp.where(is_lo, vp if kv == 0 else vp_sw, 0.0).astype(BF16)
            kva_ref[4 * kv + 3] = jnp.where(is_lo, 0.0, vp_sw if kv == 0 else vp).astype(BF16)

    cos_q = cos_ref[pl.ds(row0, LAT_TQ), :]
    sin_q = sin_ref[pl.ds(row0, LAT_TQ), :]
    qs = [_rope(_pair_rms(haq_ref[:, c * PAIR:(c + 1) * PAIR].astype(F32), qg2, is_lo), cos_q, sin_q).astype(BF16)
          for c in range(A_HEADS // 2)]
    for kv in range(A_KV_HEADS):
        kc = cak_ref[kv].astype(BF16)
        vc = cav_ref[kv].astype(BF16)
        q_st = jnp.concatenate([qs[2 * kv], qs[2 * kv + 1]], axis=0)
        o = (_softmax_pv([_qk(q_st, _pad_lo(kc)), _qk(q_st, kva_ref[4 * kv + 0])], [_pad_lo(vc), kva_ref[4 * kv + 2]])
             + _softmax_pv([_qk(q_st, _pad_hi(kc)), _qk(q_st, kva_ref[4 * kv + 1])], [_pad_hi(vc), kva_ref[4 * kv + 3]]))
        oab_ref[:, (2 * kv) * PAIR:(2 * kv + 1) * PAIR] = o[0:LAT_TQ].astype(BF16)
        oab_ref[:, (2 * kv + 1) * PAIR:(2 * kv + 2) * PAIR] = o[LAT_TQ:2 * LAT_TQ].astype(BF16)

    k0 = _na_union_row0(0) * GRID_W
    for tt in range(1, LAT_NT):
        k0 = jnp.where(t >= tt, _na_union_row0(tt) * GRID_W, k0)
    k0 = pl.multiple_of(k0, GRID_W)

    def bias(h):
        base = t * (LAT_QROWS * NA_UNION_ROWS)
        return jnp.concatenate(
            [jnp.concatenate([toep_ref[h, pick_ref[base + i * NA_UNION_ROWS + u]] for u in range(NA_UNION_ROWS)],
                             axis=1) for i in range(LAT_QROWS)], axis=0)

    for j in range(B_HEADS // 2):
        qp = (hbq_ref[:, j * PAIR:(j + 1) * PAIR] * ATT_SCALE).astype(BF16)
        k_lo, k_hi = _split_halves(hbf_ref[pl.ds(k0, NA_UNION), B_W + j * PAIR:B_W + (j + 1) * PAIR], is_lo)
        v_lo, v_hi = _split_halves(hbf_ref[pl.ds(k0, NA_UNION), 2 * B_W + j * PAIR:2 * B_W + (j + 1) * PAIR], is_lo)
        kc_lo = _pad_lo(cnk_ref[2 * j].astype(BF16))
        kc_hi = _pad_hi(cnk_ref[2 * j + 1].astype(BF16))
        vc_lo = _pad_lo(cnv_ref[2 * j].astype(BF16))
        vc_hi = _pad_hi(cnv_ref[2 * j + 1].astype(BF16))
        o = (_softmax_pv([_qk(qp, kc_lo), _qk(qp, k_lo) + bias(2 * j)], [vc_lo, v_lo])
             + _softmax_pv([_qk(qp, kc_hi), _qk(qp, k_hi) + bias(2 * j + 1)], [vc_hi, v_hi]))
        oab_ref[:, A_Q + j * PAIR:A_Q + (j + 1) * PAIR] = o.astype(BF16)


def _attn_lat(h_a, h_b, cak, cav, cnk, cnv, layer, qg2, kg2, cos128, sin128, toep):
    nt = LAT_NT
    cache_spec = lambda nh: pl.BlockSpec((None, None, nh, PAST_LEN, HEAD_DIM), lambda b, t, _: (b, layer, 0, 0, 0))
    const = lambda shape: pl.BlockSpec(shape, lambda b, t, _: (0,) * len(shape))
    grid_spec = pltpu.PrefetchScalarGridSpec(
        num_scalar_prefetch=1,
        grid=(DEC_BATCH, nt),
        in_specs=[
            pl.BlockSpec((LAT_TQ, W_A), lambda b, t, _: (b * nt + t, 0)),
            pl.BlockSpec((DEC_SEQ, W_A), lambda b, t, _: (b, 0)),
            pl.BlockSpec((LAT_TQ, W_B), lambda b, t, _: (b * nt + t, 0)),
            pl.BlockSpec((DEC_SEQ, W_B), lambda b, t, _: (b, 0)),
            cache_spec(A_KV_HEADS), cache_spec(A_KV_HEADS), cache_spec(B_HEADS), cache_spec(B_HEADS),
            const((1, PAIR)), const((1, PAIR)), const((DEC_SEQ, PAIR)), const((DEC_SEQ, PAIR)),
            const(toep.shape),
        ],
        out_specs=pl.BlockSpec((LAT_TQ, A_Q + B_W), lambda b, t, _: (b * nt + t, 0)),
        scratch_shapes=[pltpu.VMEM((4 * A_KV_HEADS, DEC_SEQ, PAIR), BF16)],
    )
    return pl.pallas_call(
        _attn_lat_kernel,
        grid_spec=grid_spec,
        out_shape=jax.ShapeDtypeStruct((N_LAT, A_Q + B_W), BF16),
        compiler_params=_params(("arbitrary", "arbitrary"), 56),
        name="attn_lat",
    )(jnp.asarray(_na_tile_picks().reshape(-1)), h_a, h_a, h_b, h_b, cak, cav, cnk, cnv, qg2, kg2, cos128, sin128,
      toep)


def _rope_tables():
    t = jnp.arange(DEC_SEQ)
    row = (t // GRID_W).astype(F32)
    col = (t % GRID_W).astype(F32)
    n_freq = HEAD_DIM // 4
    inv = ROPE_THETA ** (-jnp.arange(n_freq, dtype=F32) / n_freq)
    ang = jnp.concatenate([row[:, None] * inv, col[:, None] * inv], axis=-1)
    cos = jnp.repeat(jnp.cos(ang), 2, axis=-1)
    sign = jnp.tile(jnp.array([-1.0, 1.0], F32), HEAD_DIM // 2)
    sin = jnp.repeat(jnp.sin(ang), 2, axis=-1) * sign
    return jnp.tile(cos, (1, 2)), jnp.tile(sin, (1, 2))


NA_MASKED_TILE = 2 * NA_WIN_R - 1


def _na_toeplitz(rpb):
    col = np.arange(GRID_W)
    dc = np.clip(col[None, :] - col[:, None] + NA_WIN_C - 1, 0, 2 * NA_WIN_C - 2)
    onehot = (dc[..., None] == np.arange(2 * NA_WIN_C - 1)).astype(np.float32)
    c0 = np.clip(col - NA_WIN_C // 2, 0, GRID_W - NA_WIN_C)
    in_win = (col[None, :] >= c0[:, None]) & (col[None, :] < c0[:, None] + NA_WIN_C)
    toep = jnp.einsum('hdm,ckm->hdck', rpb, onehot, precision=lax.Precision.HIGHEST)
    toep = jnp.where(in_win[None, None], toep, MASK_NEG)
    return jnp.concatenate([toep, jnp.full((B_HEADS, 1, GRID_W, GRID_W), MASK_NEG, F32)], axis=1)


def _na_tile_picks():
    pick = np.full((LAT_NT, LAT_QROWS, NA_UNION_ROWS), NA_MASKED_TILE, np.int32)
    for t in range(LAT_NT):
        u0 = _na_union_row0(t)
        for i in range(LAT_QROWS):
            r = t * LAT_QROWS + i
            r0 = min(max(r - NA_WIN_R // 2, 0), LAT_ROWS - NA_WIN_R)
            for u in range(NA_UNION_ROWS):
                if r0 <= u0 + u < r0 + NA_WIN_R:
                    pick[t, i, u] = u0 + u - r + NA_WIN_R - 1
    return pick


def _conv_kernel(hc_ref, w_ref, o_ref, *, seq):
    bg = hc_ref[:, 0:C_WIDTH].astype(F32)
    u = hc_ref[:, C_WIDTH:2 * C_WIDTH].astype(F32) * hc_ref[:, 2 * C_WIDTH:3 * C_WIDTH].astype(F32)
    rows = u.shape[0]
    assert seq & (seq - 1) == 0
    pos = lax.broadcasted_iota(jnp.int32, u.shape, 0) & (seq - 1)
    u_prev = jnp.where(pos == 0, 0.0, pltpu.roll(u, 1, axis=0))
    u_next = jnp.where(pos == seq - 1, 0.0, pltpu.roll(u, rows - 1, axis=0))
    y = u_prev * w_ref[0:1, :] + u * w_ref[1:2, :] + u_next * w_ref[2:3, :]
    o_ref[...] = (bg * y).astype(o_ref.dtype)


def _conv(h_c, conv_w, seq):
    n = h_c.shape[0]
    tm = 1024
    return pl.pallas_call(
        functools.partial(_conv_kernel, seq=seq),
        grid=(n // tm,),
        in_specs=[pl.BlockSpec((tm, W_C), lambda i: (i, 0)), pl.BlockSpec((3, C_WIDTH), lambda i: (0, 0))],
        out_specs=pl.BlockSpec((tm, C_WIDTH), lambda i: (i, 0)),
        out_shape=jax.ShapeDtypeStruct((n, C_WIDTH), BF16),
        compiler_params=_params(("arbitrary",), 40),
        name="short_conv",
    )(h_c, conv_w)


def _mix_kernel(x_ref, oab_ref, oc_ref, hg_ref, wa_ref, wb_ref, wc_ref, wo_ref, g1_ref, n2g_ref, sc2_ref, sh2_ref,
                wrt_ref, xo_ref, xn2_ref, pt_ref):
    ga = jax.nn.sigmoid(hg_ref[:, 0:D_MODEL].astype(F32))
    gb = jax.nn.sigmoid(hg_ref[:, D_MODEL:2 * D_MODEL].astype(F32))
    gc = jax.nn.sigmoid(hg_ref[:, 2 * D_MODEL:3 * D_MODEL].astype(F32))
    pa = jnp.dot(oab_ref[:, 0:A_Q], wa_ref[...], preferred_element_type=F32)
    pb = jnp.dot(oab_ref[:, A_Q:A_Q + B_W], wb_ref[...], preferred_element_type=F32)
    pc = jnp.dot(oc_ref[...], wc_ref[...], preferred_element_type=F32)
    merged = ga * pa + gb * pb + gc * pc
    xo = x_ref[...] + g1_ref[...] * jnp.dot(merged.astype(BF16), wo_ref[...], preferred_element_type=F32)
    xo_ref[...] = xo
    y = xo * lax.rsqrt(jnp.mean(xo * xo, axis=-1, keepdims=True) + EPS) * n2g_ref[...]
    xn2 = y * (1.0 + sc2_ref[...]) + sh2_ref[...]
    rows = xn2.shape[0]
    for j in range(ROW_TILE):
        xn2_ref[pl.ds(j, rows, stride=ROW_TILE), :] = xn2[:, j * LANES:(j + 1) * LANES]
    logits_t = lax.dot_general(wrt_ref[...], xn2.astype(BF16), _NT, preferred_element_type=F32)
    z = jnp.exp(logits_t - logits_t.max(axis=0, keepdims=True))
    pt_ref[...] = z / z.sum(axis=0, keepdims=True)


def _mix(x, oab, oc, h_g, wa, wb, wc, wo, g1, n2g, sc2, sh2, wrt, seq):
    n = x.shape[0]
    tm = 512
    per_batch = g1.shape[0] > 1
    mod_map = (lambda i: ((i * tm) // seq, 0, 0)) if per_batch else (lambda i: (0, 0, 0))
    mod_spec = pl.BlockSpec((None, 1, D_MODEL), mod_map)
    full = lambda a: pl.BlockSpec(a.shape, lambda i: (0,) * a.ndim)
    row = lambda w: pl.BlockSpec((tm, w), lambda i: (i, 0))
    return pl.pallas_call(
        _mix_kernel,
        grid=(n // tm,),
        in_specs=[row(D_MODEL), row(A_Q + B_W), row(C_WIDTH), row(GATE_W), full(wa), full(wb), full(wc), full(wo),
                  mod_spec, full(n2g), mod_spec, mod_spec, full(wrt)],
        out_specs=[row(D_MODEL), pl.BlockSpec((tm * ROW_TILE, LANES), lambda i: (i, 0)),
                   pl.BlockSpec((N_EXPERTS, tm), lambda i: (0, i))],
        out_shape=[jax.ShapeDtypeStruct((n, D_MODEL), F32), jax.ShapeDtypeStruct((n * ROW_TILE, LANES), F32),
                   jax.ShapeDtypeStruct((N_EXPERTS, n), F32)],
        compiler_params=_params(("arbitrary",), 56),
        name="mix",
    )(x, oab, oc, h_g, wa, wb, wc, wo, g1, n2g, sc2, sh2, wrt)


MOE_TF = 512
MOE_ROW_CHUNKS = ((0, 640), (640, 640))
MOE_ISSUE_UNROLL = 16


def _moe_issue_rows(idx_ref, idx0, src_hbm, dst_ref, dst_row0, count, sem):
    def body(i, carry):
        for u in range(MOE_ISSUE_UNROLL):
            r = i * MOE_ISSUE_UNROLL + u
            tok = idx_ref[idx0 + r]
            src = src_hbm.at[pl.ds(pl.multiple_of(tok * ROW_TILE, ROW_TILE), ROW_TILE)]
            dst = dst_ref.at[pl.ds(pl.multiple_of((dst_row0 + r) * ROW_TILE, ROW_TILE), ROW_TILE)]
            pltpu.make_async_copy(src, dst, sem).start(priority=u % 2)
        return carry

    lax.fori_loop(0, count // MOE_ISSUE_UNROLL, body, 0)


def _moe_kernel(idx_ref, xc_hbm, xl_hbm, aff_ref, wg_ref, wu_ref, wd_ref, ye_ref, xe_ref, sem):
    e = pl.program_id(0)
    f = pl.program_id(1)
    slot = e % 2

    def gather(expert, s):
        base = expert * CAP_ALL
        _moe_issue_rows(idx_ref, base, xc_hbm, xe_ref.at[s], 0, CAP_CTX, sem.at[s])
        _moe_issue_rows(idx_ref, base + CAP_CTX, xl_hbm, xe_ref.at[s], CAP_CTX, CAP_LAT, sem.at[s])

    @pl.when(f == 0)
    def _():
        @pl.when(e == 0)
        def _():
            gather(0, 0)

        pltpu.make_async_copy(xc_hbm.at[pl.ds(0, CAP_ALL * ROW_TILE)], xe_ref.at[slot], sem.at[slot]).wait()

        @pl.when(e + 1 < N_EXPERTS)
        def _():
            gather(e + 1, 1 - slot)

    wg = wg_ref[...].astype(BF16)
    wu = wu_ref[...].astype(BF16)
    wd = wd_ref[...].astype(BF16)
    for r0, rn in MOE_ROW_CHUNKS:
        xe = jnp.concatenate([xe_ref[slot, pl.ds(r0 * ROW_TILE + j, rn, stride=ROW_TILE), :]
                              for j in range(ROW_TILE)], axis=1).astype(BF16)
        hg = jnp.dot(xe, wg, preferred_element_type=F32)
        hu = jnp.dot(xe, wu, preferred_element_type=F32)
        act = (hg * jax.nn.sigmoid(hg) * hu).astype(BF16)
        part = jnp.dot(act, wd, preferred_element_type=F32) * aff_ref[r0:r0 + rn, :]

        @pl.when(f == 0)
        def _():
            ye_ref[r0:r0 + rn, :] = part

        @pl.when(f != 0)
        def _():
            ye_ref[r0:r0 + rn, :] += part


def _moe(idx_all, xn2_ctx, xn2_lat, aff_all, w_gate_e, w_up_e, w_down_e, layer):
    nf = EXPERT_FF // MOE_TF
    grid_spec = pltpu.PrefetchScalarGridSpec(
        num_scalar_prefetch=1,
        grid=(N_EXPERTS, nf),
        in_specs=[
            pl.BlockSpec(memory_space=pl.ANY),
            pl.BlockSpec(memory_space=pl.ANY),
            pl.BlockSpec((None, CAP_ALL, 1), lambda e, f, idx: (e, 0, 0)),
            pl.BlockSpec((None, None, D_MODEL, MOE_TF), lambda e, f, idx: (layer, e, 0, f)),
            pl.BlockSpec((None, None, D_MODEL, MOE_TF), lambda e, f, idx: (layer, e, 0, f)),
            pl.BlockSpec((None, None, MOE_TF, D_MODEL), lambda e, f, idx: (layer, e, f, 0)),
        ],
        out_specs=pl.BlockSpec((None, CAP_ALL, D_MODEL), lambda e, f, idx: (e, 0, 0)),
        scratch_shapes=[pltpu.VMEM((2, CAP_ALL * ROW_TILE, LANES), F32), pltpu.SemaphoreType.DMA((2,))],
    )
    return pl.pallas_call(
        _moe_kernel,
        grid_spec=grid_spec,
        out_shape=jax.ShapeDtypeStruct((N_EXPERTS, CAP_ALL, D_MODEL), F32),
        compiler_params=_params(("arbitrary", "arbitrary"), 56),
        name="moe_experts",
    )(idx_all, xn2_ctx, xn2_lat, aff_all, w_gate_e, w_up_e, w_down_e)


POS_SPLIT = 32


def _prefix_count(mask_bf, tri_lane, tri_blk_strict):
    within = jnp.dot(mask_bf, tri_lane, preferred_element_type=F32)
    bs = jnp.broadcast_to(within[:, LANES - 1:LANES], within.shape).astype(BF16)
    return within + jnp.dot(tri_blk_strict, bs, preferred_element_type=F32)


def _route_kernel(p_ref, idx_ref, aff_ref, *, cap):
    p_all = p_ref[...]
    nb = p_all.shape[1]
    capf = float(cap)

    def as_float(word):
        return lax.bitcast_convert_type(word, F32)

    def bisect(_, c):
        lo, hi = c
        mid = lo + ((hi - lo) >> 1)
        ok = jnp.sum(jnp.where(p_all >= as_float(mid), 1.0, 0.0), axis=(1, 2), keepdims=True) >= capf
        return jnp.where(ok, mid, lo), jnp.where(ok, hi, mid)

    lo0 = jnp.zeros((N_EXPERTS, 1, 1), jnp.int32)
    hi0 = jnp.full((N_EXPERTS, 1, 1), 0x7F800000, jnp.int32)
    thr_all, _ = lax.fori_loop(0, 31, bisect, (lo0, hi0))
    thr_val = as_float(thr_all)

    li = lax.broadcasted_iota(jnp.int32, (LANES, LANES), 0)
    lj = lax.broadcasted_iota(jnp.int32, (LANES, LANES), 1)
    tri_lane = jnp.where(li <= lj, 1.0, 0.0).astype(BF16)
    bi = lax.broadcasted_iota(jnp.int32, (nb, nb), 0)
    bj = lax.broadcasted_iota(jnp.int32, (nb, nb), 1)
    tri_blk_strict = jnp.where(bj < bi, 1.0, 0.0).astype(BF16)
    tri_blk_incl = jnp.where(bi <= bj, 1.0, 0.0).astype(BF16)
    s_col = lax.broadcasted_iota(jnp.int32, (cap, 1), 0).astype(F32)
    blk_id = lax.broadcasted_iota(jnp.int32, (1, nb), 1).astype(F32)
    lane_id = lax.broadcasted_iota(jnp.int32, (1, LANES), 1).astype(F32)
    ones_rows = jnp.ones((8, LANES), BF16)

    for e in range(N_EXPERTS):
        p = p_all[e]
        thr = thr_val[e]
        gt = p > thr
        eq = p == thr
        need = capf - jnp.sum(jnp.where(gt, 1.0, 0.0), keepdims=True)
        tie_rank = _prefix_count(jnp.where(eq, 1.0, 0.0).astype(BF16), tri_lane, tri_blk_strict)
        sel = jnp.where(gt | (eq & (tie_rank <= need)), 1.0, 0.0).astype(BF16)
        pos = _prefix_count(sel, tri_lane, tri_blk_strict)

        bs_row = lax.dot_general(ones_rows, sel, _NT, preferred_element_type=F32)
        bp_row = jnp.dot(bs_row.astype(BF16), tri_blk_incl, preferred_element_type=F32)[0:1, :]
        blk = jnp.sum(jnp.where(bp_row <= s_col, 1.0, 0.0), axis=-1, keepdims=True)
        onehot = jnp.where(blk == blk_id, 1.0, 0.0).astype(BF16)
        pos_hi = jnp.floor(pos * (1.0 / POS_SPLIT))
        pos_lo = pos - pos_hi * POS_SPLIT
        pos_row = (jnp.dot(onehot, pos_hi.astype(BF16), preferred_element_type=F32) * POS_SPLIT
                   + jnp.dot(onehot, pos_lo.astype(BF16), preferred_element_type=F32))
        lane = jnp.sum(jnp.where(pos_row <= s_col, 1.0, 0.0), axis=-1, keepdims=True)
        idx_ref[e] = (blk * LANES + lane).astype(jnp.int32)

        p1 = p.astype(BF16)
        r1 = p - p1.astype(F32)
        p2 = r1.astype(BF16)
        p3 = (r1 - p2.astype(F32)).astype(BF16)
        p_row = (jnp.dot(onehot, p1, preferred_element_type=F32) + jnp.dot(onehot, p2, preferred_element_type=F32)
                 + jnp.dot(onehot, p3, preferred_element_type=F32))
        aff_ref[e] = jnp.sum(jnp.where(lane_id == lane, p_row, 0.0), axis=-1, keepdims=True)


def _route(probs_t, cap):
    nb = probs_t.shape[1] // LANES
    slot = pl.BlockSpec((N_EXPERTS, cap, 1), lambda i: (0, 0, 0))
    return pl.pallas_call(
        functools.partial(_route_kernel, cap=cap),
        grid=(1,),
        in_specs=[pl.BlockSpec((N_EXPERTS, nb, LANES), lambda i: (0, 0, 0))],
        out_specs=[slot, slot],
        out_shape=[jax.ShapeDtypeStruct((N_EXPERTS, cap, 1), jnp.int32),
                   jax.ShapeDtypeStruct((N_EXPERTS, cap, 1), F32)],
        compiler_params=_params(("arbitrary",), 56),
        name="route",
    )(probs_t.reshape(N_EXPERTS, nb, LANES))


CMB_TT = 1024
CMB_CH = 256
CMB_GROUP = 8
CMB_EPI_ROWS = 256


def _combine_kernel(idx_ref, st_ref, ye_hbm, x_ref, g2_ref, ng_ref, sc_ref, sh_ref, *rest, off, cap, nt, final):
    if final:
        y_ref, acc_ref, buf_ref, sem = rest
    else:
        xo_ref, xn_ref, acc_ref, buf_ref, sem = rest
    k = pl.program_id(0)
    t0 = k * CMB_TT
    acc_ref[...] = jnp.zeros_like(acc_ref)

    def slot_range(e):
        s0 = st_ref[e * (nt + 1) + k]
        s1 = st_ref[e * (nt + 1) + k + 1]
        return s0, s1, jnp.minimum(s0 & -CMB_GROUP, cap - CMB_CH)

    def chunk_copy(e, start):
        src = ye_hbm.at[e, pl.ds(pl.multiple_of(off + start, CMB_GROUP), CMB_CH)]
        return pltpu.make_async_copy(src, buf_ref.at[e], sem.at[e])

    def accumulate(e, start, lo, hi):
        def body(i, carry):
            r0 = pl.multiple_of(i * CMB_GROUP, CMB_GROUP)
            tile = buf_ref[e, pl.ds(r0, CMB_GROUP), :]
            dst = []
            for u in range(CMB_GROUP):
                slot = start + r0 + u
                tok = idx_ref[e * CAP_ALL + off + slot]
                dst.append(jnp.where((slot >= lo) & (slot < hi), tok - t0, CMB_TT))
            rows = [acc_ref[pl.ds(d, 1), :] for d in dst]
            for u in range(CMB_GROUP):
                acc_ref[pl.ds(dst[u], 1), :] = rows[u] + tile[u:u + 1, :]
            return carry

        lax.fori_loop((lo - start) // CMB_GROUP, (hi - start + CMB_GROUP - 1) // CMB_GROUP, body, 0)

    for e in range(N_EXPERTS):
        chunk_copy(e, slot_range(e)[2]).start()
    for e in range(N_EXPERTS):
        s0, s1, a = slot_range(e)
        chunk_copy(e, a).wait()
        accumulate(e, a, s0, jnp.minimum(s1, a + CMB_CH))

        def extra(j, carry, e=e, s1=s1, a=a):
            lo = a + CMB_CH * (j + 1)
            start = jnp.minimum(lo, cap - CMB_CH)
            cp = chunk_copy(e, start)
            cp.start()
            cp.wait()
            accumulate(e, start, lo, jnp.minimum(s1, lo + CMB_CH))
            return carry

        lax.fori_loop(0, jnp.maximum(s1 - a - 1, 0) // CMB_CH, extra, 0)

    for r0 in range(0, CMB_TT, CMB_EPI_ROWS):
        rs = slice(r0, r0 + CMB_EPI_ROWS)
        xo = x_ref[rs, :] + g2_ref[...] * acc_ref[rs, :]
        y = xo * lax.rsqrt(jnp.mean(xo * xo, axis=-1, keepdims=True) + EPS) * ng_ref[...]
        if final:
            y_ref[rs, :] = y
        else:
            xo_ref[rs, :] = xo
            xn_ref[rs, :] = (y * (1.0 + sc_ref[...]) + sh_ref[...]).astype(xn_ref.dtype)


def _combine(idx_all, starts, ye, x, g2, ng, sc, sh, seq, off, cap, final):
    n = x.shape[0]
    nt = n // CMB_TT
    per_batch = g2.shape[0] > 1
    mod_map = ((lambda i, *_: ((i * CMB_TT) // seq, 0, 0)) if per_batch else (lambda i, *_: (0, 0, 0)))
    mod_spec = pl.BlockSpec((None, 1, D_MODEL), mod_map)
    row = pl.BlockSpec((CMB_TT, D_MODEL), lambda i, *_: (i, 0))
    vec = pl.BlockSpec((1, D_MODEL), lambda i, *_: (0, 0))
    if final:
        out_specs, out_shape = row, jax.ShapeDtypeStruct((n, D_MODEL), F32)
    else:
        out_specs = [row, row]
        out_shape = [jax.ShapeDtypeStruct((n, D_MODEL), F32), jax.ShapeDtypeStruct((n, D_MODEL), BF16)]
    grid_spec = pltpu.PrefetchScalarGridSpec(
        num_scalar_prefetch=2,
        grid=(nt,),
        in_specs=[pl.BlockSpec(memory_space=pl.ANY), row, mod_spec, vec, mod_spec, mod_spec],
        out_specs=out_specs,
        scratch_shapes=[pltpu.VMEM((CMB_TT + CMB_GROUP, D_MODEL), F32),
                        pltpu.VMEM((N_EXPERTS, CMB_CH, D_MODEL), F32),
                        pltpu.SemaphoreType.DMA((N_EXPERTS,))],
    )
    return pl.pallas_call(
        functools.partial(_combine_kernel, off=off, cap=cap, nt=nt, final=final),
        grid_spec=grid_spec,
        out_shape=out_shape,
        compiler_params=_params(("arbitrary",), 60),
        name="combine",
    )(idx_all, starts.reshape(-1), ye, x, g2, ng, sc, sh)


def _tile_starts(idx_sorted, n):
    bounds = jnp.arange(n // CMB_TT + 1, dtype=jnp.int32) * CMB_TT
    return jnp.sum(idx_sorted[:, :, None] < bounds[None, None, :], axis=1, dtype=jnp.int32)


def kernel(x_prompt, x_sample, cache_attn_k, cache_attn_v, cache_na_k, cache_na_v, c, c_ctx, w_ada, b_ada, norm1_g,
           w_in, q_norm_g, k_norm_g, na_rpb, conv_w, w_proj_a, w_proj_b, w_proj_c, w_out, norm2_g, w_router,
           w_gate_e, w_up_e, w_down_e, final_g):
    cond8 = jnp.zeros((8, D_MODEL), F32).at[0].set(c_ctx).at[1:1 + DEC_BATCH].set(c)
    mods = _adaln(cond8, w_ada, b_ada).reshape(DEPTH, 8, N_MOD, D_MODEL)
    cos128, sin128 = _rope_tables()

    hp = x_prompt.reshape(N_CTX, D_MODEL)
    hs = x_sample.reshape(N_LAT, D_MODEL)
    kv_layers = []
    for l in range(DEPTH):
        mod_ctx = [mods[l, 0:1, i][:, None, :] for i in range(N_MOD)]
        mod_lat = [mods[l, 1:1 + DEC_BATCH, i][:, None, :] for i in range(N_MOD)]
        n1g = norm1_g[l][None]
        n2g = norm2_g[l][None]
        qg2 = jnp.tile(q_norm_g[l][None], (1, 2))
        kg2 = jnp.tile(k_norm_g[l][None], (1, 2))
        wa, wb, wc, wo = (w_proj_a[l].astype(BF16), w_proj_b[l].astype(BF16), w_proj_c[l].astype(BF16),
                          w_out[l].astype(BF16))
        wrt = w_router[l].T.astype(BF16)
        toep = _na_toeplitz(na_rpb[l])

        def sublayer1(x, xn, mod, seq, is_ctx):
            _, _, g1, sh2, sc2, _ = mod
            kv_dtype = F32 if is_ctx else BF16
            h_a = _inproj(xn, w_in, l, COL_A, W_A, 768, kv_dtype)
            h_b = _inproj(xn, w_in, l, COL_B, W_B, 768, kv_dtype)
            h_c = _inproj(xn, w_in, l, COL_C, W_C, 768, BF16)
            h_g = _inproj(xn, w_in, l, COL_G, GATE_W, 768, BF16)
            kv = None
            if is_ctx:
                oab, *kv = _attn_ctx(h_a, h_b, qg2, kg2, kv_layers if l == DEPTH - 1 else None)
            else:
                oab = _attn_lat(h_a, h_b, cache_attn_k, cache_attn_v, cache_na_k, cache_na_v, l, qg2, kg2,
                                cos128, sin128, toep)
            oc = _conv(h_c, conv_w[l], seq)
            xo, xn2, probs_t = _mix(x, oab, oc, h_g, wa, wb, wc, wo, g1, n2g, sc2, sh2, wrt, seq)
            return xo, xn2, probs_t, kv

        if l == 0:
            xn_p = _norm_mod(hp, n1g, mod_ctx[1], mod_ctx[0], SEQ, BF16)
            xn_s = _norm_mod(hs, n1g, mod_lat[1], mod_lat[0], DEC_SEQ, BF16)
        hp, xn2_c, pt_c, kv_out = sublayer1(hp, xn_p, mod_ctx, SEQ, True)
        kv_layers = kv_layers + list(kv_out)
        hs, xn2_l, pt_l, _ = sublayer1(hs, xn_s, mod_lat, DEC_SEQ, False)

        idx_c, aff_c = _route(pt_c, CAP_CTX)
        idx_l, aff_l = _route(pt_l, CAP_LAT)
        idx_all = jnp.concatenate([idx_c, idx_l], axis=1).reshape(-1)
        aff_all = jnp.concatenate([aff_c, aff_l], axis=1)
        ye = _moe(idx_all, xn2_c, xn2_l, aff_all, w_gate_e, w_up_e, w_down_e, l)

        final = l == DEPTH - 1
        if final:
            ng, nsc_c, nsh_c, nsc_l, nsh_l = final_g[None], mod_ctx[1], mod_ctx[0], mod_lat[1], mod_lat[0]
        else:
            nxt_ctx = [mods[l + 1, 0:1, i][:, None, :] for i in range(2)]
            nxt_lat = [mods[l + 1, 1:1 + DEC_BATCH, i][:, None, :] for i in range(2)]
            ng, nsc_c, nsh_c, nsc_l, nsh_l = norm1_g[l + 1][None], nxt_ctx[1], nxt_ctx[0], nxt_lat[1], nxt_lat[0]
        out_c = _combine(idx_all, _tile_starts(idx_c[..., 0], N_CTX), ye, hp, mod_ctx[5], ng, nsc_c, nsh_c,
                         SEQ, 0, CAP_CTX, final)
        out_l = _combine(idx_all, _tile_starts(idx_l[..., 0], N_LAT), ye, hs, mod_lat[5], ng, nsc_l, nsh_l,
                         DEC_SEQ, CAP_CTX, CAP_LAT, final)
        if final:
            y_prompt = out_c.reshape(BATCH, SEQ, D_MODEL)
            y_sample = out_l.reshape(DEC_BATCH, DEC_SEQ, D_MODEL)
        else:
            (hp, xn_p), (hs, xn_s) = out_c, out_l

    return (y_prompt, y_sample, *kv_out)
```

```python
import functools

import jax
import jax.numpy as jnp
import numpy as np
from jax import lax
from jax.experimental import pallas as pl
from jax.experimental.pallas import tpu as pltpu

D_MODEL = 1024
BATCH = 32
SEQ = 256
DEPTH = 2
DEC_BATCH = 2
DEC_SEQ = 1024
PAST_LEN = 256
GRID_W = 64
HEAD_DIM = 64
A_HEADS = 8
A_KV_HEADS = 2
B_HEADS = 8
C_WIDTH = 512
NA_WIN_R = 8
NA_WIN_C = 16
N_EXPERTS = 16
EXPERT_FF = 1024
CAP_FACTOR = 2
ROPE_THETA = 10000.0
EPS = 1e-6
N_MOD = 6

A_Q = A_HEADS * HEAD_DIM
A_KV = A_KV_HEADS * HEAD_DIM
B_W = B_HEADS * HEAD_DIM
GATE_W = 3 * D_MODEL
COL_A = 0
W_A = A_Q + 2 * A_KV
COL_B = COL_A + W_A
W_B = 3 * B_W
COL_C = COL_B + W_B
W_C = 3 * C_WIDTH
COL_G = COL_C + W_C

N_CTX = BATCH * SEQ
N_LAT = DEC_BATCH * DEC_SEQ
CAP_CTX = CAP_FACTOR * N_CTX // N_EXPERTS
CAP_LAT = CAP_FACTOR * N_LAT // N_EXPERTS
CAP_ALL = CAP_CTX + CAP_LAT
LAT_ROWS = DEC_SEQ // GRID_W
MASK_NEG = -1e30
ATT_SCALE = HEAD_DIM ** -0.5

LANES = 128
ROW_TILE = 8
assert D_MODEL == ROW_TILE * LANES

F32 = jnp.float32
BF16 = jnp.bfloat16
_NT = (((1,), (1,)), ((), ()))

_MIB = 1024 * 1024


VMEM_MIB = {"adaln": 40, "norm_mod": 32, "inproj": 48, "attn_ctx": 56, "attn_lat": 56, "short_conv": 40,
            "mix": 56, "moe_experts": 56, "route": 56, "combine": 60}


def _params(name, grid_rank):
    return pltpu.CompilerParams(dimension_semantics=("arbitrary",) * grid_rank,
                                vmem_limit_bytes=VMEM_MIB[name] * _MIB)


def _adaln_kernel(cond_ref, w_ref, b_ref, o_ref):
    c = cond_ref[...]
    s = c * jax.nn.sigmoid(c)
    o_ref[...] = jnp.dot(s.astype(BF16), w_ref[...].astype(BF16), preferred_element_type=F32) + b_ref[...]


def _adaln(cond8, w_ada, b_ada):
    tn = 1536
    ncol = N_MOD * D_MODEL
    return pl.pallas_call(
        _adaln_kernel,
        grid=(DEPTH, ncol // tn),
        in_specs=[
            pl.BlockSpec((8, D_MODEL), lambda l, j: (0, 0)),
            pl.BlockSpec((None, D_MODEL, tn), lambda l, j: (l, 0, j)),
            pl.BlockSpec((None, 1, tn), lambda l, j: (l, 0, j)),
        ],
        out_specs=pl.BlockSpec((None, 8, tn), lambda l, j: (l, 0, j)),
        out_shape=jax.ShapeDtypeStruct((DEPTH, 8, ncol), F32),
        compiler_params=_params("adaln", 2),
        name="adaln",
    )(cond8, w_ada, b_ada.reshape(DEPTH, 1, ncol))


def _norm_mod_kernel(x_ref, g_ref, sc_ref, sh_ref, o_ref):
    x = x_ref[...]
    y = x * lax.rsqrt(jnp.mean(x * x, axis=-1, keepdims=True) + EPS) * g_ref[...]
    o_ref[...] = (y * (1.0 + sc_ref[...]) + sh_ref[...]).astype(o_ref.dtype)


def _norm_mod(x, g, sc, sh, seq, out_dtype):
    n = x.shape[0]
    tm = 512
    per_batch = sc.shape[0] > 1
    mod_map = (lambda i: ((i * tm) // seq, 0, 0)) if per_batch else (lambda i: (0, 0, 0))
    return pl.pallas_call(
        _norm_mod_kernel,
        grid=(n // tm,),
        in_specs=[
            pl.BlockSpec((tm, D_MODEL), lambda i: (i, 0)),
            pl.BlockSpec((1, D_MODEL), lambda i: (0, 0)),
            pl.BlockSpec((None, 1, D_MODEL), mod_map),
            pl.BlockSpec((None, 1, D_MODEL), mod_map),
        ],
        out_specs=pl.BlockSpec((tm, D_MODEL), lambda i: (i, 0)),
        out_shape=jax.ShapeDtypeStruct((n, D_MODEL), out_dtype),
        compiler_params=_params("norm_mod", 1),
        name="norm_mod",
    )(x, g, sc, sh)


def _inproj_kernel(a_ref, w_ref, o_ref, wbf_ref):
    @pl.when(pl.program_id(1) == 0)
    def _():
        wbf_ref[...] = w_ref[...].astype(BF16)

    o_ref[...] = jnp.dot(a_ref[...], wbf_ref[...], preferred_element_type=F32).astype(o_ref.dtype)


def _inproj(xn, w_in, layer, col0, width, tn, out_dtype):
    n = xn.shape[0]
    tm = min(n, 4096 if out_dtype == BF16 else 2048)
    joff = col0 // tn
    assert col0 % tn == 0 and width % tn == 0
    return pl.pallas_call(
        _inproj_kernel,
        grid=(width // tn, n // tm),
        in_specs=[
            pl.BlockSpec((tm, D_MODEL), lambda j, i: (i, 0)),
            pl.BlockSpec((None, D_MODEL, tn), lambda j, i: (layer, 0, j + joff)),
        ],
        out_specs=pl.BlockSpec((tm, tn), lambda j, i: (i, j)),
        out_shape=jax.ShapeDtypeStruct((n, width), out_dtype),
        scratch_shapes=[pltpu.VMEM((D_MODEL, tn), BF16)],
        compiler_params=_params("inproj", 2),
        name="inproj",
    )(xn, w_in)


PAIR = 2 * HEAD_DIM


def _is_lo_half():
    return lax.broadcasted_iota(jnp.int32, (1, PAIR), 1) < HEAD_DIM


def _pair_rms(x, g2, is_lo):
    x2 = x * x
    s_lo = jnp.sum(jnp.where(is_lo, x2, 0.0), axis=-1, keepdims=True)
    s_hi = jnp.sum(jnp.where(is_lo, 0.0, x2), axis=-1, keepdims=True)
    ms = jnp.where(is_lo, s_lo, s_hi) * (1.0 / HEAD_DIM)
    return x * lax.rsqrt(ms + EPS) * g2


def _split_halves(x, is_lo):
    return jnp.where(is_lo, x, 0.0).astype(BF16), jnp.where(is_lo, 0.0, x).astype(BF16)


def _pad_lo(x):
    return jnp.concatenate([x, jnp.zeros_like(x)], axis=1)


def _pad_hi(x):
    return jnp.concatenate([jnp.zeros_like(x), x], axis=1)


def _qk(q, k):
    return lax.dot_general(q, k, _NT, preferred_element_type=F32)


def _softmax_pv(scores, values):
    m = scores[0].max(axis=-1, keepdims=True)
    for s in scores[1:]:
        m = jnp.maximum(m, s.max(axis=-1, keepdims=True))
    acc = None
    den = None
    for s, v in zip(scores, values):
        p = jnp.exp(s - m)
        d = p.sum(axis=-1, keepdims=True)
        o = jnp.dot(p.astype(BF16), v, preferred_element_type=F32)
        acc = o if acc is None else acc + o
        den = d if den is None else den + d
    return acc * (1.0 / den)


CTX_NB = 2


def _attn_ctx_kernel(ha_ref, hb_ref, qg_ref, kg_ref, *rest):
    oab_ref, *kv_refs = rest[-5:]
    prev_refs = rest[:-5]
    if prev_refs:
        for li in range(len(prev_refs) // 4):
            for dst, src in zip(kv_refs, prev_refs[4 * li:4 * li + 4]):
                dst[:, li] = src[...]
        ak_ref, av_ref, nk_ref, nv_ref = [r.at[:, DEPTH - 1] for r in kv_refs]
    else:
        ak_ref, av_ref, nk_ref, nv_ref = kv_refs
    is_lo = _is_lo_half()
    qg2 = qg_ref[...] * ATT_SCALE
    kg2 = kg_ref[...]
    for b in range(CTX_NB):
        rows = slice(b * SEQ, (b + 1) * SEQ)
        kp = _pair_rms(ha_ref[rows, A_Q:A_Q + PAIR], kg2, is_lo)
        vp = ha_ref[rows, A_Q + A_KV:A_Q + A_KV + PAIR]
        for kv in range(A_KV_HEADS):
            ak_ref[b, kv] = kp[:, kv * HEAD_DIM:(kv + 1) * HEAD_DIM]
            av_ref[b, kv] = vp[:, kv * HEAD_DIM:(kv + 1) * HEAD_DIM]
        kp_sw = pltpu.roll(kp, HEAD_DIM, axis=1)
        vp_sw = pltpu.roll(vp, HEAD_DIM, axis=1)
        qs = [_pair_rms(ha_ref[rows, c * PAIR:(c + 1) * PAIR], qg2, is_lo).astype(BF16)
              for c in range(A_HEADS // 2)]
        for kv in range(A_KV_HEADS):
            k_lo = jnp.where(is_lo, kp if kv == 0 else kp_sw, 0.0).astype(BF16)
            k_hi = jnp.where(is_lo, 0.0, kp_sw if kv == 0 else kp).astype(BF16)
            v_lo = jnp.where(is_lo, vp if kv == 0 else vp_sw, 0.0).astype(BF16)
            v_hi = jnp.where(is_lo, 0.0, vp_sw if kv == 0 else vp).astype(BF16)
            q_st = jnp.concatenate([qs[2 * kv], qs[2 * kv + 1]], axis=0)
            o = _softmax_pv([_qk(q_st, k_lo)], [v_lo]) + _softmax_pv([_qk(q_st, k_hi)], [v_hi])
            oab_ref[rows, (2 * kv) * PAIR:(2 * kv + 1) * PAIR] = o[0:SEQ].astype(BF16)
            oab_ref[rows, (2 * kv + 1) * PAIR:(2 * kv + 2) * PAIR] = o[SEQ:2 * SEQ].astype(BF16)
        for j in range(B_HEADS // 2):
            qp = (hb_ref[rows, j * PAIR:(j + 1) * PAIR] * ATT_SCALE).astype(BF16)
            kp_b = hb_ref[rows, B_W + j * PAIR:B_W + (j + 1) * PAIR]
            vp_b = hb_ref[rows, 2 * B_W + j * PAIR:2 * B_W + (j + 1) * PAIR]
            for u in range(2):
                nk_ref[b, 2 * j + u] = kp_b[:, u * HEAD_DIM:(u + 1) * HEAD_DIM]
                nv_ref[b, 2 * j + u] = vp_b[:, u * HEAD_DIM:(u + 1) * HEAD_DIM]
            k_lo, k_hi = _split_halves(kp_b, is_lo)
            v_lo, v_hi = _split_halves(vp_b, is_lo)
            o = _softmax_pv([_qk(qp, k_lo)], [v_lo]) + _softmax_pv([_qk(qp, k_hi)], [v_hi])
            oab_ref[rows, A_Q + j * PAIR:A_Q + (j + 1) * PAIR] = o.astype(BF16)


def _attn_ctx(h_a, h_b, qg2, kg2, prev_kv):
    rows = CTX_NB * SEQ
    layer_spec = lambda nh: pl.BlockSpec((CTX_NB, nh, SEQ, HEAD_DIM), lambda b: (b, 0, 0, 0))
    heads = (A_KV_HEADS, A_KV_HEADS, B_HEADS, B_HEADS)
    if prev_kv is None:
        prev_kv, prev_specs = (), []
        kv_specs = [layer_spec(nh) for nh in heads]
        kv_shapes = [jax.ShapeDtypeStruct((BATCH, nh, SEQ, HEAD_DIM), F32) for nh in heads]
    else:
        prev_specs = [layer_spec(heads[i % 4]) for i in range(len(prev_kv))]
        kv_specs = [pl.BlockSpec((CTX_NB, DEPTH, nh, SEQ, HEAD_DIM), lambda b: (b, 0, 0, 0, 0)) for nh in heads]
        kv_shapes = [jax.ShapeDtypeStruct((BATCH, DEPTH, nh, SEQ, HEAD_DIM), F32) for nh in heads]
    return pl.pallas_call(
        _attn_ctx_kernel,
        grid=(BATCH // CTX_NB,),
        in_specs=[
            pl.BlockSpec((rows, W_A), lambda b: (b, 0)),
            pl.BlockSpec((rows, W_B), lambda b: (b, 0)),
            pl.BlockSpec((1, PAIR), lambda b: (0, 0)),
            pl.BlockSpec((1, PAIR), lambda b: (0, 0)),
        ] + prev_specs,
        out_specs=[pl.BlockSpec((rows, A_Q + B_W), lambda b: (b, 0))] + kv_specs,
        out_shape=[jax.ShapeDtypeStruct((N_CTX, A_Q + B_W), BF16)] + kv_shapes,
        compiler_params=_params("attn_ctx", 1),
        name="attn_ctx",
    )(h_a, h_b, qg2, kg2, *prev_kv)


LAT_TQ = 256
LAT_QROWS = LAT_TQ // GRID_W
LAT_NT = DEC_SEQ // LAT_TQ
NA_UNION_ROWS = 12
NA_UNION = NA_UNION_ROWS * GRID_W


def _na_union_row0(t):
    lo = min(max(r - NA_WIN_R // 2, 0) for r in range(t * LAT_QROWS, (t + 1) * LAT_QROWS))
    return min(min(lo, LAT_ROWS - NA_WIN_R), LAT_ROWS - NA_UNION_ROWS)


def _rope(x, cos, sin):
    lane = lax.broadcasted_iota(jnp.int32, x.shape, 1)
    nxt = pltpu.roll(x, x.shape[1] - 1, axis=1)
    prv = pltpu.roll(x, 1, axis=1)
    partner = jnp.where((lane & 1) == 0, nxt, prv)
    return x * cos + partner * sin


def _attn_lat_kernel(pick_ref, haq_ref, haf_ref, hbq_ref, hbf_ref, cak_ref, cav_ref, cnk_ref, cnv_ref,
                     qg_ref, kg_ref, cos_ref, sin_ref, toep_ref, oab_ref, kva_ref):
    t = pl.program_id(1)
    is_lo = _is_lo_half()
    row0 = pl.multiple_of(t * LAT_TQ, LAT_TQ)
    qg2 = qg_ref[...] * ATT_SCALE
    kg2 = kg_ref[...]

    @pl.when(t == 0)
    def _():
        kp = _rope(_pair_rms(haf_ref[:, A_Q:A_Q + PAIR].astype(F32), kg2, is_lo), cos_ref[...], sin_ref[...])
        vp = haf_ref[:, A_Q + A_KV:A_Q + A_KV + PAIR].astype(F32)
        kp_sw = pltpu.roll(kp, HEAD_DIM, axis=1)
        vp_sw = pltpu.roll(vp, HEAD_DIM, axis=1)
        for kv in range(A_KV_HEADS):
            kva_ref[4 * kv + 0] = jnp.where(is_lo, kp if kv == 0 else kp_sw, 0.0).astype(BF16)
            kva_ref[4 * kv + 1] = jnp.where(is_lo, 0.0, kp_sw if kv == 0 else kp).astype(BF16)
            kva_ref[4 * kv + 2] = jnp.where(is_lo, vp if kv == 0 else vp_sw, 0.0).astype(BF16)
            kva_ref[4 * kv + 3] = jnp.where(is_lo, 0.0, vp_sw if kv == 0 else vp).astype(BF16)

    cos_q = cos_ref[pl.ds(row0, LAT_TQ), :]
    sin_q = sin_ref[pl.ds(row0, LAT_TQ), :]
    qs = [_rope(_pair_rms(haq_ref[:, c * PAIR:(c + 1) * PAIR].astype(F32), qg2, is_lo), cos_q, sin_q).astype(BF16)
          for c in range(A_HEADS // 2)]
    for kv in range(A_KV_HEADS):
        kc = cak_ref[kv].astype(BF16)
        vc = cav_ref[kv].astype(BF16)
        q_st = jnp.concatenate([qs[2 * kv], qs[2 * kv + 1]], axis=0)
        o = (_softmax_pv([_qk(q_st, _pad_lo(kc)), _qk(q_st, kva_ref[4 * kv + 0])], [_pad_lo(vc), kva_ref[4 * kv + 2]])
             + _softmax_pv([_qk(q_st, _pad_hi(kc)), _qk(q_st, kva_ref[4 * kv + 1])], [_pad_hi(vc), kva_ref[4 * kv + 3]]))
        oab_ref[:, (2 * kv) * PAIR:(2 * kv + 1) * PAIR] = o[0:LAT_TQ].astype(BF16)
        oab_ref[:, (2 * kv + 1) * PAIR:(2 * kv + 2) * PAIR] = o[LAT_TQ:2 * LAT_TQ].astype(BF16)

    k0 = _na_union_row0(0) * GRID_W
    for tt in range(1, LAT_NT):
        k0 = jnp.where(t >= tt, _na_union_row0(tt) * GRID_W, k0)
    k0 = pl.multiple_of(k0, GRID_W)

    def bias(h):
        base = t * (LAT_QROWS * NA_UNION_ROWS)
        return jnp.concatenate(
            [jnp.concatenate([toep_ref[h, pick_ref[base + i * NA_UNION_ROWS + u]] for u in range(NA_UNION_ROWS)],
                             axis=1) for i in range(LAT_QROWS)], axis=0)

    for j in range(B_HEADS // 2):
        qp = (hbq_ref[:, j * PAIR:(j + 1) * PAIR] * ATT_SCALE).astype(BF16)
        k_lo, k_hi = _split_halves(hbf_ref[pl.ds(k0, NA_UNION), B_W + j * PAIR:B_W + (j + 1) * PAIR], is_lo)
        v_lo, v_hi = _split_halves(hbf_ref[pl.ds(k0, NA_UNION), 2 * B_W + j * PAIR:2 * B_W + (j + 1) * PAIR], is_lo)
        kc_lo = _pad_lo(cnk_ref[2 * j].astype(BF16))
        kc_hi = _pad_hi(cnk_ref[2 * j + 1].astype(BF16))
        vc_lo = _pad_lo(cnv_ref[2 * j].astype(BF16))
        vc_hi = _pad_hi(cnv_ref[2 * j + 1].astype(BF16))
        o = (_softmax_pv([_qk(qp, kc_lo), _qk(qp, k_lo) + bias(2 * j)], [vc_lo, v_lo])
             + _softmax_pv([_qk(qp, kc_hi), _qk(qp, k_hi) + bias(2 * j + 1)], [vc_hi, v_hi]))
        oab_ref[:, A_Q + j * PAIR:A_Q + (j + 1) * PAIR] = o.astype(BF16)


def _attn_lat(h_a, h_b, cak, cav, cnk, cnv, layer, qg2, kg2, cos128, sin128, toep):
    nt = LAT_NT
    cache_spec = lambda nh: pl.BlockSpec((None, None, nh, PAST_LEN, HEAD_DIM), lambda b, t, _: (b, layer, 0, 0, 0))
    const = lambda shape: pl.BlockSpec(shape, lambda b, t, _: (0,) * len(shape))
    grid_spec = pltpu.PrefetchScalarGridSpec(
        num_scalar_prefetch=1,
        grid=(DEC_BATCH, nt),
        in_specs=[
            pl.BlockSpec((LAT_TQ, W_A), lambda b, t, _: (b * nt + t, 0)),
            pl.BlockSpec((DEC_SEQ, W_A), lambda b, t, _: (b, 0)),
            pl.BlockSpec((LAT_TQ, W_B), lambda b, t, _: (b * nt + t, 0)),
            pl.BlockSpec((DEC_SEQ, W_B), lambda b, t, _: (b, 0)),
            cache_spec(A_KV_HEADS), cache_spec(A_KV_HEADS), cache_spec(B_HEADS), cache_spec(B_HEADS),
            const((1, PAIR)), const((1, PAIR)), const((DEC_SEQ, PAIR)), const((DEC_SEQ, PAIR)),
            const(toep.shape),
        ],
        out_specs=pl.BlockSpec((LAT_TQ, A_Q + B_W), lambda b, t, _: (b * nt + t, 0)),
        scratch_shapes=[pltpu.VMEM((4 * A_KV_HEADS, DEC_SEQ, PAIR), BF16)],
    )
    return pl.pallas_call(
        _attn_lat_kernel,
        grid_spec=grid_spec,
        out_shape=jax.ShapeDtypeStruct((N_LAT, A_Q + B_W), BF16),
        compiler_params=_params("attn_lat", 2),
        name="attn_lat",
    )(jnp.asarray(_na_tile_picks().reshape(-1)), h_a, h_a, h_b, h_b, cak, cav, cnk, cnv, qg2, kg2, cos128, sin128,
      toep)


def _rope_tables():
    t = jnp.arange(DEC_SEQ)
    row = (t // GRID_W).astype(F32)
    col = (t % GRID_W).astype(F32)
    n_freq = HEAD_DIM // 4
    inv = ROPE_THETA ** (-jnp.arange(n_freq, dtype=F32) / n_freq)
    ang = jnp.concatenate([row[:, None] * inv, col[:, None] * inv], axis=-1)
    cos = jnp.repeat(jnp.cos(ang), 2, axis=-1)
    sign = jnp.tile(jnp.array([-1.0, 1.0], F32), HEAD_DIM // 2)
    sin = jnp.repeat(jnp.sin(ang), 2, axis=-1) * sign
    return jnp.tile(cos, (1, 2)), jnp.tile(sin, (1, 2))


NA_MASKED_TILE = 2 * NA_WIN_R - 1


def _na_toeplitz(rpb):
    col = np.arange(GRID_W)
    dc = np.clip(col[None, :] - col[:, None] + NA_WIN_C - 1, 0, 2 * NA_WIN_C - 2)
    onehot = (dc[..., None] == np.arange(2 * NA_WIN_C - 1)).astype(np.float32)
    c0 = np.clip(col - NA_WIN_C // 2, 0, GRID_W - NA_WIN_C)
    in_win = (col[None, :] >= c0[:, None]) & (col[None, :] < c0[:, None] + NA_WIN_C)
    toep = jnp.einsum('hdm,ckm->hdck', rpb, onehot, precision=lax.Precision.HIGHEST)
    toep = jnp.where(in_win[None, None], toep, MASK_NEG)
    return jnp.concatenate([toep, jnp.full((B_HEADS, 1, GRID_W, GRID_W), MASK_NEG, F32)], axis=1)


def _na_tile_picks():
    pick = np.full((LAT_NT, LAT_QROWS, NA_UNION_ROWS), NA_MASKED_TILE, np.int32)
    for t in range(LAT_NT):
        u0 = _na_union_row0(t)
        for i in range(LAT_QROWS):
            r = t * LAT_QROWS + i
            r0 = min(max(r - NA_WIN_R // 2, 0), LAT_ROWS - NA_WIN_R)
            for u in range(NA_UNION_ROWS):
                if r0 <= u0 + u < r0 + NA_WIN_R:
                    pick[t, i, u] = u0 + u - r + NA_WIN_R - 1
    return pick


def _conv_kernel(hc_ref, w_ref, o_ref, *, seq):
    bg = hc_ref[:, 0:C_WIDTH].astype(F32)
    u = hc_ref[:, C_WIDTH:2 * C_WIDTH].astype(F32) * hc_ref[:, 2 * C_WIDTH:3 * C_WIDTH].astype(F32)
    rows = u.shape[0]
    assert seq & (seq - 1) == 0
    pos = lax.broadcasted_iota(jnp.int32, u.shape, 0) & (seq - 1)
    u_prev = jnp.where(pos == 0, 0.0, pltpu.roll(u, 1, axis=0))
    u_next = jnp.where(pos == seq - 1, 0.0, pltpu.roll(u, rows - 1, axis=0))
    y = u_prev * w_ref[0:1, :] + u * w_ref[1:2, :] + u_next * w_ref[2:3, :]
    o_ref[...] = (bg * y).astype(o_ref.dtype)


def _conv(h_c, conv_w, seq):
    n = h_c.shape[0]
    tm = 1024
    return pl.pallas_call(
        functools.partial(_conv_kernel, seq=seq),
        grid=(n // tm,),
        in_specs=[pl.BlockSpec((tm, W_C), lambda i: (i, 0)), pl.BlockSpec((3, C_WIDTH), lambda i: (0, 0))],
        out_specs=pl.BlockSpec((tm, C_WIDTH), lambda i: (i, 0)),
        out_shape=jax.ShapeDtypeStruct((n, C_WIDTH), BF16),
        compiler_params=_params("short_conv", 1),
        name="short_conv",
    )(h_c, conv_w)


def _mix_kernel(x_ref, oab_ref, oc_ref, hg_ref, wa_ref, wb_ref, wc_ref, wo_ref, g1_ref, n2g_ref, sc2_ref, sh2_ref,
                wrt_ref, xo_ref, xn2_ref, pt_ref):
    ga = jax.nn.sigmoid(hg_ref[:, 0:D_MODEL].astype(F32))
    gb = jax.nn.sigmoid(hg_ref[:, D_MODEL:2 * D_MODEL].astype(F32))
    gc = jax.nn.sigmoid(hg_ref[:, 2 * D_MODEL:3 * D_MODEL].astype(F32))
    pa = jnp.dot(oab_ref[:, 0:A_Q], wa_ref[...], preferred_element_type=F32)
    pb = jnp.dot(oab_ref[:, A_Q:A_Q + B_W], wb_ref[...], preferred_element_type=F32)
    pc = jnp.dot(oc_ref[...], wc_ref[...], preferred_element_type=F32)
    merged = ga * pa + gb * pb + gc * pc
    xo = x_ref[...] + g1_ref[...] * jnp.dot(merged.astype(BF16), wo_ref[...], preferred_element_type=F32)
    xo_ref[...] = xo
    y = xo * lax.rsqrt(jnp.mean(xo * xo, axis=-1, keepdims=True) + EPS) * n2g_ref[...]
    xn2 = y * (1.0 + sc2_ref[...]) + sh2_ref[...]
    rows = xn2.shape[0]
    for j in range(ROW_TILE):
        xn2_ref[pl.ds(j, rows, stride=ROW_TILE), :] = xn2[:, j * LANES:(j + 1) * LANES]
    logits_t = lax.dot_general(wrt_ref[...], xn2.astype(BF16), _NT, preferred_element_type=F32)
    z = jnp.exp(logits_t - logits_t.max(axis=0, keepdims=True))
    pt_ref[...] = z / z.sum(axis=0, keepdims=True)


def _mix(x, oab, oc, h_g, wa, wb, wc, wo, g1, n2g, sc2, sh2, wrt, seq):
    n = x.shape[0]
    tm = 512
    per_batch = g1.shape[0] > 1
    mod_map = (lambda i: ((i * tm) // seq, 0, 0)) if per_batch else (lambda i: (0, 0, 0))
    mod_spec = pl.BlockSpec((None, 1, D_MODEL), mod_map)
    full = lambda a: pl.BlockSpec(a.shape, lambda i: (0,) * a.ndim)
    row = lambda w: pl.BlockSpec((tm, w), lambda i: (i, 0))
    return pl.pallas_call(
        _mix_kernel,
        grid=(n // tm,),
        in_specs=[row(D_MODEL), row(A_Q + B_W), row(C_WIDTH), row(GATE_W), full(wa), full(wb), full(wc), full(wo),
                  mod_spec, full(n2g), mod_spec, mod_spec, full(wrt)],
        out_specs=[row(D_MODEL), pl.BlockSpec((tm * ROW_TILE, LANES), lambda i: (i, 0)),
                   pl.BlockSpec((N_EXPERTS, tm), lambda i: (0, i))],
        out_shape=[jax.ShapeDtypeStruct((n, D_MODEL), F32), jax.ShapeDtypeStruct((n * ROW_TILE, LANES), F32),
                   jax.ShapeDtypeStruct((N_EXPERTS, n), F32)],
        compiler_params=_params("mix", 1),
        name="mix",
    )(x, oab, oc, h_g, wa, wb, wc, wo, g1, n2g, sc2, sh2, wrt)


MOE_TF = 512
MOE_ROW_CHUNKS = ((0, 640), (640, 640))
MOE_ISSUE_UNROLL = 16


def _moe_issue_rows(idx_ref, idx0, src_hbm, dst_ref, dst_row0, count, sem):
    def body(i, carry):
        for u in range(MOE_ISSUE_UNROLL):
            r = i * MOE_ISSUE_UNROLL + u
            tok = idx_ref[idx0 + r]
            src = src_hbm.at[pl.ds(pl.multiple_of(tok * ROW_TILE, ROW_TILE), ROW_TILE)]
            dst = dst_ref.at[pl.ds(pl.multiple_of((dst_row0 + r) * ROW_TILE, ROW_TILE), ROW_TILE)]
            pltpu.make_async_copy(src, dst, sem).start(priority=u % 2)
        return carry

    lax.fori_loop(0, count // MOE_ISSUE_UNROLL, body, 0)


def _moe_kernel(idx_ref, xc_hbm, xl_hbm, aff_ref, wg_ref, wu_ref, wd_ref, ye_ref, xe_ref, sem):
    e = pl.program_id(0)
    f = pl.program_id(1)
    slot = e % 2

    def gather(expert, s):
        base = expert * CAP_ALL
        _moe_issue_rows(idx_ref, base, xc_hbm, xe_ref.at[s], 0, CAP_CTX, sem.at[s])
        _moe_issue_rows(idx_ref, base + CAP_CTX, xl_hbm, xe_ref.at[s], CAP_CTX, CAP_LAT, sem.at[s])

    @pl.when(f == 0)
    def _():
        @pl.when(e == 0)
        def _():
            gather(0, 0)

        pltpu.make_async_copy(xc_hbm.at[pl.ds(0, CAP_ALL * ROW_TILE)], xe_ref.at[slot], sem.at[slot]).wait()

        @pl.when(e + 1 < N_EXPERTS)
        def _():
            gather(e + 1, 1 - slot)

    wg = wg_ref[...].astype(BF16)
    wu = wu_ref[...].astype(BF16)
    wd = wd_ref[...].astype(BF16)
    for r0, rn in MOE_ROW_CHUNKS:
        xe = jnp.concatenate([xe_ref[slot, pl.ds(r0 * ROW_TILE + j, rn, stride=ROW_TILE), :]
                              for j in range(ROW_TILE)], axis=1).astype(BF16)
        hg = jnp.dot(xe, wg, preferred_element_type=F32)
        hu = jnp.dot(xe, wu, preferred_element_type=F32)
        act = (hg * jax.nn.sigmoid(hg) * hu).astype(BF16)
        part = jnp.dot(act, wd, preferred_element_type=F32) * aff_ref[r0:r0 + rn, :]

        @pl.when(f == 0)
        def _():
            ye_ref[r0:r0 + rn, :] = part

        @pl.when(f != 0)
        def _():
            ye_ref[r0:r0 + rn, :] += part


def _moe(idx_all, xn2_ctx, xn2_lat, aff_all, w_gate_e, w_up_e, w_down_e, layer):
    nf = EXPERT_FF // MOE_TF
    grid_spec = pltpu.PrefetchScalarGridSpec(
        num_scalar_prefetch=1,
        grid=(N_EXPERTS, nf),
        in_specs=[
            pl.BlockSpec(memory_space=pl.ANY),
            pl.BlockSpec(memory_space=pl.ANY),
            pl.BlockSpec((None, CAP_ALL, 1), lambda e, f, idx: (e, 0, 0)),
            pl.BlockSpec((None, None, D_MODEL, MOE_TF), lambda e, f, idx: (layer, e, 0, f)),
            pl.BlockSpec((None, None, D_MODEL, MOE_TF), lambda e, f, idx: (layer, e, 0, f)),
            pl.BlockSpec((None, None, MOE_TF, D_MODEL), lambda e, f, idx: (layer, e, f, 0)),
        ],
        out_specs=pl.BlockSpec((None, CAP_ALL, D_MODEL), lambda e, f, idx: (e, 0, 0)),
        scratch_shapes=[pltpu.VMEM((2, CAP_ALL * ROW_TILE, LANES), F32), pltpu.SemaphoreType.DMA((2,))],
    )
    return pl.pallas_call(
        _moe_kernel,
        grid_spec=grid_spec,
        out_shape=jax.ShapeDtypeStruct((N_EXPERTS, CAP_ALL, D_MODEL), F32),
        compiler_params=_params("moe_experts", 2),
        name="moe_experts",
    )(idx_all, xn2_ctx, xn2_lat, aff_all, w_gate_e, w_up_e, w_down_e)


POS_SPLIT = 32


def _prefix_count(mask_bf, tri_lane, tri_blk_strict):
    within = jnp.dot(mask_bf, tri_lane, preferred_element_type=F32)
    bs = jnp.broadcast_to(within[:, LANES - 1:LANES], within.shape).astype(BF16)
    return within + jnp.dot(tri_blk_strict, bs, preferred_element_type=F32)


def _route_kernel(p_ref, idx_ref, aff_ref, *, cap):
    p_all = p_ref[...]
    nb = p_all.shape[1]
    capf = float(cap)

    def as_float(word):
        return lax.bitcast_convert_type(word, F32)

    def bisect(_, c):
        lo, hi = c
        mid = lo + ((hi - lo) >> 1)
        ok = jnp.sum(jnp.where(p_all >= as_float(mid), 1.0, 0.0), axis=(1, 2), keepdims=True) >= capf
        return jnp.where(ok, mid, lo), jnp.where(ok, hi, mid)

    lo0 = jnp.zeros((N_EXPERTS, 1, 1), jnp.int32)
    hi0 = jnp.full((N_EXPERTS, 1, 1), 0x7F800000, jnp.int32)
    thr_all, _ = lax.fori_loop(0, 31, bisect, (lo0, hi0))
    thr_val = as_float(thr_all)

    li = lax.broadcasted_iota(jnp.int32, (LANES, LANES), 0)
    lj = lax.broadcasted_iota(jnp.int32, (LANES, LANES), 1)
    tri_lane = jnp.where(li <= lj, 1.0, 0.0).astype(BF16)
    bi = lax.broadcasted_iota(jnp.int32, (nb, nb), 0)
    bj = lax.broadcasted_iota(jnp.int32, (nb, nb), 1)
    tri_blk_strict = jnp.where(bj < bi, 1.0, 0.0).astype(BF16)
    tri_blk_incl = jnp.where(bi <= bj, 1.0, 0.0).astype(BF16)
    s_col = lax.broadcasted_iota(jnp.int32, (cap, 1), 0).astype(F32)
    blk_id = lax.broadcasted_iota(jnp.int32, (1, nb), 1).astype(F32)
    lane_id = lax.broadcasted_iota(jnp.int32, (1, LANES), 1).astype(F32)
    ones_rows = jnp.ones((8, LANES), BF16)

    for e in range(N_EXPERTS):
        p = p_all[e]
        thr = thr_val[e]
        gt = p > thr
        eq = p == thr
        need = capf - jnp.sum(jnp.where(gt, 1.0, 0.0), keepdims=True)
        tie_rank = _prefix_count(jnp.where(eq, 1.0, 0.0).astype(BF16), tri_lane, tri_blk_strict)
        sel = jnp.where(gt | (eq & (tie_rank <= need)), 1.0, 0.0).astype(BF16)
        pos = _prefix_count(sel, tri_lane, tri_blk_strict)

        bs_row = lax.dot_general(ones_rows, sel, _NT, preferred_element_type=F32)
        bp_row = jnp.dot(bs_row.astype(BF16), tri_blk_incl, preferred_element_type=F32)[0:1, :]
        blk = jnp.sum(jnp.where(bp_row <= s_col, 1.0, 0.0), axis=-1, keepdims=True)
        onehot = jnp.where(blk == blk_id, 1.0, 0.0).astype(BF16)
        pos_hi = jnp.floor(pos * (1.0 / POS_SPLIT))
        pos_lo = pos - pos_hi * POS_SPLIT
        pos_row = (jnp.dot(onehot, pos_hi.astype(BF16), preferred_element_type=F32) * POS_SPLIT
                   + jnp.dot(onehot, pos_lo.astype(BF16), preferred_element_type=F32))
        lane = jnp.sum(jnp.where(pos_row <= s_col, 1.0, 0.0), axis=-1, keepdims=True)
        idx_ref[e] = (blk * LANES + lane).astype(jnp.int32)

        p1 = p.astype(BF16)
        r1 = p - p1.astype(F32)
        p2 = r1.astype(BF16)
        p3 = (r1 - p2.astype(F32)).astype(BF16)
        p_row = (jnp.dot(onehot, p1, preferred_element_type=F32) + jnp.dot(onehot, p2, preferred_element_type=F32)
                 + jnp.dot(onehot, p3, preferred_element_type=F32))
        aff_ref[e] = jnp.sum(jnp.where(lane_id == lane, p_row, 0.0), axis=-1, keepdims=True)


def _route(probs_t, cap):
    nb = probs_t.shape[1] // LANES
    slot = pl.BlockSpec((N_EXPERTS, cap, 1), lambda i: (0, 0, 0))
    return pl.pallas_call(
        functools.partial(_route_kernel, cap=cap),
        grid=(1,),
        in_specs=[pl.BlockSpec((N_EXPERTS, nb, LANES), lambda i: (0, 0, 0))],
        out_specs=[slot, slot],
        out_shape=[jax.ShapeDtypeStruct((N_EXPERTS, cap, 1), jnp.int32),
                   jax.ShapeDtypeStruct((N_EXPERTS, cap, 1), F32)],
        compiler_params=_params("route", 1),
        name="route",
    )(probs_t.reshape(N_EXPERTS, nb, LANES))


CMB_TT = 1024
CMB_CH = 256
CMB_GROUP = 8
CMB_EPI_ROWS = 256


def _combine_kernel(idx_ref, st_ref, ye_hbm, x_ref, g2_ref, ng_ref, sc_ref, sh_ref, *rest, off, cap, nt, final):
    if final:
        y_ref, acc_ref, buf_ref, sem = rest
    else:
        xo_ref, xn_ref, acc_ref, buf_ref, sem = rest
    k = pl.program_id(0)
    t0 = k * CMB_TT
    acc_ref[...] = jnp.zeros_like(acc_ref)

    def slot_range(e):
        s0 = st_ref[e * (nt + 1) + k]
        s1 = st_ref[e * (nt + 1) + k + 1]
        return s0, s1, jnp.minimum(s0 & -CMB_GROUP, cap - CMB_CH)

    def chunk_copy(e, start):
        src = ye_hbm.at[e, pl.ds(pl.multiple_of(off + start, CMB_GROUP), CMB_CH)]
        return pltpu.make_async_copy(src, buf_ref.at[e], sem.at[e])

    def accumulate(e, start, lo, hi):
        def group(i, carry, checked):
            r0 = pl.multiple_of(i * CMB_GROUP, CMB_GROUP)
            tile = buf_ref[e, pl.ds(r0, CMB_GROUP), :]
            dst = []
            for u in range(CMB_GROUP):
                slot = start + r0 + u
                row = idx_ref[e * CAP_ALL + off + slot] - t0
                if checked:
                    row = jnp.where((slot >= lo) & (slot < hi), row, CMB_TT)
                dst.append(row)
            rows = [acc_ref[pl.ds(d, 1), :] for d in dst]
            for u in range(CMB_GROUP):
                acc_ref[pl.ds(dst[u], 1), :] = rows[u] + tile[u:u + 1, :]
            return carry

        g_lo = (lo - start) // CMB_GROUP
        g_hi = (hi - start + CMB_GROUP - 1) // CMB_GROUP
        full_lo = jnp.minimum((lo - start + CMB_GROUP - 1) // CMB_GROUP, g_hi)
        full_hi = jnp.maximum((hi - start) // CMB_GROUP, full_lo)
        lax.fori_loop(g_lo, full_lo, functools.partial(group, checked=True), 0)
        lax.fori_loop(full_lo, full_hi, functools.partial(group, checked=False), 0)
        lax.fori_loop(full_hi, g_hi, functools.partial(group, checked=True), 0)

    for e in range(N_EXPERTS):
        chunk_copy(e, slot_range(e)[2]).start()
    for e in range(N_EXPERTS):
        s0, s1, a = slot_range(e)
        chunk_copy(e, a).wait()
        accumulate(e, a, s0, jnp.minimum(s1, a + CMB_CH))

        def extra(j, carry, e=e, s1=s1, a=a):
            lo = a + CMB_CH * (j + 1)
            start = jnp.minimum(lo, cap - CMB_CH)
            cp = chunk_copy(e, start)
            cp.start()
            cp.wait()
            accumulate(e, start, lo, jnp.minimum(s1, lo + CMB_CH))
            return carry

        lax.fori_loop(0, jnp.maximum(s1 - a - 1, 0) // CMB_CH, extra, 0)

    for r0 in range(0, CMB_TT, CMB_EPI_ROWS):
        rs = slice(r0, r0 + CMB_EPI_ROWS)
        xo = x_ref[rs, :] + g2_ref[...] * acc_ref[rs, :]
        y = xo * lax.rsqrt(jnp.mean(xo * xo, axis=-1, keepdims=True) + EPS) * ng_ref[...]
        if final:
            y_ref[rs, :] = y
        else:
            xo_ref[rs, :] = xo
            xn_ref[rs, :] = (y * (1.0 + sc_ref[...]) + sh_ref[...]).astype(xn_ref.dtype)


def _combine(idx_all, starts, ye, x, g2, ng, sc, sh, seq, off, cap, final):
    n = x.shape[0]
    nt = n // CMB_TT
    per_batch = g2.shape[0] > 1
    mod_map = ((lambda i, *_: ((i * CMB_TT) // seq, 0, 0)) if per_batch else (lambda i, *_: (0, 0, 0)))
    mod_spec = pl.BlockSpec((None, 1, D_MODEL), mod_map)
    row = pl.BlockSpec((CMB_TT, D_MODEL), lambda i, *_: (i, 0))
    vec = pl.BlockSpec((1, D_MODEL), lambda i, *_: (0, 0))
    if final:
        out_specs, out_shape = row, jax.ShapeDtypeStruct((n, D_MODEL), F32)
    else:
        out_specs = [row, row]
        out_shape = [jax.ShapeDtypeStruct((n, D_MODEL), F32), jax.ShapeDtypeStruct((n, D_MODEL), BF16)]
    grid_spec = pltpu.PrefetchScalarGridSpec(
        num_scalar_prefetch=2,
        grid=(nt,),
        in_specs=[pl.BlockSpec(memory_space=pl.ANY), row, mod_spec, vec, mod_spec, mod_spec],
        out_specs=out_specs,
        scratch_shapes=[pltpu.VMEM((CMB_TT + CMB_GROUP, D_MODEL), F32),
                        pltpu.VMEM((N_EXPERTS, CMB_CH, D_MODEL), F32),
                        pltpu.SemaphoreType.DMA((N_EXPERTS,))],
    )
    return pl.pallas_call(
        functools.partial(_combine_kernel, off=off, cap=cap, nt=nt, final=final),
        grid_spec=grid_spec,
        out_shape=out_shape,
        compiler_params=_params("combine", 1),
        name="combine",
    )(idx_all, starts.reshape(-1), ye, x, g2, ng, sc, sh)


def _tile_starts(idx_sorted, n):
    bounds = jnp.arange(n // CMB_TT + 1, dtype=jnp.int32) * CMB_TT
    return jnp.sum(idx_sorted[:, :, None] < bounds[None, None, :], axis=1, dtype=jnp.int32)


def kernel(x_prompt, x_sample, cache_attn_k, cache_attn_v, cache_na_k, cache_na_v, c, c_ctx, w_ada, b_ada, norm1_g,
           w_in, q_norm_g, k_norm_g, na_rpb, conv_w, w_proj_a, w_proj_b, w_proj_c, w_out, norm2_g, w_router,
           w_gate_e, w_up_e, w_down_e, final_g):
    cond8 = jnp.zeros((8, D_MODEL), F32).at[0].set(c_ctx).at[1:1 + DEC_BATCH].set(c)
    mods = _adaln(cond8, w_ada, b_ada).reshape(DEPTH, 8, N_MOD, D_MODEL)
    cos128, sin128 = _rope_tables()

    hp = x_prompt.reshape(N_CTX, D_MODEL)
    hs = x_sample.reshape(N_LAT, D_MODEL)
    kv_layers = []
    for l in range(DEPTH):
        mod_ctx = [mods[l, 0:1, i][:, None, :] for i in range(N_MOD)]
        mod_lat = [mods[l, 1:1 + DEC_BATCH, i][:, None, :] for i in range(N_MOD)]
        n1g = norm1_g[l][None]
        n2g = norm2_g[l][None]
        qg2 = jnp.tile(q_norm_g[l][None], (1, 2))
        kg2 = jnp.tile(k_norm_g[l][None], (1, 2))
        wa, wb, wc, wo = (w_proj_a[l].astype(BF16), w_proj_b[l].astype(BF16), w_proj_c[l].astype(BF16),
                          w_out[l].astype(BF16))
        wrt = w_router[l].T.astype(BF16)
        toep = _na_toeplitz(na_rpb[l])

        def sublayer1(x, xn, mod, seq, is_ctx):
            _, _, g1, sh2, sc2, _ = mod
            kv_dtype = F32 if is_ctx else BF16
            h_a = _inproj(xn, w_in, l, COL_A, W_A, 768, kv_dtype)
            h_b = _inproj(xn, w_in, l, COL_B, W_B, 768, kv_dtype)
            h_c = _inproj(xn, w_in, l, COL_C, W_C, 768, BF16)
            h_g = _inproj(xn, w_in, l, COL_G, GATE_W, 768, BF16)
            kv = None
            if is_ctx:
                oab, *kv = _attn_ctx(h_a, h_b, qg2, kg2, kv_layers if l == DEPTH - 1 else None)
            else:
                oab = _attn_lat(h_a, h_b, cache_attn_k, cache_attn_v, cache_na_k, cache_na_v, l, qg2, kg2,
                                cos128, sin128, toep)
            oc = _conv(h_c, conv_w[l], seq)
            xo, xn2, probs_t = _mix(x, oab, oc, h_g, wa, wb, wc, wo, g1, n2g, sc2, sh2, wrt, seq)
            return xo, xn2, probs_t, kv

        if l == 0:
            xn_p = _norm_mod(hp, n1g, mod_ctx[1], mod_ctx[0], SEQ, BF16)
            xn_s = _norm_mod(hs, n1g, mod_lat[1], mod_lat[0], DEC_SEQ, BF16)
        hp, xn2_c, pt_c, kv_out = sublayer1(hp, xn_p, mod_ctx, SEQ, True)
        kv_layers = kv_layers + list(kv_out)
        hs, xn2_l, pt_l, _ = sublayer1(hs, xn_s, mod_lat, DEC_SEQ, False)

        idx_c, aff_c = _route(pt_c, CAP_CTX)
        idx_l, aff_l = _route(pt_l, CAP_LAT)
        idx_all = jnp.concatenate([idx_c, idx_l], axis=1).reshape(-1)
        aff_all = jnp.concatenate([aff_c, aff_l], axis=1)
        ye = _moe(idx_all, xn2_c, xn2_l, aff_all, w_gate_e, w_up_e, w_down_e, l)

        final = l == DEPTH - 1
        if final:
            ng, nsc_c, nsh_c, nsc_l, nsh_l = final_g[None], mod_ctx[1], mod_ctx[0], mod_lat[1], mod_lat[0]
        else:
            nxt_ctx = [mods[l + 1, 0:1, i][:, None, :] for i in range(2)]
            nxt_lat = [mods[l + 1, 1:1 + DEC_BATCH, i][:, None, :] for i in range(2)]
            ng, nsc_c, nsh_c, nsc_l, nsh_l = norm1_g[l + 1][None], nxt_ctx[1], nxt_ctx[0], nxt_lat[1], nxt_lat[0]
        out_c = _combine(idx_all, _tile_starts(idx_c[..., 0], N_CTX), ye, hp, mod_ctx[5], ng, nsc_c, nsh_c,
                         SEQ, 0, CAP_CTX, final)
        out_l = _combine(idx_all, _tile_starts(idx_l[..., 0], N_LAT), ye, hs, mod_lat[5], ng, nsc_l, nsh_l,
                         DEC_SEQ, CAP_CTX, CAP_LAT, final)
        if final:
            y_prompt = out_c.reshape(BATCH, SEQ, D_MODEL)
            y_sample = out_l.reshape(DEC_BATCH, DEC_SEQ, D_MODEL)
        else:
            (hp, xn_p), (hs, xn_s) = out_c, out_l

    return (y_prompt, y_sample, *kv_out)
```

```python
import functools

import jax
import jax.numpy as jnp
import numpy as np
from jax import lax
from jax.experimental import pallas as pl
from jax.experimental.pallas import tpu as pltpu

D_MODEL = 1024
BATCH = 32
SEQ = 256
DEPTH = 2
DEC_BATCH = 2
DEC_SEQ = 1024
PAST_LEN = 256
GRID_W = 64
HEAD_DIM = 64
A_HEADS = 8
A_KV_HEADS = 2
B_HEADS = 8
C_WIDTH = 512
NA_WIN_R = 8
NA_WIN_C = 16
N_EXPERTS = 16
EXPERT_FF = 1024
CAP_FACTOR = 2
ROPE_THETA = 10000.0
EPS = 1e-6
N_MOD = 6

A_Q = A_HEADS * HEAD_DIM
A_KV = A_KV_HEADS * HEAD_DIM
B_W = B_HEADS * HEAD_DIM
GATE_W = 3 * D_MODEL
COL_A = 0
W_A = A_Q + 2 * A_KV
COL_B = COL_A + W_A
W_B = 3 * B_W
COL_C = COL_B + W_B
W_C = 3 * C_WIDTH
COL_G = COL_C + W_C

N_CTX = BATCH * SEQ
N_LAT = DEC_BATCH * DEC_SEQ
CAP_CTX = CAP_FACTOR * N_CTX // N_EXPERTS
CAP_LAT = CAP_FACTOR * N_LAT // N_EXPERTS
CAP_ALL = CAP_CTX + CAP_LAT
LAT_ROWS = DEC_SEQ // GRID_W
MASK_NEG = -1e30
ATT_SCALE = HEAD_DIM ** -0.5

LANES = 128
ROW_TILE = 8
assert D_MODEL == ROW_TILE * LANES

F32 = jnp.float32
BF16 = jnp.bfloat16
_NT = (((1,), (1,)), ((), ()))

_MIB = 1024 * 1024


VMEM_MIB = {"adaln": 40, "norm_mod": 32, "inproj": 48, "attn_ctx": 56, "attn_lat": 56, "short_conv": 40,
            "mix": 56, "moe_experts": 56, "route": 56, "combine": 60}


def _params(name, grid_rank):
    return pltpu.CompilerParams(dimension_semantics=("arbitrary",) * grid_rank,
                                vmem_limit_bytes=VMEM_MIB[name] * _MIB)


def _adaln_kernel(cond_ref, w_ref, b_ref, o_ref):
    c = cond_ref[...]
    s = c * jax.nn.sigmoid(c)
    o_ref[...] = jnp.dot(s.astype(BF16), w_ref[...].astype(BF16), preferred_element_type=F32) + b_ref[...]


def _adaln(cond8, w_ada, b_ada):
    tn = 1536
    ncol = N_MOD * D_MODEL
    return pl.pallas_call(
        _adaln_kernel,
        grid=(DEPTH, ncol // tn),
        in_specs=[
            pl.BlockSpec((8, D_MODEL), lambda l, j: (0, 0)),
            pl.BlockSpec((None, D_MODEL, tn), lambda l, j: (l, 0, j)),
            pl.BlockSpec((None, 1, tn), lambda l, j: (l, 0, j)),
        ],
        out_specs=pl.BlockSpec((None, 8, tn), lambda l, j: (l, 0, j)),
        out_shape=jax.ShapeDtypeStruct((DEPTH, 8, ncol), F32),
        compiler_params=_params("adaln", 2),
        name="adaln",
    )(cond8, w_ada, b_ada.reshape(DEPTH, 1, ncol))


def _norm_mod_kernel(x_ref, g_ref, sc_ref, sh_ref, o_ref):
    x = x_ref[...]
    y = x * lax.rsqrt(jnp.mean(x * x, axis=-1, keepdims=True) + EPS) * g_ref[...]
    o_ref[...] = (y * (1.0 + sc_ref[...]) + sh_ref[...]).astype(o_ref.dtype)


def _norm_mod(x, g, sc, sh, seq, out_dtype):
    n = x.shape[0]
    tm = 512
    per_batch = sc.shape[0] > 1
    mod_map = (lambda i: ((i * tm) // seq, 0, 0)) if per_batch else (lambda i: (0, 0, 0))
    return pl.pallas_call(
        _norm_mod_kernel,
        grid=(n // tm,),
        in_specs=[
            pl.BlockSpec((tm, D_MODEL), lambda i: (i, 0)),
            pl.BlockSpec((1, D_MODEL), lambda i: (0, 0)),
            pl.BlockSpec((None, 1, D_MODEL), mod_map),
            pl.BlockSpec((None, 1, D_MODEL), mod_map),
        ],
        out_specs=pl.BlockSpec((tm, D_MODEL), lambda i: (i, 0)),
        out_shape=jax.ShapeDtypeStruct((n, D_MODEL), out_dtype),
        compiler_params=_params("norm_mod", 1),
        name="norm_mod",
    )(x, g, sc, sh)


def _inproj_kernel(a_ref, w_ref, o_ref, wbf_ref):
    @pl.when(pl.program_id(1) == 0)
    def _():
        wbf_ref[...] = w_ref[...].astype(BF16)

    o_ref[...] = jnp.dot(a_ref[...], wbf_ref[...], preferred_element_type=F32).astype(o_ref.dtype)


def _inproj(xn, w_in, layer, col0, width, tn, out_dtype):
    n = xn.shape[0]
    tm = min(n, 4096 if out_dtype == BF16 else 2048)
    joff = col0 // tn
    assert col0 % tn == 0 and width % tn == 0
    return pl.pallas_call(
        _inproj_kernel,
        grid=(width // tn, n // tm),
        in_specs=[
            pl.BlockSpec((tm, D_MODEL), lambda j, i: (i, 0)),
            pl.BlockSpec((None, D_MODEL, tn), lambda j, i: (layer, 0, j + joff)),
        ],
        out_specs=pl.BlockSpec((tm, tn), lambda j, i: (i, j)),
        out_shape=jax.ShapeDtypeStruct((n, width), out_dtype),
        scratch_shapes=[pltpu.VMEM((D_MODEL, tn), BF16)],
        compiler_params=_params("inproj", 2),
        name="inproj",
    )(xn, w_in)


PAIR = 2 * HEAD_DIM


def _is_lo_half():
    return lax.broadcasted_iota(jnp.int32, (1, PAIR), 1) < HEAD_DIM


def _pair_rms(x, g2, is_lo):
    x2 = x * x
    s_lo = jnp.sum(jnp.where(is_lo, x2, 0.0), axis=-1, keepdims=True)
    s_hi = jnp.sum(jnp.where(is_lo, 0.0, x2), axis=-1, keepdims=True)
    ms = jnp.where(is_lo, s_lo, s_hi) * (1.0 / HEAD_DIM)
    return x * lax.rsqrt(ms + EPS) * g2


def _split_halves(x, is_lo):
    return jnp.where(is_lo, x, 0.0).astype(BF16), jnp.where(is_lo, 0.0, x).astype(BF16)


def _pad_lo(x):
    return jnp.concatenate([x, jnp.zeros_like(x)], axis=1)


def _pad_hi(x):
    return jnp.concatenate([jnp.zeros_like(x), x], axis=1)


def _qk(q, k):
    return lax.dot_general(q, k, _NT, preferred_element_type=F32)


def _softmax_pv(scores, values, ones=None):
    m = scores[0].max(axis=-1, keepdims=True)
    for s in scores[1:]:
        m = jnp.maximum(m, s.max(axis=-1, keepdims=True))
    acc = None
    den = None
    for s, v in zip(scores, values):
        p = jnp.exp(s - m)
        if ones is None:
            d = p.sum(axis=-1, keepdims=True)
        else:
            d = jnp.dot(p.astype(BF16), ones, preferred_element_type=F32)
        o = jnp.dot(p.astype(BF16), v, preferred_element_type=F32)
        acc = o if acc is None else acc + o
        den = d if den is None else den + d
    return acc * (1.0 / den)


CTX_NB = 2


def _attn_ctx_kernel(ha_ref, hb_ref, qg_ref, kg_ref, *rest):
    oab_ref, *kv_refs = rest[-5:]
    prev_refs = rest[:-5]
    if prev_refs:
        for li in range(len(prev_refs) // 4):
            for dst, src in zip(kv_refs, prev_refs[4 * li:4 * li + 4]):
                dst[:, li] = src[...]
        ak_ref, av_ref, nk_ref, nv_ref = [r.at[:, DEPTH - 1] for r in kv_refs]
    else:
        ak_ref, av_ref, nk_ref, nv_ref = kv_refs
    is_lo = _is_lo_half()
    ones_v = jnp.ones((SEQ, PAIR), BF16)
    qg2 = qg_ref[...] * ATT_SCALE
    kg2 = kg_ref[...]
    for b in range(CTX_NB):
        rows = slice(b * SEQ, (b + 1) * SEQ)
        kp = _pair_rms(ha_ref[rows, A_Q:A_Q + PAIR], kg2, is_lo)
        vp = ha_ref[rows, A_Q + A_KV:A_Q + A_KV + PAIR]
        for kv in range(A_KV_HEADS):
            ak_ref[b, kv] = kp[:, kv * HEAD_DIM:(kv + 1) * HEAD_DIM]
            av_ref[b, kv] = vp[:, kv * HEAD_DIM:(kv + 1) * HEAD_DIM]
        kp_sw = pltpu.roll(kp, HEAD_DIM, axis=1)
        vp_sw = pltpu.roll(vp, HEAD_DIM, axis=1)
        qs = [_pair_rms(ha_ref[rows, c * PAIR:(c + 1) * PAIR], qg2, is_lo).astype(BF16)
              for c in range(A_HEADS // 2)]
        for kv in range(A_KV_HEADS):
            k_lo = jnp.where(is_lo, kp if kv == 0 else kp_sw, 0.0).astype(BF16)
            k_hi = jnp.where(is_lo, 0.0, kp_sw if kv == 0 else kp).astype(BF16)
            v_lo = jnp.where(is_lo, vp if kv == 0 else vp_sw, 0.0).astype(BF16)
            v_hi = jnp.where(is_lo, 0.0, vp_sw if kv == 0 else vp).astype(BF16)
            q_st = jnp.concatenate([qs[2 * kv], qs[2 * kv + 1]], axis=0)
            o = _softmax_pv([_qk(q_st, k_lo)], [v_lo]) + _softmax_pv([_qk(q_st, k_hi)], [v_hi])
            oab_ref[rows, (2 * kv) * PAIR:(2 * kv + 1) * PAIR] = o[0:SEQ].astype(BF16)
            oab_ref[rows, (2 * kv + 1) * PAIR:(2 * kv + 2) * PAIR] = o[SEQ:2 * SEQ].astype(BF16)
        for j in range(B_HEADS // 2):
            qp = (hb_ref[rows, j * PAIR:(j + 1) * PAIR] * ATT_SCALE).astype(BF16)
            kp_b = hb_ref[rows, B_W + j * PAIR:B_W + (j + 1) * PAIR]
            vp_b = hb_ref[rows, 2 * B_W + j * PAIR:2 * B_W + (j + 1) * PAIR]
            for u in range(2):
                nk_ref[b, 2 * j + u] = kp_b[:, u * HEAD_DIM:(u + 1) * HEAD_DIM]
                nv_ref[b, 2 * j + u] = vp_b[:, u * HEAD_DIM:(u + 1) * HEAD_DIM]
            k_lo, k_hi = _split_halves(kp_b, is_lo)
            v_lo, v_hi = _split_halves(vp_b, is_lo)
            o = _softmax_pv([_qk(qp, k_lo)], [v_lo], ones_v) + _softmax_pv([_qk(qp, k_hi)], [v_hi], ones_v)
            oab_ref[rows, A_Q + j * PAIR:A_Q + (j + 1) * PAIR] = o.astype(BF16)


def _attn_ctx(h_a, h_b, qg2, kg2, prev_kv):
    rows = CTX_NB * SEQ
    layer_spec = lambda nh: pl.BlockSpec((CTX_NB, nh, SEQ, HEAD_DIM), lambda b: (b, 0, 0, 0))
    heads = (A_KV_HEADS, A_KV_HEADS, B_HEADS, B_HEADS)
    if prev_kv is None:
        prev_kv, prev_specs = (), []
        kv_specs = [layer_spec(nh) for nh in heads]
        kv_shapes = [jax.ShapeDtypeStruct((BATCH, nh, SEQ, HEAD_DIM), F32) for nh in heads]
    else:
        prev_specs = [layer_spec(heads[i % 4]) for i in range(len(prev_kv))]
        kv_specs = [pl.BlockSpec((CTX_NB, DEPTH, nh, SEQ, HEAD_DIM), lambda b: (b, 0, 0, 0, 0)) for nh in heads]
        kv_shapes = [jax.ShapeDtypeStruct((BATCH, DEPTH, nh, SEQ, HEAD_DIM), F32) for nh in heads]
    return pl.pallas_call(
        _attn_ctx_kernel,
        grid=(BATCH // CTX_NB,),
        in_specs=[
            pl.BlockSpec((rows, W_A), lambda b: (b, 0)),
            pl.BlockSpec((rows, W_B), lambda b: (b, 0)),
            pl.BlockSpec((1, PAIR), lambda b: (0, 0)),
            pl.BlockSpec((1, PAIR), lambda b: (0, 0)),
        ] + prev_specs,
        out_specs=[pl.BlockSpec((rows, A_Q + B_W), lambda b: (b, 0))] + kv_specs,
        out_shape=[jax.ShapeDtypeStruct((N_CTX, A_Q + B_W), BF16)] + kv_shapes,
        compiler_params=_params("attn_ctx", 1),
        name="attn_ctx",
    )(h_a, h_b, qg2, kg2, *prev_kv)


LAT_TQ = 256
LAT_QROWS = LAT_TQ // GRID_W
LAT_NT = DEC_SEQ // LAT_TQ
NA_UNION_ROWS = 12
NA_UNION = NA_UNION_ROWS * GRID_W


def _na_union_row0(t):
    lo = min(max(r - NA_WIN_R // 2, 0) for r in range(t * LAT_QROWS, (t + 1) * LAT_QROWS))
    return min(min(lo, LAT_ROWS - NA_WIN_R), LAT_ROWS - NA_UNION_ROWS)


def _rope(x, cos, sin):
    lane = lax.broadcasted_iota(jnp.int32, x.shape, 1)
    nxt = pltpu.roll(x, x.shape[1] - 1, axis=1)
    prv = pltpu.roll(x, 1, axis=1)
    partner = jnp.where((lane & 1) == 0, nxt, prv)
    return x * cos + partner * sin


def _attn_lat_kernel(pick_ref, haq_ref, haf_ref, hbq_ref, hbf_ref, cak_ref, cav_ref, cnk_ref, cnv_ref,
                     qg_ref, kg_ref, cos_ref, sin_ref, toep_ref, oab_ref, kva_ref):
    t = pl.program_id(1)
    is_lo = _is_lo_half()
    row0 = pl.multiple_of(t * LAT_TQ, LAT_TQ)
    qg2 = qg_ref[...] * ATT_SCALE
    kg2 = kg_ref[...]

    @pl.when(t == 0)
    def _():
        kp = _rope(_pair_rms(haf_ref[:, A_Q:A_Q + PAIR].astype(F32), kg2, is_lo), cos_ref[...], sin_ref[...])
        vp = haf_ref[:, A_Q + A_KV:A_Q + A_KV + PAIR].astype(F32)
        kp_sw = pltpu.roll(kp, HEAD_DIM, axis=1)
        vp_sw = pltpu.roll(vp, HEAD_DIM, axis=1)
        for kv in range(A_KV_HEADS):
            kva_ref[4 * kv + 0] = jnp.where(is_lo, kp if kv == 0 else kp_sw, 0.0).astype(BF16)
            kva_ref[4 * kv + 1] = jnp.where(is_lo, 0.0, kp_sw if kv == 0 else kp).astype(BF16)
            kva_ref[4 * kv + 2] = jnp.where(is_lo, vp if kv == 0 else vp_sw, 0.0).astype(BF16)
            kva_ref[4 * kv + 3] = jnp.where(is_lo, 0.0, vp_sw if kv == 0 else vp).astype(BF16)

    cos_q = cos_ref[pl.ds(row0, LAT_TQ), :]
    sin_q = sin_ref[pl.ds(row0, LAT_TQ), :]
    qs = [_rope(_pair_rms(haq_ref[:, c * PAIR:(c + 1) * PAIR].astype(F32), qg2, is_lo), cos_q, sin_q).astype(BF16)
          for c in range(A_HEADS // 2)]
    for kv in range(A_KV_HEADS):
        kc = cak_ref[kv].astype(BF16)
        vc = cav_ref[kv].astype(BF16)
        q_st = jnp.concatenate([qs[2 * kv], qs[2 * kv + 1]], axis=0)
        o = (_softmax_pv([_qk(q_st, _pad_lo(kc)), _qk(q_st, kva_ref[4 * kv + 0])], [_pad_lo(vc), kva_ref[4 * kv + 2]])
             + _softmax_pv([_qk(q_st, _pad_hi(kc)), _qk(q_st, kva_ref[4 * kv + 1])], [_pad_hi(vc), kva_ref[4 * kv + 3]]))
        oab_ref[:, (2 * kv) * PAIR:(2 * kv + 1) * PAIR] = o[0:LAT_TQ].astype(BF16)
        oab_ref[:, (2 * kv + 1) * PAIR:(2 * kv + 2) * PAIR] = o[LAT_TQ:2 * LAT_TQ].astype(BF16)

    k0 = _na_union_row0(0) * GRID_W
    for tt in range(1, LAT_NT):
        k0 = jnp.where(t >= tt, _na_union_row0(tt) * GRID_W, k0)
    k0 = pl.multiple_of(k0, GRID_W)

    def bias(h):
        base = t * (LAT_QROWS * NA_UNION_ROWS)
        return jnp.concatenate(
            [jnp.concatenate([toep_ref[h, pick_ref[base + i * NA_UNION_ROWS + u]] for u in range(NA_UNION_ROWS)],
                             axis=1) for i in range(LAT_QROWS)], axis=0)

    for j in range(B_HEADS // 2):
        qp = (hbq_ref[:, j * PAIR:(j + 1) * PAIR] * ATT_SCALE).astype(BF16)
        k_lo, k_hi = _split_halves(hbf_ref[pl.ds(k0, NA_UNION), B_W + j * PAIR:B_W + (j + 1) * PAIR], is_lo)
        v_lo, v_hi = _split_halves(hbf_ref[pl.ds(k0, NA_UNION), 2 * B_W + j * PAIR:2 * B_W + (j + 1) * PAIR], is_lo)
        kc_lo = _pad_lo(cnk_ref[2 * j].astype(BF16))
        kc_hi = _pad_hi(cnk_ref[2 * j + 1].astype(BF16))
        vc_lo = _pad_lo(cnv_ref[2 * j].astype(BF16))
        vc_hi = _pad_hi(cnv_ref[2 * j + 1].astype(BF16))
        o = (_softmax_pv([_qk(qp, kc_lo), _qk(qp, k_lo) + bias(2 * j)], [vc_lo, v_lo])
             + _softmax_pv([_qk(qp, kc_hi), _qk(qp, k_hi) + bias(2 * j + 1)], [vc_hi, v_hi]))
        oab_ref[:, A_Q + j * PAIR:A_Q + (j + 1) * PAIR] = o.astype(BF16)


def _attn_lat(h_a, h_b, cak, cav, cnk, cnv, layer, qg2, kg2, cos128, sin128, toep):
    nt = LAT_NT
    cache_spec = lambda nh: pl.BlockSpec((None, None, nh, PAST_LEN, HEAD_DIM), lambda b, t, _: (b, layer, 0, 0, 0))
    const = lambda shape: pl.BlockSpec(shape, lambda b, t, _: (0,) * len(shape))
    grid_spec = pltpu.PrefetchScalarGridSpec(
        num_scalar_prefetch=1,
        grid=(DEC_BATCH, nt),
        in_specs=[
            pl.BlockSpec((LAT_TQ, W_A), lambda b, t, _: (b * nt + t, 0)),
            pl.BlockSpec((DEC_SEQ, W_A), lambda b, t, _: (b, 0)),
            pl.BlockSpec((LAT_TQ, W_B), lambda b, t, _: (b * nt + t, 0)),
            pl.BlockSpec((DEC_SEQ, W_B), lambda b, t, _: (b, 0)),
            cache_spec(A_KV_HEADS), cache_spec(A_KV_HEADS), cache_spec(B_HEADS), cache_spec(B_HEADS),
            const((1, PAIR)), const((1, PAIR)), const((DEC_SEQ, PAIR)), const((DEC_SEQ, PAIR)),
            const(toep.shape),
        ],
        out_specs=pl.BlockSpec((LAT_TQ, A_Q + B_W), lambda b, t, _: (b * nt + t, 0)),
        scratch_shapes=[pltpu.VMEM((4 * A_KV_HEADS, DEC_SEQ, PAIR), BF16)],
    )
    return pl.pallas_call(
        _attn_lat_kernel,
        grid_spec=grid_spec,
        out_shape=jax.ShapeDtypeStruct((N_LAT, A_Q + B_W), BF16),
        compiler_params=_params("attn_lat", 2),
        name="attn_lat",
    )(jnp.asarray(_na_tile_picks().reshape(-1)), h_a, h_a, h_b, h_b, cak, cav, cnk, cnv, qg2, kg2, cos128, sin128,
      toep)


def _rope_tables():
    t = jnp.arange(DEC_SEQ)
    row = (t // GRID_W).astype(F32)
    col = (t % GRID_W).astype(F32)
    n_freq = HEAD_DIM // 4
    inv = ROPE_THETA ** (-jnp.arange(n_freq, dtype=F32) / n_freq)
    ang = jnp.concatenate([row[:, None] * inv, col[:, None] * inv], axis=-1)
    cos = jnp.repeat(jnp.cos(ang), 2, axis=-1)
    sign = jnp.tile(jnp.array([-1.0, 1.0], F32), HEAD_DIM // 2)
    sin = jnp.repeat(jnp.sin(ang), 2, axis=-1) * sign
    return jnp.tile(cos, (1, 2)), jnp.tile(sin, (1, 2))


NA_MASKED_TILE = 2 * NA_WIN_R - 1


def _na_toeplitz(rpb):
    col = np.arange(GRID_W)
    dc = np.clip(col[None, :] - col[:, None] + NA_WIN_C - 1, 0, 2 * NA_WIN_C - 2)
    onehot = (dc[..., None] == np.arange(2 * NA_WIN_C - 1)).astype(np.float32)
    c0 = np.clip(col - NA_WIN_C // 2, 0, GRID_W - NA_WIN_C)
    in_win = (col[None, :] >= c0[:, None]) & (col[None, :] < c0[:, None] + NA_WIN_C)
    toep = jnp.einsum('hdm,ckm->hdck', rpb, onehot, precision=lax.Precision.HIGHEST)
    toep = jnp.where(in_win[None, None], toep, MASK_NEG)
    return jnp.concatenate([toep, jnp.full((B_HEADS, 1, GRID_W, GRID_W), MASK_NEG, F32)], axis=1)


def _na_tile_picks():
    pick = np.full((LAT_NT, LAT_QROWS, NA_UNION_ROWS), NA_MASKED_TILE, np.int32)
    for t in range(LAT_NT):
        u0 = _na_union_row0(t)
        for i in range(LAT_QROWS):
            r = t * LAT_QROWS + i
            r0 = min(max(r - NA_WIN_R // 2, 0), LAT_ROWS - NA_WIN_R)
            for u in range(NA_UNION_ROWS):
                if r0 <= u0 + u < r0 + NA_WIN_R:
                    pick[t, i, u] = u0 + u - r + NA_WIN_R - 1
    return pick


def _conv_kernel(hc_ref, w_ref, o_ref, *, seq):
    bg = hc_ref[:, 0:C_WIDTH].astype(F32)
    u = hc_ref[:, C_WIDTH:2 * C_WIDTH].astype(F32) * hc_ref[:, 2 * C_WIDTH:3 * C_WIDTH].astype(F32)
    rows = u.shape[0]
    assert seq & (seq - 1) == 0
    pos = lax.broadcasted_iota(jnp.int32, u.shape, 0) & (seq - 1)
    u_prev = jnp.where(pos == 0, 0.0, pltpu.roll(u, 1, axis=0))
    u_next = jnp.where(pos == seq - 1, 0.0, pltpu.roll(u, rows - 1, axis=0))
    y = u_prev * w_ref[0:1, :] + u * w_ref[1:2, :] + u_next * w_ref[2:3, :]
    o_ref[...] = (bg * y).astype(o_ref.dtype)


def _conv(h_c, conv_w, seq):
    n = h_c.shape[0]
    tm = 1024
    return pl.pallas_call(
        functools.partial(_conv_kernel, seq=seq),
        grid=(n // tm,),
        in_specs=[pl.BlockSpec((tm, W_C), lambda i: (i, 0)), pl.BlockSpec((3, C_WIDTH), lambda i: (0, 0))],
        out_specs=pl.BlockSpec((tm, C_WIDTH), lambda i: (i, 0)),
        out_shape=jax.ShapeDtypeStruct((n, C_WIDTH), BF16),
        compiler_params=_params("short_conv", 1),
        name="short_conv",
    )(h_c, conv_w)


def _mix_kernel(x_ref, oab_ref, oc_ref, hg_ref, wa_ref, wb_ref, wc_ref, wo_ref, g1_ref, n2g_ref, sc2_ref, sh2_ref,
                wrt_ref, xo_ref, xn2_ref, pt_ref):
    ga = jax.nn.sigmoid(hg_ref[:, 0:D_MODEL].astype(F32))
    gb = jax.nn.sigmoid(hg_ref[:, D_MODEL:2 * D_MODEL].astype(F32))
    gc = jax.nn.sigmoid(hg_ref[:, 2 * D_MODEL:3 * D_MODEL].astype(F32))
    pa = jnp.dot(oab_ref[:, 0:A_Q], wa_ref[...], preferred_element_type=F32)
    pb = jnp.dot(oab_ref[:, A_Q:A_Q + B_W], wb_ref[...], preferred_element_type=F32)
    pc = jnp.dot(oc_ref[...], wc_ref[...], preferred_element_type=F32)
    merged = ga * pa + gb * pb + gc * pc
    xo = x_ref[...] + g1_ref[...] * jnp.dot(merged.astype(BF16), wo_ref[...], preferred_element_type=F32)
    xo_ref[...] = xo
    y = xo * lax.rsqrt(jnp.mean(xo * xo, axis=-1, keepdims=True) + EPS) * n2g_ref[...]
    xn2 = y * (1.0 + sc2_ref[...]) + sh2_ref[...]
    rows = xn2.shape[0]
    for j in range(ROW_TILE):
        xn2_ref[pl.ds(j, rows, stride=ROW_TILE), :] = xn2[:, j * LANES:(j + 1) * LANES]
    logits_t = lax.dot_general(wrt_ref[...], xn2.astype(BF16), _NT, preferred_element_type=F32)
    z = jnp.exp(logits_t - logits_t.max(axis=0, keepdims=True))
    pt_ref[...] = z / z.sum(axis=0, keepdims=True)


def _mix(x, oab, oc, h_g, wa, wb, wc, wo, g1, n2g, sc2, sh2, wrt, seq):
    n = x.shape[0]
    tm = 512
    per_batch = g1.shape[0] > 1
    mod_map = (lambda i: ((i * tm) // seq, 0, 0)) if per_batch else (lambda i: (0, 0, 0))
    mod_spec = pl.BlockSpec((None, 1, D_MODEL), mod_map)
    full = lambda a: pl.BlockSpec(a.shape, lambda i: (0,) * a.ndim)
    row = lambda w: pl.BlockSpec((tm, w), lambda i: (i, 0))
    return pl.pallas_call(
        _mix_kernel,
        grid=(n // tm,),
        in_specs=[row(D_MODEL), row(A_Q + B_W), row(C_WIDTH), row(GATE_W), full(wa), full(wb), full(wc), full(wo),
                  mod_spec, full(n2g), mod_spec, mod_spec, full(wrt)],
        out_specs=[row(D_MODEL), pl.BlockSpec((tm * ROW_TILE, LANES), lambda i: (i, 0)),
                   pl.BlockSpec((N_EXPERTS, tm), lambda i: (0, i))],
        out_shape=[jax.ShapeDtypeStruct((n, D_MODEL), F32), jax.ShapeDtypeStruct((n * ROW_TILE, LANES), F32),
                   jax.ShapeDtypeStruct((N_EXPERTS, n), F32)],
        compiler_params=_params("mix", 1),
        name="mix",
    )(x, oab, oc, h_g, wa, wb, wc, wo, g1, n2g, sc2, sh2, wrt)


MOE_TF = 512
MOE_ROW_CHUNKS = ((0, 640), (640, 640))
MOE_ISSUE_UNROLL = 16


def _moe_issue_rows(idx_ref, idx0, src_hbm, dst_ref, dst_row0, count, sem):
    def body(i, carry):
        for u in range(MOE_ISSUE_UNROLL):
            r = i * MOE_ISSUE_UNROLL + u
            tok = idx_ref[idx0 + r]
            src = src_hbm.at[pl.ds(pl.multiple_of(tok * ROW_TILE, ROW_TILE), ROW_TILE)]
            dst = dst_ref.at[pl.ds(pl.multiple_of((dst_row0 + r) * ROW_TILE, ROW_TILE), ROW_TILE)]
            pltpu.make_async_copy(src, dst, sem).start(priority=u % 2)
        return carry

    lax.fori_loop(0, count // MOE_ISSUE_UNROLL, body, 0)


def _moe_kernel(idx_ref, xc_hbm, xl_hbm, aff_ref, wg_ref, wu_ref, wd_ref, ye_ref, xe_ref, sem):
    e = pl.program_id(0)
    f = pl.program_id(1)
    slot = e % 2

    def gather(expert, s):
        base = expert * CAP_ALL
        _moe_issue_rows(idx_ref, base, xc_hbm, xe_ref.at[s], 0, CAP_CTX, sem.at[s])
        _moe_issue_rows(idx_ref, base + CAP_CTX, xl_hbm, xe_ref.at[s], CAP_CTX, CAP_LAT, sem.at[s])

    @pl.when(f == 0)
    def _():
        @pl.when(e == 0)
        def _():
            gather(0, 0)

        pltpu.make_async_copy(xc_hbm.at[pl.ds(0, CAP_ALL * ROW_TILE)], xe_ref.at[slot], sem.at[slot]).wait()

        @pl.when(e + 1 < N_EXPERTS)
        def _():
            gather(e + 1, 1 - slot)

    wg = wg_ref[...].astype(BF16)
    wu = wu_ref[...].astype(BF16)
    wd = wd_ref[...].astype(BF16)
    for r0, rn in MOE_ROW_CHUNKS:
        xe = jnp.concatenate([xe_ref[slot, pl.ds(r0 * ROW_TILE + j, rn, stride=ROW_TILE), :]
                              for j in range(ROW_TILE)], axis=1).astype(BF16)
        hg = jnp.dot(xe, wg, preferred_element_type=F32)
        hu = jnp.dot(xe, wu, preferred_element_type=F32)
        act = (hg * jax.nn.sigmoid(hg) * hu).astype(BF16)
        part = jnp.dot(act, wd, preferred_element_type=F32) * aff_ref[r0:r0 + rn, :]

        @pl.when(f == 0)
        def _():
            ye_ref[r0:r0 + rn, :] = part

        @pl.when(f != 0)
        def _():
            ye_ref[r0:r0 + rn, :] += part


def _moe(idx_all, xn2_ctx, xn2_lat, aff_all, w_gate_e, w_up_e, w_down_e, layer):
    nf = EXPERT_FF // MOE_TF
    grid_spec = pltpu.PrefetchScalarGridSpec(
        num_scalar_prefetch=1,
        grid=(N_EXPERTS, nf),
        in_specs=[
            pl.BlockSpec(memory_space=pl.ANY),
            pl.BlockSpec(memory_space=pl.ANY),
            pl.BlockSpec((None, CAP_ALL, 1), lambda e, f, idx: (e, 0, 0)),
            pl.BlockSpec((None, None, D_MODEL, MOE_TF), lambda e, f, idx: (layer, e, 0, f)),
            pl.BlockSpec((None, None, D_MODEL, MOE_TF), lambda e, f, idx: (layer, e, 0, f)),
            pl.BlockSpec((None, None, MOE_TF, D_MODEL), lambda e, f, idx: (layer, e, f, 0)),
        ],
        out_specs=pl.BlockSpec((None, CAP_ALL, D_MODEL), lambda e, f, idx: (e, 0, 0)),
        scratch_shapes=[pltpu.VMEM((2, CAP_ALL * ROW_TILE, LANES), F32), pltpu.SemaphoreType.DMA((2,))],
    )
    return pl.pallas_call(
        _moe_kernel,
        grid_spec=grid_spec,
        out_shape=jax.ShapeDtypeStruct((N_EXPERTS, CAP_ALL, D_MODEL), F32),
        compiler_params=_params("moe_experts", 2),
        name="moe_experts",
    )(idx_all, xn2_ctx, xn2_lat, aff_all, w_gate_e, w_up_e, w_down_e)


POS_SPLIT = 32


def _prefix_count(mask_bf, tri_lane, tri_blk_strict):
    within = jnp.dot(mask_bf, tri_lane, preferred_element_type=F32)
    bs = jnp.broadcast_to(within[:, LANES - 1:LANES], within.shape).astype(BF16)
    return within + jnp.dot(tri_blk_strict, bs, preferred_element_type=F32)


def _route_kernel(p_ref, idx_ref, aff_ref, *, cap):
    p_all = p_ref[...]
    nb = p_all.shape[1]
    capf = float(cap)

    def as_float(word):
        return lax.bitcast_convert_type(word, F32)

    def bisect(_, c):
        lo, hi = c
        mid = lo + ((hi - lo) >> 1)
        ok = jnp.sum(jnp.where(p_all >= as_float(mid), 1.0, 0.0), axis=(1, 2), keepdims=True) >= capf
        return jnp.where(ok, mid, lo), jnp.where(ok, hi, mid)

    lo0 = jnp.zeros((N_EXPERTS, 1, 1), jnp.int32)
    hi0 = jnp.full((N_EXPERTS, 1, 1), 0x7F800000, jnp.int32)
    thr_all, _ = lax.fori_loop(0, 31, bisect, (lo0, hi0))
    thr_val = as_float(thr_all)

    li = lax.broadcasted_iota(jnp.int32, (LANES, LANES), 0)
    lj = lax.broadcasted_iota(jnp.int32, (LANES, LANES), 1)
    tri_lane = jnp.where(li <= lj, 1.0, 0.0).astype(BF16)
    bi = lax.broadcasted_iota(jnp.int32, (nb, nb), 0)
    bj = lax.broadcasted_iota(jnp.int32, (nb, nb), 1)
    tri_blk_strict = jnp.where(bj < bi, 1.0, 0.0).astype(BF16)
    tri_blk_incl = jnp.where(bi <= bj, 1.0, 0.0).astype(BF16)
    s_col = lax.broadcasted_iota(jnp.int32, (cap, 1), 0).astype(F32)
    blk_id = lax.broadcasted_iota(jnp.int32, (1, nb), 1).astype(F32)
    lane_id = lax.broadcasted_iota(jnp.int32, (1, LANES), 1).astype(F32)
    ones_rows = jnp.ones((8, LANES), BF16)
    ones_blk = jnp.ones((nb, LANES), BF16)
    ones_lane = jnp.ones((LANES, LANES), BF16)

    for e in range(N_EXPERTS):
        p = p_all[e]
        thr = thr_val[e]
        gt = p > thr
        eq = p == thr
        need = capf - jnp.sum(jnp.where(gt, 1.0, 0.0), keepdims=True)
        tie_rank = _prefix_count(jnp.where(eq, 1.0, 0.0).astype(BF16), tri_lane, tri_blk_strict)
        sel = jnp.where(gt | (eq & (tie_rank <= need)), 1.0, 0.0).astype(BF16)
        pos = _prefix_count(sel, tri_lane, tri_blk_strict)

        bs_row = lax.dot_general(ones_rows, sel, _NT, preferred_element_type=F32)
        bp_row = jnp.dot(bs_row.astype(BF16), tri_blk_incl, preferred_element_type=F32)[0:1, :]
        blk = jnp.dot(jnp.where(bp_row <= s_col, 1.0, 0.0).astype(BF16), ones_blk,
                      preferred_element_type=F32)[:, 0:1]
        onehot = jnp.where(blk == blk_id, 1.0, 0.0).astype(BF16)
        pos_hi = jnp.floor(pos * (1.0 / POS_SPLIT))
        pos_lo = pos - pos_hi * POS_SPLIT
        pos_row = (jnp.dot(onehot, pos_hi.astype(BF16), preferred_element_type=F32) * POS_SPLIT
                   + jnp.dot(onehot, pos_lo.astype(BF16), preferred_element_type=F32))
        lane = jnp.dot(jnp.where(pos_row <= s_col, 1.0, 0.0).astype(BF16), ones_lane,
                       preferred_element_type=F32)[:, 0:1]
        idx_ref[e] = (blk * LANES + lane).astype(jnp.int32)

        p1 = p.astype(BF16)
        r1 = p - p1.astype(F32)
        p2 = r1.astype(BF16)
        p3 = (r1 - p2.astype(F32)).astype(BF16)
        p_row = (jnp.dot(onehot, p1, preferred_element_type=F32) + jnp.dot(onehot, p2, preferred_element_type=F32)
                 + jnp.dot(onehot, p3, preferred_element_type=F32))
        aff_ref[e] = jnp.sum(jnp.where(lane_id == lane, p_row, 0.0), axis=-1, keepdims=True)


def _route(probs_t, cap):
    nb = probs_t.shape[1] // LANES
    slot = pl.BlockSpec((N_EXPERTS, cap, 1), lambda i: (0, 0, 0))
    return pl.pallas_call(
        functools.partial(_route_kernel, cap=cap),
        grid=(1,),
        in_specs=[pl.BlockSpec((N_EXPERTS, nb, LANES), lambda i: (0, 0, 0))],
        out_specs=[slot, slot],
        out_shape=[jax.ShapeDtypeStruct((N_EXPERTS, cap, 1), jnp.int32),
                   jax.ShapeDtypeStruct((N_EXPERTS, cap, 1), F32)],
        compiler_params=_params("route", 1),
        name="route",
    )(probs_t.reshape(N_EXPERTS, nb, LANES))


CMB_TT = 1024
CMB_CH = 256
CMB_GROUP = 8
CMB_EPI_ROWS = 256


def _combine_kernel(idx_ref, st_ref, ye_hbm, x_ref, g2_ref, ng_ref, sc_ref, sh_ref, *rest, off, cap, nt, final):
    if final:
        y_ref, acc_ref, buf_ref, sem = rest
    else:
        xo_ref, xn_ref, acc_ref, buf_ref, sem = rest
    k = pl.program_id(0)
    t0 = k * CMB_TT
    acc_ref[...] = jnp.zeros_like(acc_ref)

    def slot_range(e):
        s0 = st_ref[e * (nt + 1) + k]
        s1 = st_ref[e * (nt + 1) + k + 1]
        return s0, s1, jnp.minimum(s0 & -CMB_GROUP, cap - CMB_CH)

    def chunk_copy(e, start):
        src = ye_hbm.at[e, pl.ds(pl.multiple_of(off + start, CMB_GROUP), CMB_CH)]
        return pltpu.make_async_copy(src, buf_ref.at[e], sem.at[e])

    def accumulate(e, start, lo, hi):
        def group(i, carry, checked):
            r0 = pl.multiple_of(i * CMB_GROUP, CMB_GROUP)
            tile = buf_ref[e, pl.ds(r0, CMB_GROUP), :]
            dst = []
            for u in range(CMB_GROUP):
                slot = start + r0 + u
                row = idx_ref[e * CAP_ALL + off + slot] - t0
                if checked:
                    row = jnp.where((slot >= lo) & (slot < hi), row, CMB_TT)
                dst.append(row)
            rows = [acc_ref[pl.ds(d, 1), :] for d in dst]
            for u in range(CMB_GROUP):
                acc_ref[pl.ds(dst[u], 1), :] = rows[u] + tile[u:u + 1, :]
            return carry

        g_lo = (lo - start) // CMB_GROUP
        g_hi = (hi - start + CMB_GROUP - 1) // CMB_GROUP
        full_lo = jnp.minimum((lo - start + CMB_GROUP - 1) // CMB_GROUP, g_hi)
        full_hi = jnp.maximum((hi - start) // CMB_GROUP, full_lo)
        lax.fori_loop(g_lo, full_lo, functools.partial(group, checked=True), 0)
        lax.fori_loop(full_lo, full_hi, functools.partial(group, checked=False), 0)
        lax.fori_loop(full_hi, g_hi, functools.partial(group, checked=True), 0)

    for e in range(N_EXPERTS):
        chunk_copy(e, slot_range(e)[2]).start()
    for e in range(N_EXPERTS):
        s0, s1, a = slot_range(e)
        chunk_copy(e, a).wait()
        accumulate(e, a, s0, jnp.minimum(s1, a + CMB_CH))

        def extra(j, carry, e=e, s1=s1, a=a):
            lo = a + CMB_CH * (j + 1)
            start = jnp.minimum(lo, cap - CMB_CH)
            cp = chunk_copy(e, start)
            cp.start()
            cp.wait()
            accumulate(e, start, lo, jnp.minimum(s1, lo + CMB_CH))
            return carry

        lax.fori_loop(0, jnp.maximum(s1 - a - 1, 0) // CMB_CH, extra, 0)

    for r0 in range(0, CMB_TT, CMB_EPI_ROWS):
        rs = slice(r0, r0 + CMB_EPI_ROWS)
        xo = x_ref[rs, :] + g2_ref[...] * acc_ref[rs, :]
        y = xo * lax.rsqrt(jnp.mean(xo * xo, axis=-1, keepdims=True) + EPS) * ng_ref[...]
        if final:
            y_ref[rs, :] = y
        else:
            xo_ref[rs, :] = xo
            xn_ref[rs, :] = (y * (1.0 + sc_ref[...]) + sh_ref[...]).astype(xn_ref.dtype)


def _combine(idx_all, starts, ye, x, g2, ng, sc, sh, seq, off, cap, final):
    n = x.shape[0]
    nt = n // CMB_TT
    per_batch = g2.shape[0] > 1
    mod_map = ((lambda i, *_: ((i * CMB_TT) // seq, 0, 0)) if per_batch else (lambda i, *_: (0, 0, 0)))
    mod_spec = pl.BlockSpec((None, 1, D_MODEL), mod_map)
    row = pl.BlockSpec((CMB_TT, D_MODEL), lambda i, *_: (i, 0))
    vec = pl.BlockSpec((1, D_MODEL), lambda i, *_: (0, 0))
    if final:
        out_specs, out_shape = row, jax.ShapeDtypeStruct((n, D_MODEL), F32)
    else:
        out_specs = [row, row]
        out_shape = [jax.ShapeDtypeStruct((n, D_MODEL), F32), jax.ShapeDtypeStruct((n, D_MODEL), BF16)]
    grid_spec = pltpu.PrefetchScalarGridSpec(
        num_scalar_prefetch=2,
        grid=(nt,),
        in_specs=[pl.BlockSpec(memory_space=pl.ANY), row, mod_spec, vec, mod_spec, mod_spec],
        out_specs=out_specs,
        scratch_shapes=[pltpu.VMEM((CMB_TT + CMB_GROUP, D_MODEL), F32),
                        pltpu.VMEM((N_EXPERTS, CMB_CH, D_MODEL), F32),
                        pltpu.SemaphoreType.DMA((N_EXPERTS,))],
    )
    return pl.pallas_call(
        functools.partial(_combine_kernel, off=off, cap=cap, nt=nt, final=final),
        grid_spec=grid_spec,
        out_shape=out_shape,
        compiler_params=_params("combine", 1),
        name="combine",
    )(idx_all, starts.reshape(-1), ye, x, g2, ng, sc, sh)


def _tile_starts(idx_sorted, n):
    bounds = jnp.arange(n // CMB_TT + 1, dtype=jnp.int32) * CMB_TT
    return jnp.sum(idx_sorted[:, :, None] < bounds[None, None, :], axis=1, dtype=jnp.int32)


def kernel(x_prompt, x_sample, cache_attn_k, cache_attn_v, cache_na_k, cache_na_v, c, c_ctx, w_ada, b_ada, norm1_g,
           w_in, q_norm_g, k_norm_g, na_rpb, conv_w, w_proj_a, w_proj_b, w_proj_c, w_out, norm2_g, w_router,
           w_gate_e, w_up_e, w_down_e, final_g):
    cond8 = jnp.zeros((8, D_MODEL), F32).at[0].set(c_ctx).at[1:1 + DEC_BATCH].set(c)
    mods = _adaln(cond8, w_ada, b_ada).reshape(DEPTH, 8, N_MOD, D_MODEL)
    cos128, sin128 = _rope_tables()

    hp = x_prompt.reshape(N_CTX, D_MODEL)
    hs = x_sample.reshape(N_LAT, D_MODEL)
    kv_layers = []
    for l in range(DEPTH):
        mod_ctx = [mods[l, 0:1, i][:, None, :] for i in range(N_MOD)]
        mod_lat = [mods[l, 1:1 + DEC_BATCH, i][:, None, :] for i in range(N_MOD)]
        n1g = norm1_g[l][None]
        n2g = norm2_g[l][None]
        qg2 = jnp.tile(q_norm_g[l][None], (1, 2))
        kg2 = jnp.tile(k_norm_g[l][None], (1, 2))
        wa, wb, wc, wo = (w_proj_a[l].astype(BF16), w_proj_b[l].astype(BF16), w_proj_c[l].astype(BF16),
                          w_out[l].astype(BF16))
        wrt = w_router[l].T.astype(BF16)
        toep = _na_toeplitz(na_rpb[l])

        def sublayer1(x, xn, mod, seq, is_ctx):
            _, _, g1, sh2, sc2, _ = mod
            kv_dtype = F32 if is_ctx else BF16
            h_a = _inproj(xn, w_in, l, COL_A, W_A, 768, kv_dtype)
            h_b = _inproj(xn, w_in, l, COL_B, W_B, 768, kv_dtype)
            h_c = _inproj(xn, w_in, l, COL_C, W_C, 768, BF16)
            h_g = _inproj(xn, w_in, l, COL_G, GATE_W, 768, BF16)
            kv = None
            if is_ctx:
                oab, *kv = _attn_ctx(h_a, h_b, qg2, kg2, kv_layers if l == DEPTH - 1 else None)
            else:
                oab = _attn_lat(h_a, h_b, cache_attn_k, cache_attn_v, cache_na_k, cache_na_v, l, qg2, kg2,
                                cos128, sin128, toep)
            oc = _conv(h_c, conv_w[l], seq)
            xo, xn2, probs_t = _mix(x, oab, oc, h_g, wa, wb, wc, wo, g1, n2g, sc2, sh2, wrt, seq)
            return xo, xn2, probs_t, kv

        if l == 0:
            xn_p = _norm_mod(hp, n1g, mod_ctx[1], mod_ctx[0], SEQ, BF16)
            xn_s = _norm_mod(hs, n1g, mod_lat[1], mod_lat[0], DEC_SEQ, BF16)
        hp, xn2_c, pt_c, kv_out = sublayer1(hp, xn_p, mod_ctx, SEQ, True)
        kv_layers = kv_layers + list(kv_out)
        hs, xn2_l, pt_l, _ = sublayer1(hs, xn_s, mod_lat, DEC_SEQ, False)

        idx_c, aff_c = _route(pt_c, CAP_CTX)
        idx_l, aff_l = _route(pt_l, CAP_LAT)
        idx_all = jnp.concatenate([idx_c, idx_l], axis=1).reshape(-1)
        aff_all = jnp.concatenate([aff_c, aff_l], axis=1)
        ye = _moe(idx_all, xn2_c, xn2_l, aff_all, w_gate_e, w_up_e, w_down_e, l)

        final = l == DEPTH - 1
        if final:
            ng, nsc_c, nsh_c, nsc_l, nsh_l = final_g[None], mod_ctx[1], mod_ctx[0], mod_lat[1], mod_lat[0]
        else:
            nxt_ctx = [mods[l + 1, 0:1, i][:, None, :] for i in range(2)]
            nxt_lat = [mods[l + 1, 1:1 + DEC_BATCH, i][:, None, :] for i in range(2)]
            ng, nsc_c, nsh_c, nsc_l, nsh_l = norm1_g[l + 1][None], nxt_ctx[1], nxt_ctx[0], nxt_lat[1], nxt_lat[0]
        out_c = _combine(idx_all, _tile_starts(idx_c[..., 0], N_CTX), ye, hp, mod_ctx[5], ng, nsc_c, nsh_c,
                         SEQ, 0, CAP_CTX, final)
        out_l = _combine(idx_all, _tile_starts(idx_l[..., 0], N_LAT), ye, hs, mod_lat[5], ng, nsc_l, nsh_l,
                         DEC_SEQ, CAP_CTX, CAP_LAT, final)
        if final:
            y_prompt = out_c.reshape(BATCH, SEQ, D_MODEL)
            y_sample = out_l.reshape(DEC_BATCH, DEC_SEQ, D_MODEL)
        else:
            (hp, xn_p), (hs, xn_s) = out_c, out_l

    return (y_prompt, y_sample, *kv_out)
```

```python
import functools

import jax
import jax.numpy as jnp
import numpy as np
from jax import lax
from jax.experimental import pallas as pl
from jax.experimental.pallas import tpu as pltpu

D_MODEL = 1024
BATCH = 32
SEQ = 256
DEPTH = 2
DEC_BATCH = 2
DEC_SEQ = 1024
PAST_LEN = 256
GRID_W = 64
HEAD_DIM = 64
A_HEADS = 8
A_KV_HEADS = 2
B_HEADS = 8
C_WIDTH = 512
NA_WIN_R = 8
NA_WIN_C = 16
N_EXPERTS = 16
EXPERT_FF = 1024
CAP_FACTOR = 2
ROPE_THETA = 10000.0
EPS = 1e-6
N_MOD = 6

A_Q = A_HEADS * HEAD_DIM
A_KV = A_KV_HEADS * HEAD_DIM
B_W = B_HEADS * HEAD_DIM
GATE_W = 3 * D_MODEL
COL_A = 0
W_A = A_Q + 2 * A_KV
COL_B = COL_A + W_A
W_B = 3 * B_W
COL_C = COL_B + W_B
W_C = 3 * C_WIDTH
COL_G = COL_C + W_C

N_CTX = BATCH * SEQ
N_LAT = DEC_BATCH * DEC_SEQ
CAP_CTX = CAP_FACTOR * N_CTX // N_EXPERTS
CAP_LAT = CAP_FACTOR * N_LAT // N_EXPERTS
CAP_ALL = CAP_CTX + CAP_LAT
LAT_ROWS = DEC_SEQ // GRID_W
MASK_NEG = -1e30
ATT_SCALE = HEAD_DIM ** -0.5

LANES = 128
ROW_TILE = 8
assert D_MODEL == ROW_TILE * LANES

F32 = jnp.float32
BF16 = jnp.bfloat16
_NT = (((1,), (1,)), ((), ()))

_MIB = 1024 * 1024


VMEM_MIB = {"adaln": 40, "norm_mod": 32, "inproj": 48, "attn_ctx": 56, "attn_lat": 56, "short_conv": 40,
            "mix": 56, "moe_experts": 56, "route": 56, "combine": 60}


def _params(name, grid_rank):
    return pltpu.CompilerParams(dimension_semantics=("arbitrary",) * grid_rank,
                                vmem_limit_bytes=VMEM_MIB[name] * _MIB)


def _adaln_kernel(cond_ref, w_ref, b_ref, o_ref):
    c = cond_ref[...]
    s = c * jax.nn.sigmoid(c)
    o_ref[...] = jnp.dot(s.astype(BF16), w_ref[...].astype(BF16), preferred_element_type=F32) + b_ref[...]


def _adaln(cond8, w_ada, b_ada):
    tn = 1536
    ncol = N_MOD * D_MODEL
    return pl.pallas_call(
        _adaln_kernel,
        grid=(DEPTH, ncol // tn),
        in_specs=[
            pl.BlockSpec((8, D_MODEL), lambda l, j: (0, 0)),
            pl.BlockSpec((None, D_MODEL, tn), lambda l, j: (l, 0, j)),
            pl.BlockSpec((None, 1, tn), lambda l, j: (l, 0, j)),
        ],
        out_specs=pl.BlockSpec((None, 8, tn), lambda l, j: (l, 0, j)),
        out_shape=jax.ShapeDtypeStruct((DEPTH, 8, ncol), F32),
        compiler_params=_params("adaln", 2),
        name="adaln",
    )(cond8, w_ada, b_ada.reshape(DEPTH, 1, ncol))


def _norm_mod_kernel(x_ref, g_ref, sc_ref, sh_ref, o_ref):
    x = x_ref[...]
    y = x * lax.rsqrt(jnp.mean(x * x, axis=-1, keepdims=True) + EPS) * g_ref[...]
    o_ref[...] = (y * (1.0 + sc_ref[...]) + sh_ref[...]).astype(o_ref.dtype)


def _norm_mod(x, g, sc, sh, seq, out_dtype):
    n = x.shape[0]
    tm = 512
    per_batch = sc.shape[0] > 1
    mod_map = (lambda i: ((i * tm) // seq, 0, 0)) if per_batch else (lambda i: (0, 0, 0))
    return pl.pallas_call(
        _norm_mod_kernel,
        grid=(n // tm,),
        in_specs=[
            pl.BlockSpec((tm, D_MODEL), lambda i: (i, 0)),
            pl.BlockSpec((1, D_MODEL), lambda i: (0, 0)),
            pl.BlockSpec((None, 1, D_MODEL), mod_map),
            pl.BlockSpec((None, 1, D_MODEL), mod_map),
        ],
        out_specs=pl.BlockSpec((tm, D_MODEL), lambda i: (i, 0)),
        out_shape=jax.ShapeDtypeStruct((n, D_MODEL), out_dtype),
        compiler_params=_params("norm_mod", 1),
        name="norm_mod",
    )(x, g, sc, sh)


def _inproj_kernel(a_ref, w_ref, o_ref, wbf_ref):
    @pl.when(pl.program_id(1) == 0)
    def _():
        wbf_ref[...] = w_ref[...].astype(BF16)

    o_ref[...] = jnp.dot(a_ref[...], wbf_ref[...], preferred_element_type=F32).astype(o_ref.dtype)


def _inproj(xn, w_in, layer, col0, width, tn, out_dtype):
    n = xn.shape[0]
    tm = min(n, 4096 if out_dtype == BF16 else 2048)
    joff = col0 // tn
    assert col0 % tn == 0 and width % tn == 0
    return pl.pallas_call(
        _inproj_kernel,
        grid=(width // tn, n // tm),
        in_specs=[
            pl.BlockSpec((tm, D_MODEL), lambda j, i: (i, 0)),
            pl.BlockSpec((None, D_MODEL, tn), lambda j, i: (layer, 0, j + joff)),
        ],
        out_specs=pl.BlockSpec((tm, tn), lambda j, i: (i, j)),
        out_shape=jax.ShapeDtypeStruct((n, width), out_dtype),
        scratch_shapes=[pltpu.VMEM((D_MODEL, tn), BF16)],
        compiler_params=_params("inproj", 2),
        name="inproj",
    )(xn, w_in)


PAIR = 2 * HEAD_DIM


def _is_lo_half():
    return lax.broadcasted_iota(jnp.int32, (1, PAIR), 1) < HEAD_DIM


def _pair_rms(x, g2, is_lo):
    x2 = x * x
    s_lo = jnp.sum(jnp.where(is_lo, x2, 0.0), axis=-1, keepdims=True)
    s_hi = jnp.sum(jnp.where(is_lo, 0.0, x2), axis=-1, keepdims=True)
    ms = jnp.where(is_lo, s_lo, s_hi) * (1.0 / HEAD_DIM)
    return x * lax.rsqrt(ms + EPS) * g2


def _split_halves(x, is_lo):
    return jnp.where(is_lo, x, 0.0).astype(BF16), jnp.where(is_lo, 0.0, x).astype(BF16)


def _pad_lo(x):
    return jnp.concatenate([x, jnp.zeros_like(x)], axis=1)


def _pad_hi(x):
    return jnp.concatenate([jnp.zeros_like(x), x], axis=1)


def _qk(q, k):
    return lax.dot_general(q, k, _NT, preferred_element_type=F32)


def _softmax_pv(scores, values):
    m = scores[0].max(axis=-1, keepdims=True)
    for s in scores[1:]:
        m = jnp.maximum(m, s.max(axis=-1, keepdims=True))
    acc = None
    den = None
    for s, v in zip(scores, values):
        p = jnp.exp(s - m)
        d = p.sum(axis=-1, keepdims=True)
        o = jnp.dot(p.astype(BF16), v, preferred_element_type=F32)
        acc = o if acc is None else acc + o
        den = d if den is None else den + d
    return acc * (1.0 / den)


CTX_NB = 2


def _attn_ctx_kernel(ha_ref, hb_ref, qg_ref, kg_ref, *rest, first):
    oab_ref, *kv_refs = rest[-5:]
    if first:
        for r in kv_refs:
            r[:, 1:] = jnp.zeros((CTX_NB, DEPTH - 1) + r.shape[2:], F32)
        ak_ref, av_ref, nk_ref, nv_ref = [r.at[:, 0] for r in kv_refs]
    else:
        ak_ref, av_ref, nk_ref, nv_ref = kv_refs
    is_lo = _is_lo_half()
    qg2 = qg_ref[...] * ATT_SCALE
    kg2 = kg_ref[...]
    for b in range(CTX_NB):
        rows = slice(b * SEQ, (b + 1) * SEQ)
        kp = _pair_rms(ha_ref[rows, A_Q:A_Q + PAIR], kg2, is_lo)
        vp = ha_ref[rows, A_Q + A_KV:A_Q + A_KV + PAIR]
        for kv in range(A_KV_HEADS):
            ak_ref[b, kv] = kp[:, kv * HEAD_DIM:(kv + 1) * HEAD_DIM]
            av_ref[b, kv] = vp[:, kv * HEAD_DIM:(kv + 1) * HEAD_DIM]
        kp_sw = pltpu.roll(kp, HEAD_DIM, axis=1)
        vp_sw = pltpu.roll(vp, HEAD_DIM, axis=1)
        qs = [_pair_rms(ha_ref[rows, c * PAIR:(c + 1) * PAIR], qg2, is_lo).astype(BF16)
              for c in range(A_HEADS // 2)]
        for kv in range(A_KV_HEADS):
            k_lo = jnp.where(is_lo, kp if kv == 0 else kp_sw, 0.0).astype(BF16)
            k_hi = jnp.where(is_lo, 0.0, kp_sw if kv == 0 else kp).astype(BF16)
            v_lo = jnp.where(is_lo, vp if kv == 0 else vp_sw, 0.0).astype(BF16)
            v_hi = jnp.where(is_lo, 0.0, vp_sw if kv == 0 else vp).astype(BF16)
            q_st = jnp.concatenate([qs[2 * kv], qs[2 * kv + 1]], axis=0)
            o = _softmax_pv([_qk(q_st, k_lo)], [v_lo]) + _softmax_pv([_qk(q_st, k_hi)], [v_hi])
            oab_ref[rows, (2 * kv) * PAIR:(2 * kv + 1) * PAIR] = o[0:SEQ].astype(BF16)
            oab_ref[rows, (2 * kv + 1) * PAIR:(2 * kv + 2) * PAIR] = o[SEQ:2 * SEQ].astype(BF16)
        for j in range(B_HEADS // 2):
            qp = (hb_ref[rows, j * PAIR:(j + 1) * PAIR] * ATT_SCALE).astype(BF16)
            kp_b = hb_ref[rows, B_W + j * PAIR:B_W + (j + 1) * PAIR]
            vp_b = hb_ref[rows, 2 * B_W + j * PAIR:2 * B_W + (j + 1) * PAIR]
            for u in range(2):
                nk_ref[b, 2 * j + u] = kp_b[:, u * HEAD_DIM:(u + 1) * HEAD_DIM]
                nv_ref[b, 2 * j + u] = vp_b[:, u * HEAD_DIM:(u + 1) * HEAD_DIM]
            k_lo, k_hi = _split_halves(kp_b, is_lo)
            v_lo, v_hi = _split_halves(vp_b, is_lo)
            o = _softmax_pv([_qk(qp, k_lo)], [v_lo]) + _softmax_pv([_qk(qp, k_hi)], [v_hi])
            oab_ref[rows, A_Q + j * PAIR:A_Q + (j + 1) * PAIR] = o.astype(BF16)


def _attn_ctx(h_a, h_b, qg2, kg2, layer, prev_kv):
    rows = CTX_NB * SEQ
    heads = (A_KV_HEADS, A_KV_HEADS, B_HEADS, B_HEADS)
    first = layer == 0
    assert first == (len(prev_kv) == 0)
    if first:
        kv_specs = [pl.BlockSpec((CTX_NB, DEPTH, nh, SEQ, HEAD_DIM), lambda b: (b, 0, 0, 0, 0)) for nh in heads]
    else:
        kv_specs = [pl.BlockSpec((CTX_NB, None, nh, SEQ, HEAD_DIM), lambda b: (b, layer, 0, 0, 0)) for nh in heads]
    n_in = 4
    return pl.pallas_call(
        functools.partial(_attn_ctx_kernel, first=first),
        grid=(BATCH // CTX_NB,),
        in_specs=[
            pl.BlockSpec((rows, W_A), lambda b: (b, 0)),
            pl.BlockSpec((rows, W_B), lambda b: (b, 0)),
            pl.BlockSpec((1, PAIR), lambda b: (0, 0)),
            pl.BlockSpec((1, PAIR), lambda b: (0, 0)),
        ] + [pl.BlockSpec(memory_space=pl.ANY)] * len(prev_kv),
        out_specs=[pl.BlockSpec((rows, A_Q + B_W), lambda b: (b, 0))] + kv_specs,
        out_shape=[jax.ShapeDtypeStruct((N_CTX, A_Q + B_W), BF16)]
        + [jax.ShapeDtypeStruct((BATCH, DEPTH, nh, SEQ, HEAD_DIM), F32) for nh in heads],
        input_output_aliases={n_in + i: 1 + i for i in range(len(prev_kv))},
        compiler_params=_params("attn_ctx", 1),
        name="attn_ctx",
    )(h_a, h_b, qg2, kg2, *prev_kv)


LAT_TQ = 256
LAT_QROWS = LAT_TQ // GRID_W
LAT_NT = DEC_SEQ // LAT_TQ
NA_UNION_ROWS = 12
NA_UNION = NA_UNION_ROWS * GRID_W


def _na_union_row0(t):
    lo = min(max(r - NA_WIN_R // 2, 0) for r in range(t * LAT_QROWS, (t + 1) * LAT_QROWS))
    return min(min(lo, LAT_ROWS - NA_WIN_R), LAT_ROWS - NA_UNION_ROWS)


def _rope(x, cos, sin):
    lane = lax.broadcasted_iota(jnp.int32, x.shape, 1)
    nxt = pltpu.roll(x, x.shape[1] - 1, axis=1)
    prv = pltpu.roll(x, 1, axis=1)
    partner = jnp.where((lane & 1) == 0, nxt, prv)
    return x * cos + partner * sin


def _attn_lat_kernel(pick_ref, haq_ref, haf_ref, hbq_ref, hbf_ref, cak_ref, cav_ref, cnk_ref, cnv_ref,
                     qg_ref, kg_ref, cos_ref, sin_ref, toep_ref, oab_ref, kva_ref):
    t = pl.program_id(1)
    is_lo = _is_lo_half()
    row0 = pl.multiple_of(t * LAT_TQ, LAT_TQ)
    qg2 = qg_ref[...] * ATT_SCALE
    kg2 = kg_ref[...]

    @pl.when(t == 0)
    def _():
        kp = _rope(_pair_rms(haf_ref[:, A_Q:A_Q + PAIR].astype(F32), kg2, is_lo), cos_ref[...], sin_ref[...])
        vp = haf_ref[:, A_Q + A_KV:A_Q + A_KV + PAIR].astype(F32)
        kp_sw = pltpu.roll(kp, HEAD_DIM, axis=1)
        vp_sw = pltpu.roll(vp, HEAD_DIM, axis=1)
        for kv in range(A_KV_HEADS):
            kva_ref[4 * kv + 0] = jnp.where(is_lo, kp if kv == 0 else kp_sw, 0.0).astype(BF16)
            kva_ref[4 * kv + 1] = jnp.where(is_lo, 0.0, kp_sw if kv == 0 else kp).astype(BF16)
            kva_ref[4 * kv + 2] = jnp.where(is_lo, vp if kv == 0 else vp_sw, 0.0).astype(BF16)
            kva_ref[4 * kv + 3] = jnp.where(is_lo, 0.0, vp_sw if kv == 0 else vp).astype(BF16)

    cos_q = cos_ref[pl.ds(row0, LAT_TQ), :]
    sin_q = sin_ref[pl.ds(row0, LAT_TQ), :]
    qs = [_rope(_pair_rms(haq_ref[:, c * PAIR:(c + 1) * PAIR].astype(F32), qg2, is_lo), cos_q, sin_q).astype(BF16)
          for c in range(A_HEADS // 2)]
    for kv in range(A_KV_HEADS):
        kc = cak_ref[kv].astype(BF16)
        vc = cav_ref[kv].astype(BF16)
        q_st = jnp.concatenate([qs[2 * kv], qs[2 * kv + 1]], axis=0)
        o = (_softmax_pv([_qk(q_st, _pad_lo(kc)), _qk(q_st, kva_ref[4 * kv + 0])], [_pad_lo(vc), kva_ref[4 * kv + 2]])
             + _softmax_pv([_qk(q_st, _pad_hi(kc)), _qk(q_st, kva_ref[4 * kv + 1])], [_pad_hi(vc), kva_ref[4 * kv + 3]]))
        oab_ref[:, (2 * kv) * PAIR:(2 * kv + 1) * PAIR] = o[0:LAT_TQ].astype(BF16)
        oab_ref[:, (2 * kv + 1) * PAIR:(2 * kv + 2) * PAIR] = o[LAT_TQ:2 * LAT_TQ].astype(BF16)

    k0 = _na_union_row0(0) * GRID_W
    for tt in range(1, LAT_NT):
        k0 = jnp.where(t >= tt, _na_union_row0(tt) * GRID_W, k0)
    k0 = pl.multiple_of(k0, GRID_W)

    def bias(h):
        base = t * (LAT_QROWS * NA_UNION_ROWS)
        return jnp.concatenate(
            [jnp.concatenate([toep_ref[h, pick_ref[base + i * NA_UNION_ROWS + u]] for u in range(NA_UNION_ROWS)],
                             axis=1) for i in range(LAT_QROWS)], axis=0)

    for j in range(B_HEADS // 2):
        qp = (hbq_ref[:, j * PAIR:(j + 1) * PAIR] * ATT_SCALE).astype(BF16)
        k_lo, k_hi = _split_halves(hbf_ref[pl.ds(k0, NA_UNION), B_W + j * PAIR:B_W + (j + 1) * PAIR], is_lo)
        v_lo, v_hi = _split_halves(hbf_ref[pl.ds(k0, NA_UNION), 2 * B_W + j * PAIR:2 * B_W + (j + 1) * PAIR], is_lo)
        kc_lo = _pad_lo(cnk_ref[2 * j].astype(BF16))
        kc_hi = _pad_hi(cnk_ref[2 * j + 1].astype(BF16))
        vc_lo = _pad_lo(cnv_ref[2 * j].astype(BF16))
        vc_hi = _pad_hi(cnv_ref[2 * j + 1].astype(BF16))
        o = (_softmax_pv([_qk(qp, kc_lo), _qk(qp, k_lo) + bias(2 * j)], [vc_lo, v_lo])
             + _softmax_pv([_qk(qp, kc_hi), _qk(qp, k_hi) + bias(2 * j + 1)], [vc_hi, v_hi]))
        oab_ref[:, A_Q + j * PAIR:A_Q + (j + 1) * PAIR] = o.astype(BF16)


def _attn_lat(h_a, h_b, cak, cav, cnk, cnv, layer, qg2, kg2, cos128, sin128, toep):
    nt = LAT_NT
    cache_spec = lambda nh: pl.BlockSpec((None, None, nh, PAST_LEN, HEAD_DIM), lambda b, t, _: (b, layer, 0, 0, 0))
    const = lambda shape: pl.BlockSpec(shape, lambda b, t, _: (0,) * len(shape))
    grid_spec = pltpu.PrefetchScalarGridSpec(
        num_scalar_prefetch=1,
        grid=(DEC_BATCH, nt),
        in_specs=[
            pl.BlockSpec((LAT_TQ, W_A), lambda b, t, _: (b * nt + t, 0)),
            pl.BlockSpec((DEC_SEQ, W_A), lambda b, t, _: (b, 0)),
            pl.BlockSpec((LAT_TQ, W_B), lambda b, t, _: (b * nt + t, 0)),
            pl.BlockSpec((DEC_SEQ, W_B), lambda b, t, _: (b, 0)),
            cache_spec(A_KV_HEADS), cache_spec(A_KV_HEADS), cache_spec(B_HEADS), cache_spec(B_HEADS),
            const((1, PAIR)), const((1, PAIR)), const((DEC_SEQ, PAIR)), const((DEC_SEQ, PAIR)),
            const(toep.shape),
        ],
        out_specs=pl.BlockSpec((LAT_TQ, A_Q + B_W), lambda b, t, _: (b * nt + t, 0)),
        scratch_shapes=[pltpu.VMEM((4 * A_KV_HEADS, DEC_SEQ, PAIR), BF16)],
    )
    return pl.pallas_call(
        _attn_lat_kernel,
        grid_spec=grid_spec,
        out_shape=jax.ShapeDtypeStruct((N_LAT, A_Q + B_W), BF16),
        compiler_params=_params("attn_lat", 2),
        name="attn_lat",
    )(jnp.asarray(_na_tile_picks().reshape(-1)), h_a, h_a, h_b, h_b, cak, cav, cnk, cnv, qg2, kg2, cos128, sin128,
      toep)


def _rope_tables():
    t = jnp.arange(DEC_SEQ)
    row = (t // GRID_W).astype(F32)
    col = (t % GRID_W).astype(F32)
    n_freq = HEAD_DIM // 4
    inv = ROPE_THETA ** (-jnp.arange(n_freq, dtype=F32) / n_freq)
    ang = jnp.concatenate([row[:, None] * inv, col[:, None] * inv], axis=-1)
    cos = jnp.repeat(jnp.cos(ang), 2, axis=-1)
    sign = jnp.tile(jnp.array([-1.0, 1.0], F32), HEAD_DIM // 2)
    sin = jnp.repeat(jnp.sin(ang), 2, axis=-1) * sign
    return jnp.tile(cos, (1, 2)), jnp.tile(sin, (1, 2))


NA_MASKED_TILE = 2 * NA_WIN_R - 1


def _na_toeplitz(rpb):
    col = np.arange(GRID_W)
    dc = np.clip(col[None, :] - col[:, None] + NA_WIN_C - 1, 0, 2 * NA_WIN_C - 2)
    onehot = (dc[..., None] == np.arange(2 * NA_WIN_C - 1)).astype(np.float32)
    c0 = np.clip(col - NA_WIN_C // 2, 0, GRID_W - NA_WIN_C)
    in_win = (col[None, :] >= c0[:, None]) & (col[None, :] < c0[:, None] + NA_WIN_C)
    toep = jnp.einsum('hdm,ckm->hdck', rpb, onehot, precision=lax.Precision.HIGHEST)
    toep = jnp.where(in_win[None, None], toep, MASK_NEG)
    return jnp.concatenate([toep, jnp.full((B_HEADS, 1, GRID_W, GRID_W), MASK_NEG, F32)], axis=1)


def _na_tile_picks():
    pick = np.full((LAT_NT, LAT_QROWS, NA_UNION_ROWS), NA_MASKED_TILE, np.int32)
    for t in range(LAT_NT):
        u0 = _na_union_row0(t)
        for i in range(LAT_QROWS):
            r = t * LAT_QROWS + i
            r0 = min(max(r - NA_WIN_R // 2, 0), LAT_ROWS - NA_WIN_R)
            for u in range(NA_UNION_ROWS):
                if r0 <= u0 + u < r0 + NA_WIN_R:
                    pick[t, i, u] = u0 + u - r + NA_WIN_R - 1
    return pick


def _conv_kernel(hc_ref, w_ref, o_ref, *, seq):
    bg = hc_ref[:, 0:C_WIDTH].astype(F32)
    u = hc_ref[:, C_WIDTH:2 * C_WIDTH].astype(F32) * hc_ref[:, 2 * C_WIDTH:3 * C_WIDTH].astype(F32)
    rows = u.shape[0]
    assert seq & (seq - 1) == 0
    pos = lax.broadcasted_iota(jnp.int32, u.shape, 0) & (seq - 1)
    u_prev = jnp.where(pos == 0, 0.0, pltpu.roll(u, 1, axis=0))
    u_next = jnp.where(pos == seq - 1, 0.0, pltpu.roll(u, rows - 1, axis=0))
    y = u_prev * w_ref[0:1, :] + u * w_ref[1:2, :] + u_next * w_ref[2:3, :]
    o_ref[...] = (bg * y).astype(o_ref.dtype)


def _conv(h_c, conv_w, seq):
    n = h_c.shape[0]
    tm = 1024
    return pl.pallas_call(
        functools.partial(_conv_kernel, seq=seq),
        grid=(n // tm,),
        in_specs=[pl.BlockSpec((tm, W_C), lambda i: (i, 0)), pl.BlockSpec((3, C_WIDTH), lambda i: (0, 0))],
        out_specs=pl.BlockSpec((tm, C_WIDTH), lambda i: (i, 0)),
        out_shape=jax.ShapeDtypeStruct((n, C_WIDTH), BF16),
        compiler_params=_params("short_conv", 1),
        name="short_conv",
    )(h_c, conv_w)


def _mix_kernel(x_ref, oab_ref, oc_ref, hg_ref, wa_ref, wb_ref, wc_ref, wo_ref, g1_ref, n2g_ref, sc2_ref, sh2_ref,
                wrt_ref, xo_ref, xn2_ref, pt_ref):
    ga = jax.nn.sigmoid(hg_ref[:, 0:D_MODEL].astype(F32))
    gb = jax.nn.sigmoid(hg_ref[:, D_MODEL:2 * D_MODEL].astype(F32))
    gc = jax.nn.sigmoid(hg_ref[:, 2 * D_MODEL:3 * D_MODEL].astype(F32))
    pa = jnp.dot(oab_ref[:, 0:A_Q], wa_ref[...], preferred_element_type=F32)
    pb = jnp.dot(oab_ref[:, A_Q:A_Q + B_W], wb_ref[...], preferred_element_type=F32)
    pc = jnp.dot(oc_ref[...], wc_ref[...], preferred_element_type=F32)
    merged = ga * pa + gb * pb + gc * pc
    xo = x_ref[...] + g1_ref[...] * jnp.dot(merged.astype(BF16), wo_ref[...], preferred_element_type=F32)
    xo_ref[...] = xo
    y = xo * lax.rsqrt(jnp.mean(xo * xo, axis=-1, keepdims=True) + EPS) * n2g_ref[...]
    xn2 = y * (1.0 + sc2_ref[...]) + sh2_ref[...]
    rows = xn2.shape[0]
    for j in range(ROW_TILE):
        xn2_ref[pl.ds(j, rows, stride=ROW_TILE), :] = xn2[:, j * LANES:(j + 1) * LANES]
    logits_t = lax.dot_general(wrt_ref[...], xn2.astype(BF16), _NT, preferred_element_type=F32)
    z = jnp.exp(logits_t - logits_t.max(axis=0, keepdims=True))
    pt_ref[...] = z / z.sum(axis=0, keepdims=True)


def _mix(x, oab, oc, h_g, wa, wb, wc, wo, g1, n2g, sc2, sh2, wrt, seq):
    n = x.shape[0]
    tm = 512
    per_batch = g1.shape[0] > 1
    mod_map = (lambda i: ((i * tm) // seq, 0, 0)) if per_batch else (lambda i: (0, 0, 0))
    mod_spec = pl.BlockSpec((None, 1, D_MODEL), mod_map)
    full = lambda a: pl.BlockSpec(a.shape, lambda i: (0,) * a.ndim)
    row = lambda w: pl.BlockSpec((tm, w), lambda i: (i, 0))
    return pl.pallas_call(
        _mix_kernel,
        grid=(n // tm,),
        in_specs=[row(D_MODEL), row(A_Q + B_W), row(C_WIDTH), row(GATE_W), full(wa), full(wb), full(wc), full(wo),
                  mod_spec, full(n2g), mod_spec, mod_spec, full(wrt)],
        out_specs=[row(D_MODEL), pl.BlockSpec((tm * ROW_TILE, LANES), lambda i: (i, 0)),
                   pl.BlockSpec((N_EXPERTS, tm), lambda i: (0, i))],
        out_shape=[jax.ShapeDtypeStruct((n, D_MODEL), F32), jax.ShapeDtypeStruct((n * ROW_TILE, LANES), F32),
                   jax.ShapeDtypeStruct((N_EXPERTS, n), F32)],
        compiler_params=_params("mix", 1),
        name="mix",
    )(x, oab, oc, h_g, wa, wb, wc, wo, g1, n2g, sc2, sh2, wrt)


MOE_TF = 512
MOE_ROW_CHUNKS = ((0, 640), (640, 640))
MOE_ISSUE_UNROLL = 16


def _moe_issue_rows(idx_ref, idx0, src_hbm, dst_ref, dst_row0, count, sem):
    def body(i, carry):
        for u in range(MOE_ISSUE_UNROLL):
            r = i * MOE_ISSUE_UNROLL + u
            tok = idx_ref[idx0 + r]
            src = src_hbm.at[pl.ds(pl.multiple_of(tok * ROW_TILE, ROW_TILE), ROW_TILE)]
            dst = dst_ref.at[pl.ds(pl.multiple_of((dst_row0 + r) * ROW_TILE, ROW_TILE), ROW_TILE)]
            pltpu.make_async_copy(src, dst, sem).start(priority=u % 2)
        return carry

    lax.fori_loop(0, count // MOE_ISSUE_UNROLL, body, 0)


def _moe_kernel(idx_ref, xc_hbm, xl_hbm, aff_ref, wg_ref, wu_ref, wd_ref, ye_ref, xe_ref, sem):
    e = pl.program_id(0)
    f = pl.program_id(1)
    slot = e % 2

    def gather(expert, s):
        base = expert * CAP_ALL
        _moe_issue_rows(idx_ref, base, xc_hbm, xe_ref.at[s], 0, CAP_CTX, sem.at[s])
        _moe_issue_rows(idx_ref, base + CAP_CTX, xl_hbm, xe_ref.at[s], CAP_CTX, CAP_LAT, sem.at[s])

    @pl.when(f == 0)
    def _():
        @pl.when(e == 0)
        def _():
            gather(0, 0)

        pltpu.make_async_copy(xc_hbm.at[pl.ds(0, CAP_ALL * ROW_TILE)], xe_ref.at[slot], sem.at[slot]).wait()

        @pl.when(e + 1 < N_EXPERTS)
        def _():
            gather(e + 1, 1 - slot)

    wg = wg_ref[...].astype(BF16)
    wu = wu_ref[...].astype(BF16)
    wd = wd_ref[...].astype(BF16)
    for r0, rn in MOE_ROW_CHUNKS:
        xe = jnp.concatenate([xe_ref[slot, pl.ds(r0 * ROW_TILE + j, rn, stride=ROW_TILE), :]
                              for j in range(ROW_TILE)], axis=1).astype(BF16)
        hg = jnp.dot(xe, wg, preferred_element_type=F32)
        hu = jnp.dot(xe, wu, preferred_element_type=F32)
        act = (hg * jax.nn.sigmoid(hg) * hu).astype(BF16)
        part = jnp.dot(act, wd, preferred_element_type=F32) * aff_ref[r0:r0 + rn, :]

        @pl.when(f == 0)
        def _():
            ye_ref[r0:r0 + rn, :] = part

        @pl.when(f != 0)
        def _():
            ye_ref[r0:r0 + rn, :] += part


def _moe(idx_all, xn2_ctx, xn2_lat, aff_all, w_gate_e, w_up_e, w_down_e, layer):
    nf = EXPERT_FF // MOE_TF
    grid_spec = pltpu.PrefetchScalarGridSpec(
        num_scalar_prefetch=1,
        grid=(N_EXPERTS, nf),
        in_specs=[
            pl.BlockSpec(memory_space=pl.ANY),
            pl.BlockSpec(memory_space=pl.ANY),
            pl.BlockSpec((None, CAP_ALL, 1), lambda e, f, idx: (e, 0, 0)),
            pl.BlockSpec((None, None, D_MODEL, MOE_TF), lambda e, f, idx: (layer, e, 0, f)),
            pl.BlockSpec((None, None, D_MODEL, MOE_TF), lambda e, f, idx: (layer, e, 0, f)),
            pl.BlockSpec((None, None, MOE_TF, D_MODEL), lambda e, f, idx: (layer, e, f, 0)),
        ],
        out_specs=pl.BlockSpec((None, CAP_ALL, D_MODEL), lambda e, f, idx: (e, 0, 0)),
        scratch_shapes=[pltpu.VMEM((2, CAP_ALL * ROW_TILE, LANES), F32), pltpu.SemaphoreType.DMA((2,))],
    )
    return pl.pallas_call(
        _moe_kernel,
        grid_spec=grid_spec,
        out_shape=jax.ShapeDtypeStruct((N_EXPERTS, CAP_ALL, D_MODEL), F32),
        compiler_params=_params("moe_experts", 2),
        name="moe_experts",
    )(idx_all, xn2_ctx, xn2_lat, aff_all, w_gate_e, w_up_e, w_down_e)


POS_SPLIT = 32


def _prefix_count(mask_bf, tri_lane, tri_blk_strict):
    within = jnp.dot(mask_bf, tri_lane, preferred_element_type=F32)
    bs = jnp.broadcast_to(within[:, LANES - 1:LANES], within.shape).astype(BF16)
    return within + jnp.dot(tri_blk_strict, bs, preferred_element_type=F32)


def _route_kernel(p_ref, idx_ref, aff_ref, *, cap):
    p_all = p_ref[...]
    nb = p_all.shape[1]
    capf = float(cap)

    def as_float(word):
        return lax.bitcast_convert_type(word, F32)

    def bisect(_, c):
        lo, hi = c
        mid = lo + ((hi - lo) >> 1)
        ok = jnp.sum(jnp.where(p_all >= as_float(mid), 1.0, 0.0), axis=(1, 2), keepdims=True) >= capf
        return jnp.where(ok, mid, lo), jnp.where(ok, hi, mid)

    lo0 = jnp.zeros((N_EXPERTS, 1, 1), jnp.int32)
    hi0 = jnp.full((N_EXPERTS, 1, 1), 0x7F800000, jnp.int32)
    thr_all, _ = lax.fori_loop(0, 31, bisect, (lo0, hi0))
    thr_val = as_float(thr_all)

    li = lax.broadcasted_iota(jnp.int32, (LANES, LANES), 0)
    lj = lax.broadcasted_iota(jnp.int32, (LANES, LANES), 1)
    tri_lane = jnp.where(li <= lj, 1.0, 0.0).astype(BF16)
    bi = lax.broadcasted_iota(jnp.int32, (nb, nb), 0)
    bj = lax.broadcasted_iota(jnp.int32, (nb, nb), 1)
    tri_blk_strict = jnp.where(bj < bi, 1.0, 0.0).astype(BF16)
    tri_blk_incl = jnp.where(bi <= bj, 1.0, 0.0).astype(BF16)
    s_col = lax.broadcasted_iota(jnp.int32, (cap, 1), 0).astype(F32)
    blk_id = lax.broadcasted_iota(jnp.int32, (1, nb), 1).astype(F32)
    lane_id = lax.broadcasted_iota(jnp.int32, (1, LANES), 1).astype(F32)
    ones_rows = jnp.ones((8, LANES), BF16)

    for e in range(N_EXPERTS):
        p = p_all[e]
        thr = thr_val[e]
        gt = p > thr
        eq = p == thr
        need = capf - jnp.sum(jnp.where(gt, 1.0, 0.0), keepdims=True)
        tie_rank = _prefix_count(jnp.where(eq, 1.0, 0.0).astype(BF16), tri_lane, tri_blk_strict)
        sel = jnp.where(gt | (eq & (tie_rank <= need)), 1.0, 0.0).astype(BF16)
        pos = _prefix_count(sel, tri_lane, tri_blk_strict)

        bs_row = lax.dot_general(ones_rows, sel, _NT, preferred_element_type=F32)
        bp_row = jnp.dot(bs_row.astype(BF16), tri_blk_incl, preferred_element_type=F32)[0:1, :]
        blk = jnp.sum(jnp.where(bp_row <= s_col, 1.0, 0.0), axis=-1, keepdims=True)
        onehot = jnp.where(blk == blk_id, 1.0, 0.0).astype(BF16)
        pos_hi = jnp.floor(pos * (1.0 / POS_SPLIT))
        pos_lo = pos - pos_hi * POS_SPLIT
        pos_row = (jnp.dot(onehot, pos_hi.astype(BF16), preferred_element_type=F32) * POS_SPLIT
                   + jnp.dot(onehot, pos_lo.astype(BF16), preferred_element_type=F32))
        lane = jnp.sum(jnp.where(pos_row <= s_col, 1.0, 0.0), axis=-1, keepdims=True)
        idx_ref[e] = (blk * LANES + lane).astype(jnp.int32)

        p1 = p.astype(BF16)
        r1 = p - p1.astype(F32)
        p2 = r1.astype(BF16)
        p3 = (r1 - p2.astype(F32)).astype(BF16)
        p_row = (jnp.dot(onehot, p1, preferred_element_type=F32) + jnp.dot(onehot, p2, preferred_element_type=F32)
                 + jnp.dot(onehot, p3, preferred_element_type=F32))
        aff_ref[e] = jnp.sum(jnp.where(lane_id == lane, p_row, 0.0), axis=-1, keepdims=True)


def _route(probs_t, cap):
    nb = probs_t.shape[1] // LANES
    slot = pl.BlockSpec((N_EXPERTS, cap, 1), lambda i: (0, 0, 0))
    return pl.pallas_call(
        functools.partial(_route_kernel, cap=cap),
        grid=(1,),
        in_specs=[pl.BlockSpec((N_EXPERTS, nb, LANES), lambda i: (0, 0, 0))],
        out_specs=[slot, slot],
        out_shape=[jax.ShapeDtypeStruct((N_EXPERTS, cap, 1), jnp.int32),
                   jax.ShapeDtypeStruct((N_EXPERTS, cap, 1), F32)],
        compiler_params=_params("route", 1),
        name="route",
    )(probs_t.reshape(N_EXPERTS, nb, LANES))


CMB_TT = 1024
CMB_CH = 256
CMB_GROUP = 8
CMB_EPI_ROWS = 256


def _combine_kernel(idx_ref, st_ref, ye_hbm, x_ref, g2_ref, ng_ref, sc_ref, sh_ref, *rest, off, cap, nt, final):
    if final:
        y_ref, acc_ref, buf_ref, sem = rest
    else:
        xo_ref, xn_ref, acc_ref, buf_ref, sem = rest
    k = pl.program_id(0)
    t0 = k * CMB_TT
    acc_ref[...] = jnp.zeros_like(acc_ref)

    def slot_range(e):
        s0 = st_ref[e * (nt + 1) + k]
        s1 = st_ref[e * (nt + 1) + k + 1]
        return s0, s1, jnp.minimum(s0 & -CMB_GROUP, cap - CMB_CH)

    def chunk_copy(e, start):
        src = ye_hbm.at[e, pl.ds(pl.multiple_of(off + start, CMB_GROUP), CMB_CH)]
        return pltpu.make_async_copy(src, buf_ref.at[e], sem.at[e])

    def accumulate(e, start, lo, hi):
        def group(i, carry, checked):
            r0 = pl.multiple_of(i * CMB_GROUP, CMB_GROUP)
            tile = buf_ref[e, pl.ds(r0, CMB_GROUP), :]
            dst = []
            for u in range(CMB_GROUP):
                slot = start + r0 + u
                row = idx_ref[e * CAP_ALL + off + slot] - t0
                if checked:
                    row = jnp.where((slot >= lo) & (slot < hi), row, CMB_TT)
                dst.append(row)
            rows = [acc_ref[pl.ds(d, 1), :] for d in dst]
            for u in range(CMB_GROUP):
                acc_ref[pl.ds(dst[u], 1), :] = rows[u] + tile[u:u + 1, :]
            return carry

        g_lo = (lo - start) // CMB_GROUP
        g_hi = (hi - start + CMB_GROUP - 1) // CMB_GROUP
        full_lo = jnp.minimum((lo - start + CMB_GROUP - 1) // CMB_GROUP, g_hi)
        full_hi = jnp.maximum((hi - start) // CMB_GROUP, full_lo)
        lax.fori_loop(g_lo, full_lo, functools.partial(group, checked=True), 0)
        lax.fori_loop(full_lo, full_hi, functools.partial(group, checked=False), 0)
        lax.fori_loop(full_hi, g_hi, functools.partial(group, checked=True), 0)

    for e in range(N_EXPERTS):
        chunk_copy(e, slot_range(e)[2]).start()
    for e in range(N_EXPERTS):
        s0, s1, a = slot_range(e)
        chunk_copy(e, a).wait()
        accumulate(e, a, s0, jnp.minimum(s1, a + CMB_CH))

        def extra(j, carry, e=e, s1=s1, a=a):
            lo = a + CMB_CH * (j + 1)
            start = jnp.minimum(lo, cap - CMB_CH)
            cp = chunk_copy(e, start)
            cp.start()
            cp.wait()
            accumulate(e, start, lo, jnp.minimum(s1, lo + CMB_CH))
            return carry

        lax.fori_loop(0, jnp.maximum(s1 - a - 1, 0) // CMB_CH, extra, 0)

    for r0 in range(0, CMB_TT, CMB_EPI_ROWS):
        rs = slice(r0, r0 + CMB_EPI_ROWS)
        xo = x_ref[rs, :] + g2_ref[...] * acc_ref[rs, :]
        y = xo * lax.rsqrt(jnp.mean(xo * xo, axis=-1, keepdims=True) + EPS) * ng_ref[...]
        if final:
            y_ref[rs, :] = y
        else:
            xo_ref[rs, :] = xo
            xn_ref[rs, :] = (y * (1.0 + sc_ref[...]) + sh_ref[...]).astype(xn_ref.dtype)


def _combine(idx_all, starts, ye, x, g2, ng, sc, sh, seq, off, cap, final):
    n = x.shape[0]
    nt = n // CMB_TT
    per_batch = g2.shape[0] > 1
    mod_map = ((lambda i, *_: ((i * CMB_TT) // seq, 0, 0)) if per_batch else (lambda i, *_: (0, 0, 0)))
    mod_spec = pl.BlockSpec((None, 1, D_MODEL), mod_map)
    row = pl.BlockSpec((CMB_TT, D_MODEL), lambda i, *_: (i, 0))
    vec = pl.BlockSpec((1, D_MODEL), lambda i, *_: (0, 0))
    if final:
        out_specs, out_shape = row, jax.ShapeDtypeStruct((n, D_MODEL), F32)
    else:
        out_specs = [row, row]
        out_shape = [jax.ShapeDtypeStruct((n, D_MODEL), F32), jax.ShapeDtypeStruct((n, D_MODEL), BF16)]
    grid_spec = pltpu.PrefetchScalarGridSpec(
        num_scalar_prefetch=2,
        grid=(nt,),
        in_specs=[pl.BlockSpec(memory_space=pl.ANY), row, mod_spec, vec, mod_spec, mod_spec],
        out_specs=out_specs,
        scratch_shapes=[pltpu.VMEM((CMB_TT + CMB_GROUP, D_MODEL), F32),
                        pltpu.VMEM((N_EXPERTS, CMB_CH, D_MODEL), F32),
                        pltpu.SemaphoreType.DMA((N_EXPERTS,))],
    )
    return pl.pallas_call(
        functools.partial(_combine_kernel, off=off, cap=cap, nt=nt, final=final),
        grid_spec=grid_spec,
        out_shape=out_shape,
        compiler_params=_params("combine", 1),
        name="combine",
    )(idx_all, starts.reshape(-1), ye, x, g2, ng, sc, sh)


def _tile_starts(idx_sorted, n):
    bounds = jnp.arange(n // CMB_TT + 1, dtype=jnp.int32) * CMB_TT
    return jnp.sum(idx_sorted[:, :, None] < bounds[None, None, :], axis=1, dtype=jnp.int32)


def kernel(x_prompt, x_sample, cache_attn_k, cache_attn_v, cache_na_k, cache_na_v, c, c_ctx, w_ada, b_ada, norm1_g,
           w_in, q_norm_g, k_norm_g, na_rpb, conv_w, w_proj_a, w_proj_b, w_proj_c, w_out, norm2_g, w_router,
           w_gate_e, w_up_e, w_down_e, final_g):
    cond8 = jnp.zeros((8, D_MODEL), F32).at[0].set(c_ctx).at[1:1 + DEC_BATCH].set(c)
    mods = _adaln(cond8, w_ada, b_ada).reshape(DEPTH, 8, N_MOD, D_MODEL)
    cos128, sin128 = _rope_tables()

    hp = x_prompt.reshape(N_CTX, D_MODEL)
    hs = x_sample.reshape(N_LAT, D_MODEL)
    kv_stacked = []
    for l in range(DEPTH):
        mod_ctx = [mods[l, 0:1, i][:, None, :] for i in range(N_MOD)]
        mod_lat = [mods[l, 1:1 + DEC_BATCH, i][:, None, :] for i in range(N_MOD)]
        n1g = norm1_g[l][None]
        n2g = norm2_g[l][None]
        qg2 = jnp.tile(q_norm_g[l][None], (1, 2))
        kg2 = jnp.tile(k_norm_g[l][None], (1, 2))
        wa, wb, wc, wo = (w_proj_a[l].astype(BF16), w_proj_b[l].astype(BF16), w_proj_c[l].astype(BF16),
                          w_out[l].astype(BF16))
        wrt = w_router[l].T.astype(BF16)
        toep = _na_toeplitz(na_rpb[l])

        def sublayer1(x, xn, mod, seq, is_ctx):
            _, _, g1, sh2, sc2, _ = mod
            kv_dtype = F32 if is_ctx else BF16
            h_a = _inproj(xn, w_in, l, COL_A, W_A, 768, kv_dtype)
            h_b = _inproj(xn, w_in, l, COL_B, W_B, 768, kv_dtype)
            h_c = _inproj(xn, w_in, l, COL_C, W_C, 768, BF16)
            h_g = _inproj(xn, w_in, l, COL_G, GATE_W, 768, BF16)
            kv = None
            if is_ctx:
                oab, *kv = _attn_ctx(h_a, h_b, qg2, kg2, l, kv_stacked)
            else:
                oab = _attn_lat(h_a, h_b, cache_attn_k, cache_attn_v, cache_na_k, cache_na_v, l, qg2, kg2,
                                cos128, sin128, toep)
            oc = _conv(h_c, conv_w[l], seq)
            xo, xn2, probs_t = _mix(x, oab, oc, h_g, wa, wb, wc, wo, g1, n2g, sc2, sh2, wrt, seq)
            return xo, xn2, probs_t, kv

        if l == 0:
            xn_p = _norm_mod(hp, n1g, mod_ctx[1], mod_ctx[0], SEQ, BF16)
            xn_s = _norm_mod(hs, n1g, mod_lat[1], mod_lat[0], DEC_SEQ, BF16)
        hp, xn2_c, pt_c, kv_stacked = sublayer1(hp, xn_p, mod_ctx, SEQ, True)
        hs, xn2_l, pt_l, _ = sublayer1(hs, xn_s, mod_lat, DEC_SEQ, False)

        idx_c, aff_c = _route(pt_c, CAP_CTX)
        idx_l, aff_l = _route(pt_l, CAP_LAT)
        idx_all = jnp.concatenate([idx_c, idx_l], axis=1).reshape(-1)
        aff_all = jnp.concatenate([aff_c, aff_l], axis=1)
        ye = _moe(idx_all, xn2_c, xn2_l, aff_all, w_gate_e, w_up_e, w_down_e, l)

        final = l == DEPTH - 1
        if final:
            ng, nsc_c, nsh_c, nsc_l, nsh_l = final_g[None], mod_ctx[1], mod_ctx[0], mod_lat[1], mod_lat[0]
        else:
            nxt_ctx = [mods[l + 1, 0:1, i][:, None, :] for i in range(2)]
            nxt_lat = [mods[l + 1, 1:1 + DEC_BATCH, i][:, None, :] for i in range(2)]
            ng, nsc_c, nsh_c, nsc_l, nsh_l = norm1_g[l + 1][None], nxt_ctx[1], nxt_ctx[0], nxt_lat[1], nxt_lat[0]
        out_c = _combine(idx_all, _tile_starts(idx_c[..., 0], N_CTX), ye, hp, mod_ctx[5], ng, nsc_c, nsh_c,
                         SEQ, 0, CAP_CTX, final)
        out_l = _combine(idx_all, _tile_starts(idx_l[..., 0], N_LAT), ye, hs, mod_lat[5], ng, nsc_l, nsh_l,
                         DEC_SEQ, CAP_CTX, CAP_LAT, final)
        if final:
            y_prompt = out_c.reshape(BATCH, SEQ, D_MODEL)
            y_sample = out_l.reshape(DEC_BATCH, DEC_SEQ, D_MODEL)
        else:
            (hp, xn_p), (hs, xn_s) = out_c, out_l

    return (y_prompt, y_sample, *kv_stacked)
```

```python
import functools

import jax
import jax.numpy as jnp
import numpy as np
from jax import lax
from jax.experimental import pallas as pl
from jax.experimental.pallas import tpu as pltpu

D_MODEL = 1024
BATCH = 32
SEQ = 256
DEPTH = 2
DEC_BATCH = 2
DEC_SEQ = 1024
PAST_LEN = 256
GRID_W = 64
HEAD_DIM = 64
A_HEADS = 8
A_KV_HEADS = 2
B_HEADS = 8
C_WIDTH = 512
NA_WIN_R = 8
NA_WIN_C = 16
N_EXPERTS = 16
EXPERT_FF = 1024
CAP_FACTOR = 2
ROPE_THETA = 10000.0
EPS = 1e-6
N_MOD = 6

A_Q = A_HEADS * HEAD_DIM
A_KV = A_KV_HEADS * HEAD_DIM
B_W = B_HEADS * HEAD_DIM
GATE_W = 3 * D_MODEL
COL_A = 0
W_A = A_Q + 2 * A_KV
COL_B = COL_A + W_A
W_B = 3 * B_W
COL_C = COL_B + W_B
W_C = 3 * C_WIDTH
COL_G = COL_C + W_C

N_CTX = BATCH * SEQ
N_LAT = DEC_BATCH * DEC_SEQ
CAP_CTX = CAP_FACTOR * N_CTX // N_EXPERTS
CAP_LAT = CAP_FACTOR * N_LAT // N_EXPERTS
CAP_ALL = CAP_CTX + CAP_LAT
LAT_ROWS = DEC_SEQ // GRID_W
MASK_NEG = -1e30
ATT_SCALE = HEAD_DIM ** -0.5

LANES = 128
ROW_TILE = 8
assert D_MODEL == ROW_TILE * LANES

F32 = jnp.float32
BF16 = jnp.bfloat16
_NT = (((1,), (1,)), ((), ()))

_MIB = 1024 * 1024


VMEM_MIB = {"adaln": 40, "norm_mod": 32, "inproj": 48, "attn_ctx": 56, "attn_lat": 56, "short_conv": 40,
            "mix": 56, "moe_experts": 56, "route": 56, "combine": 60}


def _params(name, grid_rank):
    return pltpu.CompilerParams(dimension_semantics=("arbitrary",) * grid_rank,
                                vmem_limit_bytes=VMEM_MIB[name] * _MIB)


def _adaln_kernel(cond_ref, w_ref, b_ref, o_ref):
    c = cond_ref[...]
    s = c * jax.nn.sigmoid(c)
    o_ref[...] = jnp.dot(s.astype(BF16), w_ref[...].astype(BF16), preferred_element_type=F32) + b_ref[...]


def _adaln(cond8, w_ada, b_ada):
    tn = 1536
    ncol = N_MOD * D_MODEL
    return pl.pallas_call(
        _adaln_kernel,
        grid=(DEPTH, ncol // tn),
        in_specs=[
            pl.BlockSpec((8, D_MODEL), lambda l, j: (0, 0)),
            pl.BlockSpec((None, D_MODEL, tn), lambda l, j: (l, 0, j)),
            pl.BlockSpec((None, 1, tn), lambda l, j: (l, 0, j)),
        ],
        out_specs=pl.BlockSpec((None, 8, tn), lambda l, j: (l, 0, j)),
        out_shape=jax.ShapeDtypeStruct((DEPTH, 8, ncol), F32),
        compiler_params=_params("adaln", 2),
        name="adaln",
    )(cond8, w_ada, b_ada.reshape(DEPTH, 1, ncol))


def _norm_mod_kernel(x_ref, g_ref, sc_ref, sh_ref, o_ref):
    x = x_ref[...]
    y = x * lax.rsqrt(jnp.mean(x * x, axis=-1, keepdims=True) + EPS) * g_ref[...]
    o_ref[...] = (y * (1.0 + sc_ref[...]) + sh_ref[...]).astype(o_ref.dtype)


def _norm_mod(x, g, sc, sh, seq, out_dtype):
    n = x.shape[0]
    tm = 512
    per_batch = sc.shape[0] > 1
    mod_map = (lambda i: ((i * tm) // seq, 0, 0)) if per_batch else (lambda i: (0, 0, 0))
    return pl.pallas_call(
        _norm_mod_kernel,
        grid=(n // tm,),
        in_specs=[
            pl.BlockSpec((tm, D_MODEL), lambda i: (i, 0)),
            pl.BlockSpec((1, D_MODEL), lambda i: (0, 0)),
            pl.BlockSpec((None, 1, D_MODEL), mod_map),
            pl.BlockSpec((None, 1, D_MODEL), mod_map),
        ],
        out_specs=pl.BlockSpec((tm, D_MODEL), lambda i: (i, 0)),
        out_shape=jax.ShapeDtypeStruct((n, D_MODEL), out_dtype),
        compiler_params=_params("norm_mod", 1),
        name="norm_mod",
    )(x, g, sc, sh)


def _inproj_kernel(a_ref, w_ref, o_ref, wbf_ref):
    @pl.when(pl.program_id(1) == 0)
    def _():
        wbf_ref[...] = w_ref[...].astype(BF16)

    o_ref[...] = jnp.dot(a_ref[...], wbf_ref[...], preferred_element_type=F32).astype(o_ref.dtype)


def _inproj(xn, w_in, layer, col0, width, tn, out_dtype):
    n = xn.shape[0]
    tm = min(n, 4096 if out_dtype == BF16 else 2048)
    joff = col0 // tn
    assert col0 % tn == 0 and width % tn == 0
    return pl.pallas_call(
        _inproj_kernel,
        grid=(width // tn, n // tm),
        in_specs=[
            pl.BlockSpec((tm, D_MODEL), lambda j, i: (i, 0)),
            pl.BlockSpec((None, D_MODEL, tn), lambda j, i: (layer, 0, j + joff)),
        ],
        out_specs=pl.BlockSpec((tm, tn), lambda j, i: (i, j)),
        out_shape=jax.ShapeDtypeStruct((n, width), out_dtype),
        scratch_shapes=[pltpu.VMEM((D_MODEL, tn), BF16)],
        compiler_params=_params("inproj", 2),
        name="inproj",
    )(xn, w_in)


PAIR = 2 * HEAD_DIM


def _is_lo_half():
    return lax.broadcasted_iota(jnp.int32, (1, PAIR), 1) < HEAD_DIM


def _pair_rms(x, g2, is_lo):
    x2 = x * x
    s_lo = jnp.sum(jnp.where(is_lo, x2, 0.0), axis=-1, keepdims=True)
    s_hi = jnp.sum(jnp.where(is_lo, 0.0, x2), axis=-1, keepdims=True)
    ms = jnp.where(is_lo, s_lo, s_hi) * (1.0 / HEAD_DIM)
    return x * lax.rsqrt(ms + EPS) * g2


def _split_halves(x, is_lo):
    return jnp.where(is_lo, x, 0.0).astype(BF16), jnp.where(is_lo, 0.0, x).astype(BF16)


def _pad_lo(x):
    return jnp.concatenate([x, jnp.zeros_like(x)], axis=1)


def _pad_hi(x):
    return jnp.concatenate([jnp.zeros_like(x), x], axis=1)


def _qk(q, k):
    return lax.dot_general(q, k, _NT, preferred_element_type=F32)


def _softmax_pv(scores, values):
    m = scores[0].max(axis=-1, keepdims=True)
    for s in scores[1:]:
        m = jnp.maximum(m, s.max(axis=-1, keepdims=True))
    acc = None
    den = None
    for s, v in zip(scores, values):
        p = jnp.exp(s - m)
        d = p.sum(axis=-1, keepdims=True)
        o = jnp.dot(p.astype(BF16), v, preferred_element_type=F32)
        acc = o if acc is None else acc + o
        den = d if den is None else den + d
    return acc * (1.0 / den)


CTX_NB = 2


def _attn_ctx_kernel(ha_ref, hb_ref, qg_ref, kg_ref, *rest, first):
    oab_ref, *kv_refs = rest[-5:]
    if first:
        for r in kv_refs:
            r[:, 1:] = jnp.zeros((CTX_NB, DEPTH - 1) + r.shape[2:], F32)
        ak_ref, av_ref, nk_ref, nv_ref = [r.at[:, 0] for r in kv_refs]
    else:
        ak_ref, av_ref, nk_ref, nv_ref = kv_refs
    is_lo = _is_lo_half()
    qg2 = qg_ref[...] * ATT_SCALE
    kg2 = kg_ref[...]
    for b in range(CTX_NB):
        rows = slice(b * SEQ, (b + 1) * SEQ)
        kp = _pair_rms(ha_ref[rows, A_Q:A_Q + PAIR], kg2, is_lo)
        vp = ha_ref[rows, A_Q + A_KV:A_Q + A_KV + PAIR]
        for kv in range(A_KV_HEADS):
            ak_ref[b, kv] = kp[:, kv * HEAD_DIM:(kv + 1) * HEAD_DIM]
            av_ref[b, kv] = vp[:, kv * HEAD_DIM:(kv + 1) * HEAD_DIM]
        kp_sw = pltpu.roll(kp, HEAD_DIM, axis=1)
        vp_sw = pltpu.roll(vp, HEAD_DIM, axis=1)
        qs = [_pair_rms(ha_ref[rows, c * PAIR:(c + 1) * PAIR], qg2, is_lo).astype(BF16)
              for c in range(A_HEADS // 2)]
        for kv in range(A_KV_HEADS):
            k_lo = jnp.where(is_lo, kp if kv == 0 else kp_sw, 0.0).astype(BF16)
            k_hi = jnp.where(is_lo, 0.0, kp_sw if kv == 0 else kp).astype(BF16)
            v_lo = jnp.where(is_lo, vp if kv == 0 else vp_sw, 0.0).astype(BF16)
            v_hi = jnp.where(is_lo, 0.0, vp_sw if kv == 0 else vp).astype(BF16)
            q_st = jnp.concatenate([qs[2 * kv], qs[2 * kv + 1]], axis=0)
            o = _softmax_pv([_qk(q_st, k_lo)], [v_lo]) + _softmax_pv([_qk(q_st, k_hi)], [v_hi])
            oab_ref[rows, (2 * kv) * PAIR:(2 * kv + 1) * PAIR] = o[0:SEQ].astype(BF16)
            oab_ref[rows, (2 * kv + 1) * PAIR:(2 * kv + 2) * PAIR] = o[SEQ:2 * SEQ].astype(BF16)
        for j in range(B_HEADS // 2):
            qp = (hb_ref[rows, j * PAIR:(j + 1) * PAIR] * ATT_SCALE).astype(BF16)
            kp_b = hb_ref[rows, B_W + j * PAIR:B_W + (j + 1) * PAIR]
            vp_b = hb_ref[rows, 2 * B_W + j * PAIR:2 * B_W + (j + 1) * PAIR]
            for u in range(2):
                nk_ref[b, 2 * j + u] = kp_b[:, u * HEAD_DIM:(u + 1) * HEAD_DIM]
                nv_ref[b, 2 * j + u] = vp_b[:, u * HEAD_DIM:(u + 1) * HEAD_DIM]
            k_lo, k_hi = _split_halves(kp_b, is_lo)
            v_lo, v_hi = _split_halves(vp_b, is_lo)
            o = _softmax_pv([_qk(qp, k_lo)], [v_lo]) + _softmax_pv([_qk(qp, k_hi)], [v_hi])
            oab_ref[rows, A_Q + j * PAIR:A_Q + (j + 1) * PAIR] = o.astype(BF16)


def _attn_ctx(h_a, h_b, qg2, kg2, layer, prev_kv):
    rows = CTX_NB * SEQ
    heads = (A_KV_HEADS, A_KV_HEADS, B_HEADS, B_HEADS)
    first = layer == 0
    assert first == (len(prev_kv) == 0)
    if first:
        kv_specs = [pl.BlockSpec((CTX_NB, DEPTH, nh, SEQ, HEAD_DIM), lambda b: (b, 0, 0, 0, 0)) for nh in heads]
    else:
        kv_specs = [pl.BlockSpec((CTX_NB, None, nh, SEQ, HEAD_DIM), lambda b: (b, layer, 0, 0, 0)) for nh in heads]
    n_in = 4
    return pl.pallas_call(
        functools.partial(_attn_ctx_kernel, first=first),
        grid=(BATCH // CTX_NB,),
        in_specs=[
            pl.BlockSpec((rows, W_A), lambda b: (b, 0)),
            pl.BlockSpec((rows, W_B), lambda b: (b, 0)),
            pl.BlockSpec((1, PAIR), lambda b: (0, 0)),
            pl.BlockSpec((1, PAIR), lambda b: (0, 0)),
        ] + [pl.BlockSpec(memory_space=pl.ANY)] * len(prev_kv),
        out_specs=[pl.BlockSpec((rows, A_Q + B_W), lambda b: (b, 0))] + kv_specs,
        out_shape=[jax.ShapeDtypeStruct((N_CTX, A_Q + B_W), BF16)]
        + [jax.ShapeDtypeStruct((BATCH, DEPTH, nh, SEQ, HEAD_DIM), F32) for nh in heads],
        input_output_aliases={n_in + i: 1 + i for i in range(len(prev_kv))},
        compiler_params=_params("attn_ctx", 1),
        name="attn_ctx",
    )(h_a, h_b, qg2, kg2, *prev_kv)


LAT_TQ = 256
LAT_QROWS = LAT_TQ // GRID_W
LAT_NT = DEC_SEQ // LAT_TQ
NA_UNION_ROWS = 12
NA_UNION = NA_UNION_ROWS * GRID_W


def _na_union_row0(t):
    lo = min(max(r - NA_WIN_R // 2, 0) for r in range(t * LAT_QROWS, (t + 1) * LAT_QROWS))
    return min(min(lo, LAT_ROWS - NA_WIN_R), LAT_ROWS - NA_UNION_ROWS)


def _rope(x, cos, sin):
    lane = lax.broadcasted_iota(jnp.int32, x.shape, 1)
    nxt = pltpu.roll(x, x.shape[1] - 1, axis=1)
    prv = pltpu.roll(x, 1, axis=1)
    partner = jnp.where((lane & 1) == 0, nxt, prv)
    return x * cos + partner * sin


def _attn_lat_kernel(pick_ref, haq_ref, haf_ref, hbq_ref, hbf_ref, cak_ref, cav_ref, cnk_ref, cnv_ref,
                     qg_ref, kg_ref, cos_ref, sin_ref, toep_ref, oab_ref, kva_ref):
    t = pl.program_id(1)
    is_lo = _is_lo_half()
    row0 = pl.multiple_of(t * LAT_TQ, LAT_TQ)
    qg2 = qg_ref[...] * ATT_SCALE
    kg2 = kg_ref[...]

    @pl.when(t == 0)
    def _():
        kp = _rope(_pair_rms(haf_ref[:, A_Q:A_Q + PAIR].astype(F32), kg2, is_lo), cos_ref[...], sin_ref[...])
        vp = haf_ref[:, A_Q + A_KV:A_Q + A_KV + PAIR].astype(F32)
        kp_sw = pltpu.roll(kp, HEAD_DIM, axis=1)
        vp_sw = pltpu.roll(vp, HEAD_DIM, axis=1)
        for kv in range(A_KV_HEADS):
            kva_ref[4 * kv + 0] = jnp.where(is_lo, kp if kv == 0 else kp_sw, 0.0).astype(BF16)
            kva_ref[4 * kv + 1] = jnp.where(is_lo, 0.0, kp_sw if kv == 0 else kp).astype(BF16)
            kva_ref[4 * kv + 2] = jnp.where(is_lo, vp if kv == 0 else vp_sw, 0.0).astype(BF16)
            kva_ref[4 * kv + 3] = jnp.where(is_lo, 0.0, vp_sw if kv == 0 else vp).astype(BF16)

    cos_q = cos_ref[pl.ds(row0, LAT_TQ), :]
    sin_q = sin_ref[pl.ds(row0, LAT_TQ), :]
    qs = [_rope(_pair_rms(haq_ref[:, c * PAIR:(c + 1) * PAIR].astype(F32), qg2, is_lo), cos_q, sin_q).astype(BF16)
          for c in range(A_HEADS // 2)]
    for kv in range(A_KV_HEADS):
        kc = cak_ref[kv].astype(BF16)
        vc = cav_ref[kv].astype(BF16)
        q_st = jnp.concatenate([qs[2 * kv], qs[2 * kv + 1]], axis=0)
        o = (_softmax_pv([_qk(q_st, _pad_lo(kc)), _qk(q_st, kva_ref[4 * kv + 0])], [_pad_lo(vc), kva_ref[4 * kv + 2]])
             + _softmax_pv([_qk(q_st, _pad_hi(kc)), _qk(q_st, kva_ref[4 * kv + 1])], [_pad_hi(vc), kva_ref[4 * kv + 3]]))
        oab_ref[:, (2 * kv) * PAIR:(2 * kv + 1) * PAIR] = o[0:LAT_TQ].astype(BF16)
        oab_ref[:, (2 * kv + 1) * PAIR:(2 * kv + 2) * PAIR] = o[LAT_TQ:2 * LAT_TQ].astype(BF16)

    k0 = _na_union_row0(0) * GRID_W
    for tt in range(1, LAT_NT):
        k0 = jnp.where(t >= tt, _na_union_row0(tt) * GRID_W, k0)
    k0 = pl.multiple_of(k0, GRID_W)

    def bias(h):
        base = t * (LAT_QROWS * NA_UNION_ROWS)
        return jnp.concatenate(
            [jnp.concatenate([toep_ref[h, pick_ref[base + i * NA_UNION_ROWS + u]] for u in range(NA_UNION_ROWS)],
                             axis=1) for i in range(LAT_QROWS)], axis=0)

    for j in range(B_HEADS // 2):
        qp = (hbq_ref[:, j * PAIR:(j + 1) * PAIR] * ATT_SCALE).astype(BF16)
        k_lo, k_hi = _split_halves(hbf_ref[pl.ds(k0, NA_UNION), B_W + j * PAIR:B_W + (j + 1) * PAIR], is_lo)
        v_lo, v_hi = _split_halves(hbf_ref[pl.ds(k0, NA_UNION), 2 * B_W + j * PAIR:2 * B_W + (j + 1) * PAIR], is_lo)
        kc_lo = _pad_lo(cnk_ref[2 * j].astype(BF16))
        kc_hi = _pad_hi(cnk_ref[2 * j + 1].astype(BF16))
        vc_lo = _pad_lo(cnv_ref[2 * j].astype(BF16))
        vc_hi = _pad_hi(cnv_ref[2 * j + 1].astype(BF16))
        o = (_softmax_pv([_qk(qp, kc_lo), _qk(qp, k_lo) + bias(2 * j)], [vc_lo, v_lo])
             + _softmax_pv([_qk(qp, kc_hi), _qk(qp, k_hi) + bias(2 * j + 1)], [vc_hi, v_hi]))
        oab_ref[:, A_Q + j * PAIR:A_Q + (j + 1) * PAIR] = o.astype(BF16)


def _attn_lat(h_a, h_b, cak, cav, cnk, cnv, layer, qg2, kg2, cos128, sin128, toep):
    nt = LAT_NT
    cache_spec = lambda nh: pl.BlockSpec((None, None, nh, PAST_LEN, HEAD_DIM), lambda b, t, _: (b, layer, 0, 0, 0))
    const = lambda shape: pl.BlockSpec(shape, lambda b, t, _: (0,) * len(shape))
    grid_spec = pltpu.PrefetchScalarGridSpec(
        num_scalar_prefetch=1,
        grid=(DEC_BATCH, nt),
        in_specs=[
            pl.BlockSpec((LAT_TQ, W_A), lambda b, t, _: (b * nt + t, 0)),
            pl.BlockSpec((DEC_SEQ, W_A), lambda b, t, _: (b, 0)),
            pl.BlockSpec((LAT_TQ, W_B), lambda b, t, _: (b * nt + t, 0)),
            pl.BlockSpec((DEC_SEQ, W_B), lambda b, t, _: (b, 0)),
            cache_spec(A_KV_HEADS), cache_spec(A_KV_HEADS), cache_spec(B_HEADS), cache_spec(B_HEADS),
            const((1, PAIR)), const((1, PAIR)), const((DEC_SEQ, PAIR)), const((DEC_SEQ, PAIR)),
            const(toep.shape),
        ],
        out_specs=pl.BlockSpec((LAT_TQ, A_Q + B_W), lambda b, t, _: (b * nt + t, 0)),
        scratch_shapes=[pltpu.VMEM((4 * A_KV_HEADS, DEC_SEQ, PAIR), BF16)],
    )
    return pl.pallas_call(
        _attn_lat_kernel,
        grid_spec=grid_spec,
        out_shape=jax.ShapeDtypeStruct((N_LAT, A_Q + B_W), BF16),
        compiler_params=_params("attn_lat", 2),
        name="attn_lat",
    )(jnp.asarray(_na_tile_picks().reshape(-1)), h_a, h_a, h_b, h_b, cak, cav, cnk, cnv, qg2, kg2, cos128, sin128,
      toep)


def _rope_tables():
    t = jnp.arange(DEC_SEQ)
    row = (t // GRID_W).astype(F32)
    col = (t % GRID_W).astype(F32)
    n_freq = HEAD_DIM // 4
    inv = ROPE_THETA ** (-jnp.arange(n_freq, dtype=F32) / n_freq)
    ang = jnp.concatenate([row[:, None] * inv, col[:, None] * inv], axis=-1)
    cos = jnp.repeat(jnp.cos(ang), 2, axis=-1)
    sign = jnp.tile(jnp.array([-1.0, 1.0], F32), HEAD_DIM // 2)
    sin = jnp.repeat(jnp.sin(ang), 2, axis=-1) * sign
    return jnp.tile(cos, (1, 2)), jnp.tile(sin, (1, 2))


NA_MASKED_TILE = 2 * NA_WIN_R - 1


def _na_toeplitz(rpb):
    col = np.arange(GRID_W)
    dc = np.clip(col[None, :] - col[:, None] + NA_WIN_C - 1, 0, 2 * NA_WIN_C - 2)
    onehot = (dc[..., None] == np.arange(2 * NA_WIN_C - 1)).astype(np.float32)
    c0 = np.clip(col - NA_WIN_C // 2, 0, GRID_W - NA_WIN_C)
    in_win = (col[None, :] >= c0[:, None]) & (col[None, :] < c0[:, None] + NA_WIN_C)
    toep = jnp.einsum('hdm,ckm->hdck', rpb, onehot, precision=lax.Precision.HIGHEST)
    toep = jnp.where(in_win[None, None], toep, MASK_NEG)
    return jnp.concatenate([toep, jnp.full((B_HEADS, 1, GRID_W, GRID_W), MASK_NEG, F32)], axis=1)


def _na_tile_picks():
    pick = np.full((LAT_NT, LAT_QROWS, NA_UNION_ROWS), NA_MASKED_TILE, np.int32)
    for t in range(LAT_NT):
        u0 = _na_union_row0(t)
        for i in range(LAT_QROWS):
            r = t * LAT_QROWS + i
            r0 = min(max(r - NA_WIN_R // 2, 0), LAT_ROWS - NA_WIN_R)
            for u in range(NA_UNION_ROWS):
                if r0 <= u0 + u < r0 + NA_WIN_R:
                    pick[t, i, u] = u0 + u - r + NA_WIN_R - 1
    return pick


def _conv_kernel(hc_ref, w_ref, o_ref, *, seq):
    bg = hc_ref[:, 0:C_WIDTH].astype(F32)
    u = hc_ref[:, C_WIDTH:2 * C_WIDTH].astype(F32) * hc_ref[:, 2 * C_WIDTH:3 * C_WIDTH].astype(F32)
    rows = u.shape[0]
    assert seq & (seq - 1) == 0
    pos = lax.broadcasted_iota(jnp.int32, u.shape, 0) & (seq - 1)
    u_prev = jnp.where(pos == 0, 0.0, pltpu.roll(u, 1, axis=0))
    u_next = jnp.where(pos == seq - 1, 0.0, pltpu.roll(u, rows - 1, axis=0))
    y = u_prev * w_ref[0:1, :] + u * w_ref[1:2, :] + u_next * w_ref[2:3, :]
    o_ref[...] = (bg * y).astype(o_ref.dtype)


def _conv(h_c, conv_w, seq):
    n = h_c.shape[0]
    tm = 1024
    return pl.pallas_call(
        functools.partial(_conv_kernel, seq=seq),
        grid=(n // tm,),
        in_specs=[pl.BlockSpec((tm, W_C), lambda i: (i, 0)), pl.BlockSpec((3, C_WIDTH), lambda i: (0, 0))],
        out_specs=pl.BlockSpec((tm, C_WIDTH), lambda i: (i, 0)),
        out_shape=jax.ShapeDtypeStruct((n, C_WIDTH), BF16),
        compiler_params=_params("short_conv", 1),
        name="short_conv",
    )(h_c, conv_w)


def _mix_kernel(x_ref, oab_ref, oc_ref, hg_ref, wa_ref, wb_ref, wc_ref, wo_ref, g1_ref, n2g_ref, sc2_ref, sh2_ref,
                wrt_ref, xo_ref, xn2_ref, pt_ref):
    ga = jax.nn.sigmoid(hg_ref[:, 0:D_MODEL].astype(F32))
    gb = jax.nn.sigmoid(hg_ref[:, D_MODEL:2 * D_MODEL].astype(F32))
    gc = jax.nn.sigmoid(hg_ref[:, 2 * D_MODEL:3 * D_MODEL].astype(F32))
    pa = jnp.dot(oab_ref[:, 0:A_Q], wa_ref[...], preferred_element_type=F32)
    pb = jnp.dot(oab_ref[:, A_Q:A_Q + B_W], wb_ref[...], preferred_element_type=F32)
    pc = jnp.dot(oc_ref[...], wc_ref[...], preferred_element_type=F32)
    merged = ga * pa + gb * pb + gc * pc
    xo = x_ref[...] + g1_ref[...] * jnp.dot(merged.astype(BF16), wo_ref[...], preferred_element_type=F32)
    xo_ref[...] = xo
    y = xo * lax.rsqrt(jnp.mean(xo * xo, axis=-1, keepdims=True) + EPS) * n2g_ref[...]
    xn2 = y * (1.0 + sc2_ref[...]) + sh2_ref[...]
    rows = xn2.shape[0]
    for j in range(ROW_TILE):
        xn2_ref[pl.ds(j, rows, stride=ROW_TILE), :] = xn2[:, j * LANES:(j + 1) * LANES]
    logits_t = lax.dot_general(wrt_ref[...], xn2.astype(BF16), _NT, preferred_element_type=F32)
    z = jnp.exp(logits_t - logits_t.max(axis=0, keepdims=True))
    pt_ref[...] = z / z.sum(axis=0, keepdims=True)


def _mix(x, oab, oc, h_g, wa, wb, wc, wo, g1, n2g, sc2, sh2, wrt, seq):
    n = x.shape[0]
    tm = 512
    per_batch = g1.shape[0] > 1
    mod_map = (lambda i: ((i * tm) // seq, 0, 0)) if per_batch else (lambda i: (0, 0, 0))
    mod_spec = pl.BlockSpec((None, 1, D_MODEL), mod_map)
    full = lambda a: pl.BlockSpec(a.shape, lambda i: (0,) * a.ndim)
    row = lambda w: pl.BlockSpec((tm, w), lambda i: (i, 0))
    return pl.pallas_call(
        _mix_kernel,
        grid=(n // tm,),
        in_specs=[row(D_MODEL), row(A_Q + B_W), row(C_WIDTH), row(GATE_W), full(wa), full(wb), full(wc), full(wo),
                  mod_spec, full(n2g), mod_spec, mod_spec, full(wrt)],
        out_specs=[row(D_MODEL), pl.BlockSpec((tm * ROW_TILE, LANES), lambda i: (i, 0)),
                   pl.BlockSpec((N_EXPERTS, tm), lambda i: (0, i))],
        out_shape=[jax.ShapeDtypeStruct((n, D_MODEL), F32), jax.ShapeDtypeStruct((n * ROW_TILE, LANES), F32),
                   jax.ShapeDtypeStruct((N_EXPERTS, n), F32)],
        compiler_params=_params("mix", 1),
        name="mix",
    )(x, oab, oc, h_g, wa, wb, wc, wo, g1, n2g, sc2, sh2, wrt)


MOE_TF = 512
MOE_ROW_CHUNKS = ((0, 640), (640, 640))
MOE_ISSUE_UNROLL = 16


def _moe_issue_rows(idx_ref, idx0, src_hbm, dst_ref, dst_row0, count, sem):
    def body(i, carry):
        for u in range(MOE_ISSUE_UNROLL):
            r = i * MOE_ISSUE_UNROLL + u
            tok = idx_ref[idx0 + r]
            src = src_hbm.at[pl.ds(pl.multiple_of(tok * ROW_TILE, ROW_TILE), ROW_TILE)]
            dst = dst_ref.at[pl.ds(pl.multiple_of((dst_row0 + r) * ROW_TILE, ROW_TILE), ROW_TILE)]
            pltpu.make_async_copy(src, dst, sem).start(priority=u % 2)
        return carry

    lax.fori_loop(0, count // MOE_ISSUE_UNROLL, body, 0)


def _moe_kernel(idx_ref, xc_hbm, xl_hbm, aff_ref, wg_ref, wu_ref, wd_ref, ye_ref, xe_ref, sem):
    e = pl.program_id(0)
    f = pl.program_id(1)
    slot = e % 2

    def gather(expert, s):
        base = expert * CAP_ALL
        _moe_issue_rows(idx_ref, base, xc_hbm, xe_ref.at[s], 0, CAP_CTX, sem.at[s])
        _moe_issue_rows(idx_ref, base + CAP_CTX, xl_hbm, xe_ref.at[s], CAP_CTX, CAP_LAT, sem.at[s])

    @pl.when(f == 0)
    def _():
        @pl.when(e == 0)
        def _():
            gather(0, 0)

        @pl.when(e + 1 < N_EXPERTS)
        def _():
            gather(e + 1, 1 - slot)

        pltpu.make_async_copy(xc_hbm.at[pl.ds(0, CAP_ALL * ROW_TILE)], xe_ref.at[slot], sem.at[slot]).wait()

    wg = wg_ref[...].astype(BF16)
    wu = wu_ref[...].astype(BF16)
    wd = wd_ref[...].astype(BF16)
    for r0, rn in MOE_ROW_CHUNKS:
        xe = jnp.concatenate([xe_ref[slot, pl.ds(r0 * ROW_TILE + j, rn, stride=ROW_TILE), :]
                              for j in range(ROW_TILE)], axis=1).astype(BF16)
        hg = jnp.dot(xe, wg, preferred_element_type=F32)
        hu = jnp.dot(xe, wu, preferred_element_type=F32)
        act = (hg * jax.nn.sigmoid(hg) * hu).astype(BF16)
        part = jnp.dot(act, wd, preferred_element_type=F32) * aff_ref[r0:r0 + rn, :]

        @pl.when(f == 0)
        def _():
            ye_ref[r0:r0 + rn, :] = part

        @pl.when(f != 0)
        def _():
            ye_ref[r0:r0 + rn, :] += part


def _moe(idx_all, xn2_ctx, xn2_lat, aff_all, w_gate_e, w_up_e, w_down_e, layer):
    nf = EXPERT_FF // MOE_TF
    grid_spec = pltpu.PrefetchScalarGridSpec(
        num_scalar_prefetch=1,
        grid=(N_EXPERTS, nf),
        in_specs=[
            pl.BlockSpec(memory_space=pl.ANY),
            pl.BlockSpec(memory_space=pl.ANY),
            pl.BlockSpec((None, CAP_ALL, 1), lambda e, f, idx: (e, 0, 0)),
            pl.BlockSpec((None, None, D_MODEL, MOE_TF), lambda e, f, idx: (layer, e, 0, f)),
            pl.BlockSpec((None, None, D_MODEL, MOE_TF), lambda e, f, idx: (layer, e, 0, f)),
            pl.BlockSpec((None, None, MOE_TF, D_MODEL), lambda e, f, idx: (layer, e, f, 0)),
        ],
        out_specs=pl.BlockSpec((None, CAP_ALL, D_MODEL), lambda e, f, idx: (e, 0, 0)),
        scratch_shapes=[pltpu.VMEM((2, CAP_ALL * ROW_TILE, LANES), F32), pltpu.SemaphoreType.DMA((2,))],
    )
    return pl.pallas_call(
        _moe_kernel,
        grid_spec=grid_spec,
        out_shape=jax.ShapeDtypeStruct((N_EXPERTS, CAP_ALL, D_MODEL), F32),
        compiler_params=_params("moe_experts", 2),
        name="moe_experts",
    )(idx_all, xn2_ctx, xn2_lat, aff_all, w_gate_e, w_up_e, w_down_e)


POS_SPLIT = 32


def _prefix_count(mask_bf, tri_lane, tri_blk_strict):
    within = jnp.dot(mask_bf, tri_lane, preferred_element_type=F32)
    bs = jnp.broadcast_to(within[:, LANES - 1:LANES], within.shape).astype(BF16)
    return within + jnp.dot(tri_blk_strict, bs, preferred_element_type=F32)


def _route_kernel(p_ref, idx_ref, aff_ref, *, cap):
    p_all = p_ref[...]
    nb = p_all.shape[1]
    capf = float(cap)

    def as_float(word):
        return lax.bitcast_convert_type(word, F32)

    def bisect(_, c):
        lo, hi = c
        mid = lo + ((hi - lo) >> 1)
        ok = jnp.sum(jnp.where(p_all >= as_float(mid), 1.0, 0.0), axis=(1, 2), keepdims=True) >= capf
        return jnp.where(ok, mid, lo), jnp.where(ok, hi, mid)

    lo0 = jnp.zeros((N_EXPERTS, 1, 1), jnp.int32)
    hi0 = jnp.full((N_EXPERTS, 1, 1), 0x7F800000, jnp.int32)
    thr_all, _ = lax.fori_loop(0, 31, bisect, (lo0, hi0))
    thr_val = as_float(thr_all)

    li = lax.broadcasted_iota(jnp.int32, (LANES, LANES), 0)
    lj = lax.broadcasted_iota(jnp.int32, (LANES, LANES), 1)
    tri_lane = jnp.where(li <= lj, 1.0, 0.0).astype(BF16)
    bi = lax.broadcasted_iota(jnp.int32, (nb, nb), 0)
    bj = lax.broadcasted_iota(jnp.int32, (nb, nb), 1)
    tri_blk_strict = jnp.where(bj < bi, 1.0, 0.0).astype(BF16)
    tri_blk_incl = jnp.where(bi <= bj, 1.0, 0.0).astype(BF16)
    s_col = lax.broadcasted_iota(jnp.int32, (cap, 1), 0).astype(F32)
    blk_id = lax.broadcasted_iota(jnp.int32, (1, nb), 1).astype(F32)
    lane_id = lax.broadcasted_iota(jnp.int32, (1, LANES), 1).astype(F32)
    ones_rows = jnp.ones((8, LANES), BF16)

    for e in range(N_EXPERTS):
        p = p_all[e]
        thr = thr_val[e]
        gt = p > thr
        eq = p == thr
        need = capf - jnp.sum(jnp.where(gt, 1.0, 0.0), keepdims=True)
        tie_rank = _prefix_count(jnp.where(eq, 1.0, 0.0).astype(BF16), tri_lane, tri_blk_strict)
        sel = jnp.where(gt | (eq & (tie_rank <= need)), 1.0, 0.0).astype(BF16)
        pos = _prefix_count(sel, tri_lane, tri_blk_strict)

        bs_row = lax.dot_general(ones_rows, sel, _NT, preferred_element_type=F32)
        bp_row = jnp.dot(bs_row.astype(BF16), tri_blk_incl, preferred_element_type=F32)[0:1, :]
        blk = jnp.sum(jnp.where(bp_row <= s_col, 1.0, 0.0), axis=-1, keepdims=True)
        onehot = jnp.where(blk == blk_id, 1.0, 0.0).astype(BF16)
        pos_hi = jnp.floor(pos * (1.0 / POS_SPLIT))
        pos_lo = pos - pos_hi * POS_SPLIT
        pos_row = (jnp.dot(onehot, pos_hi.astype(BF16), preferred_element_type=F32) * POS_SPLIT
                   + jnp.dot(onehot, pos_lo.astype(BF16), preferred_element_type=F32))
        lane = jnp.sum(jnp.where(pos_row <= s_col, 1.0, 0.0), axis=-1, keepdims=True)
        idx_ref[e] = (blk * LANES + lane).astype(jnp.int32)

        p1 = p.astype(BF16)
        r1 = p - p1.astype(F32)
        p2 = r1.astype(BF16)
        p3 = (r1 - p2.astype(F32)).astype(BF16)
        p_row = (jnp.dot(onehot, p1, preferred_element_type=F32) + jnp.dot(onehot, p2, preferred_element_type=F32)
                 + jnp.dot(onehot, p3, preferred_element_type=F32))
        aff_ref[e] = jnp.sum(jnp.where(lane_id == lane, p_row, 0.0), axis=-1, keepdims=True)


def _route(probs_t, cap):
    nb = probs_t.shape[1] // LANES
    slot = pl.BlockSpec((N_EXPERTS, cap, 1), lambda i: (0, 0, 0))
    return pl.pallas_call(
        functools.partial(_route_kernel, cap=cap),
        grid=(1,),
        in_specs=[pl.BlockSpec((N_EXPERTS, nb, LANES), lambda i: (0, 0, 0))],
        out_specs=[slot, slot],
        out_shape=[jax.ShapeDtypeStruct((N_EXPERTS, cap, 1), jnp.int32),
                   jax.ShapeDtypeStruct((N_EXPERTS, cap, 1), F32)],
        compiler_params=_params("route", 1),
        name="route",
    )(probs_t.reshape(N_EXPERTS, nb, LANES))


CMB_TT = 1024
CMB_CH = 256
CMB_GROUP = 8
CMB_EPI_ROWS = 256


def _combine_kernel(idx_ref, st_ref, ye_hbm, x_ref, g2_ref, ng_ref, sc_ref, sh_ref, *rest, off, cap, nt, final):
    if final:
        y_ref, acc_ref, buf_ref, sem = rest
    else:
        xo_ref, xn_ref, acc_ref, buf_ref, sem = rest
    k = pl.program_id(0)
    t0 = k * CMB_TT
    acc_ref[...] = jnp.zeros_like(acc_ref)

    def slot_range(e):
        s0 = st_ref[e * (nt + 1) + k]
        s1 = st_ref[e * (nt + 1) + k + 1]
        return s0, s1, jnp.minimum(s0 & -CMB_GROUP, cap - CMB_CH)

    def chunk_copy(e, start):
        src = ye_hbm.at[e, pl.ds(pl.multiple_of(off + start, CMB_GROUP), CMB_CH)]
        return pltpu.make_async_copy(src, buf_ref.at[e], sem.at[e])

    def accumulate(e, start, lo, hi):
        def group(i, carry, checked):
            r0 = pl.multiple_of(i * CMB_GROUP, CMB_GROUP)
            tile = buf_ref[e, pl.ds(r0, CMB_GROUP), :]
            dst = []
            for u in range(CMB_GROUP):
                slot = start + r0 + u
                row = idx_ref[e * CAP_ALL + off + slot] - t0
                if checked:
                    row = jnp.where((slot >= lo) & (slot < hi), row, CMB_TT)
                dst.append(row)
            rows = [acc_ref[pl.ds(d, 1), :] for d in dst]
            for u in range(CMB_GROUP):
                acc_ref[pl.ds(dst[u], 1), :] = rows[u] + tile[u:u + 1, :]
            return carry

        g_lo = (lo - start) // CMB_GROUP
        g_hi = (hi - start + CMB_GROUP - 1) // CMB_GROUP
        full_lo = jnp.minimum((lo - start + CMB_GROUP - 1) // CMB_GROUP, g_hi)
        full_hi = jnp.maximum((hi - start) // CMB_GROUP, full_lo)
        lax.fori_loop(g_lo, full_lo, functools.partial(group, checked=True), 0)
        lax.fori_loop(full_lo, full_hi, functools.partial(group, checked=False), 0)
        lax.fori_loop(full_hi, g_hi, functools.partial(group, checked=True), 0)

    for e in range(N_EXPERTS):
        chunk_copy(e, slot_range(e)[2]).start()
    for e in range(N_EXPERTS):
        s0, s1, a = slot_range(e)
        chunk_copy(e, a).wait()
        accumulate(e, a, s0, jnp.minimum(s1, a + CMB_CH))

        def extra(j, carry, e=e, s1=s1, a=a):
            lo = a + CMB_CH * (j + 1)
            start = jnp.minimum(lo, cap - CMB_CH)
            cp = chunk_copy(e, start)
            cp.start()
            cp.wait()
            accumulate(e, start, lo, jnp.minimum(s1, lo + CMB_CH))
            return carry

        lax.fori_loop(0, jnp.maximum(s1 - a - 1, 0) // CMB_CH, extra, 0)

    for r0 in range(0, CMB_TT, CMB_EPI_ROWS):
        rs = slice(r0, r0 + CMB_EPI_ROWS)
        xo = x_ref[rs, :] + g2_ref[...] * acc_ref[rs, :]
        y = xo * lax.rsqrt(jnp.mean(xo * xo, axis=-1, keepdims=True) + EPS) * ng_ref[...]
        if final:
            y_ref[rs, :] = y
        else:
            xo_ref[rs, :] = xo
            xn_ref[rs, :] = (y * (1.0 + sc_ref[...]) + sh_ref[...]).astype(xn_ref.dtype)


def _combine(idx_all, starts, ye, x, g2, ng, sc, sh, seq, off, cap, final):
    n = x.shape[0]
    nt = n // CMB_TT
    per_batch = g2.shape[0] > 1
    mod_map = ((lambda i, *_: ((i * CMB_TT) // seq, 0, 0)) if per_batch else (lambda i, *_: (0, 0, 0)))
    mod_spec = pl.BlockSpec((None, 1, D_MODEL), mod_map)
    row = pl.BlockSpec((CMB_TT, D_MODEL), lambda i, *_: (i, 0))
    vec = pl.BlockSpec((1, D_MODEL), lambda i, *_: (0, 0))
    if final:
        out_specs, out_shape = row, jax.ShapeDtypeStruct((n, D_MODEL), F32)
    else:
        out_specs = [row, row]
        out_shape = [jax.ShapeDtypeStruct((n, D_MODEL), F32), jax.ShapeDtypeStruct((n, D_MODEL), BF16)]
    grid_spec = pltpu.PrefetchScalarGridSpec(
        num_scalar_prefetch=2,
        grid=(nt,),
        in_specs=[pl.BlockSpec(memory_space=pl.ANY), row, mod_spec, vec, mod_spec, mod_spec],
        out_specs=out_specs,
        scratch_shapes=[pltpu.VMEM((CMB_TT + CMB_GROUP, D_MODEL), F32),
                        pltpu.VMEM((N_EXPERTS, CMB_CH, D_MODEL), F32),
                        pltpu.SemaphoreType.DMA((N_EXPERTS,))],
    )
    return pl.pallas_call(
        functools.partial(_combine_kernel, off=off, cap=cap, nt=nt, final=final),
        grid_spec=grid_spec,
        out_shape=out_shape,
        compiler_params=_params("combine", 1),
        name="combine",
    )(idx_all, starts.reshape(-1), ye, x, g2, ng, sc, sh)


def _tile_starts(idx_sorted, n):
    bounds = jnp.arange(n // CMB_TT + 1, dtype=jnp.int32) * CMB_TT
    return jnp.sum(idx_sorted[:, :, None] < bounds[None, None, :], axis=1, dtype=jnp.int32)


def kernel(x_prompt, x_sample, cache_attn_k, cache_attn_v, cache_na_k, cache_na_v, c, c_ctx, w_ada, b_ada, norm1_g,
           w_in, q_norm_g, k_norm_g, na_rpb, conv_w, w_proj_a, w_proj_b, w_proj_c, w_out, norm2_g, w_router,
           w_gate_e, w_up_e, w_down_e, final_g):
    cond8 = jnp.zeros((8, D_MODEL), F32).at[0].set(c_ctx).at[1:1 + DEC_BATCH].set(c)
    mods = _adaln(cond8, w_ada, b_ada).reshape(DEPTH, 8, N_MOD, D_MODEL)
    cos128, sin128 = _rope_tables()

    hp = x_prompt.reshape(N_CTX, D_MODEL)
    hs = x_sample.reshape(N_LAT, D_MODEL)
    kv_stacked = []
    for l in range(DEPTH):
        mod_ctx = [mods[l, 0:1, i][:, None, :] for i in range(N_MOD)]
        mod_lat = [mods[l, 1:1 + DEC_BATCH, i][:, None, :] for i in range(N_MOD)]
        n1g = norm1_g[l][None]
        n2g = norm2_g[l][None]
        qg2 = jnp.tile(q_norm_g[l][None], (1, 2))
        kg2 = jnp.tile(k_norm_g[l][None], (1, 2))
        wa, wb, wc, wo = (w_proj_a[l].astype(BF16), w_proj_b[l].astype(BF16), w_proj_c[l].astype(BF16),
                          w_out[l].astype(BF16))
        wrt = w_router[l].T.astype(BF16)
        toep = _na_toeplitz(na_rpb[l])

        def sublayer1(x, xn, mod, seq, is_ctx):
            _, _, g1, sh2, sc2, _ = mod
            kv_dtype = F32 if is_ctx else BF16
            h_a = _inproj(xn, w_in, l, COL_A, W_A, 768, kv_dtype)
            h_b = _inproj(xn, w_in, l, COL_B, W_B, 768, kv_dtype)
            h_c = _inproj(xn, w_in, l, COL_C, W_C, 768, BF16)
            h_g = _inproj(xn, w_in, l, COL_G, GATE_W, 768, BF16)
            kv = None
            if is_ctx:
                oab, *kv = _attn_ctx(h_a, h_b, qg2, kg2, l, kv_stacked)
            else:
                oab = _attn_lat(h_a, h_b, cache_attn_k, cache_attn_v, cache_na_k, cache_na_v, l, qg2, kg2,
                                cos128, sin128, toep)
            oc = _conv(h_c, conv_w[l], seq)
            xo, xn2, probs_t = _mix(x, oab, oc, h_g, wa, wb, wc, wo, g1, n2g, sc2, sh2, wrt, seq)
            return xo, xn2, probs_t, kv

        if l == 0:
            xn_p = _norm_mod(hp, n1g, mod_ctx[1], mod_ctx[0], SEQ, BF16)
            xn_s = _norm_mod(hs, n1g, mod_lat[1], mod_lat[0], DEC_SEQ, BF16)
        hp, xn2_c, pt_c, kv_stacked = sublayer1(hp, xn_p, mod_ctx, SEQ, True)
        hs, xn2_l, pt_l, _ = sublayer1(hs, xn_s, mod_lat, DEC_SEQ, False)

        idx_c, aff_c = _route(pt_c, CAP_CTX)
        idx_l, aff_l = _route(pt_l, CAP_LAT)
        idx_all = jnp.concatenate([idx_c, idx_l], axis=1).reshape(-1)
        aff_all = jnp.concatenate([aff_c, aff_l], axis=1)
        ye = _moe(idx_all, xn2_c, xn2_l, aff_all, w_gate_e, w_up_e, w_down_e, l)

        final = l == DEPTH - 1
        if final:
            ng, nsc_c, nsh_c, nsc_l, nsh_l = final_g[None], mod_ctx[1], mod_ctx[0], mod_lat[1], mod_lat[0]
        else:
            nxt_ctx = [mods[l + 1, 0:1, i][:, None, :] for i in range(2)]
            nxt_lat = [mods[l + 1, 1:1 + DEC_BATCH, i][:, None, :] for i in range(2)]
            ng, nsc_c, nsh_c, nsc_l, nsh_l = norm1_g[l + 1][None], nxt_ctx[1], nxt_ctx[0], nxt_lat[1], nxt_lat[0]
        out_c = _combine(idx_all, _tile_starts(idx_c[..., 0], N_CTX), ye, hp, mod_ctx[5], ng, nsc_c, nsh_c,
                         SEQ, 0, CAP_CTX, final)
        out_l = _combine(idx_all, _tile_starts(idx_l[..., 0], N_LAT), ye, hs, mod_lat[5], ng, nsc_l, nsh_l,
                         DEC_SEQ, CAP_CTX, CAP_LAT, final)
        if final:
            y_prompt = out_c.reshape(BATCH, SEQ, D_MODEL)
            y_sample = out_l.reshape(DEC_BATCH, DEC_SEQ, D_MODEL)
        else:
            (hp, xn_p), (hs, xn_s) = out_c, out_l

    return (y_prompt, y_sample, *kv_stacked)
```
